```python
import math
import jax, jax.numpy as jnp
from jax import lax
import numpy as np

D_MODEL = 1024
BATCH = 8
SEQ = 2048
DEPTH = 1

SSM_WIDTH = D_MODEL // 2
SSM_GROUP = 16
SSM_GROUPS = SSM_WIDTH // SSM_GROUP
SSM_STATE = 64
HEAD_DIM = 64
ATTN_SLOTS = (D_MODEL // 2) // HEAD_DIM
ATTN_PATTERNS = ((128, 1), (512, 4), (2048, 16))
N_PATTERNS = len(ATTN_PATTERNS)
N_Q_HEADS = ATTN_SLOTS * N_PATTERNS
ATTN_WIDTH = ATTN_SLOTS * HEAD_DIM
ROPE_THETA = 500000.0
ROPE_DIM = HEAD_DIM // 4
BAND_BLOCK = 64
N_BRANCHES = 2
EPS = 1e-6
NEG_INF = -1e30
COL_SIZES = (SSM_WIDTH, SSM_WIDTH, N_Q_HEADS * HEAD_DIM, ATTN_WIDTH, ATTN_WIDTH, ATTN_WIDTH, N_BRANCHES * D_MODEL)
IN_WIDTH = int(sum(COL_SIZES))
SPLITS = tuple(int(s) for s in np.cumsum(COL_SIZES)[:-1])

kernel_name = "hybrid_s5_dilated_attn_gated_block"


def rms_norm(x, w):
    xf = x.astype(jnp.float32)
    var = jnp.mean(xf * xf, axis=-1, keepdims=True)
    return xf * lax.rsqrt(var + EPS) * w.astype(jnp.float32)


def rope_partial(t, pos):
    inv = ROPE_THETA ** (-jnp.arange(0, ROPE_DIM, 2, dtype=jnp.float32) / ROPE_DIM)
    ang = pos[:, None] * inv[None, :]
    shape = (1, ang.shape[0]) + (1,) * (t.ndim - 3) + (ang.shape[1],)
    cos = jnp.cos(ang).reshape(shape)
    sin = jnp.sin(ang).reshape(shape)
    half = ROPE_DIM // 2
    t1 = t[..., :half]
    t2 = t[..., half:ROPE_DIM]
    rot = jnp.concatenate([t1 * cos - t2 * sin, t2 * cos + t1 * sin], axis=-1)
    return jnp.concatenate([rot, t[..., ROPE_DIM:]], axis=-1)


def dilated_band_attention(q, k, v, window, dilation):
    bsz, L, H, D = q.shape
    d = dilation
    half = window // (2 * d)
    n = L // d
    nb = -(-n // BAND_BLOCK)
    npad = nb * BAND_BLOCK
    nsub = bsz * d

    def to_sub(t):
        return t.reshape(bsz, n, d, H, D).transpose(0, 2, 3, 1, 4).reshape(nsub, H, n, D)

    qs, ks, vs = to_sub(q), to_sub(k), to_sub(v)
    qb = jnp.pad(qs, ((0, 0), (0, 0), (0, npad - n), (0, 0))).reshape(nsub, H, nb, BAND_BLOCK, D)

    def neighbours(t):
        tp = jnp.pad(t, ((0, 0), (0, 0), (BAND_BLOCK, npad - n + BAND_BLOCK), (0, 0)))
        tp = tp.reshape(nsub, H, nb + 2, BAND_BLOCK, D)
        return jnp.concatenate([tp[:, :, 0:nb], tp[:, :, 1:nb + 1], tp[:, :, 2:nb + 2]], axis=3)

    kb, vb = neighbours(ks), neighbours(vs)
    qi = jnp.arange(nb)[:, None, None] * BAND_BLOCK + jnp.arange(BAND_BLOCK)[None, :, None]
    ki = (jnp.arange(nb)[:, None, None] - 1) * BAND_BLOCK + jnp.arange(3 * BAND_BLOCK)[None, None, :]
    valid = (jnp.abs(ki - qi) <= half) & (ki >= 0) & (ki < n)

    s = jnp.einsum('nhbqd,nhbkd->nhbqk', qb, kb) * (1.0 / math.sqrt(D))
    s = jnp.where(valid, s, NEG_INF)
    m = jnp.max(s, axis=-1, keepdims=True)
    p = jnp.where(valid, jnp.exp(s - m), 0.0)
    den = jnp.sum(p, axis=-1, keepdims=True)
    o = jnp.einsum('nhbqk,nhbkd->nhbqd', p, vb) / den
    lse = (m + jnp.log(den))[..., 0]

    o = o.reshape(nsub, H, npad, D)[:, :, :n]
    lse = lse.reshape(nsub, H, npad)[:, :, :n]
    o = o.reshape(bsz, d, H, n, D).transpose(0, 3, 1, 2, 4).reshape(bsz, L, H, D)
    lse = lse.reshape(bsz, d, H, n).transpose(0, 3, 1, 2).reshape(bsz, L, H)
    return o, lse


def _complex_scan_op(e1, e2):
    ar1, ai1, br1, bi1 = e1
    ar2, ai2, br2, bi2 = e2
    ar = ar2 * ar1 - ai2 * ai1
    ai = ar2 * ai1 + ai2 * ar1
    br = ar2 * br1 - ai2 * bi1 + br2
    bi = ar2 * bi1 + ai2 * br1 + bi2
    return (ar, ai, br, bi)


def bidir_s5(u, lam_re, lam_im, log_dt, b_re, b_im, c_re, c_im, d_skip):
    bsz, L, W = u.shape
    ug = u.reshape(bsz, L, SSM_GROUPS, SSM_GROUP)
    y = jnp.zeros_like(ug)
    for direction in range(2):
        lr = jnp.minimum(lam_re[direction].astype(jnp.float32), -1e-4)
        li = lam_im[direction].astype(jnp.float32)
        dt = jnp.exp(log_dt[direction].astype(jnp.float32))[:, None]
        er = jnp.exp(lr * dt)
        ab_re = er * jnp.cos(li * dt)
        ab_im = er * jnp.sin(li * dt)
        nr = ab_re - 1.0
        ni = ab_im
        mag = lr * lr + li * li
        f_re = (nr * lr + ni * li) / mag
        f_im = (ni * lr - nr * li) / mag
        br = b_re[direction].astype(jnp.float32)
        bi = b_im[direction].astype(jnp.float32)
        bb_re = f_re[..., None] * br - f_im[..., None] * bi
        bb_im = f_re[..., None] * bi + f_im[..., None] * br
        bu_re = jnp.einsum('blgh,gph->blgp', ug, bb_re)
        bu_im = jnp.einsum('blgh,gph->blgp', ug, bb_im)
        a_re = jnp.broadcast_to(ab_re, bu_re.shape)
        a_im = jnp.broadcast_to(ab_im, bu_im.shape)
        _, _, xs_re, xs_im = lax.associative_scan(
            _complex_scan_op, (a_re, a_im, bu_re, bu_im), reverse=(direction == 1), axis=1)
        y = y + jnp.einsum('blgp,ghp->blgh', xs_re, c_re[direction].astype(jnp.float32)) \
              - jnp.einsum('blgp,ghp->blgh', xs_im, c_im[direction].astype(jnp.float32))
    return y.reshape(bsz, L, W) + d_skip.astype(jnp.float32) * u


def setup_inputs(seed: int = 0) -> dict:
    key = jax.random.key(seed)
    ks = jax.random.split(key, 20)
    f32 = jnp.float32
    G, P, H = SSM_GROUPS, SSM_STATE, SSM_GROUP
    nrm = lambda k, shape, scale: jax.random.normal(k, shape, f32) * scale
    x = jax.random.normal(ks[0], (BATCH, SEQ, D_MODEL), f32)
    norm_w = 1.0 + nrm(ks[1], (DEPTH, D_MODEL), 0.02)
    w_in = nrm(ks[2], (DEPTH, D_MODEL, IN_WIDTH), D_MODEL ** -0.5)
    b_gate = nrm(ks[3], (DEPTH, N_BRANCHES * D_MODEL), 0.02)
    q_norm_w = 1.0 + nrm(ks[4], (DEPTH, HEAD_DIM), 0.02)
    k_norm_w = 1.0 + nrm(ks[5], (DEPTH, HEAD_DIM), 0.02)
    ssm_lam_re = -0.5 + nrm(ks[6], (DEPTH, 2, G, P), 0.01)
    ssm_lam_im = jnp.pi * jnp.arange(P, dtype=f32)[None, None, None, :] + nrm(ks[7], (DEPTH, 2, G, P), 0.01)
    ssm_log_dt = jax.random.uniform(ks[8], (DEPTH, 2, G), f32, math.log(1e-3), math.log(1e-1))
    ssm_b_re = nrm(ks[9], (DEPTH, 2, G, P, H), (2.0 * H) ** -0.5)
    ssm_b_im = nrm(ks[10], (DEPTH, 2, G, P, H), (2.0 * H) ** -0.5)
    ssm_c_re = nrm(ks[11], (DEPTH, 2, G, H, P), (P / 2.0) ** -0.5)
    ssm_c_im = nrm(ks[12], (DEPTH, 2, G, H, P), (P / 2.0) ** -0.5)
    ssm_d = 1.0 + nrm(ks[13], (DEPTH, SSM_WIDTH), 0.1)
    w_glu = nrm(ks[14], (DEPTH, SSM_WIDTH, 2 * SSM_WIDTH), SSM_WIDTH ** -0.5)
    b_glu = nrm(ks[15], (DEPTH, 2 * SSM_WIDTH), 0.02)
    w_proj_ssm = nrm(ks[16], (DEPTH, SSM_WIDTH, D_MODEL), SSM_WIDTH ** -0.5)
    w_proj_attn = nrm(ks[17], (DEPTH, ATTN_WIDTH, D_MODEL), ATTN_WIDTH ** -0.5)
    w_out = nrm(ks[18], (DEPTH, D_MODEL, D_MODEL), D_MODEL ** -0.5)
    return {"x": x, "norm_w": norm_w, "w_in": w_in, "b_gate": b_gate,
            "q_norm_w": q_norm_w, "k_norm_w": k_norm_w,
            "ssm_lam_re": ssm_lam_re, "ssm_lam_im": ssm_lam_im, "ssm_log_dt": ssm_log_dt,
            "ssm_b_re": ssm_b_re, "ssm_b_im": ssm_b_im, "ssm_c_re": ssm_c_re, "ssm_c_im": ssm_c_im,
            "ssm_d": ssm_d, "w_glu": w_glu, "b_glu": b_glu,
            "w_proj_ssm": w_proj_ssm, "w_proj_attn": w_proj_attn, "w_out": w_out}


def reference(x, norm_w, w_in, b_gate, q_norm_w, k_norm_w, ssm_lam_re, ssm_lam_im, ssm_log_dt,
              ssm_b_re, ssm_b_im, ssm_c_re, ssm_c_im, ssm_d, w_glu, b_glu,
              w_proj_ssm, w_proj_attn, w_out):
    bsz, L, _ = x.shape
    pos = jnp.arange(L, dtype=jnp.float32)
    for layer in range(DEPTH):
        h = rms_norm(x, norm_w[layer])
        proj = h @ w_in[layer].astype(jnp.float32)
        u_a, z_a, q, k, v, z_b, g = jnp.split(proj, SPLITS, axis=-1)
        g_a, g_b = jnp.split(g + b_gate[layer].astype(jnp.float32), N_BRANCHES, axis=-1)

        q = q.reshape(bsz, L, N_PATTERNS, ATTN_SLOTS, HEAD_DIM)
        k = k.reshape(bsz, L, ATTN_SLOTS, HEAD_DIM)
        v = v.reshape(bsz, L, ATTN_SLOTS, HEAD_DIM)
        q = rope_partial(rms_norm(q, q_norm_w[layer]), pos)
        k = rope_partial(rms_norm(k, k_norm_w[layer]), pos)
        outs, lses = [], []
        for p_idx, (window, dilation) in enumerate(ATTN_PATTERNS):
            o, lse = dilated_band_attention(q[:, :, p_idx], k, v, window, dilation)
            outs.append(o)
            lses.append(lse)
        wts = jax.nn.softmax(jnp.stack(lses, axis=0), axis=0)
        attn = jnp.einsum('gblh,gblhd->blhd', wts, jnp.stack(outs, axis=0)).reshape(bsz, L, ATTN_WIDTH)
        y_b = (attn * jax.nn.silu(z_b)) @ w_proj_attn[layer].astype(jnp.float32)

        y_s = bidir_s5(u_a, ssm_lam_re[layer], ssm_lam_im[layer], ssm_log_dt[layer],
                       ssm_b_re[layer], ssm_b_im[layer], ssm_c_re[layer], ssm_c_im[layer], ssm_d[layer])
        y_s = jax.nn.gelu(y_s)
        glu_val, glu_gate = jnp.split(y_s @ w_glu[layer].astype(jnp.float32) + b_glu[layer].astype(jnp.float32), 2, axis=-1)
        y_a = (glu_val * jax.nn.sigmoid(glu_gate) * jax.nn.silu(z_a)) @ w_proj_ssm[layer].astype(jnp.float32)

        mix = jax.nn.sigmoid(g_a) * y_a + jax.nn.sigmoid(g_b) * y_b
        x = x + (mix @ w_out[layer].astype(jnp.float32)).astype(x.dtype)
    return x
```

```python
import functools
import math

import jax
import jax.numpy as jnp
import numpy as np
from jax import lax
from jax.experimental import pallas as pl
from jax.experimental.pallas import tpu as pltpu

F32 = jnp.float32
BF16 = jnp.bfloat16

D_MODEL = 1024
SSM_WIDTH = 512
SSM_GROUP = 16
SSM_GROUPS = 32
SSM_STATE = 64
HEAD_DIM = 64
ATTN_SLOTS = 8
ATTN_WIDTH = 512
DILATIONS = (1, 4, 16)
BAND_HALF = 64
ROPE_THETA = 500000.0
ROPE_DIM = 16
EPS = 1e-6
NEG_INF = -1e30
IN_WIDTH = 6144
COL_U, COL_ZA, COL_Q, COL_K, COL_V, COL_ZB, COL_G = 0, 512, 1024, 2560, 3072, 3584, 4096

LANES = 128
VMEM_LIMIT = 56 * 1024 * 1024

CHUNK = 16
SG = 8
N_SG = SSM_GROUPS // SG
SGP = SG * SSM_STATE
NSTATE = 4 * SGP
TPC = SGP // LANES
CW = CHUNK * LANES


def _dot(a, b):
    return jnp.dot(a, b, preferred_element_type=F32)


def _dot_nt(a, b):
    return lax.dot_general(a, b, (((1,), (1,)), ((), ())), preferred_element_type=F32)


def _dot_nt_f32(a, b):
    return lax.dot_general(a, b, (((1,), (1,)), ((), ())), preferred_element_type=F32,
                           precision=lax.Precision.HIGHEST)


IN_TT = 512
IN_TN = 512


def _in_proj_kernel(x_ref, nw_ref, w_ref, b_ref, o_ref):
    x = x_ref[...]
    var = jnp.mean(x * x, axis=-1, keepdims=True)
    h = (x * lax.rsqrt(var + EPS) * nw_ref[...]).astype(BF16)
    for j in range(IN_WIDTH // IN_TN):
        cols = slice(j * IN_TN, (j + 1) * IN_TN)
        o_ref[:, cols] = _dot(h, w_ref[:, cols]) + b_ref[:, cols]


def _in_proj(x2d, norm_w, w_in_bf16, bias_row):
    t = x2d.shape[0]
    return pl.pallas_call(
        _in_proj_kernel,
        grid=(t // IN_TT,),
        in_specs=[
            pl.BlockSpec((IN_TT, D_MODEL), lambda i: (i, 0)),
            pl.BlockSpec((1, D_MODEL), lambda i: (0, 0)),
            pl.BlockSpec((D_MODEL, IN_WIDTH), lambda i: (0, 0), pipeline_mode=pl.Buffered(1)),
            pl.BlockSpec((1, IN_WIDTH), lambda i: (0, 0)),
        ],
        out_specs=pl.BlockSpec((IN_TT, IN_WIDTH), lambda i: (i, 0)),
        out_shape=jax.ShapeDtypeStruct((t, IN_WIDTH), F32),
        compiler_params=pltpu.CompilerParams(
            dimension_semantics=("arbitrary",), vmem_limit_bytes=VMEM_LIMIT),
        name="in_proj",
    )(x2d, norm_w, w_in_bf16, bias_row)


QBLK = 128
NORM_ROWS = 256


def _attn_kernel(q0_ref, q1_ref, q2_ref, k_ref, v_ref, cos_ref, sa_ref, sb_ref,
                 qw_ref, kw_ref, ones_ref, o_ref, qn_s, kn_s, oacc_s, lse_s):
    seq = k_ref.shape[1]
    ones_blk = ones_ref[...]
    lane = lax.broadcasted_iota(jnp.int32, (QBLK, LANES), 1)
    head0 = lane < HEAD_DIM

    def norm_rope(x, w, rows):
        x2 = x * x
        hi = x2.astype(BF16)
        lo = (x2 - hi.astype(F32)).astype(BF16)
        ms = _dot(hi, ones_blk) + _dot(lo, ones_blk)
        xn = x * lax.rsqrt(ms + EPS) * w
        return (xn * cos_ref[rows, :]
                + pltpu.roll(xn, LANES - ROPE_DIM // 2, 1) * sa_ref[rows, :]
                + pltpu.roll(xn, ROPE_DIM // 2, 1) * sb_ref[rows, :])

    def prep(i, carry):
        rows = pl.ds(pl.multiple_of(i * NORM_ROWS, NORM_ROWS), NORM_ROWS)
        kn_s[rows, :] = norm_rope(k_ref[0, rows, :], kw_ref[...], rows)
        for p, q_ref in enumerate((q0_ref, q1_ref, q2_ref)):
            qn_s[p, rows, :] = norm_rope(q_ref[0, rows, :], qw_ref[...], rows) * (HEAD_DIM ** -0.5)
        return carry

    lax.fori_loop(0, seq // NORM_ROWS, prep, 0)

    for p, d in enumerate(DILATIONS):
        n = seq // d
        nblk = n // QBLK
        kw = min(n, QBLK + 2 * BAND_HALF)
        base = (lax.broadcasted_iota(jnp.int32, (QBLK, kw), 1)
                - lax.broadcasted_iota(jnp.int32, (QBLK, kw), 0))

        def block(idx, carry, p=p, d=d, n=n, nblk=nblk, kw=kw, base=base):
            r = idx // nblk
            q0 = (idx % nblk) * QBLK
            ks = jnp.clip(q0 - BAND_HALF, 0, n - kw)
            qrows = pl.ds(r + d * q0, QBLK, stride=d)
            krows = pl.ds(r + d * ks, kw, stride=d)
            qb = qn_s[p, qrows, :]
            kb = kn_s[krows, :].astype(BF16)
            vb = v_ref[0, krows, :].astype(BF16)
            valid = jnp.abs(base + (ks - q0)) <= BAND_HALF
            outs, lses = [], []
            for h in range(2):
                qh = jnp.where(head0 if h == 0 else ~head0, qb, 0.0).astype(BF16)
                s = jnp.where(valid, _dot_nt(qh, kb), NEG_INF)
                m = jnp.max(s, axis=-1, keepdims=True)
                e = jnp.exp(s - m)
                den = jnp.sum(e, axis=-1, keepdims=True)
                outs.append(_dot(e.astype(BF16), vb) / den)
                lses.append(jnp.broadcast_to(m + jnp.log(den), (QBLK, LANES)))
            oacc_s[p, qrows, :] = jnp.where(head0, outs[0], outs[1])
            lse_s[p, qrows, :] = jnp.where(head0, lses[0], lses[1])
            return carry

        lax.fori_loop(0, d * nblk, block, 0)

    def combine(i, carry):
        rows = pl.ds(pl.multiple_of(i * NORM_ROWS, NORM_ROWS), NORM_ROWS)
        l0, l1, l2 = lse_s[0, rows, :], lse_s[1, rows, :], lse_s[2, rows, :]
        m = jnp.maximum(jnp.maximum(l0, l1), l2)
        w0, w1, w2 = jnp.exp(l0 - m), jnp.exp(l1 - m), jnp.exp(l2 - m)
        num = w0 * oacc_s[0, rows, :] + w1 * oacc_s[1, rows, :] + w2 * oacc_s[2, rows, :]
        o_ref[0, rows, :] = num / (w0 + w1 + w2)
        return carry

    lax.fori_loop(0, seq // NORM_ROWS, combine, 0)


def _attention(proj3d, cosf, sa, sb, qw_row, kw_row, ones_blk):
    bsz, seq, _ = proj3d.shape
    n_pairs = ATTN_WIDTH // LANES

    def qspec(p):
        return pl.BlockSpec((1, seq, LANES),
                            lambda b, hp, p=p: (b, 0, COL_Q // LANES + p * n_pairs + hp))

    const2d = lambda shape: pl.BlockSpec(shape, lambda b, hp: (0, 0))
    return pl.pallas_call(
        _attn_kernel,
        grid=(bsz, n_pairs),
        in_specs=[
            qspec(0), qspec(1), qspec(2),
            pl.BlockSpec((1, seq, LANES), lambda b, hp: (b, 0, COL_K // LANES + hp)),
            pl.BlockSpec((1, seq, LANES), lambda b, hp: (b, 0, COL_V // LANES + hp)),
            const2d((seq, LANES)), const2d((seq, LANES)), const2d((seq, LANES)),
            const2d((1, LANES)), const2d((1, LANES)), const2d((LANES, LANES)),
        ],
        out_specs=pl.BlockSpec((1, seq, LANES), lambda b, hp: (b, 0, hp)),
        out_shape=jax.ShapeDtypeStruct((bsz, seq, ATTN_WIDTH), F32),
        scratch_shapes=[
            pltpu.VMEM((len(DILATIONS), seq, LANES), F32),
            pltpu.VMEM((seq, LANES), F32),
            pltpu.VMEM((len(DILATIONS), seq, LANES), F32),
            pltpu.VMEM((len(DILATIONS), seq, LANES), F32),
        ],
        compiler_params=pltpu.CompilerParams(
            dimension_semantics=("arbitrary", "arbitrary"), vmem_limit_bytes=VMEM_LIMIT),
        name="dilated_attention",
    )(proj3d, proj3d, proj3d, proj3d, proj3d, cosf, sa, sb, qw_row, kw_row, ones_blk)


S5_BB = 4
STRIP_W = 2 * CHUNK * LANES


def _s5_kernel(u_ref, bblk_ref, cblk_ref, apow_ref, d_ref, o_ref,
               win_s, wout_s, strip_s, x_s, y_s, p_s, st_s):
    nc = u_ref.shape[1] // CHUNK
    bb = u_ref.shape[0]

    @pl.when(pl.program_id(1) == 0)
    def _build_weights():
        strip_s[...] = jnp.zeros_like(strip_s)
        for dr in range(2):
            re_c = slice((2 * dr) * SGP, (2 * dr + 1) * SGP)
            im_c = slice((2 * dr + 1) * SGP, (2 * dr + 2) * SGP)
            b_re, b_im = bblk_ref[0, :, re_c], bblk_ref[0, :, im_c]
            c_re, c_im = cblk_ref[0, :, re_c], cblk_ref[0, :, im_c]
            c_cat = jnp.concatenate([c_re, -c_im], axis=1)
            for s in range(CHUNK):
                rows = slice(s * LANES, (s + 1) * LANES)
                k_in = CHUNK - 1 - s if dr == 0 else s
                a_re = apow_ref[0, dr, k_in:k_in + 1, :SGP]
                a_im = apow_ref[0, dr, k_in:k_in + 1, SGP:]
                w_re = b_re * a_re - b_im * a_im
                w_im = b_re * a_im + b_im * a_re
                win_s[rows, re_c] = w_re.astype(BF16)
                win_s[rows, im_c] = w_im.astype(BF16)
                lag = _dot_nt_f32(jnp.concatenate([w_re, w_im], axis=1), c_cat)
                if k_in == 0 and dr == 0:
                    lag0_fwd = lag
                else:
                    if k_in == 0:
                        lag = lag + lag0_fwd
                    j_top = CHUNK - 1 + (k_in if dr == 0 else -k_in)
                    lag_bf = lag.astype(BF16)
                    strip_s[:LANES, j_top * LANES:(j_top + 1) * LANES] = lag_bf
                    strip_s[LANES:, (j_top + 1) * LANES:(j_top + 2) * LANES] = lag_bf
                k_out = s + 1 if dr == 0 else CHUNK - s
                a_re = apow_ref[0, dr, k_out:k_out + 1, :SGP]
                a_im = apow_ref[0, dr, k_out:k_out + 1, SGP:]
                wout_s[rows, re_c] = (c_re * a_re - c_im * a_im).astype(BF16)
                wout_s[rows, im_c] = (-(c_re * a_im + c_im * a_re)).astype(BF16)

    for b in range(bb):
        for s in range(CHUNK):
            x_s[b * nc:(b + 1) * nc, s * LANES:(s + 1) * LANES] = (
                u_ref[b, pl.ds(s, nc, stride=CHUNK), :].astype(BF16))

    for s in range(0, CHUNK, 2):
        lo = (CHUNK - 1 - s) * LANES
        part = _dot(x_s[:, s * LANES:(s + 2) * LANES], strip_s[:, lo:lo + CW])
        if s == 0:
            y_s[...] = part
        else:
            y_s[...] += part

    for jj in range(NSTATE // (2 * LANES)):
        part = _dot(x_s[...], win_s[:, jj * 2 * LANES:(jj + 1) * 2 * LANES])
        p_s[2 * jj] = part[:, :LANES]
        p_s[2 * jj + 1] = part[:, LANES:]
    a16 = [[jnp.broadcast_to(apow_ref[0, dr, CHUNK:CHUNK + 1, t * LANES:(t + 1) * LANES],
                             (bb, LANES)) for t in range(2 * TPC)] for dr in range(2)]

    def scan(c, carry):
        rows = (pl.ds(c, bb, stride=nc), pl.ds(nc - 1 - c, bb, stride=nc))
        new = list(carry)
        for dr in range(2):
            for q in range(TPC):
                i_re, i_im = (2 * dr) * TPC + q, (2 * dr + 1) * TPC + q
                s_re, s_im = carry[i_re], carry[i_im]
                st_s[i_re, rows[dr], :] = s_re
                st_s[i_im, rows[dr], :] = s_im
                a_re, a_im = a16[dr][q], a16[dr][TPC + q]
                new[i_re] = a_re * s_re - a_im * s_im + p_s[i_re, rows[dr], :]
                new[i_im] = a_re * s_im + a_im * s_re + p_s[i_im, rows[dr], :]
        return tuple(new)

    zero = jnp.zeros((bb, LANES), F32)
    lax.fori_loop(0, nc, scan, (zero,) * (4 * TPC))

    for jj in range(NSTATE // (2 * LANES)):
        st = jnp.concatenate([st_s[2 * jj], st_s[2 * jj + 1]], axis=1).astype(BF16)
        y_s[...] += _dot_nt(st, wout_s[:, jj * 2 * LANES:(jj + 1) * 2 * LANES])

    d_row = d_ref[0]
    for b in range(bb):
        for s in range(CHUNK):
            rows = pl.ds(s, nc, stride=CHUNK)
            o_ref[b, rows, :] = (y_s[b * nc:(b + 1) * nc, s * LANES:(s + 1) * LANES]
                                 + d_row * u_ref[b, rows, :])


def _s5(proj3d, bblk, cblk, apow, d_diag):
    bsz, seq, _ = proj3d.shape
    rows = S5_BB * (seq // CHUNK)
    return pl.pallas_call(
        _s5_kernel,
        grid=(N_SG, bsz // S5_BB),
        in_specs=[
            pl.BlockSpec((S5_BB, seq, LANES), lambda g, b: (b, 0, COL_U // LANES + g)),
            pl.BlockSpec((1, LANES, NSTATE), lambda g, b: (g, 0, 0)),
            pl.BlockSpec((1, LANES, NSTATE), lambda g, b: (g, 0, 0)),
            pl.BlockSpec((1, 2, 24, 2 * SGP), lambda g, b: (g, 0, 0, 0)),
            pl.BlockSpec((1, 1, LANES), lambda g, b: (g, 0, 0)),
        ],
        out_specs=pl.BlockSpec((S5_BB, seq, LANES), lambda g, b: (b, 0, g)),
        out_shape=jax.ShapeDtypeStruct((bsz, seq, SSM_WIDTH), F32),
        scratch_shapes=[
            pltpu.VMEM((CW, NSTATE), BF16),
            pltpu.VMEM((CW, NSTATE), BF16),
            pltpu.VMEM((2 * LANES, STRIP_W), BF16),
            pltpu.VMEM((rows, CW), BF16),
            pltpu.VMEM((rows, CW), F32),
            pltpu.VMEM((NSTATE // LANES, rows, LANES), F32),
            pltpu.VMEM((NSTATE // LANES, rows, LANES), F32),
        ],
        compiler_params=pltpu.CompilerParams(
            dimension_semantics=("arbitrary", "arbitrary"), vmem_limit_bytes=VMEM_LIMIT),
        name="s5_chunked",
    )(proj3d, bblk, cblk, apow, d_diag)


def _s5_params(lam_re, lam_im, log_dt, b_re, b_im, c_re, c_im, d_skip):
    lr = jnp.minimum(lam_re, -1e-4)
    li = lam_im
    dt = jnp.exp(log_dt)[..., None]
    k = jnp.arange(24, dtype=F32)[None, :, None, None]
    mag = jnp.exp(k * (lr * dt)[:, None])
    ang = k * (li * dt)[:, None]
    pw_re, pw_im = mag * jnp.cos(ang), mag * jnp.sin(ang)
    apow = jnp.concatenate([
        pw_re.reshape(2, 24, N_SG, SGP), pw_im.reshape(2, 24, N_SG, SGP)], axis=-1)
    apow = apow.transpose(2, 0, 1, 3)

    er = jnp.exp(lr * dt)
    ab_re, ab_im = er * jnp.cos(li * dt), er * jnp.sin(li * dt)
    nr, ni = ab_re - 1.0, ab_im
    mg = lr * lr + li * li
    f_re = (nr * lr + ni * li) / mg
    f_im = (ni * lr - nr * li) / mg
    bb_re = f_re[..., None] * b_re - f_im[..., None] * b_im
    bb_im = f_re[..., None] * b_im + f_im[..., None] * b_re

    eye = jnp.eye(SG, dtype=F32)

    def blockdiag(t_gph):
        t = t_gph.reshape(2, N_SG, SG, SSM_STATE, SSM_GROUP)
        t = jnp.einsum('dqgph,gk->dqghkp', t, eye)
        return t.reshape(2, N_SG, LANES, SGP).transpose(1, 0, 2, 3)

    bblk = jnp.stack([blockdiag(bb_re), blockdiag(bb_im)], axis=2)
    bblk = bblk.transpose(0, 3, 1, 2, 4).reshape(N_SG, LANES, NSTATE)
    c_gph_re, c_gph_im = c_re.transpose(0, 1, 3, 2), c_im.transpose(0, 1, 3, 2)
    cblk = jnp.stack([blockdiag(c_gph_re), blockdiag(c_gph_im)], axis=2)
    cblk = cblk.transpose(0, 3, 1, 2, 4).reshape(N_SG, LANES, NSTATE)
    return bblk, cblk, apow, d_skip.reshape(N_SG, 1, LANES)


TAIL_TT = 512


def _tail_kernel(x_ref, ys_ref, za_ref, at_ref, zb_ref, g_ref, wg_ref, bg_ref, wps_ref, wpa_ref,
                 wo_ref, o_ref):
    ys = jax.nn.gelu(ys_ref[...]).astype(BF16)
    glu = _dot(ys, wg_ref[...]) + bg_ref[...]
    a_in = glu[:, :SSM_WIDTH] * jax.nn.sigmoid(glu[:, SSM_WIDTH:]) * jax.nn.silu(za_ref[...])
    y_a = _dot(a_in.astype(BF16), wps_ref[...])
    y_b = _dot((at_ref[...] * jax.nn.silu(zb_ref[...])).astype(BF16), wpa_ref[...])
    g = g_ref[...]
    mix = jax.nn.sigmoid(g[:, :D_MODEL]) * y_a + jax.nn.sigmoid(g[:, D_MODEL:]) * y_b
    o_ref[...] = x_ref[...] + _dot(mix.astype(BF16), wo_ref[...])


def _tail(x2d, ys2d, attn2d, proj2d, w_glu, b_glu, w_ps, w_pa, w_out):
    t = x2d.shape[0]
    row = lambda w, c: pl.BlockSpec((TAIL_TT, w), lambda i, c=c: (i, c))
    const = lambda shape: pl.BlockSpec(shape, lambda i: (0, 0))
    return pl.pallas_call(
        _tail_kernel,
        grid=(t // TAIL_TT,),
        in_specs=[
            row(D_MODEL, 0), row(SSM_WIDTH, 0),
            row(SSM_WIDTH, COL_ZA // SSM_WIDTH),
            row(ATTN_WIDTH, 0),
            row(ATTN_WIDTH, COL_ZB // ATTN_WIDTH),
            row(2 * D_MODEL, COL_G // (2 * D_MODEL)),
            const((SSM_WIDTH, 2 * SSM_WIDTH)), const((1, 2 * SSM_WIDTH)),
            const((SSM_WIDTH, D_MODEL)), const((ATTN_WIDTH, D_MODEL)), const((D_MODEL, D_MODEL)),
        ],
        out_specs=pl.BlockSpec((TAIL_TT, D_MODEL), lambda i: (i, 0)),
        out_shape=jax.ShapeDtypeStruct((t, D_MODEL), F32),
        compiler_params=pltpu.CompilerParams(
            dimension_semantics=("arbitrary",), vmem_limit_bytes=VMEM_LIMIT),
        name="tail",
    )(x2d, ys2d, proj2d, attn2d, proj2d, proj2d, w_glu, b_glu, w_ps, w_pa, w_out)


def _rope_tables(seq):
    half = ROPE_DIM // 2
    inv = ROPE_THETA ** (-jnp.arange(0, ROPE_DIM, 2, dtype=F32) / ROPE_DIM)
    ang = jnp.arange(seq, dtype=F32)[:, None] * inv[None, :]
    cos, sin = jnp.cos(ang), jnp.sin(ang)
    zeros = jnp.zeros((seq, HEAD_DIM - ROPE_DIM), F32)
    z8 = jnp.zeros((seq, half), F32)
    cos_h = jnp.concatenate([cos, cos, jnp.ones_like(zeros)], axis=1)
    sa_h = jnp.concatenate([-sin, z8, zeros], axis=1)
    sb_h = jnp.concatenate([z8, sin, zeros], axis=1)
    two = lambda t: jnp.concatenate([t, t], axis=1)
    return two(cos_h), two(sa_h), two(sb_h)


def kernel(x, norm_w, w_in, b_gate, q_norm_w, k_norm_w, ssm_lam_re, ssm_lam_im, ssm_log_dt,
           ssm_b_re, ssm_b_im, ssm_c_re, ssm_c_im, ssm_d, w_glu, b_glu,
           w_proj_ssm, w_proj_attn, w_out):
    bsz, seq, _ = x.shape
    depth = norm_w.shape[0]
    cosf, sa, sb = _rope_tables(seq)
    ones_blk = jnp.kron(jnp.eye(2, dtype=F32),
                        jnp.full((HEAD_DIM, HEAD_DIM), 1.0 / HEAD_DIM, F32)).astype(BF16)
    for layer in range(depth):
        x2d = x.reshape(bsz * seq, D_MODEL)
        bias_row = jnp.concatenate(
            [jnp.zeros((COL_G,), F32), b_gate[layer].astype(F32)])[None, :]
        proj2d = _in_proj(x2d, norm_w[layer][None, :].astype(F32), w_in[layer].astype(BF16),
                          bias_row)
        proj3d = proj2d.reshape(bsz, seq, IN_WIDTH)

        two = lambda t: jnp.concatenate([t, t])[None, :].astype(F32)
        attn = _attention(proj3d, cosf, sa, sb, two(q_norm_w[layer]), two(k_norm_w[layer]),
                          ones_blk)

        bblk, cblk, apow, d_diag = _s5_params(
            ssm_lam_re[layer].astype(F32), ssm_lam_im[layer].astype(F32),
            ssm_log_dt[layer].astype(F32), ssm_b_re[layer].astype(F32),
            ssm_b_im[layer].astype(F32), ssm_c_re[layer].astype(F32),
            ssm_c_im[layer].astype(F32), ssm_d[layer].astype(F32))
        y_s = _s5(proj3d, bblk, cblk, apow, d_diag)

        out2d = _tail(x2d, y_s.reshape(bsz * seq, SSM_WIDTH), attn.reshape(bsz * seq, ATTN_WIDTH),
                      proj2d, w_glu[layer].astype(BF16), b_glu[layer][None, :].astype(F32),
                      w_proj_ssm[layer].astype(BF16), w_proj_attn[layer].astype(BF16),
                      w_out[layer].astype(BF16))
        x = out2d.reshape(bsz, seq, D_MODEL)
    return x
```

```python
import functools
import math

import jax
import jax.numpy as jnp
import numpy as np
from jax import lax
from jax.experimental import pallas as pl
from jax.experimental.pallas import tpu as pltpu

F32 = jnp.float32
BF16 = jnp.bfloat16

D_MODEL = 1024
SSM_WIDTH = 512
SSM_GROUP = 16
SSM_GROUPS = 32
SSM_STATE = 64
HEAD_DIM = 64
ATTN_SLOTS = 8
ATTN_WIDTH = 512
DILATIONS = (1, 4, 16)
BAND_HALF = 64
ROPE_THETA = 500000.0
ROPE_DIM = 16
EPS = 1e-6
NEG_INF = -1e30
IN_WIDTH = 6144
COL_U, COL_ZA, COL_Q, COL_K, COL_V, COL_ZB, COL_G = 0, 512, 1024, 2560, 3072, 3584, 4096

LANES = 128
VMEM_LIMIT = 56 * 1024 * 1024

CHUNK = 16
SG = 8
N_SG = SSM_GROUPS // SG
SGP = SG * SSM_STATE
NSTATE = 4 * SGP
TPC = SGP // LANES
CW = CHUNK * LANES


def _dot(a, b):
    return jnp.dot(a, b, preferred_element_type=F32)


def _dot_nt(a, b):
    return lax.dot_general(a, b, (((1,), (1,)), ((), ())), preferred_element_type=F32)


def _dot_nt_f32(a, b):
    return lax.dot_general(a, b, (((1,), (1,)), ((), ())), preferred_element_type=F32,
                           precision=lax.Precision.HIGHEST)


IN_TT = 512
IN_TN = 512


def _in_proj_kernel(x_ref, nw_ref, w_ref, b_ref, o_ref):
    x = x_ref[...]
    var = jnp.mean(x * x, axis=-1, keepdims=True)
    h = (x * lax.rsqrt(var + EPS) * nw_ref[...]).astype(BF16)
    for j in range(IN_WIDTH // IN_TN):
        cols = slice(j * IN_TN, (j + 1) * IN_TN)
        o_ref[:, cols] = _dot(h, w_ref[:, cols]) + b_ref[:, cols]


def _in_proj(x2d, norm_w, w_in_bf16, bias_row):
    t = x2d.shape[0]
    return pl.pallas_call(
        _in_proj_kernel,
        grid=(t // IN_TT,),
        in_specs=[
            pl.BlockSpec((IN_TT, D_MODEL), lambda i: (i, 0)),
            pl.BlockSpec((1, D_MODEL), lambda i: (0, 0)),
            pl.BlockSpec((D_MODEL, IN_WIDTH), lambda i: (0, 0), pipeline_mode=pl.Buffered(1)),
            pl.BlockSpec((1, IN_WIDTH), lambda i: (0, 0)),
        ],
        out_specs=pl.BlockSpec((IN_TT, IN_WIDTH), lambda i: (i, 0)),
        out_shape=jax.ShapeDtypeStruct((t, IN_WIDTH), F32),
        compiler_params=pltpu.CompilerParams(
            dimension_semantics=("arbitrary",), vmem_limit_bytes=VMEM_LIMIT),
        name="in_proj",
    )(x2d, norm_w, w_in_bf16, bias_row)


QBLK = 128
KWIN = QBLK + 2 * BAND_HALF
NORM_ROWS = 256
LOG2E = math.log2(math.e)


def _attn_kernel(q0_ref, q1_ref, q2_ref, k_ref, v_ref, cos_ref, sa_ref, sb_ref,
                 qw_ref, kw_ref, ones_ref, bias_ref, o_ref,
                 qn_s, kn_s, va_s, vb_s, oacc_s, den_s, max_s, s_s):
    seq = k_ref.shape[1]
    ones_blk = ones_ref[...]
    head0 = lax.broadcasted_iota(jnp.int32, (QBLK, LANES), 1) < HEAD_DIM
    head0_rows = lax.broadcasted_iota(jnp.int32, (NORM_ROWS, LANES), 1) < HEAD_DIM

    def norm_rope(x, w, rows):
        x2 = x * x
        hi = x2.astype(BF16)
        lo = (x2 - hi.astype(F32)).astype(BF16)
        ms = _dot(hi, ones_blk) + _dot(lo, ones_blk)
        xn = x * lax.rsqrt(ms + EPS) * w
        return (xn * cos_ref[rows, :]
                + pltpu.roll(xn, LANES - ROPE_DIM // 2, 1) * sa_ref[rows, :]
                + pltpu.roll(xn, ROPE_DIM // 2, 1) * sb_ref[rows, :])

    def prep(i, carry):
        rows = pl.ds(pl.multiple_of(i * NORM_ROWS, NORM_ROWS), NORM_ROWS)
        kn_s[rows, :] = norm_rope(k_ref[0, rows, :], kw_ref[...], rows)
        for p, q_ref in enumerate((q0_ref, q1_ref, q2_ref)):
            qn_s[p, rows, :] = (norm_rope(q_ref[0, rows, :], qw_ref[...], rows)
                                * (LOG2E * HEAD_DIM ** -0.5))
        v = v_ref[0, rows, :]
        va_s[rows, :] = jnp.where(head0_rows, v, 1.0)
        vb_s[rows, :] = jnp.where(head0_rows, 1.0, v)
        return carry

    lax.fori_loop(0, seq // NORM_ROWS, prep, 0)

    for p, d in enumerate(DILATIONS):
        n = seq // d
        nblk = n // QBLK
        kw = min(n, KWIN)

        def rows_of(idx, d=d, n=n, nblk=nblk, kw=kw):
            r = idx // nblk
            q0 = (idx % nblk) * QBLK
            ks = jnp.clip(q0 - BAND_HALF, 0, n - kw)
            return (pl.ds(r + d * q0, QBLK, stride=d), pl.ds(r + d * ks, kw, stride=d),
                    (q0 - ks) // BAND_HALF)

        def scores(idx, slot, p=p, kw=kw):
            qrows, krows, case = rows_of(idx)
            qb = qn_s[p, qrows, :]
            kb = kn_s[krows, :].astype(BF16)
            bias = bias_ref[case, :, :kw]
            s_s[slot, :QBLK, :kw] = _dot_nt(jnp.where(head0, qb, 0.0).astype(BF16), kb) + bias
            s_s[slot, QBLK:, :kw] = _dot_nt(jnp.where(head0, 0.0, qb).astype(BF16), kb) + bias

        def weigh(idx, slot, p=p, kw=kw):
            qrows, krows, _ = rows_of(idx)
            s = s_s[slot, :, :kw]
            m = jnp.max(s, axis=-1, keepdims=True)
            e = jnp.exp2(s - m).astype(BF16)
            o_a = _dot(e[:QBLK], va_s[krows, :].astype(BF16))
            o_b = _dot(e[QBLK:], vb_s[krows, :].astype(BF16))
            oacc_s[p, qrows, :] = jnp.where(head0, o_a, o_b)
            den_s[p, qrows, :] = jnp.where(head0, o_b, o_a)
            max_s[p, qrows, :] = jnp.where(head0, jnp.broadcast_to(m[:QBLK], (QBLK, LANES)),
                                           jnp.broadcast_to(m[QBLK:], (QBLK, LANES)))

        n_blocks = d * nblk
        scores(0, 0)

        def pair(j, carry):
            scores(2 * j + 1, 1)
            weigh(2 * j, 0)
            scores(2 * j + 2, 0)
            weigh(2 * j + 1, 1)
            return carry

        lax.fori_loop(0, n_blocks // 2 - 1, pair, 0)
        scores(n_blocks - 1, 1)
        weigh(n_blocks - 2, 0)
        weigh(n_blocks - 1, 1)

    def combine(i, carry):
        rows = pl.ds(pl.multiple_of(i * NORM_ROWS, NORM_ROWS), NORM_ROWS)
        m0, m1, m2 = max_s[0, rows, :], max_s[1, rows, :], max_s[2, rows, :]
        m = jnp.maximum(jnp.maximum(m0, m1), m2)
        w0, w1, w2 = jnp.exp2(m0 - m), jnp.exp2(m1 - m), jnp.exp2(m2 - m)
        num = w0 * oacc_s[0, rows, :] + w1 * oacc_s[1, rows, :] + w2 * oacc_s[2, rows, :]
        d0, d1, d2 = (pltpu.roll(den_s[p, rows, :], HEAD_DIM, 1) for p in range(3))
        o_ref[0, rows, :] = num / (w0 * d0 + w1 * d1 + w2 * d2)
        return carry

    lax.fori_loop(0, seq // NORM_ROWS, combine, 0)


def _band_bias():
    i = jnp.arange(QBLK)[:, None]
    j = jnp.arange(KWIN)[None, :]
    off = jnp.arange(3)[:, None, None] * BAND_HALF
    return jnp.where(jnp.abs(j - i - off) <= BAND_HALF, 0.0, NEG_INF).astype(F32)


def _attention(proj3d, cosf, sa, sb, qw_row, kw_row, ones_blk):
    bsz, seq, _ = proj3d.shape
    n_pairs = ATTN_WIDTH // LANES

    def qspec(p):
        return pl.BlockSpec((1, seq, LANES),
                            lambda b, hp, p=p: (b, 0, COL_Q // LANES + p * n_pairs + hp))

    const2d = lambda shape: pl.BlockSpec(shape, lambda b, hp: (0, 0))
    seq_tile = lambda: pltpu.VMEM((seq, LANES), F32)
    pat_tile = lambda: pltpu.VMEM((len(DILATIONS), seq, LANES), F32)
    return pl.pallas_call(
        _attn_kernel,
        grid=(bsz, n_pairs),
        in_specs=[
            qspec(0), qspec(1), qspec(2),
            pl.BlockSpec((1, seq, LANES), lambda b, hp: (b, 0, COL_K // LANES + hp)),
            pl.BlockSpec((1, seq, LANES), lambda b, hp: (b, 0, COL_V // LANES + hp)),
            const2d((seq, LANES)), const2d((seq, LANES)), const2d((seq, LANES)),
            const2d((1, LANES)), const2d((1, LANES)), const2d((LANES, LANES)),
            pl.BlockSpec((3, QBLK, KWIN), lambda b, hp: (0, 0, 0)),
        ],
        out_specs=pl.BlockSpec((1, seq, LANES), lambda b, hp: (b, 0, hp)),
        out_shape=jax.ShapeDtypeStruct((bsz, seq, ATTN_WIDTH), F32),
        scratch_shapes=[
            pat_tile(),
            seq_tile(),
            seq_tile(), seq_tile(),
            pat_tile(), pat_tile(), pat_tile(),
            pltpu.VMEM((2, 2 * QBLK, KWIN), F32),
        ],
        compiler_params=pltpu.CompilerParams(
            dimension_semantics=("arbitrary", "arbitrary"), vmem_limit_bytes=VMEM_LIMIT),
        name="dilated_attention",
    )(proj3d, proj3d, proj3d, proj3d, proj3d, cosf, sa, sb, qw_row, kw_row, ones_blk,
      _band_bias())


S5_BB = 4
STRIP_W = 2 * CHUNK * LANES


def _s5_kernel(u_ref, bblk_ref, cblk_ref, apow_ref, d_ref, o_ref,
               win_s, wout_s, strip_s, x_s, y_s, p_s, st_s):
    nc = u_ref.shape[1] // CHUNK
    bb = u_ref.shape[0]

    @pl.when(pl.program_id(1) == 0)
    def _build_weights():
        strip_s[...] = jnp.zeros_like(strip_s)
        for dr in range(2):
            re_c = slice((2 * dr) * SGP, (2 * dr + 1) * SGP)
            im_c = slice((2 * dr + 1) * SGP, (2 * dr + 2) * SGP)
            b_re, b_im = bblk_ref[0, :, re_c], bblk_ref[0, :, im_c]
            c_re, c_im = cblk_ref[0, :, re_c], cblk_ref[0, :, im_c]
            c_cat = jnp.concatenate([c_re, -c_im], axis=1)
            for s in range(CHUNK):
                rows = slice(s * LANES, (s + 1) * LANES)
                k_in = CHUNK - 1 - s if dr == 0 else s
                a_re = apow_ref[0, dr, k_in:k_in + 1, :SGP]
                a_im = apow_ref[0, dr, k_in:k_in + 1, SGP:]
                w_re = b_re * a_re - b_im * a_im
                w_im = b_re * a_im + b_im * a_re
                win_s[rows, re_c] = w_re.astype(BF16)
                win_s[rows, im_c] = w_im.astype(BF16)
                lag = _dot_nt_f32(jnp.concatenate([w_re, w_im], axis=1), c_cat)
                if k_in == 0 and dr == 0:
                    lag0_fwd = lag
                else:
                    if k_in == 0:
                        lag = lag + lag0_fwd
                    j_top = CHUNK - 1 + (k_in if dr == 0 else -k_in)
                    lag_bf = lag.astype(BF16)
                    strip_s[:LANES, j_top * LANES:(j_top + 1) * LANES] = lag_bf
                    strip_s[LANES:, (j_top + 1) * LANES:(j_top + 2) * LANES] = lag_bf
                k_out = s + 1 if dr == 0 else CHUNK - s
                a_re = apow_ref[0, dr, k_out:k_out + 1, :SGP]
                a_im = apow_ref[0, dr, k_out:k_out + 1, SGP:]
                wout_s[rows, re_c] = (c_re * a_re - c_im * a_im).astype(BF16)
                wout_s[rows, im_c] = (-(c_re * a_im + c_im * a_re)).astype(BF16)

    for b in range(bb):
        for s in range(CHUNK):
            x_s[b * nc:(b + 1) * nc, s * LANES:(s + 1) * LANES] = (
                u_ref[b, pl.ds(s, nc, stride=CHUNK), :].astype(BF16))

    for s in range(0, CHUNK, 2):
        lo = (CHUNK - 1 - s) * LANES
        part = _dot(x_s[:, s * LANES:(s + 2) * LANES], strip_s[:, lo:lo + CW])
        if s == 0:
            y_s[...] = part
        else:
            y_s[...] += part

    for jj in range(NSTATE // (2 * LANES)):
        part = _dot(x_s[...], win_s[:, jj * 2 * LANES:(jj + 1) * 2 * LANES])
        p_s[2 * jj] = part[:, :LANES]
        p_s[2 * jj + 1] = part[:, LANES:]
    a16 = [[jnp.broadcast_to(apow_ref[0, dr, CHUNK:CHUNK + 1, t * LANES:(t + 1) * LANES],
                             (bb, LANES)) for t in range(2 * TPC)] for dr in range(2)]

    def scan(c, carry):
        rows = (pl.ds(c, bb, stride=nc), pl.ds(nc - 1 - c, bb, stride=nc))
        new = list(carry)
        for dr in range(2):
            for q in range(TPC):
                i_re, i_im = (2 * dr) * TPC + q, (2 * dr + 1) * TPC + q
                s_re, s_im = carry[i_re], carry[i_im]
                st_s[i_re, rows[dr], :] = s_re
                st_s[i_im, rows[dr], :] = s_im
                a_re, a_im = a16[dr][q], a16[dr][TPC + q]
                new[i_re] = a_re * s_re - a_im * s_im + p_s[i_re, rows[dr], :]
                new[i_im] = a_re * s_im + a_im * s_re + p_s[i_im, rows[dr], :]
        return tuple(new)

    zero = jnp.zeros((bb, LANES), F32)
    lax.fori_loop(0, nc, scan, (zero,) * (4 * TPC))

    for jj in range(NSTATE // (2 * LANES)):
        st = jnp.concatenate([st_s[2 * jj], st_s[2 * jj + 1]], axis=1).astype(BF16)
        y_s[...] += _dot_nt(st, wout_s[:, jj * 2 * LANES:(jj + 1) * 2 * LANES])

    d_row = d_ref[0]
    for b in range(bb):
        for s in range(CHUNK):
            rows = pl.ds(s, nc, stride=CHUNK)
            o_ref[b, rows, :] = (y_s[b * nc:(b + 1) * nc, s * LANES:(s + 1) * LANES]
                                 + d_row * u_ref[b, rows, :])


def _s5(proj3d, bblk, cblk, apow, d_diag):
    bsz, seq, _ = proj3d.shape
    rows = S5_BB * (seq // CHUNK)
    return pl.pallas_call(
        _s5_kernel,
        grid=(N_SG, bsz // S5_BB),
        in_specs=[
            pl.BlockSpec((S5_BB, seq, LANES), lambda g, b: (b, 0, COL_U // LANES + g)),
            pl.BlockSpec((1, LANES, NSTATE), lambda g, b: (g, 0, 0)),
            pl.BlockSpec((1, LANES, NSTATE), lambda g, b: (g, 0, 0)),
            pl.BlockSpec((1, 2, 24, 2 * SGP), lambda g, b: (g, 0, 0, 0)),
            pl.BlockSpec((1, 1, LANES), lambda g, b: (g, 0, 0)),
        ],
        out_specs=pl.BlockSpec((S5_BB, seq, LANES), lambda g, b: (b, 0, g)),
        out_shape=jax.ShapeDtypeStruct((bsz, seq, SSM_WIDTH), F32),
        scratch_shapes=[
            pltpu.VMEM((CW, NSTATE), BF16),
            pltpu.VMEM((CW, NSTATE), BF16),
            pltpu.VMEM((2 * LANES, STRIP_W), BF16),
            pltpu.VMEM((rows, CW), BF16),
            pltpu.VMEM((rows, CW), F32),
            pltpu.VMEM((NSTATE // LANES, rows, LANES), F32),
            pltpu.VMEM((NSTATE // LANES, rows, LANES), F32),
        ],
        compiler_params=pltpu.CompilerParams(
            dimension_semantics=("arbitrary", "arbitrary"), vmem_limit_bytes=VMEM_LIMIT),
        name="s5_chunked",
    )(proj3d, bblk, cblk, apow, d_diag)


def _s5_params(lam_re, lam_im, log_dt, b_re, b_im, c_re, c_im, d_skip):
    lr = jnp.minimum(lam_re, -1e-4)
    li = lam_im
    dt = jnp.exp(log_dt)[..., None]
    k = jnp.arange(24, dtype=F32)[None, :, None, None]
    mag = jnp.exp(k * (lr * dt)[:, None])
    ang = k * (li * dt)[:, None]
    pw_re, pw_im = mag * jnp.cos(ang), mag * jnp.sin(ang)
    apow = jnp.concatenate([
        pw_re.reshape(2, 24, N_SG, SGP), pw_im.reshape(2, 24, N_SG, SGP)], axis=-1)
    apow = apow.transpose(2, 0, 1, 3)

    er = jnp.exp(lr * dt)
    ab_re, ab_im = er * jnp.cos(li * dt), er * jnp.sin(li * dt)
    nr, ni = ab_re - 1.0, ab_im
    mg = lr * lr + li * li
    f_re = (nr * lr + ni * li) / mg
    f_im = (ni * lr - nr * li) / mg
    bb_re = f_re[..., None] * b_re - f_im[..., None] * b_im
    bb_im = f_re[..., None] * b_im + f_im[..., None] * b_re

    eye = jnp.eye(SG, dtype=F32)

    def blockdiag(t_gph):
        t = t_gph.reshape(2, N_SG, SG, SSM_STATE, SSM_GROUP)
        t = jnp.einsum('dqgph,gk->dqghkp', t, eye)
        return t.reshape(2, N_SG, LANES, SGP).transpose(1, 0, 2, 3)

    bblk = jnp.stack([blockdiag(bb_re), blockdiag(bb_im)], axis=2)
    bblk = bblk.transpose(0, 3, 1, 2, 4).reshape(N_SG, LANES, NSTATE)
    c_gph_re, c_gph_im = c_re.transpose(0, 1, 3, 2), c_im.transpose(0, 1, 3, 2)
    cblk = jnp.stack([blockdiag(c_gph_re), blockdiag(c_gph_im)], axis=2)
    cblk = cblk.transpose(0, 3, 1, 2, 4).reshape(N_SG, LANES, NSTATE)
    return bblk, cblk, apow, d_skip.reshape(N_SG, 1, LANES)


TAIL_TT = 512


def _tail_kernel(x_ref, ys_ref, za_ref, at_ref, zb_ref, g_ref, wg_ref, bg_ref, wps_ref, wpa_ref,
                 wo_ref, o_ref):
    ys = jax.nn.gelu(ys_ref[...]).astype(BF16)
    glu = _dot(ys, wg_ref[...]) + bg_ref[...]
    a_in = glu[:, :SSM_WIDTH] * jax.nn.sigmoid(glu[:, SSM_WIDTH:]) * jax.nn.silu(za_ref[...])
    y_a = _dot(a_in.astype(BF16), wps_ref[...])
    y_b = _dot((at_ref[...] * jax.nn.silu(zb_ref[...])).astype(BF16), wpa_ref[...])
    g = g_ref[...]
    mix = jax.nn.sigmoid(g[:, :D_MODEL]) * y_a + jax.nn.sigmoid(g[:, D_MODEL:]) * y_b
    o_ref[...] = x_ref[...] + _dot(mix.astype(BF16), wo_ref[...])


def _tail(x2d, ys2d, attn2d, proj2d, w_glu, b_glu, w_ps, w_pa, w_out):
    t = x2d.shape[0]
    row = lambda w, c: pl.BlockSpec((TAIL_TT, w), lambda i, c=c: (i, c))
    const = lambda shape: pl.BlockSpec(shape, lambda i: (0, 0))
    return pl.pallas_call(
        _tail_kernel,
        grid=(t // TAIL_TT,),
        in_specs=[
            row(D_MODEL, 0), row(SSM_WIDTH, 0),
            row(SSM_WIDTH, COL_ZA // SSM_WIDTH),
            row(ATTN_WIDTH, 0),
            row(ATTN_WIDTH, COL_ZB // ATTN_WIDTH),
            row(2 * D_MODEL, COL_G // (2 * D_MODEL)),
            const((SSM_WIDTH, 2 * SSM_WIDTH)), const((1, 2 * SSM_WIDTH)),
            const((SSM_WIDTH, D_MODEL)), const((ATTN_WIDTH, D_MODEL)), const((D_MODEL, D_MODEL)),
        ],
        out_specs=pl.BlockSpec((TAIL_TT, D_MODEL), lambda i: (i, 0)),
        out_shape=jax.ShapeDtypeStruct((t, D_MODEL), F32),
        compiler_params=pltpu.CompilerParams(
            dimension_semantics=("arbitrary",), vmem_limit_bytes=VMEM_LIMIT),
        name="tail",
    )(x2d, ys2d, proj2d, attn2d, proj2d, proj2d, w_glu, b_glu, w_ps, w_pa, w_out)


def _rope_tables(seq):
    half = ROPE_DIM // 2
    inv = ROPE_THETA ** (-jnp.arange(0, ROPE_DIM, 2, dtype=F32) / ROPE_DIM)
    ang = jnp.arange(seq, dtype=F32)[:, None] * inv[None, :]
    cos, sin = jnp.cos(ang), jnp.sin(ang)
    zeros = jnp.zeros((seq, HEAD_DIM - ROPE_DIM), F32)
    z8 = jnp.zeros((seq, half), F32)
    cos_h = jnp.concatenate([cos, cos, jnp.ones_like(zeros)], axis=1)
    sa_h = jnp.concatenate([-sin, z8, zeros], axis=1)
    sb_h = jnp.concatenate([z8, sin, zeros], axis=1)
    two = lambda t: jnp.concatenate([t, t], axis=1)
    return two(cos_h), two(sa_h), two(sb_h)


def kernel(x, norm_w, w_in, b_gate, q_norm_w, k_norm_w, ssm_lam_re, ssm_lam_im, ssm_log_dt,
           ssm_b_re, ssm_b_im, ssm_c_re, ssm_c_im, ssm_d, w_glu, b_glu,
           w_proj_ssm, w_proj_attn, w_out):
    bsz, seq, _ = x.shape
    depth = norm_w.shape[0]
    cosf, sa, sb = _rope_tables(seq)
    ones_blk = jnp.kron(jnp.eye(2, dtype=F32),
                        jnp.full((HEAD_DIM, HEAD_DIM), 1.0 / HEAD_DIM, F32)).astype(BF16)
    for layer in range(depth):
        x2d = x.reshape(bsz * seq, D_MODEL)
        bias_row = jnp.concatenate(
            [jnp.zeros((COL_G,), F32), b_gate[layer].astype(F32)])[None, :]
        proj2d = _in_proj(x2d, norm_w[layer][None, :].astype(F32), w_in[layer].astype(BF16),
                          bias_row)
        proj3d = proj2d.reshape(bsz, seq, IN_WIDTH)

        two = lambda t: jnp.concatenate([t, t])[None, :].astype(F32)
        attn = _attention(proj3d, cosf, sa, sb, two(q_norm_w[layer]), two(k_norm_w[layer]),
                          ones_blk)

        bblk, cblk, apow, d_diag = _s5_params(
            ssm_lam_re[layer].astype(F32), ssm_lam_im[layer].astype(F32),
            ssm_log_dt[layer].astype(F32), ssm_b_re[layer].astype(F32),
            ssm_b_im[layer].astype(F32), ssm_c_re[layer].astype(F32),
            ssm_c_im[layer].astype(F32), ssm_d[layer].astype(F32))
        y_s = _s5(proj3d, bblk, cblk, apow, d_diag)

        out2d = _tail(x2d, y_s.reshape(bsz * seq, SSM_WIDTH), attn.reshape(bsz * seq, ATTN_WIDTH),
                      proj2d, w_glu[layer].astype(BF16), b_glu[layer][None, :].astype(F32),
                      w_proj_ssm[layer].astype(BF16), w_proj_attn[layer].astype(BF16),
                      w_out[layer].astype(BF16))
        x = out2d.reshape(bsz, seq, D_MODEL)
    return x
```

```python
import functools
import math

import jax
import jax.numpy as jnp
import numpy as np
from jax import lax
from jax.experimental import pallas as pl
from jax.experimental.pallas import tpu as pltpu

F32 = jnp.float32
BF16 = jnp.bfloat16

D_MODEL = 1024
SSM_WIDTH = 512
SSM_GROUP = 16
SSM_GROUPS = 32
SSM_STATE = 64
HEAD_DIM = 64
ATTN_SLOTS = 8
ATTN_WIDTH = 512
DILATIONS = (1, 4, 16)
BAND_HALF = 64
ROPE_THETA = 500000.0
ROPE_DIM = 16
EPS = 1e-6
NEG_INF = -1e30
IN_WIDTH = 6144
COL_U, COL_ZA, COL_Q, COL_K, COL_V, COL_ZB, COL_G = 0, 512, 1024, 2560, 3072, 3584, 4096

LANES = 128
VMEM_LIMIT = 56 * 1024 * 1024

CHUNK = 16
SG = 8
N_SG = SSM_GROUPS // SG
SGP = SG * SSM_STATE
NSTATE = 4 * SGP
TPC = SGP // LANES
CW = CHUNK * LANES


def _dot(a, b):
    return jnp.dot(a, b, preferred_element_type=F32)


def _dot_nt(a, b):
    return lax.dot_general(a, b, (((1,), (1,)), ((), ())), preferred_element_type=F32)


def _dot_nt_f32(a, b):
    return lax.dot_general(a, b, (((1,), (1,)), ((), ())), preferred_element_type=F32,
                           precision=lax.Precision.HIGHEST)


IN_TT = 512
IN_TN = 512
QK_ROWS = 256
LOG2E = math.log2(math.e)


def _in_proj_kernel(x_ref, nw_ref, w_ref, b_ref, cos_ref, sa_ref, sb_ref, qkw_ref, ones_ref,
                    o_ref):
    x = x_ref[...]
    var = jnp.mean(x * x, axis=-1, keepdims=True)
    h = (x * lax.rsqrt(var + EPS) * nw_ref[...]).astype(BF16)
    ones_blk = ones_ref[...]
    for j in range(IN_WIDTH // IN_TN):
        c0 = j * IN_TN
        acc = _dot(h, w_ref[:, c0:c0 + IN_TN])
        if c0 >= COL_G:
            o_ref[:, c0:c0 + IN_TN] = acc + b_ref[:, c0 - COL_G:c0 - COL_G + IN_TN]
        elif COL_Q <= c0 < COL_V:
            for r0 in range(0, IN_TT, QK_ROWS):
                rows = slice(r0, r0 + QK_ROWS)
                for t2 in range(0, IN_TN, 2 * LANES):
                    xq2 = acc[rows, t2:t2 + 2 * LANES]
                    ms2 = _dot((xq2 * xq2).astype(BF16), ones_blk)
                    xn2 = xq2 * lax.rsqrt(ms2 + EPS) * qkw_ref[:, c0 - COL_Q + t2:
                                                                c0 - COL_Q + t2 + 2 * LANES]
                    for t in (0, LANES):
                        xn = xn2[:, t:t + LANES]
                        lanes = slice(c0 + t2 + t, c0 + t2 + t + LANES)
                        o_ref[rows, lanes] = (
                            xn * cos_ref[rows, :]
                            + pltpu.roll(xn, LANES - ROPE_DIM // 2, 1) * sa_ref[rows, :]
                            + pltpu.roll(xn, ROPE_DIM // 2, 1) * sb_ref[rows, :])
        else:
            o_ref[:, c0:c0 + IN_TN] = acc


def _in_proj(x2d, norm_w, w_in_bf16, b_gate_row, cosf, sa, sb, qk_w_row, ones_blk, seq):
    t = x2d.shape[0]
    tiles_per_seq = seq // IN_TT
    const = lambda shape: pl.BlockSpec(shape, lambda i: (0, 0))
    rope = lambda: pl.BlockSpec((IN_TT, LANES), lambda i: (i % tiles_per_seq, 0))
    return pl.pallas_call(
        _in_proj_kernel,
        grid=(t // IN_TT,),
        in_specs=[
            pl.BlockSpec((IN_TT, D_MODEL), lambda i: (i, 0)),
            const((1, D_MODEL)),
            pl.BlockSpec((D_MODEL, IN_WIDTH), lambda i: (0, 0), pipeline_mode=pl.Buffered(1)),
            const((1, IN_WIDTH - COL_G)),
            rope(), rope(), rope(),
            const((1, COL_V - COL_Q)),
            const((2 * LANES, 2 * LANES)),
        ],
        out_specs=pl.BlockSpec((IN_TT, IN_WIDTH), lambda i: (i, 0)),
        out_shape=jax.ShapeDtypeStruct((t, IN_WIDTH), F32),
        compiler_params=pltpu.CompilerParams(
            dimension_semantics=("arbitrary",), vmem_limit_bytes=VMEM_LIMIT),
        name="in_proj",
    )(x2d, norm_w, w_in_bf16, b_gate_row, cosf, sa, sb, qk_w_row, ones_blk)


QBLK = 128
KWIN = QBLK + 2 * BAND_HALF
NORM_ROWS = 256


def _attn_kernel(q0_ref, q1_ref, q2_ref, k_ref, v_ref, bias_ref, o_ref,
                 va_s, vb_s, oacc_s, den_s, max_s, s_s):
    seq = k_ref.shape[1]
    q_refs = (q0_ref, q1_ref, q2_ref)
    head0 = lax.broadcasted_iota(jnp.int32, (QBLK, LANES), 1) < HEAD_DIM
    head0_rows = lax.broadcasted_iota(jnp.int32, (NORM_ROWS, LANES), 1) < HEAD_DIM

    def prep(i, carry):
        rows = pl.ds(pl.multiple_of(i * NORM_ROWS, NORM_ROWS), NORM_ROWS)
        v = v_ref[0, rows, :]
        va_s[rows, :] = jnp.where(head0_rows, v, 1.0)
        vb_s[rows, :] = jnp.where(head0_rows, 1.0, v)
        return carry

    lax.fori_loop(0, seq // NORM_ROWS, prep, 0)

    for p, d in enumerate(DILATIONS):
        n = seq // d
        nblk = n // QBLK
        kw = min(n, KWIN)

        def rows_of(idx, d=d, n=n, nblk=nblk, kw=kw):
            r = idx // nblk
            q0 = (idx % nblk) * QBLK
            ks = jnp.clip(q0 - BAND_HALF, 0, n - kw)
            return (pl.ds(r + d * q0, QBLK, stride=d), pl.ds(r + d * ks, kw, stride=d),
                    (q0 - ks) // BAND_HALF)

        def scores(idx, slot, p=p, kw=kw):
            qrows, krows, case = rows_of(idx)
            qb = q_refs[p][0, qrows, :]
            kb = k_ref[0, krows, :].astype(BF16)
            bias = bias_ref[case, :, :kw]
            s_s[slot, :QBLK, :kw] = _dot_nt(jnp.where(head0, qb, 0.0).astype(BF16), kb) + bias
            s_s[slot, QBLK:, :kw] = _dot_nt(jnp.where(head0, 0.0, qb).astype(BF16), kb) + bias

        def weigh(idx, slot, p=p, kw=kw):
            qrows, krows, _ = rows_of(idx)
            s = s_s[slot, :, :kw]
            m = jnp.max(s, axis=-1, keepdims=True)
            e = jnp.exp2(s - m).astype(BF16)
            o_a = _dot(e[:QBLK], va_s[krows, :].astype(BF16))
            o_b = _dot(e[QBLK:], vb_s[krows, :].astype(BF16))
            oacc_s[p, qrows, :] = jnp.where(head0, o_a, o_b)
            den_s[p, qrows, :] = jnp.where(head0, o_b, o_a)
            max_s[p, qrows, :] = jnp.where(head0, jnp.broadcast_to(m[:QBLK], (QBLK, LANES)),
                                           jnp.broadcast_to(m[QBLK:], (QBLK, LANES)))

        def pair(i0, cur, ahead=True):
            nxt = 2 - cur
            if ahead:
                scores(i0 + 2, nxt)
            weigh(i0, cur)
            if ahead:
                scores(i0 + 3, nxt + 1)
            weigh(i0 + 1, cur + 1)

        def quad(j, carry):
            pair(4 * j, 0)
            pair(4 * j + 2, 2)
            return carry

        n_quads = d * nblk // 4
        scores(0, 0)
        scores(1, 1)
        lax.fori_loop(0, n_quads - 1, quad, 0)
        pair(4 * n_quads - 4, 0)
        pair(4 * n_quads - 2, 2, ahead=False)

    def combine(i, carry):
        rows = pl.ds(pl.multiple_of(i * NORM_ROWS, NORM_ROWS), NORM_ROWS)
        m0, m1, m2 = max_s[0, rows, :], max_s[1, rows, :], max_s[2, rows, :]
        m = jnp.maximum(jnp.maximum(m0, m1), m2)
        w0, w1, w2 = jnp.exp2(m0 - m), jnp.exp2(m1 - m), jnp.exp2(m2 - m)
        num = w0 * oacc_s[0, rows, :] + w1 * oacc_s[1, rows, :] + w2 * oacc_s[2, rows, :]
        d0, d1, d2 = (pltpu.roll(den_s[p, rows, :], HEAD_DIM, 1) for p in range(3))
        o_ref[0, rows, :] = num / (w0 * d0 + w1 * d1 + w2 * d2)
        return carry

    lax.fori_loop(0, seq // NORM_ROWS, combine, 0)


def _band_bias():
    i = jnp.arange(QBLK)[:, None]
    j = jnp.arange(KWIN)[None, :]
    off = jnp.arange(3)[:, None, None] * BAND_HALF
    return jnp.where(jnp.abs(j - i - off) <= BAND_HALF, 0.0, NEG_INF).astype(F32)


def _attention(proj3d):
    bsz, seq, _ = proj3d.shape
    n_pairs = ATTN_WIDTH // LANES

    def qspec(p):
        return pl.BlockSpec((1, seq, LANES),
                            lambda b, hp, p=p: (b, 0, COL_Q // LANES + p * n_pairs + hp))

    seq_tile = lambda: pltpu.VMEM((seq, LANES), F32)
    pat_tile = lambda: pltpu.VMEM((len(DILATIONS), seq, LANES), F32)
    return pl.pallas_call(
        _attn_kernel,
        grid=(bsz, n_pairs),
        in_specs=[
            qspec(0), qspec(1), qspec(2),
            pl.BlockSpec((1, seq, LANES), lambda b, hp: (b, 0, COL_K // LANES + hp)),
            pl.BlockSpec((1, seq, LANES), lambda b, hp: (b, 0, COL_V // LANES + hp)),
            pl.BlockSpec((3, QBLK, KWIN), lambda b, hp: (0, 0, 0)),
        ],
        out_specs=pl.BlockSpec((1, seq, LANES), lambda b, hp: (b, 0, hp)),
        out_shape=jax.ShapeDtypeStruct((bsz, seq, ATTN_WIDTH), F32),
        scratch_shapes=[
            seq_tile(), seq_tile(),
            pat_tile(), pat_tile(), pat_tile(),
            pltpu.VMEM((4, 2 * QBLK, KWIN), F32),
        ],
        compiler_params=pltpu.CompilerParams(
            dimension_semantics=("arbitrary", "arbitrary"), vmem_limit_bytes=VMEM_LIMIT),
        name="dilated_attention",
    )(proj3d, proj3d, proj3d, proj3d, proj3d, _band_bias())


S5_BB = 4
STRIP_W = 2 * CHUNK * LANES


def _s5_kernel(u_ref, bblk_ref, cblk_ref, apow_ref, d_ref, o_ref,
               win_s, wout_s, strip_s, x_s, y_s, p_s, st_s):
    nc = u_ref.shape[1] // CHUNK
    bb = u_ref.shape[0]

    @pl.when(pl.program_id(1) == 0)
    def _build_weights():
        strip_s[...] = jnp.zeros_like(strip_s)
        for dr in range(2):
            re_c = slice((2 * dr) * SGP, (2 * dr + 1) * SGP)
            im_c = slice((2 * dr + 1) * SGP, (2 * dr + 2) * SGP)
            b_re, b_im = bblk_ref[0, :, re_c], bblk_ref[0, :, im_c]
            c_re, c_im = cblk_ref[0, :, re_c], cblk_ref[0, :, im_c]
            c_cat = jnp.concatenate([c_re, -c_im], axis=1)
            for s in range(CHUNK):
                rows = slice(s * LANES, (s + 1) * LANES)
                k_in = CHUNK - 1 - s if dr == 0 else s
                a_re = apow_ref[0, dr, k_in:k_in + 1, :SGP]
                a_im = apow_ref[0, dr, k_in:k_in + 1, SGP:]
                w_re = b_re * a_re - b_im * a_im
                w_im = b_re * a_im + b_im * a_re
                win_s[rows, re_c] = w_re.astype(BF16)
                win_s[rows, im_c] = w_im.astype(BF16)
                lag = _dot_nt_f32(jnp.concatenate([w_re, w_im], axis=1), c_cat)
                if k_in == 0 and dr == 0:
                    lag0_fwd = lag
                else:
                    if k_in == 0:
                        lag = lag + lag0_fwd
                    j_top = CHUNK - 1 + (k_in if dr == 0 else -k_in)
                    lag_bf = lag.astype(BF16)
                    strip_s[:LANES, j_top * LANES:(j_top + 1) * LANES] = lag_bf
                    strip_s[LANES:, (j_top + 1) * LANES:(j_top + 2) * LANES] = lag_bf
                k_out = s + 1 if dr == 0 else CHUNK - s
                a_re = apow_ref[0, dr, k_out:k_out + 1, :SGP]
                a_im = apow_ref[0, dr, k_out:k_out + 1, SGP:]
                wout_s[rows, re_c] = (c_re * a_re - c_im * a_im).astype(BF16)
                wout_s[rows, im_c] = (-(c_re * a_im + c_im * a_re)).astype(BF16)

    for b in range(bb):
        for s in range(CHUNK):
            x_s[b * nc:(b + 1) * nc, s * LANES:(s + 1) * LANES] = (
                u_ref[b, pl.ds(s, nc, stride=CHUNK), :].astype(BF16))

    for s in range(0, CHUNK, 2):
        lo = (CHUNK - 1 - s) * LANES
        part = _dot(x_s[:, s * LANES:(s + 2) * LANES], strip_s[:, lo:lo + CW])
        if s == 0:
            y_s[...] = part
        else:
            y_s[...] += part

    for jj in range(NSTATE // (2 * LANES)):
        part = _dot(x_s[...], win_s[:, jj * 2 * LANES:(jj + 1) * 2 * LANES])
        p_s[2 * jj] = part[:, :LANES]
        p_s[2 * jj + 1] = part[:, LANES:]
    a16 = [[jnp.broadcast_to(apow_ref[0, dr, CHUNK:CHUNK + 1, t * LANES:(t + 1) * LANES],
                             (bb, LANES)) for t in range(2 * TPC)] for dr in range(2)]

    def scan(c, carry):
        rows = (pl.ds(c, bb, stride=nc), pl.ds(nc - 1 - c, bb, stride=nc))
        new = list(carry)
        for dr in range(2):
            for q in range(TPC):
                i_re, i_im = (2 * dr) * TPC + q, (2 * dr + 1) * TPC + q
                s_re, s_im = carry[i_re], carry[i_im]
                st_s[i_re, rows[dr], :] = s_re
                st_s[i_im, rows[dr], :] = s_im
                a_re, a_im = a16[dr][q], a16[dr][TPC + q]
                new[i_re] = a_re * s_re - a_im * s_im + p_s[i_re, rows[dr], :]
                new[i_im] = a_re * s_im + a_im * s_re + p_s[i_im, rows[dr], :]
        return tuple(new)

    zero = jnp.zeros((bb, LANES), F32)
    lax.fori_loop(0, nc, scan, (zero,) * (4 * TPC))

    for jj in range(NSTATE // (2 * LANES)):
        st = jnp.concatenate([st_s[2 * jj], st_s[2 * jj + 1]], axis=1).astype(BF16)
        y_s[...] += _dot_nt(st, wout_s[:, jj * 2 * LANES:(jj + 1) * 2 * LANES])

    d_row = d_ref[0]
    for b in range(bb):
        for s in range(CHUNK):
            rows = pl.ds(s, nc, stride=CHUNK)
            o_ref[b, rows, :] = (y_s[b * nc:(b + 1) * nc, s * LANES:(s + 1) * LANES]
                                 + d_row * u_ref[b, rows, :])


def _s5(proj3d, bblk, cblk, apow, d_diag):
    bsz, seq, _ = proj3d.shape
    rows = S5_BB * (seq // CHUNK)
    return pl.pallas_call(
        _s5_kernel,
        grid=(N_SG, bsz // S5_BB),
        in_specs=[
            pl.BlockSpec((S5_BB, seq, LANES), lambda g, b: (b, 0, COL_U // LANES + g)),
            pl.BlockSpec((1, LANES, NSTATE), lambda g, b: (g, 0, 0)),
            pl.BlockSpec((1, LANES, NSTATE), lambda g, b: (g, 0, 0)),
            pl.BlockSpec((1, 2, 24, 2 * SGP), lambda g, b: (g, 0, 0, 0)),
            pl.BlockSpec((1, 1, LANES), lambda g, b: (g, 0, 0)),
        ],
        out_specs=pl.BlockSpec((S5_BB, seq, LANES), lambda g, b: (b, 0, g)),
        out_shape=jax.ShapeDtypeStruct((bsz, seq, SSM_WIDTH), F32),
        scratch_shapes=[
            pltpu.VMEM((CW, NSTATE), BF16),
            pltpu.VMEM((CW, NSTATE), BF16),
            pltpu.VMEM((2 * LANES, STRIP_W), BF16),
            pltpu.VMEM((rows, CW), BF16),
            pltpu.VMEM((rows, CW), F32),
            pltpu.VMEM((NSTATE // LANES, rows, LANES), F32),
            pltpu.VMEM((NSTATE // LANES, rows, LANES), F32),
        ],
        compiler_params=pltpu.CompilerParams(
            dimension_semantics=("arbitrary", "arbitrary"), vmem_limit_bytes=VMEM_LIMIT),
        name="s5_chunked",
    )(proj3d, bblk, cblk, apow, d_diag)


def _s5_params(lam_re, lam_im, log_dt, b_re, b_im, c_re, c_im, d_skip):
    lr = jnp.minimum(lam_re, -1e-4)
    li = lam_im
    dt = jnp.exp(log_dt)[..., None]
    k = jnp.arange(24, dtype=F32)[None, :, None, None]
    mag = jnp.exp(k * (lr * dt)[:, None])
    ang = k * (li * dt)[:, None]
    pw_re, pw_im = mag * jnp.cos(ang), mag * jnp.sin(ang)
    apow = jnp.concatenate([
        pw_re.reshape(2, 24, N_SG, SGP), pw_im.reshape(2, 24, N_SG, SGP)], axis=-1)
    apow = apow.transpose(2, 0, 1, 3)

    er = jnp.exp(lr * dt)
    ab_re, ab_im = er * jnp.cos(li * dt), er * jnp.sin(li * dt)
    nr, ni = ab_re - 1.0, ab_im
    mg = lr * lr + li * li
    f_re = (nr * lr + ni * li) / mg
    f_im = (ni * lr - nr * li) / mg
    bb_re = f_re[..., None] * b_re - f_im[..., None] * b_im
    bb_im = f_re[..., None] * b_im + f_im[..., None] * b_re

    eye = jnp.eye(SG, dtype=F32)

    def blockdiag(t_gph):
        t = t_gph.reshape(2, N_SG, SG, SSM_STATE, SSM_GROUP)
        t = jnp.einsum('dqgph,gk->dqghkp', t, eye)
        return t.reshape(2, N_SG, LANES, SGP).transpose(1, 0, 2, 3)

    bblk = jnp.stack([blockdiag(bb_re), blockdiag(bb_im)], axis=2)
    bblk = bblk.transpose(0, 3, 1, 2, 4).reshape(N_SG, LANES, NSTATE)
    c_gph_re, c_gph_im = c_re.transpose(0, 1, 3, 2), c_im.transpose(0, 1, 3, 2)
    cblk = jnp.stack([blockdiag(c_gph_re), blockdiag(c_gph_im)], axis=2)
    cblk = cblk.transpose(0, 3, 1, 2, 4).reshape(N_SG, LANES, NSTATE)
    return bblk, cblk, apow, d_skip.reshape(N_SG, 1, LANES)


TAIL_TT = 512


def _tail_kernel(x_ref, ys_ref, za_ref, at_ref, zb_ref, g_ref, wg_ref, bg_ref, wps_ref, wpa_ref,
                 wo_ref, o_ref):
    ys = jax.nn.gelu(ys_ref[...]).astype(BF16)
    glu = _dot(ys, wg_ref[...]) + bg_ref[...]
    a_in = glu[:, :SSM_WIDTH] * jax.nn.sigmoid(glu[:, SSM_WIDTH:]) * jax.nn.silu(za_ref[...])
    y_a = _dot(a_in.astype(BF16), wps_ref[...])
    y_b = _dot((at_ref[...] * jax.nn.silu(zb_ref[...])).astype(BF16), wpa_ref[...])
    g = g_ref[...]
    mix = jax.nn.sigmoid(g[:, :D_MODEL]) * y_a + jax.nn.sigmoid(g[:, D_MODEL:]) * y_b
    o_ref[...] = x_ref[...] + _dot(mix.astype(BF16), wo_ref[...])


def _tail(x2d, ys2d, attn2d, proj2d, w_glu, b_glu, w_ps, w_pa, w_out):
    t = x2d.shape[0]
    row = lambda w, c: pl.BlockSpec((TAIL_TT, w), lambda i, c=c: (i, c))
    const = lambda shape: pl.BlockSpec(shape, lambda i: (0, 0))
    return pl.pallas_call(
        _tail_kernel,
        grid=(t // TAIL_TT,),
        in_specs=[
            row(D_MODEL, 0), row(SSM_WIDTH, 0),
            row(SSM_WIDTH, COL_ZA // SSM_WIDTH),
            row(ATTN_WIDTH, 0),
            row(ATTN_WIDTH, COL_ZB // ATTN_WIDTH),
            row(2 * D_MODEL, COL_G // (2 * D_MODEL)),
            const((SSM_WIDTH, 2 * SSM_WIDTH)), const((1, 2 * SSM_WIDTH)),
            const((SSM_WIDTH, D_MODEL)), const((ATTN_WIDTH, D_MODEL)), const((D_MODEL, D_MODEL)),
        ],
        out_specs=pl.BlockSpec((TAIL_TT, D_MODEL), lambda i: (i, 0)),
        out_shape=jax.ShapeDtypeStruct((t, D_MODEL), F32),
        compiler_params=pltpu.CompilerParams(
            dimension_semantics=("arbitrary",), vmem_limit_bytes=VMEM_LIMIT),
        name="tail",
    )(x2d, ys2d, proj2d, attn2d, proj2d, proj2d, w_glu, b_glu, w_ps, w_pa, w_out)


def _rope_tables(seq):
    half = ROPE_DIM // 2
    inv = ROPE_THETA ** (-jnp.arange(0, ROPE_DIM, 2, dtype=F32) / ROPE_DIM)
    ang = jnp.arange(seq, dtype=F32)[:, None] * inv[None, :]
    cos, sin = jnp.cos(ang), jnp.sin(ang)
    zeros = jnp.zeros((seq, HEAD_DIM - ROPE_DIM), F32)
    z8 = jnp.zeros((seq, half), F32)
    cos_h = jnp.concatenate([cos, cos, jnp.ones_like(zeros)], axis=1)
    sa_h = jnp.concatenate([-sin, z8, zeros], axis=1)
    sb_h = jnp.concatenate([z8, sin, zeros], axis=1)
    two = lambda t: jnp.concatenate([t, t], axis=1)
    return two(cos_h), two(sa_h), two(sb_h)


def kernel(x, norm_w, w_in, b_gate, q_norm_w, k_norm_w, ssm_lam_re, ssm_lam_im, ssm_log_dt,
           ssm_b_re, ssm_b_im, ssm_c_re, ssm_c_im, ssm_d, w_glu, b_glu,
           w_proj_ssm, w_proj_attn, w_out):
    bsz, seq, _ = x.shape
    depth = norm_w.shape[0]
    cosf, sa, sb = _rope_tables(seq)
    ones_blk = jnp.kron(jnp.eye(2 * LANES // HEAD_DIM, dtype=F32),
                        jnp.full((HEAD_DIM, HEAD_DIM), 1.0 / HEAD_DIM, F32)).astype(BF16)
    for layer in range(depth):
        x2d = x.reshape(bsz * seq, D_MODEL)
        q_gain = q_norm_w[layer].astype(F32) * (LOG2E * HEAD_DIM ** -0.5)
        qk_w_row = jnp.concatenate([jnp.tile(q_gain, len(DILATIONS) * ATTN_SLOTS),
                                    jnp.tile(k_norm_w[layer].astype(F32), ATTN_SLOTS)])[None, :]
        proj2d = _in_proj(x2d, norm_w[layer][None, :].astype(F32), w_in[layer].astype(BF16),
                          b_gate[layer][None, :].astype(F32), cosf, sa, sb, qk_w_row, ones_blk,
                          seq)
        proj3d = proj2d.reshape(bsz, seq, IN_WIDTH)
        attn = _attention(proj3d)

        bblk, cblk, apow, d_diag = _s5_params(
            ssm_lam_re[layer].astype(F32), ssm_lam_im[layer].astype(F32),
            ssm_log_dt[layer].astype(F32), ssm_b_re[layer].astype(F32),
            ssm_b_im[layer].astype(F32), ssm_c_re[layer].astype(F32),
            ssm_c_im[layer].astype(F32), ssm_d[layer].astype(F32))
        y_s = _s5(proj3d, bblk, cblk, apow, d_diag)

        out2d = _tail(x2d, y_s.reshape(bsz * seq, SSM_WIDTH), attn.reshape(bsz * seq, ATTN_WIDTH),
                      proj2d, w_glu[layer].astype(BF16), b_glu[layer][None, :].astype(F32),
                      w_proj_ssm[layer].astype(BF16), w_proj_attn[layer].astype(BF16),
                      w_out[layer].astype(BF16))
        x = out2d.reshape(bsz, seq, D_MODEL)
    return x
```

```python
import math

import jax
import jax.numpy as jnp
from jax import lax
from jax.experimental import pallas as pl
from jax.experimental.pallas import tpu as pltpu

F32 = jnp.float32
BF16 = jnp.bfloat16

D_MODEL = 1024
SSM_WIDTH = 512
SSM_GROUP = 16
SSM_GROUPS = 32
SSM_STATE = 64
HEAD_DIM = 64
ATTN_SLOTS = 8
ATTN_WIDTH = 512
DILATIONS = (1, 4, 16)
BAND_HALF = 64
ROPE_THETA = 500000.0
ROPE_DIM = 16
EPS = 1e-6
NEG_INF = -1e30
IN_WIDTH = 6144
COL_U, COL_ZA, COL_Q, COL_K, COL_V, COL_ZB, COL_G = 0, 512, 1024, 2560, 3072, 3584, 4096

LANES = 128
VMEM_LIMIT = 56 * 1024 * 1024

CHUNK = 16


def _dot(a, b):
    return jnp.dot(a, b, preferred_element_type=F32)


def _dot_nt(a, b):
    return lax.dot_general(a, b, (((1,), (1,)), ((), ())), preferred_element_type=F32)


def _dot_nt_f32(a, b):
    return lax.dot_general(a, b, (((1,), (1,)), ((), ())), preferred_element_type=F32,
                           precision=lax.Precision.HIGHEST)


IN_TT = 512
IN_TN = 512
QK_ROWS = 256
LOG2E = math.log2(math.e)


def _in_proj_kernel(x_ref, nw_ref, w_ref, b_ref, cos_ref, sa_ref, sb_ref, qkw_ref, ones_ref,
                    o_ref):
    x = x_ref[...]
    var = jnp.mean(x * x, axis=-1, keepdims=True)
    h = (x * lax.rsqrt(var + EPS) * nw_ref[...]).astype(BF16)
    ones_blk = ones_ref[...]
    for j in range(IN_WIDTH // IN_TN):
        c0 = j * IN_TN
        acc = _dot(h, w_ref[:, c0:c0 + IN_TN])
        if c0 >= COL_G:
            o_ref[:, c0:c0 + IN_TN] = acc + b_ref[:, c0 - COL_G:c0 - COL_G + IN_TN]
        elif COL_Q <= c0 < COL_V:
            for r0 in range(0, IN_TT, QK_ROWS):
                rows = slice(r0, r0 + QK_ROWS)
                for t2 in range(0, IN_TN, 2 * LANES):
                    xq2 = acc[rows, t2:t2 + 2 * LANES]
                    ms2 = _dot((xq2 * xq2).astype(BF16), ones_blk)
                    xn2 = xq2 * lax.rsqrt(ms2 + EPS) * qkw_ref[:, c0 - COL_Q + t2:
                                                                c0 - COL_Q + t2 + 2 * LANES]
                    for t in (0, LANES):
                        xn = xn2[:, t:t + LANES]
                        lanes = slice(c0 + t2 + t, c0 + t2 + t + LANES)
                        o_ref[rows, lanes] = (
                            xn * cos_ref[rows, :]
                            + pltpu.roll(xn, LANES - ROPE_DIM // 2, 1) * sa_ref[rows, :]
                            + pltpu.roll(xn, ROPE_DIM // 2, 1) * sb_ref[rows, :])
        else:
            o_ref[:, c0:c0 + IN_TN] = acc


def _in_proj(x2d, norm_w, w_in_bf16, b_gate_row, cosf, sa, sb, qk_w_row, ones_blk, seq):
    t = x2d.shape[0]
    tiles_per_seq = seq // IN_TT
    const = lambda shape: pl.BlockSpec(shape, lambda i: (0, 0))
    rope = lambda: pl.BlockSpec((IN_TT, LANES), lambda i: (i % tiles_per_seq, 0))
    return pl.pallas_call(
        _in_proj_kernel,
        grid=(t // IN_TT,),
        in_specs=[
            pl.BlockSpec((IN_TT, D_MODEL), lambda i: (i, 0)),
            const((1, D_MODEL)),
            pl.BlockSpec((D_MODEL, IN_WIDTH), lambda i: (0, 0), pipeline_mode=pl.Buffered(1)),
            const((1, IN_WIDTH - COL_G)),
            rope(), rope(), rope(),
            const((1, COL_V - COL_Q)),
            const((2 * LANES, 2 * LANES)),
        ],
        out_specs=pl.BlockSpec((IN_TT, IN_WIDTH), lambda i: (i, 0)),
        out_shape=jax.ShapeDtypeStruct((t, IN_WIDTH), F32),
        compiler_params=pltpu.CompilerParams(
            dimension_semantics=("arbitrary",), vmem_limit_bytes=VMEM_LIMIT),
        name="in_proj",
    )(x2d, norm_w, w_in_bf16, b_gate_row, cosf, sa, sb, qk_w_row, ones_blk)


QBLK = 128
KWIN = QBLK + 2 * BAND_HALF
NORM_ROWS = 256


def _attn_kernel(q0_ref, q1_ref, q2_ref, k_ref, v_ref, bias_ref, o_ref,
                 va_s, vb_s, oacc_s, den_s, max_s, s_s):
    seq = k_ref.shape[1]
    q_refs = (q0_ref, q1_ref, q2_ref)
    head0 = lax.broadcasted_iota(jnp.int32, (QBLK, LANES), 1) < HEAD_DIM
    head0_rows = lax.broadcasted_iota(jnp.int32, (NORM_ROWS, LANES), 1) < HEAD_DIM

    def prep(i, carry):
        rows = pl.ds(pl.multiple_of(i * NORM_ROWS, NORM_ROWS), NORM_ROWS)
        v = v_ref[0, rows, :]
        va_s[rows, :] = jnp.where(head0_rows, v, 1.0)
        vb_s[rows, :] = jnp.where(head0_rows, 1.0, v)
        return carry

    lax.fori_loop(0, seq // NORM_ROWS, prep, 0)

    for p, d in enumerate(DILATIONS):
        n = seq // d
        nblk = n // QBLK
        kw = min(n, KWIN)

        def rows_of(idx, d=d, n=n, nblk=nblk, kw=kw):
            r = idx // nblk
            q0 = (idx % nblk) * QBLK
            ks = jnp.clip(q0 - BAND_HALF, 0, n - kw)
            return (pl.ds(r + d * q0, QBLK, stride=d), pl.ds(r + d * ks, kw, stride=d),
                    (q0 - ks) // BAND_HALF)

        def scores(idx, slot, p=p, kw=kw):
            qrows, krows, case = rows_of(idx)
            qb = q_refs[p][0, qrows, :]
            kb = k_ref[0, krows, :].astype(BF16)
            bias = bias_ref[case, :, :kw]
            s_s[slot, :QBLK, :kw] = _dot_nt(jnp.where(head0, qb, 0.0).astype(BF16), kb) + bias
            s_s[slot, QBLK:, :kw] = _dot_nt(jnp.where(head0, 0.0, qb).astype(BF16), kb) + bias

        def weigh(idx, slot, p=p, kw=kw):
            qrows, krows, _ = rows_of(idx)
            s = s_s[slot, :, :kw]
            m = jnp.max(s, axis=-1, keepdims=True)
            e = jnp.exp2(s - m).astype(BF16)
            o_a = _dot(e[:QBLK], va_s[krows, :].astype(BF16))
            o_b = _dot(e[QBLK:], vb_s[krows, :].astype(BF16))
            oacc_s[p, qrows, :] = jnp.where(head0, o_a, o_b)
            den_s[p, qrows, :] = jnp.where(head0, o_b, o_a)
            max_s[p, qrows, :] = jnp.where(head0, jnp.broadcast_to(m[:QBLK], (QBLK, LANES)),
                                           jnp.broadcast_to(m[QBLK:], (QBLK, LANES)))

        def pair(i0, cur, ahead=True):
            nxt = 2 - cur
            if ahead:
                scores(i0 + 2, nxt)
            weigh(i0, cur)
            if ahead:
                scores(i0 + 3, nxt + 1)
            weigh(i0 + 1, cur + 1)

        def quad(j, carry):
            pair(4 * j, 0)
            pair(4 * j + 2, 2)
            return carry

        n_quads = d * nblk // 4
        scores(0, 0)
        scores(1, 1)
        lax.fori_loop(0, n_quads - 1, quad, 0)
        pair(4 * n_quads - 4, 0)
        pair(4 * n_quads - 2, 2, ahead=False)

    def combine(i, carry):
        rows = pl.ds(pl.multiple_of(i * NORM_ROWS, NORM_ROWS), NORM_ROWS)
        m0, m1, m2 = max_s[0, rows, :], max_s[1, rows, :], max_s[2, rows, :]
        m = jnp.maximum(jnp.maximum(m0, m1), m2)
        w0, w1, w2 = jnp.exp2(m0 - m), jnp.exp2(m1 - m), jnp.exp2(m2 - m)
        num = w0 * oacc_s[0, rows, :] + w1 * oacc_s[1, rows, :] + w2 * oacc_s[2, rows, :]
        d0, d1, d2 = (pltpu.roll(den_s[p, rows, :], HEAD_DIM, 1) for p in range(3))
        o_ref[0, rows, :] = num / (w0 * d0 + w1 * d1 + w2 * d2)
        return carry

    lax.fori_loop(0, seq // NORM_ROWS, combine, 0)


def _band_bias():
    i = jnp.arange(QBLK)[:, None]
    j = jnp.arange(KWIN)[None, :]
    off = jnp.arange(3)[:, None, None] * BAND_HALF
    return jnp.where(jnp.abs(j - i - off) <= BAND_HALF, 0.0, NEG_INF).astype(F32)


def _attention(proj3d):
    bsz, seq, _ = proj3d.shape
    n_pairs = ATTN_WIDTH // LANES

    def qspec(p):
        return pl.BlockSpec((1, seq, LANES),
                            lambda b, hp, p=p: (b, 0, COL_Q // LANES + p * n_pairs + hp))

    seq_tile = lambda: pltpu.VMEM((seq, LANES), F32)
    pat_tile = lambda: pltpu.VMEM((len(DILATIONS), seq, LANES), F32)
    return pl.pallas_call(
        _attn_kernel,
        grid=(bsz, n_pairs),
        in_specs=[
            qspec(0), qspec(1), qspec(2),
            pl.BlockSpec((1, seq, LANES), lambda b, hp: (b, 0, COL_K // LANES + hp)),
            pl.BlockSpec((1, seq, LANES), lambda b, hp: (b, 0, COL_V // LANES + hp)),
            pl.BlockSpec((3, QBLK, KWIN), lambda b, hp: (0, 0, 0)),
        ],
        out_specs=pl.BlockSpec((1, seq, LANES), lambda b, hp: (b, 0, hp)),
        out_shape=jax.ShapeDtypeStruct((bsz, seq, ATTN_WIDTH), F32),
        scratch_shapes=[
            seq_tile(), seq_tile(),
            pat_tile(), pat_tile(), pat_tile(),
            pltpu.VMEM((4, 2 * QBLK, KWIN), F32),
        ],
        compiler_params=pltpu.CompilerParams(
            dimension_semantics=("arbitrary", "arbitrary"), vmem_limit_bytes=VMEM_LIMIT),
        name="dilated_attention",
    )(proj3d, proj3d, proj3d, proj3d, proj3d, _band_bias())


GP = 2
N_PAIRS = SSM_GROUPS // GP
PG = GP * SSM_GROUP
PW = CHUNK * PG
SW = 4 * LANES
LAG_REP = LANES // PG


def _s5_prep_kernel(lre_ref, lim_ref, ldt_ref, bre_ref, bim_ref, cre_ref, cim_ref,
                    toep_ref, win_ref, wout_ref, a_ref, wf32_s):
    four = lambda ref: jnp.concatenate([ref[0, 0], ref[0, 0], ref[1, 0], ref[1, 0]], axis=1)
    lr = jnp.minimum(four(lre_ref), -1e-4)
    li = four(lim_ref)
    dt = jnp.exp(four(ldt_ref))
    col = lax.broadcasted_iota(jnp.int32, (1, SW), 1)
    is_re = (col // LANES) % 2 == 0
    is_fwd = col < 2 * LANES

    def power(k):
        mag = jnp.exp(k * (lr * dt))
        return mag * jnp.cos(k * (li * dt)), mag * jnp.sin(k * (li * dt))

    a_re, a_im = power(1.0)
    nr, ni, mag2 = a_re - 1.0, a_im, lr * lr + li * li
    f_re, f_im = (nr * lr + ni * li) / mag2, (ni * lr - nr * li) / mag2

    row_g = lax.broadcasted_iota(jnp.int32, (PG, SW), 0) // SSM_GROUP
    col_g = (lax.broadcasted_iota(jnp.int32, (PG, SW), 1) % LANES) // SSM_STATE
    diag = row_g == col_g
    b_re = jnp.where(diag, four(bre_ref), 0.0)
    b_im = jnp.where(diag, four(bim_ref), 0.0)
    c_re = jnp.where(diag, four(cre_ref), 0.0)
    c_im = jnp.where(diag, four(cim_ref), 0.0)
    bb_re = f_re * b_re - f_im * b_im
    bb_im = f_re * b_im + f_im * b_re

    step = lax.broadcasted_iota(jnp.int32, (CHUNK, SW), 0)
    k_in = jnp.where(is_fwd, CHUNK - 1 - step, step).astype(F32)
    k_out = jnp.where(is_fwd, step + 1, CHUNK - step).astype(F32)
    in_re, in_im = power(k_in)
    out_re, out_im = power(k_out)
    x1, x2 = jnp.where(is_re, in_re, in_im), jnp.where(is_re, -in_im, in_re)
    y1, y2 = jnp.where(is_re, out_re, -out_im), jnp.where(is_re, -out_im, -out_re)
    for s in range(CHUNK):
        rows = slice(s * PG, (s + 1) * PG)
        w = bb_re * x1[s:s + 1] + bb_im * x2[s:s + 1]
        wf32_s[rows, :] = w
        win_ref[0, rows, :] = w.astype(BF16)
        wout_ref[0, rows, :] = (c_re * y1[s:s + 1] + c_im * y2[s:s + 1]).astype(BF16)

    a16_re, a16_im = power(float(CHUNK))
    a_ref[0] = jnp.broadcast_to(jnp.where(is_re, a16_re, a16_im), a_ref.shape[1:])

    c_cat = jnp.where(is_re, c_re, -c_im)
    c_rep = jnp.concatenate([c_cat] * LAG_REP, axis=0)
    half = 2 * LANES
    lag_f = _dot_nt_f32(wf32_s[:, :half], c_rep[:, :half])
    lag_b = _dot_nt_f32(wf32_s[:, half:], c_rep[:, half:])
    row_t = lax.broadcasted_iota(jnp.int32, (PW, LANES), 0) // PG
    lane_q = lax.broadcasted_iota(jnp.int32, (PW, LANES), 1) // PG
    zeros = lambda n: jnp.zeros((n * PG, LANES), F32)
    for q in range(PW // LANES):
        tile = jnp.zeros((PW, LANES), F32)
        for u in range(LAG_REP):
            t = q * LAG_REP + u
            up = CHUNK - 1 - t
            sh_f = lag_f if up == 0 else jnp.concatenate([lag_f[up * PG:], zeros(up)], axis=0)
            sh_b = lag_b if t == 0 else jnp.concatenate([zeros(t), lag_b[:(CHUNK - t) * PG]], axis=0)
            col_t = jnp.where(row_t <= t, sh_f, 0.0) + jnp.where(row_t >= t, sh_b, 0.0)
            tile = jnp.where(lane_q == u, col_t, tile)
        toep_ref[0, :, q * LANES:(q + 1) * LANES] = tile.astype(BF16)


def _s5_prep(lam_re, lam_im, log_dt, b_re, b_im, c_re, c_im):
    lanes = lambda t: t.reshape(2, N_PAIRS, 1, LANES)
    dt_b = jnp.broadcast_to(log_dt[..., None], lam_re.shape)

    def tiles(t_ghp):
        t = jnp.broadcast_to(t_ghp[:, :, :, None, :], t_ghp.shape[:3] + (GP, SSM_STATE))
        return t.reshape(2, N_PAIRS, PG, LANES)

    row = pl.BlockSpec((2, 1, 1, LANES), lambda i: (0, i, 0, 0))
    mat = pl.BlockSpec((2, 1, PG, LANES), lambda i: (0, i, 0, 0))
    out = lambda r: pl.BlockSpec((1, r, SW), lambda i: (i, 0, 0))
    return pl.pallas_call(
        _s5_prep_kernel,
        grid=(N_PAIRS,),
        in_specs=[row, row, row, mat, mat, mat, mat],
        out_specs=[out(PW), out(PW), out(PW), out(8)],
        out_shape=[jax.ShapeDtypeStruct((N_PAIRS, PW, PW), BF16),
                   jax.ShapeDtypeStruct((N_PAIRS, PW, SW), BF16),
                   jax.ShapeDtypeStruct((N_PAIRS, PW, SW), BF16),
                   jax.ShapeDtypeStruct((N_PAIRS, 8, SW), F32)],
        scratch_shapes=[pltpu.VMEM((PW, SW), F32)],
        compiler_params=pltpu.CompilerParams(dimension_semantics=("arbitrary",)),
        name="s5_prep",
    )(lanes(lam_re), lanes(lam_im), lanes(dt_b), tiles(b_re.transpose(0, 1, 3, 2)),
      tiles(b_im.transpose(0, 1, 3, 2)), tiles(c_re), tiles(c_im))


SCAN_UNROLL = 4


def _s5_kernel(x_ref, toep_ref, win_ref, wout_ref, a_ref, y_ref, p_s, st_s):
    rows_per_chunk = a_ref.shape[1]
    nc = x_ref.shape[1] // rows_per_chunk
    x = x_ref[0]
    p_s[...] = _dot(x, win_ref[0])
    a_fr, a_fi, a_br, a_bi = (a_ref[0, :, q * LANES:(q + 1) * LANES] for q in range(4))
    tile = lambda q: slice(q * LANES, (q + 1) * LANES)

    def scan(c, carry):
        s_fr, s_fi, s_br, s_bi = carry
        rf = pl.ds(pl.multiple_of(c * rows_per_chunk, rows_per_chunk), rows_per_chunk)
        rb = pl.ds(pl.multiple_of((nc - 1 - c) * rows_per_chunk, rows_per_chunk), rows_per_chunk)
        st_s[rf, tile(0)] = s_fr
        st_s[rf, tile(1)] = s_fi
        st_s[rb, tile(2)] = s_br
        st_s[rb, tile(3)] = s_bi
        return (a_fr * s_fr - a_fi * s_fi + p_s[rf, tile(0)],
                a_fr * s_fi + a_fi * s_fr + p_s[rf, tile(1)],
                a_br * s_br - a_bi * s_bi + p_s[rb, tile(2)],
                a_br * s_bi + a_bi * s_br + p_s[rb, tile(3)])

    zero = jnp.zeros((rows_per_chunk, LANES), F32)
    lax.fori_loop(0, nc, scan, (zero,) * 4, unroll=SCAN_UNROLL)
    y_ref[0] = _dot(x, toep_ref[0]) + _dot_nt(st_s[...].astype(BF16), wout_ref[0])


def _s5(x_pairs, toep, w_in, w_out, a_pow, bsz):
    rows = x_pairs.shape[1]
    pair = lambda r, c: pl.BlockSpec((1, r, c), lambda i: (i, 0, 0))
    return pl.pallas_call(
        _s5_kernel,
        grid=(N_PAIRS,),
        in_specs=[pair(rows, PW), pair(PW, PW), pair(PW, SW), pair(PW, SW), pair(bsz, SW)],
        out_specs=pair(rows, PW),
        out_shape=jax.ShapeDtypeStruct((N_PAIRS, rows, PW), F32),
        scratch_shapes=[pltpu.VMEM((rows, SW), F32),
                        pltpu.VMEM((rows, SW), F32)],
        compiler_params=pltpu.CompilerParams(
            dimension_semantics=("arbitrary",), vmem_limit_bytes=VMEM_LIMIT),
        name="s5_chunked",
    )(x_pairs, toep, w_in, w_out, a_pow)


def _to_pairs(u3d):
    bsz, seq, _ = u3d.shape
    t = u3d.reshape(bsz, seq // CHUNK, CHUNK, N_PAIRS, PG).transpose(3, 1, 0, 2, 4)
    return t.reshape(N_PAIRS, (seq // CHUNK) * bsz, PW).astype(BF16)


def _from_pairs(y_pairs, bsz, seq):
    t = y_pairs.reshape(N_PAIRS, seq // CHUNK, bsz, CHUNK, PG).transpose(2, 1, 3, 0, 4)
    return t.reshape(bsz * seq, SSM_WIDTH)


TAIL_TT = 512


def _tail_kernel(x_ref, ys_ref, u_ref, d_ref, za_ref, at_ref, zb_ref, g_ref, wg_ref, bg_ref,
                 wps_ref, wpa_ref, wo_ref, o_ref):
    ys = jax.nn.gelu(ys_ref[...] + d_ref[...] * u_ref[...]).astype(BF16)
    glu = _dot(ys, wg_ref[...]) + bg_ref[...]
    a_in = glu[:, :SSM_WIDTH] * jax.nn.sigmoid(glu[:, SSM_WIDTH:]) * jax.nn.silu(za_ref[...])
    y_a = _dot(a_in.astype(BF16), wps_ref[...])
    y_b = _dot((at_ref[...] * jax.nn.silu(zb_ref[...])).astype(BF16), wpa_ref[...])
    g = g_ref[...]
    mix = jax.nn.sigmoid(g[:, :D_MODEL]) * y_a + jax.nn.sigmoid(g[:, D_MODEL:]) * y_b
    o_ref[...] = x_ref[...] + _dot(mix.astype(BF16), wo_ref[...])


def _tail(x2d, ys2d, d_row, attn2d, proj2d, w_glu, b_glu, w_ps, w_pa, w_out):
    t = x2d.shape[0]
    row = lambda w, c: pl.BlockSpec((TAIL_TT, w), lambda i, c=c: (i, c))
    const = lambda shape: pl.BlockSpec(shape, lambda i: (0, 0))
    return pl.pallas_call(
        _tail_kernel,
        grid=(t // TAIL_TT,),
        in_specs=[
            row(D_MODEL, 0), row(SSM_WIDTH, 0),
            row(SSM_WIDTH, COL_U // SSM_WIDTH), const((1, SSM_WIDTH)),
            row(SSM_WIDTH, COL_ZA // SSM_WIDTH),
            row(ATTN_WIDTH, 0),
            row(ATTN_WIDTH, COL_ZB // ATTN_WIDTH),
            row(2 * D_MODEL, COL_G // (2 * D_MODEL)),
            const((SSM_WIDTH, 2 * SSM_WIDTH)), const((1, 2 * SSM_WIDTH)),
            const((SSM_WIDTH, D_MODEL)), const((ATTN_WIDTH, D_MODEL)), const((D_MODEL, D_MODEL)),
        ],
        out_specs=pl.BlockSpec((TAIL_TT, D_MODEL), lambda i: (i, 0)),
        out_shape=jax.ShapeDtypeStruct((t, D_MODEL), F32),
        compiler_params=pltpu.CompilerParams(
            dimension_semantics=("arbitrary",), vmem_limit_bytes=VMEM_LIMIT),
        name="tail",
    )(x2d, ys2d, proj2d, d_row, proj2d, attn2d, proj2d, proj2d, w_glu, b_glu, w_ps, w_pa, w_out)


def _rope_tables(seq):
    half = ROPE_DIM // 2
    inv = ROPE_THETA ** (-jnp.arange(0, ROPE_DIM, 2, dtype=F32) / ROPE_DIM)
    ang = jnp.arange(seq, dtype=F32)[:, None] * inv[None, :]
    cos, sin = jnp.cos(ang), jnp.sin(ang)
    zeros = jnp.zeros((seq, HEAD_DIM - ROPE_DIM), F32)
    z8 = jnp.zeros((seq, half), F32)
    cos_h = jnp.concatenate([cos, cos, jnp.ones_like(zeros)], axis=1)
    sa_h = jnp.concatenate([-sin, z8, zeros], axis=1)
    sb_h = jnp.concatenate([z8, sin, zeros], axis=1)
    two = lambda t: jnp.concatenate([t, t], axis=1)
    return two(cos_h), two(sa_h), two(sb_h)


def kernel(x, norm_w, w_in, b_gate, q_norm_w, k_norm_w, ssm_lam_re, ssm_lam_im, ssm_log_dt,
           ssm_b_re, ssm_b_im, ssm_c_re, ssm_c_im, ssm_d, w_glu, b_glu,
           w_proj_ssm, w_proj_attn, w_out):
    bsz, seq, _ = x.shape
    depth = norm_w.shape[0]
    cosf, sa, sb = _rope_tables(seq)
    ones_blk = jnp.kron(jnp.eye(2 * LANES // HEAD_DIM, dtype=F32),
                        jnp.full((HEAD_DIM, HEAD_DIM), 1.0 / HEAD_DIM, F32)).astype(BF16)
    for layer in range(depth):
        x2d = x.reshape(bsz * seq, D_MODEL)
        q_gain = q_norm_w[layer].astype(F32) * (LOG2E * HEAD_DIM ** -0.5)
        qk_w_row = jnp.concatenate([jnp.tile(q_gain, len(DILATIONS) * ATTN_SLOTS),
                                    jnp.tile(k_norm_w[layer].astype(F32), ATTN_SLOTS)])[None, :]
        proj2d = _in_proj(x2d, norm_w[layer][None, :].astype(F32), w_in[layer].astype(BF16),
                          b_gate[layer][None, :].astype(F32), cosf, sa, sb, qk_w_row, ones_blk,
                          seq)
        proj3d = proj2d.reshape(bsz, seq, IN_WIDTH)
        attn = _attention(proj3d)

        toep, s5_in, s5_out, a_pow = _s5_prep(
            ssm_lam_re[layer].astype(F32), ssm_lam_im[layer].astype(F32),
            ssm_log_dt[layer].astype(F32), ssm_b_re[layer].astype(F32),
            ssm_b_im[layer].astype(F32), ssm_c_re[layer].astype(F32),
            ssm_c_im[layer].astype(F32))
        y_pairs = _s5(_to_pairs(proj3d[:, :, COL_U:COL_U + SSM_WIDTH]), toep, s5_in, s5_out, a_pow,
                      bsz)

        out2d = _tail(x2d, _from_pairs(y_pairs, bsz, seq), ssm_d[layer][None, :].astype(F32),
                      attn.reshape(bsz * seq, ATTN_WIDTH), proj2d,
                      w_glu[layer].astype(BF16), b_glu[layer][None, :].astype(F32),
                      w_proj_ssm[layer].astype(BF16), w_proj_attn[layer].astype(BF16),
                      w_out[layer].astype(BF16))
        x = out2d.reshape(bsz, seq, D_MODEL)
    return x
```

```python
import math

import jax
import jax.numpy as jnp
from jax import lax
from jax.experimental import pallas as pl
from jax.experimental.pallas import tpu as pltpu

F32 = jnp.float32
BF16 = jnp.bfloat16

D_MODEL = 1024
SSM_WIDTH = 512
SSM_GROUP = 16
SSM_GROUPS = 32
SSM_STATE = 64
HEAD_DIM = 64
ATTN_SLOTS = 8
ATTN_WIDTH = 512
DILATIONS = (1, 4, 16)
BAND_HALF = 64
ROPE_THETA = 500000.0
ROPE_DIM = 16
EPS = 1e-6
NEG_INF = -1e30
IN_WIDTH = 6144
COL_U, COL_ZA, COL_Q, COL_K, COL_V, COL_ZB, COL_G = 0, 512, 1024, 2560, 3072, 3584, 4096

LANES = 128
VMEM_LIMIT = 56 * 1024 * 1024

CHUNK = 16


def _dot(a, b):
    return jnp.dot(a, b, preferred_element_type=F32)


def _dot_nt(a, b):
    return lax.dot_general(a, b, (((1,), (1,)), ((), ())), preferred_element_type=F32)


def _dot_nt_f32(a, b):
    return lax.dot_general(a, b, (((1,), (1,)), ((), ())), preferred_element_type=F32,
                           precision=lax.Precision.HIGHEST)


IN_TT = 512
IN_TN = 512
QK_ROWS = 256
LOG2E = math.log2(math.e)
PF_U, PF_Q, PF_K, PF_V, PF_WIDTH = 0, 512, 2048, 2560, 3072
PZ_G, PZ_ZA, PZ_ZB, PZ_WIDTH = 0, 2048, 2560, 3072
_Q_COLS = tuple((COL_Q + o, PF_Q + o) for o in range(0, COL_K - COL_Q, IN_TN))
_G_COLS = tuple((COL_G + o, PZ_G + o) for o in range(0, IN_WIDTH - COL_G, IN_TN))
F32_DEST = dict(((COL_U, PF_U), (COL_K, PF_K), (COL_V, PF_V)) + _Q_COLS)
BF16_DEST = dict(((COL_ZA, PZ_ZA), (COL_ZB, PZ_ZB)) + _G_COLS)


def _in_proj_kernel(x_ref, nw_ref, w_ref, b_ref, cos_ref, sa_ref, sb_ref, qkw_ref, ones_ref,
                    pf_ref, pz_ref, utb_ref):
    bsz = utb_ref.shape[1] // IN_TT
    batch = pl.program_id(0) % bsz
    x = x_ref[...]
    var = jnp.mean(x * x, axis=-1, keepdims=True)
    h = (x * lax.rsqrt(var + EPS) * nw_ref[...]).astype(BF16)
    ones_blk = ones_ref[...]
    for c0 in range(0, IN_WIDTH, IN_TN):
        acc = _dot(h, w_ref[:, c0:c0 + IN_TN])
        if c0 in BF16_DEST:
            if c0 >= COL_G:
                acc = acc + b_ref[:, c0 - COL_G:c0 - COL_G + IN_TN]
            pz_ref[:, BF16_DEST[c0]:BF16_DEST[c0] + IN_TN] = acc.astype(BF16)
            continue
        d0 = F32_DEST[c0]
        if c0 == COL_U:
            for lt in range(SSM_WIDTH // LANES):
                utb_ref[lt, pl.ds(batch, IN_TT, stride=bsz), :] = acc[:, lt * LANES:(lt + 1) * LANES]
        if not COL_Q <= c0 < COL_V:
            pf_ref[:, d0:d0 + IN_TN] = acc
            continue
        for r0 in range(0, IN_TT, QK_ROWS):
            rows = slice(r0, r0 + QK_ROWS)
            for t2 in range(0, IN_TN, 2 * LANES):
                xq2 = acc[rows, t2:t2 + 2 * LANES]
                ms2 = _dot((xq2 * xq2).astype(BF16), ones_blk)
                xn2 = xq2 * lax.rsqrt(ms2 + EPS) * qkw_ref[:, c0 - COL_Q + t2:
                                                            c0 - COL_Q + t2 + 2 * LANES]
                for t in (0, LANES):
                    xn = xn2[:, t:t + LANES]
                    pf_ref[rows, d0 + t2 + t:d0 + t2 + t + LANES] = (
                        xn * cos_ref[rows, :]
                        + pltpu.roll(xn, LANES - ROPE_DIM // 2, 1) * sa_ref[rows, :]
                        + pltpu.roll(xn, ROPE_DIM // 2, 1) * sb_ref[rows, :])


def _token_tile(bsz, tiles_per_seq):
    return lambda i: (i % bsz) * tiles_per_seq + i // bsz


def _in_proj(x2d, norm_w, w_in_bf16, b_gate_row, cosf, sa, sb, qk_w_row, ones_blk, bsz, seq):
    t = x2d.shape[0]
    tile = _token_tile(bsz, seq // IN_TT)
    const = lambda shape: pl.BlockSpec(shape, lambda i: (0, 0))
    rope = lambda: pl.BlockSpec((IN_TT, LANES), lambda i: (i // bsz, 0))
    return pl.pallas_call(
        _in_proj_kernel,
        grid=(t // IN_TT,),
        in_specs=[
            pl.BlockSpec((IN_TT, D_MODEL), lambda i: (tile(i), 0)),
            const((1, D_MODEL)),
            pl.BlockSpec((D_MODEL, IN_WIDTH), lambda i: (0, 0), pipeline_mode=pl.Buffered(1)),
            const((1, IN_WIDTH - COL_G)),
            rope(), rope(), rope(),
            const((1, COL_V - COL_Q)),
            const((2 * LANES, 2 * LANES)),
        ],
        out_specs=[
            pl.BlockSpec((IN_TT, PF_WIDTH), lambda i: (tile(i), 0)),
            pl.BlockSpec((IN_TT, PZ_WIDTH), lambda i: (tile(i), 0)),
            pl.BlockSpec((SSM_WIDTH // LANES, IN_TT * bsz, LANES), lambda i: (0, i // bsz, 0)),
        ],
        out_shape=[
            jax.ShapeDtypeStruct((t, PF_WIDTH), F32),
            jax.ShapeDtypeStruct((t, PZ_WIDTH), BF16),
            jax.ShapeDtypeStruct((SSM_WIDTH // LANES, t, LANES), F32),
        ],
        compiler_params=pltpu.CompilerParams(
            dimension_semantics=("arbitrary",), vmem_limit_bytes=VMEM_LIMIT),
        name="in_proj",
    )(x2d, norm_w, w_in_bf16, b_gate_row, cosf, sa, sb, qk_w_row, ones_blk)


QBLK = 128
KWIN = QBLK + 2 * BAND_HALF
NORM_ROWS = 256


def _attn_kernel(q0_ref, q1_ref, q2_ref, k_ref, v_ref, bias_ref, o_ref,
                 va_s, vb_s, oacc_s, den_s, max_s, s_s):
    seq = k_ref.shape[1]
    q_refs = (q0_ref, q1_ref, q2_ref)
    head0 = lax.broadcasted_iota(jnp.int32, (QBLK, LANES), 1) < HEAD_DIM
    head0_rows = lax.broadcasted_iota(jnp.int32, (NORM_ROWS, LANES), 1) < HEAD_DIM

    def prep(i, carry):
        rows = pl.ds(pl.multiple_of(i * NORM_ROWS, NORM_ROWS), NORM_ROWS)
        v = v_ref[0, rows, :]
        va_s[rows, :] = jnp.where(head0_rows, v, 1.0)
        vb_s[rows, :] = jnp.where(head0_rows, 1.0, v)
        return carry

    lax.fori_loop(0, seq // NORM_ROWS, prep, 0)

    for p, d in enumerate(DILATIONS):
        n = seq // d
        nblk = n // QBLK
        kw = min(n, KWIN)

        def rows_of(idx, d=d, n=n, nblk=nblk, kw=kw):
            r = idx // nblk
            q0 = (idx % nblk) * QBLK
            ks = jnp.clip(q0 - BAND_HALF, 0, n - kw)
            return (pl.ds(r + d * q0, QBLK, stride=d), pl.ds(r + d * ks, kw, stride=d),
                    (q0 - ks) // BAND_HALF)

        def scores(idx, slot, p=p, kw=kw):
            qrows, krows, case = rows_of(idx)
            qb = q_refs[p][0, qrows, :]
            kb = k_ref[0, krows, :].astype(BF16)
            bias = bias_ref[case, :, :kw]
            s_s[slot, :QBLK, :kw] = _dot_nt(jnp.where(head0, qb, 0.0).astype(BF16), kb) + bias
            s_s[slot, QBLK:, :kw] = _dot_nt(jnp.where(head0, 0.0, qb).astype(BF16), kb) + bias

        def weigh(idx, slot, p=p, kw=kw):
            qrows, krows, _ = rows_of(idx)
            s = s_s[slot, :, :kw]
            m = jnp.max(s, axis=-1, keepdims=True)
            e = jnp.exp2(s - m).astype(BF16)
            o_a = _dot(e[:QBLK], va_s[krows, :].astype(BF16))
            o_b = _dot(e[QBLK:], vb_s[krows, :].astype(BF16))
            oacc_s[p, qrows, :] = jnp.where(head0, o_a, o_b)
            den_s[p, qrows, :] = jnp.where(head0, o_b, o_a)
            max_s[p, qrows, :] = jnp.where(head0, jnp.broadcast_to(m[:QBLK], (QBLK, LANES)),
                                           jnp.broadcast_to(m[QBLK:], (QBLK, LANES)))

        def pair(i0, cur, ahead=True):
            nxt = 2 - cur
            if ahead:
                scores(i0 + 2, nxt)
            weigh(i0, cur)
            if ahead:
                scores(i0 + 3, nxt + 1)
            weigh(i0 + 1, cur + 1)

        def quad(j, carry):
            pair(4 * j, 0)
            pair(4 * j + 2, 2)
            return carry

        n_quads = d * nblk // 4
        scores(0, 0)
        scores(1, 1)
        lax.fori_loop(0, n_quads - 1, quad, 0)
        pair(4 * n_quads - 4, 0)
        pair(4 * n_quads - 2, 2, ahead=False)

    def combine(i, carry):
        rows = pl.ds(pl.multiple_of(i * NORM_ROWS, NORM_ROWS), NORM_ROWS)
        m0, m1, m2 = max_s[0, rows, :], max_s[1, rows, :], max_s[2, rows, :]
        m = jnp.maximum(jnp.maximum(m0, m1), m2)
        w0, w1, w2 = jnp.exp2(m0 - m), jnp.exp2(m1 - m), jnp.exp2(m2 - m)
        num = w0 * oacc_s[0, rows, :] + w1 * oacc_s[1, rows, :] + w2 * oacc_s[2, rows, :]
        d0, d1, d2 = (pltpu.roll(den_s[p, rows, :], HEAD_DIM, 1) for p in range(3))
        o_ref[0, rows, :] = num / (w0 * d0 + w1 * d1 + w2 * d2)
        return carry

    lax.fori_loop(0, seq // NORM_ROWS, combine, 0)


def _band_bias():
    i = jnp.arange(QBLK)[:, None]
    j = jnp.arange(KWIN)[None, :]
    off = jnp.arange(3)[:, None, None] * BAND_HALF
    return jnp.where(jnp.abs(j - i - off) <= BAND_HALF, 0.0, NEG_INF).astype(F32)


def _attention(proj3d):
    bsz, seq, _ = proj3d.shape
    n_pairs = ATTN_WIDTH // LANES

    def qspec(p):
        return pl.BlockSpec((1, seq, LANES),
                            lambda b, hp, p=p: (b, 0, PF_Q // LANES + p * n_pairs + hp))

    seq_tile = lambda: pltpu.VMEM((seq, LANES), F32)
    pat_tile = lambda: pltpu.VMEM((len(DILATIONS), seq, LANES), F32)
    return pl.pallas_call(
        _attn_kernel,
        grid=(bsz, n_pairs),
        in_specs=[
            qspec(0), qspec(1), qspec(2),
            pl.BlockSpec((1, seq, LANES), lambda b, hp: (b, 0, PF_K // LANES + hp)),
            pl.BlockSpec((1, seq, LANES), lambda b, hp: (b, 0, PF_V // LANES + hp)),
            pl.BlockSpec((3, QBLK, KWIN), lambda b, hp: (0, 0, 0)),
        ],
        out_specs=pl.BlockSpec((1, seq, LANES), lambda b, hp: (b, 0, hp)),
        out_shape=jax.ShapeDtypeStruct((bsz, seq, ATTN_WIDTH), F32),
        scratch_shapes=[
            seq_tile(), seq_tile(),
            pat_tile(), pat_tile(), pat_tile(),
            pltpu.VMEM((4, 2 * QBLK, KWIN), F32),
        ],
        compiler_params=pltpu.CompilerParams(
            dimension_semantics=("arbitrary", "arbitrary"), vmem_limit_bytes=VMEM_LIMIT),
        name="dilated_attention",
    )(proj3d, proj3d, proj3d, proj3d, proj3d, _band_bias())


GP = 2
N_PAIRS = SSM_GROUPS // GP
PG = GP * SSM_GROUP
PW = CHUNK * PG
SW = 4 * LANES
LAG_REP = LANES // PG


def _s5_prep_kernel(lre_ref, lim_ref, ldt_ref, bre_ref, bim_ref, cre_ref, cim_ref,
                    toep_ref, win_ref, wout_ref, a_ref, wf32_s):
    four = lambda ref: jnp.concatenate([ref[0, 0], ref[0, 0], ref[1, 0], ref[1, 0]], axis=1)
    lr = jnp.minimum(four(lre_ref), -1e-4)
    li = four(lim_ref)
    dt = jnp.exp(four(ldt_ref))
    col = lax.broadcasted_iota(jnp.int32, (1, SW), 1)
    is_re = (col // LANES) % 2 == 0
    is_fwd = col < 2 * LANES

    def power(k):
        mag = jnp.exp(k * (lr * dt))
        return mag * jnp.cos(k * (li * dt)), mag * jnp.sin(k * (li * dt))

    a_re, a_im = power(1.0)
    nr, ni, mag2 = a_re - 1.0, a_im, lr * lr + li * li
    f_re, f_im = (nr * lr + ni * li) / mag2, (ni * lr - nr * li) / mag2

    row_g = lax.broadcasted_iota(jnp.int32, (PG, SW), 0) // SSM_GROUP
    col_g = (lax.broadcasted_iota(jnp.int32, (PG, SW), 1) % LANES) // SSM_STATE
    diag = row_g == col_g
    b_re = jnp.where(diag, four(bre_ref), 0.0)
    b_im = jnp.where(diag, four(bim_ref), 0.0)
    c_re = jnp.where(diag, four(cre_ref), 0.0)
    c_im = jnp.where(diag, four(cim_ref), 0.0)
    bb_re = f_re * b_re - f_im * b_im
    bb_im = f_re * b_im + f_im * b_re

    step = lax.broadcasted_iota(jnp.int32, (CHUNK, SW), 0)
    k_in = jnp.where(is_fwd, CHUNK - 1 - step, step).astype(F32)
    k_out = jnp.where(is_fwd, step + 1, CHUNK - step).astype(F32)
    in_re, in_im = power(k_in)
    out_re, out_im = power(k_out)
    x1, x2 = jnp.where(is_re, in_re, in_im), jnp.where(is_re, -in_im, in_re)
    y1, y2 = jnp.where(is_re, out_re, -out_im), jnp.where(is_re, -out_im, -out_re)
    for s in range(CHUNK):
        rows = slice(s * PG, (s + 1) * PG)
        w = bb_re * x1[s:s + 1] + bb_im * x2[s:s + 1]
        wf32_s[rows, :] = w
        win_ref[0, rows, :] = w.astype(BF16)
        wout_ref[0, rows, :] = (c_re * y1[s:s + 1] + c_im * y2[s:s + 1]).astype(BF16)

    a16_re, a16_im = power(float(CHUNK))
    a_ref[0] = jnp.broadcast_to(jnp.where(is_re, a16_re, a16_im), a_ref.shape[1:])

    c_cat = jnp.where(is_re, c_re, -c_im)
    c_rep = jnp.concatenate([c_cat] * LAG_REP, axis=0)
    half = 2 * LANES
    lag_f = _dot_nt_f32(wf32_s[:, :half], c_rep[:, :half])
    lag_b = _dot_nt_f32(wf32_s[:, half:], c_rep[:, half:])
    row_t = lax.broadcasted_iota(jnp.int32, (PW, LANES), 0) // PG
    lane_q = lax.broadcasted_iota(jnp.int32, (PW, LANES), 1) // PG
    zeros = lambda n: jnp.zeros((n * PG, LANES), F32)
    for q in range(PW // LANES):
        tile = jnp.zeros((PW, LANES), F32)
        for u in range(LAG_REP):
            t = q * LAG_REP + u
            up = CHUNK - 1 - t
            sh_f = lag_f if up == 0 else jnp.concatenate([lag_f[up * PG:], zeros(up)], axis=0)
            sh_b = lag_b if t == 0 else jnp.concatenate([zeros(t), lag_b[:(CHUNK - t) * PG]], axis=0)
            col_t = jnp.where(row_t <= t, sh_f, 0.0) + jnp.where(row_t >= t, sh_b, 0.0)
            tile = jnp.where(lane_q == u, col_t, tile)
        toep_ref[0, :, q * LANES:(q + 1) * LANES] = tile.astype(BF16)


def _s5_prep(lam_re, lam_im, log_dt, b_re, b_im, c_re, c_im):
    lanes = lambda t: t.reshape(2, N_PAIRS, 1, LANES)
    dt_b = jnp.broadcast_to(log_dt[..., None], lam_re.shape)

    def tiles(t_ghp):
        t = jnp.broadcast_to(t_ghp[:, :, :, None, :], t_ghp.shape[:3] + (GP, SSM_STATE))
        return t.reshape(2, N_PAIRS, PG, LANES)

    row = pl.BlockSpec((2, 1, 1, LANES), lambda i: (0, i, 0, 0))
    mat = pl.BlockSpec((2, 1, PG, LANES), lambda i: (0, i, 0, 0))
    out = lambda r: pl.BlockSpec((1, r, SW), lambda i: (i, 0, 0))
    return pl.pallas_call(
        _s5_prep_kernel,
        grid=(N_PAIRS,),
        in_specs=[row, row, row, mat, mat, mat, mat],
        out_specs=[out(PW), out(PW), out(PW), out(8)],
        out_shape=[jax.ShapeDtypeStruct((N_PAIRS, PW, PW), BF16),
                   jax.ShapeDtypeStruct((N_PAIRS, PW, SW), BF16),
                   jax.ShapeDtypeStruct((N_PAIRS, PW, SW), BF16),
                   jax.ShapeDtypeStruct((N_PAIRS, 8, SW), F32)],
        scratch_shapes=[pltpu.VMEM((PW, SW), F32)],
        compiler_params=pltpu.CompilerParams(dimension_semantics=("arbitrary",)),
        name="s5_prep",
    )(lanes(lam_re), lanes(lam_im), lanes(dt_b), tiles(b_re.transpose(0, 1, 3, 2)),
      tiles(b_im.transpose(0, 1, 3, 2)), tiles(c_re), tiles(c_im))


SCAN_UNROLL = 4
GPS = LANES // PG
REGROUP_CHUNKS = 16


def _s5_kernel(u_ref, toep_ref, win_ref, wout_ref, a_ref, y_ref, x_s, p_s, st_s):
    nc = u_ref.shape[1]
    bsz = u_ref.shape[2] // CHUNK
    piece = REGROUP_CHUNKS * bsz
    pp = pl.program_id(1)
    slot = lax.broadcasted_iota(jnp.int32, (piece, LANES), 1) // PG
    time_rows = lambda t: pl.ds(pl.multiple_of(t * bsz, bsz), bsz)

    def gather(cb, carry):
        chunks = pl.ds(pl.multiple_of(cb * REGROUP_CHUNKS, REGROUP_CHUNKS), REGROUP_CHUNKS)
        rows = pl.ds(pl.multiple_of(cb * piece, piece), piece)
        for q in range(CHUNK // GPS):
            acc = None
            for r in range(GPS):
                k = (pp + r) % GPS
                v = u_ref[0, chunks, time_rows(q * GPS + k), :].reshape(piece, LANES)
                v = v if r == 0 else pltpu.roll(v, r * PG, 1)
                acc = v if acc is None else jnp.where(slot == k, v, acc)
            x_s[rows, q * LANES:(q + 1) * LANES] = acc.astype(BF16)
        return carry

    lax.fori_loop(0, nc // REGROUP_CHUNKS, gather, 0)

    x = x_s[...]
    p_s[...] = _dot(x, win_ref[0])
    a_fr, a_fi, a_br, a_bi = (a_ref[0, :, q * LANES:(q + 1) * LANES] for q in range(4))
    tile = lambda q: slice(q * LANES, (q + 1) * LANES)

    def scan(c, carry):
        s_fr, s_fi, s_br, s_bi = carry
        rf = pl.ds(pl.multiple_of(c * bsz, bsz), bsz)
        rb = pl.ds(pl.multiple_of((nc - 1 - c) * bsz, bsz), bsz)
        st_s[rf, tile(0)] = s_fr
        st_s[rf, tile(1)] = s_fi
        st_s[rb, tile(2)] = s_br
        st_s[rb, tile(3)] = s_bi
        return (a_fr * s_fr - a_fi * s_fi + p_s[rf, tile(0)],
                a_fr * s_fi + a_fi * s_fr + p_s[rf, tile(1)],
                a_br * s_br - a_bi * s_bi + p_s[rb, tile(2)],
                a_br * s_bi + a_bi * s_br + p_s[rb, tile(3)])

    zero = jnp.zeros((bsz, LANES), F32)
    lax.fori_loop(0, nc, scan, (zero,) * 4, unroll=SCAN_UNROLL)
    p_s[...] = _dot(x, toep_ref[0]) + _dot_nt(st_s[...].astype(BF16), wout_ref[0])

    @pl.when(pp == 0)
    def _clear():
        y_ref[...] = jnp.zeros_like(y_ref)

    mine = (slot == pp).reshape(REGROUP_CHUNKS, bsz, LANES)

    def scatter(cb, carry):
        chunks = pl.ds(pl.multiple_of(cb * REGROUP_CHUNKS, REGROUP_CHUNKS), REGROUP_CHUNKS)
        rows = pl.ds(pl.multiple_of(cb * piece, piece), piece)
        for q in range(CHUNK // GPS):
            w = p_s[rows, q * LANES:(q + 1) * LANES]
            for r in range(GPS):
                k = (pp + GPS - r) % GPS
                e = w if r == 0 else pltpu.roll(w, r * PG, 1)
                dst = (0, chunks, time_rows(q * GPS + k), slice(None))
                y_ref[dst] = jnp.where(mine, e.reshape(REGROUP_CHUNKS, bsz, LANES), y_ref[dst])
        return carry

    lax.fori_loop(0, nc // REGROUP_CHUNKS, scatter, 0)


def _s5(u_tb, toep, w_in, w_out, a_pow, bsz):
    n_tiles, nc, cb, _ = u_tb.shape
    rows = nc * bsz
    tile = pl.BlockSpec((1, nc, cb, LANES), lambda j, p: (j, 0, 0, 0))
    pair = lambda r, c: pl.BlockSpec((1, r, c), lambda j, p: (j * GPS + p, 0, 0))
    return pl.pallas_call(
        _s5_kernel,
        grid=(n_tiles, GPS),
        in_specs=[tile, pair(PW, PW), pair(PW, SW), pair(PW, SW), pair(bsz, SW)],
        out_specs=tile,
        out_shape=jax.ShapeDtypeStruct(u_tb.shape, F32),
        scratch_shapes=[pltpu.VMEM((rows, PW), BF16),
                        pltpu.VMEM((rows, SW), F32),
                        pltpu.VMEM((rows, SW), F32)],
        compiler_params=pltpu.CompilerParams(
            dimension_semantics=("arbitrary", "arbitrary"), vmem_limit_bytes=VMEM_LIMIT),
        name="s5_chunked",
    )(u_tb, toep, w_in, w_out, a_pow)


TAIL_TT = 512


def _tail_kernel(x_ref, ytb_ref, u_ref, d_ref, za_ref, at_ref, zb_ref, g_ref, wg_ref, bg_ref,
                 wps_ref, wpa_ref, wo_ref, o_ref):
    bsz = ytb_ref.shape[1] // TAIL_TT
    batch = pl.program_id(0) % bsz
    y_s5 = jnp.concatenate([ytb_ref[lt, pl.ds(batch, TAIL_TT, stride=bsz), :]
                            for lt in range(SSM_WIDTH // LANES)], axis=1)
    ys = jax.nn.gelu(y_s5 + d_ref[...] * u_ref[...]).astype(BF16)
    glu = _dot(ys, wg_ref[...]) + bg_ref[...]
    a_in = (glu[:, :SSM_WIDTH] * jax.nn.sigmoid(glu[:, SSM_WIDTH:])
            * jax.nn.silu(za_ref[...].astype(F32)))
    y_a = _dot(a_in.astype(BF16), wps_ref[...])
    y_b = _dot((at_ref[...] * jax.nn.silu(zb_ref[...].astype(F32))).astype(BF16), wpa_ref[...])
    g = g_ref[...].astype(F32)
    mix = jax.nn.sigmoid(g[:, :D_MODEL]) * y_a + jax.nn.sigmoid(g[:, D_MODEL:]) * y_b
    o_ref[...] = x_ref[...] + _dot(mix.astype(BF16), wo_ref[...])


def _tail(x2d, y_tb, d_row, attn2d, pf2d, pz2d, w_glu, b_glu, w_ps, w_pa, w_out, bsz, seq):
    t = x2d.shape[0]
    tile = _token_tile(bsz, seq // TAIL_TT)
    row = lambda w, c: pl.BlockSpec((TAIL_TT, w), lambda i, c=c: (tile(i), c))
    const = lambda shape: pl.BlockSpec(shape, lambda i: (0, 0))
    return pl.pallas_call(
        _tail_kernel,
        grid=(t // TAIL_TT,),
        in_specs=[
            row(D_MODEL, 0),
            pl.BlockSpec((SSM_WIDTH // LANES, TAIL_TT * bsz, LANES), lambda i: (0, i // bsz, 0)),
            row(SSM_WIDTH, PF_U // SSM_WIDTH), const((1, SSM_WIDTH)),
            row(SSM_WIDTH, PZ_ZA // SSM_WIDTH),
            row(ATTN_WIDTH, 0),
            row(ATTN_WIDTH, PZ_ZB // ATTN_WIDTH),
            row(2 * D_MODEL, PZ_G // (2 * D_MODEL)),
            const((SSM_WIDTH, 2 * SSM_WIDTH)), const((1, 2 * SSM_WIDTH)),
            const((SSM_WIDTH, D_MODEL)), const((ATTN_WIDTH, D_MODEL)), const((D_MODEL, D_MODEL)),
        ],
        out_specs=pl.BlockSpec((TAIL_TT, D_MODEL), lambda i: (tile(i), 0)),
        out_shape=jax.ShapeDtypeStruct((t, D_MODEL), F32),
        compiler_params=pltpu.CompilerParams(
            dimension_semantics=("arbitrary",), vmem_limit_bytes=VMEM_LIMIT),
        name="tail",
    )(x2d, y_tb, pf2d, d_row, pz2d, attn2d, pz2d, pz2d, w_glu, b_glu, w_ps, w_pa, w_out)


def _rope_tables(seq):
    half = ROPE_DIM // 2
    inv = ROPE_THETA ** (-jnp.arange(0, ROPE_DIM, 2, dtype=F32) / ROPE_DIM)
    ang = jnp.arange(seq, dtype=F32)[:, None] * inv[None, :]
    cos, sin = jnp.cos(ang), jnp.sin(ang)
    zeros = jnp.zeros((seq, HEAD_DIM - ROPE_DIM), F32)
    z8 = jnp.zeros((seq, half), F32)
    cos_h = jnp.concatenate([cos, cos, jnp.ones_like(zeros)], axis=1)
    sa_h = jnp.concatenate([-sin, z8, zeros], axis=1)
    sb_h = jnp.concatenate([z8, sin, zeros], axis=1)
    two = lambda t: jnp.concatenate([t, t], axis=1)
    return two(cos_h), two(sa_h), two(sb_h)


def kernel(x, norm_w, w_in, b_gate, q_norm_w, k_norm_w, ssm_lam_re, ssm_lam_im, ssm_log_dt,
           ssm_b_re, ssm_b_im, ssm_c_re, ssm_c_im, ssm_d, w_glu, b_glu,
           w_proj_ssm, w_proj_attn, w_out):
    bsz, seq, _ = x.shape
    depth = norm_w.shape[0]
    cosf, sa, sb = _rope_tables(seq)
    ones_blk = jnp.kron(jnp.eye(2 * LANES // HEAD_DIM, dtype=F32),
                        jnp.full((HEAD_DIM, HEAD_DIM), 1.0 / HEAD_DIM, F32)).astype(BF16)
    for layer in range(depth):
        x2d = x.reshape(bsz * seq, D_MODEL)
        q_gain = q_norm_w[layer].astype(F32) * (LOG2E * HEAD_DIM ** -0.5)
        qk_w_row = jnp.concatenate([jnp.tile(q_gain, len(DILATIONS) * ATTN_SLOTS),
                                    jnp.tile(k_norm_w[layer].astype(F32), ATTN_SLOTS)])[None, :]
        pf2d, pz2d, u_tb = _in_proj(
            x2d, norm_w[layer][None, :].astype(F32), w_in[layer].astype(BF16),
            b_gate[layer][None, :].astype(F32), cosf, sa, sb, qk_w_row, ones_blk, bsz, seq)
        attn = _attention(pf2d.reshape(bsz, seq, PF_WIDTH))

        toep, s5_in, s5_out, a_pow = _s5_prep(
            ssm_lam_re[layer].astype(F32), ssm_lam_im[layer].astype(F32),
            ssm_log_dt[layer].astype(F32), ssm_b_re[layer].astype(F32),
            ssm_b_im[layer].astype(F32), ssm_c_re[layer].astype(F32),
            ssm_c_im[layer].astype(F32))
        n_tiles = SSM_WIDTH // LANES
        y_tb = _s5(u_tb.reshape(n_tiles, seq // CHUNK, CHUNK * bsz, LANES), toep, s5_in, s5_out,
                   a_pow, bsz)

        out2d = _tail(x2d, y_tb.reshape(n_tiles, seq * bsz, LANES),
                      ssm_d[layer][None, :].astype(F32), attn.reshape(bsz * seq, ATTN_WIDTH),
                      pf2d, pz2d, w_glu[layer].astype(BF16), b_glu[layer][None, :].astype(F32),
                      w_proj_ssm[layer].astype(BF16), w_proj_attn[layer].astype(BF16),
                      w_out[layer].astype(BF16), bsz, seq)
        x = out2d.reshape(bsz, seq, D_MODEL)
    return x
```

```python
import math

import jax
import jax.numpy as jnp
import numpy as np
from jax import lax
from jax.experimental import pallas as pl
from jax.experimental.pallas import tpu as pltpu

F32 = jnp.float32
BF16 = jnp.bfloat16

D_MODEL = 1024
SSM_WIDTH = 512
SSM_GROUP = 16
SSM_GROUPS = 32
SSM_STATE = 64
HEAD_DIM = 64
ATTN_SLOTS = 8
ATTN_WIDTH = 512
DILATIONS = (1, 4, 16)
BAND_HALF = 64
ROPE_THETA = 500000.0
ROPE_DIM = 16
EPS = 1e-6
NEG_INF = -1e30
IN_WIDTH = 6144
COL_U, COL_ZA, COL_Q, COL_K, COL_V, COL_ZB, COL_G = 0, 512, 1024, 2560, 3072, 3584, 4096

LANES = 128
VMEM_LIMIT = 56 * 1024 * 1024

CHUNK = 16


def _dot(a, b):
    return jnp.dot(a, b, preferred_element_type=F32)


def _dot_nt(a, b):
    return lax.dot_general(a, b, (((1,), (1,)), ((), ())), preferred_element_type=F32)


def _dot_nt_f32(a, b):
    a_hi, b_hi = a.astype(BF16), b.astype(BF16)
    a_lo = (a - a_hi.astype(F32)).astype(BF16)
    b_lo = (b - b_hi.astype(F32)).astype(BF16)
    return _dot_nt(a_hi, b_hi) + (_dot_nt(a_hi, b_lo) + _dot_nt(a_lo, b_hi))


IN_TT = 512
IN_TN = 512
QK_ROWS = 256
LOG2E = math.log2(math.e)
PF_U, PF_Q, PF_K, PF_V, PF_WIDTH = 0, 512, 2048, 2560, 3072
PZ_G, PZ_ZA, PZ_ZB, PZ_WIDTH = 0, 2048, 2560, 3072
_Q_COLS = tuple((COL_Q + o, PF_Q + o) for o in range(0, COL_K - COL_Q, IN_TN))
_G_COLS = tuple((COL_G + o, PZ_G + o) for o in range(0, IN_WIDTH - COL_G, IN_TN))
F32_DEST = dict(((COL_U, PF_U), (COL_K, PF_K), (COL_V, PF_V)) + _Q_COLS)
BF16_DEST = dict(((COL_ZA, PZ_ZA), (COL_ZB, PZ_ZB)) + _G_COLS)


def _in_proj_kernel(x_ref, nw_ref, w_ref, b_ref, cos_ref, sa_ref, sb_ref, qkw_ref, ones_ref,
                    pf_ref, pz_ref, utb_ref):
    bsz = utb_ref.shape[1] // IN_TT
    batch = pl.program_id(0) % bsz
    x = x_ref[...]
    var = jnp.mean(x * x, axis=-1, keepdims=True)
    h = (x * lax.rsqrt(var + EPS) * nw_ref[...]).astype(BF16)
    ones_blk = ones_ref[...]
    for c0 in range(0, IN_WIDTH, IN_TN):
        acc = _dot(h, w_ref[:, c0:c0 + IN_TN])
        if c0 in BF16_DEST:
            if c0 >= COL_G:
                acc = acc + b_ref[:, c0 - COL_G:c0 - COL_G + IN_TN]
            pz_ref[:, BF16_DEST[c0]:BF16_DEST[c0] + IN_TN] = acc.astype(BF16)
            continue
        d0 = F32_DEST[c0]
        if c0 == COL_U:
            for lt in range(SSM_WIDTH // LANES):
                utb_ref[lt, pl.ds(batch, IN_TT, stride=bsz), :] = acc[:, lt * LANES:(lt + 1) * LANES]
        if not COL_Q <= c0 < COL_V:
            pf_ref[:, d0:d0 + IN_TN] = acc
            continue
        for r0 in range(0, IN_TT, QK_ROWS):
            rows = slice(r0, r0 + QK_ROWS)
            for t2 in range(0, IN_TN, 2 * LANES):
                xq2 = acc[rows, t2:t2 + 2 * LANES]
                ms2 = _dot((xq2 * xq2).astype(BF16), ones_blk)
                xn2 = xq2 * lax.rsqrt(ms2 + EPS) * qkw_ref[:, c0 - COL_Q + t2:
                                                            c0 - COL_Q + t2 + 2 * LANES]
                for t in (0, LANES):
                    xn = xn2[:, t:t + LANES]
                    pf_ref[rows, d0 + t2 + t:d0 + t2 + t + LANES] = (
                        xn * cos_ref[rows, :]
                        + pltpu.roll(xn, LANES - ROPE_DIM // 2, 1) * sa_ref[rows, :]
                        + pltpu.roll(xn, ROPE_DIM // 2, 1) * sb_ref[rows, :])


def _token_tile(bsz, tiles_per_seq):
    return lambda i: (i % bsz) * tiles_per_seq + i // bsz


def _in_proj(x2d, norm_w, w_in_bf16, b_gate_row, cosf, sa, sb, qk_w_row, ones_blk, bsz, seq):
    t = x2d.shape[0]
    tile = _token_tile(bsz, seq // IN_TT)
    const = lambda shape: pl.BlockSpec(shape, lambda i: (0, 0))
    rope = lambda: pl.BlockSpec((IN_TT, LANES), lambda i: (i // bsz, 0))
    return pl.pallas_call(
        _in_proj_kernel,
        grid=(t // IN_TT,),
        in_specs=[
            pl.BlockSpec((IN_TT, D_MODEL), lambda i: (tile(i), 0)),
            const((1, D_MODEL)),
            pl.BlockSpec((D_MODEL, IN_WIDTH), lambda i: (0, 0), pipeline_mode=pl.Buffered(1)),
            const((1, IN_WIDTH - COL_G)),
            rope(), rope(), rope(),
            const((1, COL_V - COL_Q)),
            const((2 * LANES, 2 * LANES)),
        ],
        out_specs=[
            pl.BlockSpec((IN_TT, PF_WIDTH), lambda i: (tile(i), 0)),
            pl.BlockSpec((IN_TT, PZ_WIDTH), lambda i: (tile(i), 0)),
            pl.BlockSpec((SSM_WIDTH // LANES, IN_TT * bsz, LANES), lambda i: (0, i // bsz, 0)),
        ],
        out_shape=[
            jax.ShapeDtypeStruct((t, PF_WIDTH), F32),
            jax.ShapeDtypeStruct((t, PZ_WIDTH), BF16),
            jax.ShapeDtypeStruct((SSM_WIDTH // LANES, t, LANES), F32),
        ],
        compiler_params=pltpu.CompilerParams(
            dimension_semantics=("arbitrary",), vmem_limit_bytes=VMEM_LIMIT),
        name="in_proj",
    )(x2d, norm_w, w_in_bf16, b_gate_row, cosf, sa, sb, qk_w_row, ones_blk)


QBLK = 128
KWIN = QBLK + 2 * BAND_HALF
NORM_ROWS = 256


def _attn_kernel(q0_ref, q1_ref, q2_ref, k_ref, v_ref, bias_ref, o_ref,
                 va_s, vb_s, oacc_s, den_s, max_s, s_s):
    seq = k_ref.shape[1]
    q_refs = (q0_ref, q1_ref, q2_ref)
    head0 = lax.broadcasted_iota(jnp.int32, (QBLK, LANES), 1) < HEAD_DIM
    head0_rows = lax.broadcasted_iota(jnp.int32, (NORM_ROWS, LANES), 1) < HEAD_DIM

    def prep(i, carry):
        rows = pl.ds(pl.multiple_of(i * NORM_ROWS, NORM_ROWS), NORM_ROWS)
        v = v_ref[0, rows, :]
        va_s[rows, :] = jnp.where(head0_rows, v, 1.0)
        vb_s[rows, :] = jnp.where(head0_rows, 1.0, v)
        return carry

    lax.fori_loop(0, seq // NORM_ROWS, prep, 0)

    for p, d in enumerate(DILATIONS):
        n = seq // d
        nblk = n // QBLK
        kw = min(n, KWIN)

        def rows_of(idx, d=d, n=n, nblk=nblk, kw=kw):
            r = idx // nblk
            q0 = (idx % nblk) * QBLK
            ks = jnp.clip(q0 - BAND_HALF, 0, n - kw)
            return (pl.ds(r + d * q0, QBLK, stride=d), pl.ds(r + d * ks, kw, stride=d),
                    (q0 - ks) // BAND_HALF)

        def scores(idx, slot, p=p, kw=kw):
            qrows, krows, case = rows_of(idx)
            qb = q_refs[p][0, qrows, :]
            kb = k_ref[0, krows, :].astype(BF16)
            bias = bias_ref[case, :, :kw]
            s_s[slot, :QBLK, :kw] = _dot_nt(jnp.where(head0, qb, 0.0).astype(BF16), kb) + bias
            s_s[slot, QBLK:, :kw] = _dot_nt(jnp.where(head0, 0.0, qb).astype(BF16), kb) + bias

        def weigh(idx, slot, p=p, kw=kw):
            qrows, krows, _ = rows_of(idx)
            s = s_s[slot, :, :kw]
            m = jnp.max(s, axis=-1, keepdims=True)
            e = jnp.exp2(s - m).astype(BF16)
            o_a = _dot(e[:QBLK], va_s[krows, :].astype(BF16))
            o_b = _dot(e[QBLK:], vb_s[krows, :].astype(BF16))
            oacc_s[p, qrows, :] = jnp.where(head0, o_a, o_b)
            den_s[p, qrows, :] = jnp.where(head0, o_b, o_a)
            max_s[p, qrows, :] = jnp.where(head0, jnp.broadcast_to(m[:QBLK], (QBLK, LANES)),
                                           jnp.broadcast_to(m[QBLK:], (QBLK, LANES)))

        def pair(i0, cur, ahead=True):
            nxt = 2 - cur
            if ahead:
                scores(i0 + 2, nxt)
            weigh(i0, cur)
            if ahead:
                scores(i0 + 3, nxt + 1)
            weigh(i0 + 1, cur + 1)

        def quad(j, carry):
            pair(4 * j, 0)
            pair(4 * j + 2, 2)
            return carry

        n_quads = d * nblk // 4
        scores(0, 0)
        scores(1, 1)
        lax.fori_loop(0, n_quads - 1, quad, 0)
        pair(4 * n_quads - 4, 0)
        pair(4 * n_quads - 2, 2, ahead=False)

    def combine(i, carry):
        rows = pl.ds(pl.multiple_of(i * NORM_ROWS, NORM_ROWS), NORM_ROWS)
        m0, m1, m2 = max_s[0, rows, :], max_s[1, rows, :], max_s[2, rows, :]
        m = jnp.maximum(jnp.maximum(m0, m1), m2)
        w0, w1, w2 = jnp.exp2(m0 - m), jnp.exp2(m1 - m), jnp.exp2(m2 - m)
        num = w0 * oacc_s[0, rows, :] + w1 * oacc_s[1, rows, :] + w2 * oacc_s[2, rows, :]
        d0, d1, d2 = (pltpu.roll(den_s[p, rows, :], HEAD_DIM, 1) for p in range(3))
        o_ref[0, rows, :] = num / (w0 * d0 + w1 * d1 + w2 * d2)
        return carry

    lax.fori_loop(0, seq // NORM_ROWS, combine, 0)


def _band_bias():
    i = np.arange(QBLK)[:, None]
    j = np.arange(KWIN)[None, :]
    off = np.arange(3)[:, None, None] * BAND_HALF
    return np.where(np.abs(j - i - off) <= BAND_HALF, 0.0, NEG_INF).astype(np.float32)


def _attention(proj3d):
    bsz, seq, _ = proj3d.shape
    n_pairs = ATTN_WIDTH // LANES

    def qspec(p):
        return pl.BlockSpec((1, seq, LANES),
                            lambda b, hp, p=p: (b, 0, PF_Q // LANES + p * n_pairs + hp))

    seq_tile = lambda: pltpu.VMEM((seq, LANES), F32)
    pat_tile = lambda: pltpu.VMEM((len(DILATIONS), seq, LANES), F32)
    return pl.pallas_call(
        _attn_kernel,
        grid=(bsz, n_pairs),
        in_specs=[
            qspec(0), qspec(1), qspec(2),
            pl.BlockSpec((1, seq, LANES), lambda b, hp: (b, 0, PF_K // LANES + hp)),
            pl.BlockSpec((1, seq, LANES), lambda b, hp: (b, 0, PF_V // LANES + hp)),
            pl.BlockSpec((3, QBLK, KWIN), lambda b, hp: (0, 0, 0)),
        ],
        out_specs=pl.BlockSpec((1, seq, LANES), lambda b, hp: (b, 0, hp)),
        out_shape=jax.ShapeDtypeStruct((bsz, seq, ATTN_WIDTH), F32),
        scratch_shapes=[
            seq_tile(), seq_tile(),
            pat_tile(), pat_tile(), pat_tile(),
            pltpu.VMEM((4, 2 * QBLK, KWIN), F32),
        ],
        compiler_params=pltpu.CompilerParams(
            dimension_semantics=("arbitrary", "arbitrary"), vmem_limit_bytes=VMEM_LIMIT),
        name="dilated_attention",
    )(proj3d, proj3d, proj3d, proj3d, proj3d, _band_bias())


GP = 2
N_PAIRS = SSM_GROUPS // GP
PG = GP * SSM_GROUP
PW = CHUNK * PG
SW = 4 * LANES
LAG_REP = LANES // PG


def _s5_prep_kernel(lre_ref, lim_ref, ldt_ref, bre_ref, bim_ref, cre_ref, cim_ref,
                    toep_ref, win_ref, wout_ref, a_ref, wf32_s):
    four = lambda ref: jnp.concatenate([ref[0, 0], ref[0, 0], ref[1, 0], ref[1, 0]], axis=1)
    lr = jnp.minimum(four(lre_ref), -1e-4)
    li = four(lim_ref)
    dt = jnp.exp(four(ldt_ref))
    col = lax.broadcasted_iota(jnp.int32, (1, SW), 1)
    is_re = (col // LANES) % 2 == 0
    is_fwd = col < 2 * LANES

    def power(k):
        mag = jnp.exp(k * (lr * dt))
        return mag * jnp.cos(k * (li * dt)), mag * jnp.sin(k * (li * dt))

    a_re, a_im = power(1.0)
    nr, ni, mag2 = a_re - 1.0, a_im, lr * lr + li * li
    f_re, f_im = (nr * lr + ni * li) / mag2, (ni * lr - nr * li) / mag2

    row_g = lax.broadcasted_iota(jnp.int32, (PG, SW), 0) // SSM_GROUP
    col_g = (lax.broadcasted_iota(jnp.int32, (PG, SW), 1) % LANES) // SSM_STATE
    diag = row_g == col_g
    b_re = jnp.where(diag, four(bre_ref), 0.0)
    b_im = jnp.where(diag, four(bim_ref), 0.0)
    c_re = jnp.where(diag, four(cre_ref), 0.0)
    c_im = jnp.where(diag, four(cim_ref), 0.0)
    bb_re = f_re * b_re - f_im * b_im
    bb_im = f_re * b_im + f_im * b_re

    step = lax.broadcasted_iota(jnp.int32, (CHUNK, SW), 0)
    k_in = jnp.where(is_fwd, CHUNK - 1 - step, step).astype(F32)
    k_out = jnp.where(is_fwd, step + 1, CHUNK - step).astype(F32)
    in_re, in_im = power(k_in)
    out_re, out_im = power(k_out)
    x1, x2 = jnp.where(is_re, in_re, in_im), jnp.where(is_re, -in_im, in_re)
    y1, y2 = jnp.where(is_re, out_re, -out_im), jnp.where(is_re, -out_im, -out_re)
    for s in range(CHUNK):
        rows = slice(s * PG, (s + 1) * PG)
        w = bb_re * x1[s:s + 1] + bb_im * x2[s:s + 1]
        wf32_s[rows, :] = w
        win_ref[0, rows, :] = w.astype(BF16)
        wout_ref[0, rows, :] = (c_re * y1[s:s + 1] + c_im * y2[s:s + 1]).astype(BF16)

    a16_re, a16_im = power(float(CHUNK))
    a_ref[0] = jnp.broadcast_to(jnp.where(is_re, a16_re, a16_im), a_ref.shape[1:])

    c_cat = jnp.where(is_re, c_re, -c_im)
    c_rep = jnp.concatenate([c_cat] * LAG_REP, axis=0)
    half = 2 * LANES
    lag_f = _dot_nt_f32(wf32_s[:, :half], c_rep[:, :half])
    lag_b = _dot_nt_f32(wf32_s[:, half:], c_rep[:, half:])
    row_t = lax.broadcasted_iota(jnp.int32, (PW, LANES), 0) // PG
    lane_q = lax.broadcasted_iota(jnp.int32, (PW, LANES), 1) // PG
    zeros = lambda n: jnp.zeros((n * PG, LANES), F32)
    for q in range(PW // LANES):
        tile = jnp.zeros((PW, LANES), F32)
        for u in range(LAG_REP):
            t = q * LAG_REP + u
            up = CHUNK - 1 - t
            sh_f = lag_f if up == 0 else jnp.concatenate([lag_f[up * PG:], zeros(up)], axis=0)
            sh_b = lag_b if t == 0 else jnp.concatenate([zeros(t), lag_b[:(CHUNK - t) * PG]], axis=0)
            col_t = jnp.where(row_t <= t, sh_f, 0.0) + jnp.where(row_t >= t, sh_b, 0.0)
            tile = jnp.where(lane_q == u, col_t, tile)
        toep_ref[0, :, q * LANES:(q + 1) * LANES] = tile.astype(BF16)


def _s5_prep(lam_re, lam_im, log_dt, b_re, b_im, c_re, c_im):
    lanes = lambda t: t.reshape(2, N_PAIRS, 1, LANES)
    dt_b = jnp.broadcast_to(log_dt[..., None], lam_re.shape)

    def tiles(t_ghp):
        t = jnp.broadcast_to(t_ghp[:, :, :, None, :], t_ghp.shape[:3] + (GP, SSM_STATE))
        return t.reshape(2, N_PAIRS, PG, LANES)

    row = pl.BlockSpec((2, 1, 1, LANES), lambda i: (0, i, 0, 0))
    mat = pl.BlockSpec((2, 1, PG, LANES), lambda i: (0, i, 0, 0))
    out = lambda r: pl.BlockSpec((1, r, SW), lambda i: (i, 0, 0))
    return pl.pallas_call(
        _s5_prep_kernel,
        grid=(N_PAIRS,),
        in_specs=[row, row, row, mat, mat, mat, mat],
        out_specs=[out(PW), out(PW), out(PW), out(8)],
        out_shape=[jax.ShapeDtypeStruct((N_PAIRS, PW, PW), BF16),
                   jax.ShapeDtypeStruct((N_PAIRS, PW, SW), BF16),
                   jax.ShapeDtypeStruct((N_PAIRS, PW, SW), BF16),
                   jax.ShapeDtypeStruct((N_PAIRS, 8, SW), F32)],
        scratch_shapes=[pltpu.VMEM((PW, SW), F32)],
        compiler_params=pltpu.CompilerParams(dimension_semantics=("arbitrary",)),
        name="s5_prep",
    )(lanes(lam_re), lanes(lam_im), lanes(dt_b), tiles(b_re.transpose(0, 1, 3, 2)),
      tiles(b_im.transpose(0, 1, 3, 2)), tiles(c_re), tiles(c_im))


SCAN_UNROLL = 4
GPS = LANES // PG
REGROUP_CHUNKS = 16


def _s5_kernel(u_ref, toep_ref, win_ref, wout_ref, a_ref, y_ref, x_s, p_s, st_s):
    nc = u_ref.shape[1]
    bsz = u_ref.shape[2] // CHUNK
    piece = REGROUP_CHUNKS * bsz
    pp = pl.program_id(1)
    slot = lax.broadcasted_iota(jnp.int32, (piece, LANES), 1) // PG
    time_rows = lambda t: pl.ds(pl.multiple_of(t * bsz, bsz), bsz)

    def gather(cb, carry):
        chunks = pl.ds(pl.multiple_of(cb * REGROUP_CHUNKS, REGROUP_CHUNKS), REGROUP_CHUNKS)
        rows = pl.ds(pl.multiple_of(cb * piece, piece), piece)
        for q in range(CHUNK // GPS):
            acc = None
            for r in range(GPS):
                k = (pp + r) % GPS
                v = u_ref[0, chunks, time_rows(q * GPS + k), :].reshape(piece, LANES)
                v = v if r == 0 else pltpu.roll(v, r * PG, 1)
                acc = v if acc is None else jnp.where(slot == k, v, acc)
            x_s[rows, q * LANES:(q + 1) * LANES] = acc.astype(BF16)
        return carry

    lax.fori_loop(0, nc // REGROUP_CHUNKS, gather, 0)

    x = x_s[...]
    p_s[...] = _dot(x, win_ref[0])
    a_fr, a_fi, a_br, a_bi = (a_ref[0, :, q * LANES:(q + 1) * LANES] for q in range(4))
    tile = lambda q: slice(q * LANES, (q + 1) * LANES)

    def scan(c, carry):
        s_fr, s_fi, s_br, s_bi = carry
        rf = pl.ds(pl.multiple_of(c * bsz, bsz), bsz)
        rb = pl.ds(pl.multiple_of((nc - 1 - c) * bsz, bsz), bsz)
        st_s[rf, tile(0)] = s_fr
        st_s[rf, tile(1)] = s_fi
        st_s[rb, tile(2)] = s_br
        st_s[rb, tile(3)] = s_bi
        return (a_fr * s_fr - a_fi * s_fi + p_s[rf, tile(0)],
                a_fr * s_fi + a_fi * s_fr + p_s[rf, tile(1)],
                a_br * s_br - a_bi * s_bi + p_s[rb, tile(2)],
                a_br * s_bi + a_bi * s_br + p_s[rb, tile(3)])

    zero = jnp.zeros((bsz, LANES), F32)
    lax.fori_loop(0, nc, scan, (zero,) * 4, unroll=SCAN_UNROLL)
    p_s[...] = _dot(x, toep_ref[0]) + _dot_nt(st_s[...].astype(BF16), wout_ref[0])

    mine = (slot == pp).reshape(1, REGROUP_CHUNKS, bsz, LANES)

    def scatter(cb, carry):
        chunks = pl.ds(pl.multiple_of(cb * REGROUP_CHUNKS, REGROUP_CHUNKS), REGROUP_CHUNKS)
        rows = pl.ds(pl.multiple_of(cb * piece, piece), piece)
        for q in range(CHUNK // GPS):
            w = p_s[rows, q * LANES:(q + 1) * LANES]
            for r in range(GPS):
                k = (pp + GPS - r) % GPS
                e = w if r == 0 else pltpu.roll(w, r * PG, 1)
                pltpu.store(y_ref.at[pl.ds(0, 1), chunks, time_rows(q * GPS + k), :],
                            e.reshape(1, REGROUP_CHUNKS, bsz, LANES), mask=mine)
        return carry

    lax.fori_loop(0, nc // REGROUP_CHUNKS, scatter, 0)


def _s5(u_tb, toep, w_in, w_out, a_pow, bsz):
    n_tiles, nc, cb, _ = u_tb.shape
    rows = nc * bsz
    tile = pl.BlockSpec((1, nc, cb, LANES), lambda j, p: (j, 0, 0, 0))
    pair = lambda r, c: pl.BlockSpec((1, r, c), lambda j, p: (j * GPS + p, 0, 0))
    return pl.pallas_call(
        _s5_kernel,
        grid=(n_tiles, GPS),
        in_specs=[tile, pair(PW, PW), pair(PW, SW), pair(PW, SW), pair(bsz, SW)],
        out_specs=tile,
        out_shape=jax.ShapeDtypeStruct(u_tb.shape, F32),
        scratch_shapes=[pltpu.VMEM((rows, PW), BF16),
                        pltpu.VMEM((rows, SW), F32),
                        pltpu.VMEM((rows, SW), F32)],
        compiler_params=pltpu.CompilerParams(
            dimension_semantics=("arbitrary", "arbitrary"), vmem_limit_bytes=VMEM_LIMIT),
        name="s5_chunked",
    )(u_tb, toep, w_in, w_out, a_pow)


TAIL_TT = 512


def _sigmoid(v):
    return 0.5 * jnp.tanh(0.5 * v) + 0.5


def _tail_kernel(x_ref, ytb_ref, u_ref, d_ref, za_ref, at_ref, zb_ref, g_ref, wg_ref, bg_ref,
                 wps_ref, wpa_ref, wo_ref, o_ref):
    bsz = ytb_ref.shape[1] // TAIL_TT
    batch = pl.program_id(0) % bsz
    y_s5 = jnp.concatenate([ytb_ref[lt, pl.ds(batch, TAIL_TT, stride=bsz), :]
                            for lt in range(SSM_WIDTH // LANES)], axis=1)
    ys = jax.nn.gelu(y_s5 + d_ref[...] * u_ref[...]).astype(BF16)
    glu = _dot(ys, wg_ref[...]) + bg_ref[...]
    z_a = za_ref[...].astype(F32)
    a_in = glu[:, :SSM_WIDTH] * _sigmoid(glu[:, SSM_WIDTH:]) * (z_a * _sigmoid(z_a))
    y_a = _dot(a_in.astype(BF16), wps_ref[...])
    z_b = zb_ref[...].astype(F32)
    y_b = _dot((at_ref[...] * (z_b * _sigmoid(z_b))).astype(BF16), wpa_ref[...])
    g = g_ref[...].astype(F32)
    mix = _sigmoid(g[:, :D_MODEL]) * y_a + _sigmoid(g[:, D_MODEL:]) * y_b
    o_ref[...] = x_ref[...] + _dot(mix.astype(BF16), wo_ref[...])


def _tail(x2d, y_tb, d_row, attn2d, pf2d, pz2d, w_glu, b_glu, w_ps, w_pa, w_out, bsz, seq):
    t = x2d.shape[0]
    tile = _token_tile(bsz, seq // TAIL_TT)
    row = lambda w, c: pl.BlockSpec((TAIL_TT, w), lambda i, c=c: (tile(i), c))
    const = lambda shape: pl.BlockSpec(shape, lambda i: (0, 0))
    return pl.pallas_call(
        _tail_kernel,
        grid=(t // TAIL_TT,),
        in_specs=[
            row(D_MODEL, 0),
            pl.BlockSpec((SSM_WIDTH // LANES, TAIL_TT * bsz, LANES), lambda i: (0, i // bsz, 0)),
            row(SSM_WIDTH, PF_U // SSM_WIDTH), const((1, SSM_WIDTH)),
            row(SSM_WIDTH, PZ_ZA // SSM_WIDTH),
            row(ATTN_WIDTH, 0),
            row(ATTN_WIDTH, PZ_ZB // ATTN_WIDTH),
            row(2 * D_MODEL, PZ_G // (2 * D_MODEL)),
            const((SSM_WIDTH, 2 * SSM_WIDTH)), const((1, 2 * SSM_WIDTH)),
            const((SSM_WIDTH, D_MODEL)), const((ATTN_WIDTH, D_MODEL)), const((D_MODEL, D_MODEL)),
        ],
        out_specs=pl.BlockSpec((TAIL_TT, D_MODEL), lambda i: (tile(i), 0)),
        out_shape=jax.ShapeDtypeStruct((t, D_MODEL), F32),
        compiler_params=pltpu.CompilerParams(
            dimension_semantics=("arbitrary",), vmem_limit_bytes=VMEM_LIMIT),
        name="tail",
    )(x2d, y_tb, pf2d, d_row, pz2d, attn2d, pz2d, pz2d, w_glu, b_glu, w_ps, w_pa, w_out)


def _rope_tables(seq):
    half = ROPE_DIM // 2
    inv = ROPE_THETA ** (-np.arange(0, ROPE_DIM, 2, dtype=np.float64) / ROPE_DIM)
    ang = np.arange(seq, dtype=np.float64)[:, None] * inv[None, :]
    cos, sin = np.cos(ang).astype(np.float32), np.sin(ang).astype(np.float32)
    zeros = np.zeros((seq, HEAD_DIM - ROPE_DIM), np.float32)
    z8 = np.zeros((seq, half), np.float32)
    cos_h = np.concatenate([cos, cos, np.ones_like(zeros)], axis=1)
    sa_h = np.concatenate([-sin, z8, zeros], axis=1)
    sb_h = np.concatenate([z8, sin, zeros], axis=1)
    two = lambda t: np.concatenate([t, t], axis=1).astype(np.float32)
    return two(cos_h), two(sa_h), two(sb_h)


def kernel(x, norm_w, w_in, b_gate, q_norm_w, k_norm_w, ssm_lam_re, ssm_lam_im, ssm_log_dt,
           ssm_b_re, ssm_b_im, ssm_c_re, ssm_c_im, ssm_d, w_glu, b_glu,
           w_proj_ssm, w_proj_attn, w_out):
    bsz, seq, _ = x.shape
    depth = norm_w.shape[0]
    cosf, sa, sb = _rope_tables(seq)
    ones_blk = jnp.asarray(np.kron(np.eye(2 * LANES // HEAD_DIM, dtype=np.float32),
                                   np.full((HEAD_DIM, HEAD_DIM), 1.0 / HEAD_DIM, np.float32)), BF16)
    for layer in range(depth):
        x2d = x.reshape(bsz * seq, D_MODEL)
        q_gain = q_norm_w[layer].astype(F32) * (LOG2E * HEAD_DIM ** -0.5)
        qk_w_row = jnp.concatenate([jnp.tile(q_gain, len(DILATIONS) * ATTN_SLOTS),
                                    jnp.tile(k_norm_w[layer].astype(F32), ATTN_SLOTS)])[None, :]
        pf2d, pz2d, u_tb = _in_proj(
            x2d, norm_w[layer][None, :].astype(F32), w_in[layer].astype(BF16),
            b_gate[layer][None, :].astype(F32), cosf, sa, sb, qk_w_row, ones_blk, bsz, seq)
        attn = _attention(pf2d.reshape(bsz, seq, PF_WIDTH))

        toep, s5_in, s5_out, a_pow = _s5_prep(
            ssm_lam_re[layer].astype(F32), ssm_lam_im[layer].astype(F32),
            ssm_log_dt[layer].astype(F32), ssm_b_re[layer].astype(F32),
            ssm_b_im[layer].astype(F32), ssm_c_re[layer].astype(F32),
            ssm_c_im[layer].astype(F32))
        n_tiles = SSM_WIDTH // LANES
        y_tb = _s5(u_tb.reshape(n_tiles, seq // CHUNK, CHUNK * bsz, LANES), toep, s5_in, s5_out,
                   a_pow, bsz)

        out2d = _tail(x2d, y_tb.reshape(n_tiles, seq * bsz, LANES),
                      ssm_d[layer][None, :].astype(F32), attn.reshape(bsz * seq, ATTN_WIDTH),
                      pf2d, pz2d, w_glu[layer].astype(BF16), b_glu[layer][None, :].astype(F32),
                      w_proj_ssm[layer].astype(BF16), w_proj_attn[layer].astype(BF16),
                      w_out[layer].astype(BF16), bsz, seq)
        x = out2d.reshape(bsz, seq, D_MODEL)
    return x
```

```python
import math

import jax
import jax.numpy as jnp
import numpy as np
from jax import lax
from jax.experimental import pallas as pl
from jax.experimental.pallas import tpu as pltpu

F32 = jnp.float32
BF16 = jnp.bfloat16

D_MODEL = 1024
SSM_WIDTH = 512
SSM_GROUP = 16
SSM_GROUPS = 32
SSM_STATE = 64
HEAD_DIM = 64
ATTN_SLOTS = 8
ATTN_WIDTH = 512
DILATIONS = (1, 4, 16)
BAND_HALF = 64
ROPE_THETA = 500000.0
ROPE_DIM = 16
EPS = 1e-6
NEG_INF = -1e30
IN_WIDTH = 6144
COL_U, COL_ZA, COL_Q, COL_K, COL_V, COL_ZB, COL_G = 0, 512, 1024, 2560, 3072, 3584, 4096

LANES = 128
VMEM_LIMIT = 56 * 1024 * 1024

CHUNK = 16


def _dot(a, b):
    return jnp.dot(a, b, preferred_element_type=F32)


def _dot_nt(a, b):
    return lax.dot_general(a, b, (((1,), (1,)), ((), ())), preferred_element_type=F32)


def _dot_nt_f32(a, b):
    a_hi, b_hi = a.astype(BF16), b.astype(BF16)
    a_lo = (a - a_hi.astype(F32)).astype(BF16)
    b_lo = (b - b_hi.astype(F32)).astype(BF16)
    return _dot_nt(a_hi, b_hi) + (_dot_nt(a_hi, b_lo) + _dot_nt(a_lo, b_hi))


IN_TT = 512
IN_TN = 512
QK_ROWS = 256
LOG2E = math.log2(math.e)
PF_Q, PF_K, PF_V, PF_WIDTH = 0, 1536, 2048, 2560
PZ_G, PZ_ZA, PZ_ZB, PZ_WIDTH = 0, 2048, 2560, 3072
_Q_COLS = tuple((COL_Q + o, PF_Q + o) for o in range(0, COL_K - COL_Q, IN_TN))
_G_COLS = tuple((COL_G + o, PZ_G + o) for o in range(0, IN_WIDTH - COL_G, IN_TN))
F32_DEST = dict(((COL_K, PF_K), (COL_V, PF_V)) + _Q_COLS)
BF16_DEST = dict(((COL_ZA, PZ_ZA), (COL_ZB, PZ_ZB)) + _G_COLS)


def _in_proj_kernel(x_ref, nw_ref, w_ref, b_ref, cos_ref, sa_ref, sb_ref, qkw_ref, ones_ref,
                    pf_ref, pz_ref, utb_ref):
    bsz = utb_ref.shape[1] // IN_TT
    batch = pl.program_id(0) % bsz
    x = x_ref[...]
    var = jnp.mean(x * x, axis=-1, keepdims=True)
    h = (x * lax.rsqrt(var + EPS) * nw_ref[...]).astype(BF16)
    ones_blk = ones_ref[...]
    for c0 in range(0, IN_WIDTH, IN_TN):
        acc = _dot(h, w_ref[:, c0:c0 + IN_TN])
        if c0 in BF16_DEST:
            if c0 >= COL_G:
                acc = acc + b_ref[:, c0 - COL_G:c0 - COL_G + IN_TN]
            pz_ref[:, BF16_DEST[c0]:BF16_DEST[c0] + IN_TN] = acc.astype(BF16)
            continue
        if c0 == COL_U:
            for lt in range(SSM_WIDTH // LANES):
                utb_ref[lt, pl.ds(batch, IN_TT, stride=bsz), :] = acc[:, lt * LANES:(lt + 1) * LANES]
            continue
        d0 = F32_DEST[c0]
        if not COL_Q <= c0 < COL_V:
            pf_ref[:, d0:d0 + IN_TN] = acc
            continue
        for r0 in range(0, IN_TT, QK_ROWS):
            rows = slice(r0, r0 + QK_ROWS)
            for t2 in range(0, IN_TN, 2 * LANES):
                xq2 = acc[rows, t2:t2 + 2 * LANES]
                ms2 = _dot((xq2 * xq2).astype(BF16), ones_blk)
                xn2 = xq2 * lax.rsqrt(ms2 + EPS) * qkw_ref[:, c0 - COL_Q + t2:
                                                            c0 - COL_Q + t2 + 2 * LANES]
                for t in (0, LANES):
                    xn = xn2[:, t:t + LANES]
                    pf_ref[rows, d0 + t2 + t:d0 + t2 + t + LANES] = (
                        xn * cos_ref[rows, :]
                        + pltpu.roll(xn, LANES - ROPE_DIM // 2, 1) * sa_ref[rows, :]
                        + pltpu.roll(xn, ROPE_DIM // 2, 1) * sb_ref[rows, :])


def _token_tile(bsz, tiles_per_seq):
    return lambda i: (i % bsz) * tiles_per_seq + i // bsz


def _in_proj(x2d, norm_w, w_in_bf16, b_gate_row, cosf, sa, sb, qk_w_row, ones_blk, bsz, seq):
    t = x2d.shape[0]
    tile = _token_tile(bsz, seq // IN_TT)
    const = lambda shape: pl.BlockSpec(shape, lambda i: (0, 0))
    rope = lambda: pl.BlockSpec((IN_TT, LANES), lambda i: (i // bsz, 0))
    return pl.pallas_call(
        _in_proj_kernel,
        grid=(t // IN_TT,),
        in_specs=[
            pl.BlockSpec((IN_TT, D_MODEL), lambda i: (tile(i), 0)),
            const((1, D_MODEL)),
            pl.BlockSpec((D_MODEL, IN_WIDTH), lambda i: (0, 0), pipeline_mode=pl.Buffered(1)),
            const((1, IN_WIDTH - COL_G)),
            rope(), rope(), rope(),
            const((1, COL_V - COL_Q)),
            const((2 * LANES, 2 * LANES)),
        ],
        out_specs=[
            pl.BlockSpec((IN_TT, PF_WIDTH), lambda i: (tile(i), 0)),
            pl.BlockSpec((IN_TT, PZ_WIDTH), lambda i: (tile(i), 0)),
            pl.BlockSpec((SSM_WIDTH // LANES, IN_TT * bsz, LANES), lambda i: (0, i // bsz, 0)),
        ],
        out_shape=[
            jax.ShapeDtypeStruct((t, PF_WIDTH), F32),
            jax.ShapeDtypeStruct((t, PZ_WIDTH), BF16),
            jax.ShapeDtypeStruct((SSM_WIDTH // LANES, t, LANES), F32),
        ],
        compiler_params=pltpu.CompilerParams(
            dimension_semantics=("arbitrary",), vmem_limit_bytes=VMEM_LIMIT),
        name="in_proj",
    )(x2d, norm_w, w_in_bf16, b_gate_row, cosf, sa, sb, qk_w_row, ones_blk)


QBLK = 128
KWIN = QBLK + 2 * BAND_HALF
NORM_ROWS = 256


def _attn_kernel(q0_ref, q1_ref, q2_ref, k_ref, v_ref, bias_ref, o_ref,
                 va_s, vb_s, oacc_s, den_s, max_s, s_s):
    seq = k_ref.shape[1]
    q_refs = (q0_ref, q1_ref, q2_ref)
    head0 = lax.broadcasted_iota(jnp.int32, (QBLK, LANES), 1) < HEAD_DIM
    head0_rows = lax.broadcasted_iota(jnp.int32, (NORM_ROWS, LANES), 1) < HEAD_DIM

    def prep(i, carry):
        rows = pl.ds(pl.multiple_of(i * NORM_ROWS, NORM_ROWS), NORM_ROWS)
        v = v_ref[0, rows, :]
        va_s[rows, :] = jnp.where(head0_rows, v, 1.0)
        vb_s[rows, :] = jnp.where(head0_rows, 1.0, v)
        return carry

    lax.fori_loop(0, seq // NORM_ROWS, prep, 0)

    patterns = []
    for p, d in enumerate(DILATIONS):
        n = seq // d
        nblk = n // QBLK
        kw = min(n, KWIN)

        def rows_of(idx, d=d, n=n, nblk=nblk, kw=kw):
            r = idx // nblk
            q0 = (idx % nblk) * QBLK
            ks = jnp.clip(q0 - BAND_HALF, 0, n - kw)
            return (pl.ds(r + d * q0, QBLK, stride=d), pl.ds(r + d * ks, kw, stride=d),
                    (q0 - ks) // BAND_HALF)

        def scores(idx, slot, p=p, kw=kw, rows_of=rows_of):
            qrows, krows, case = rows_of(idx)
            qb = q_refs[p][0, qrows, :]
            kb = k_ref[0, krows, :].astype(BF16)
            bias = bias_ref[case, :, :kw]
            s_s[slot, :QBLK, :kw] = _dot_nt(jnp.where(head0, qb, 0.0).astype(BF16), kb) + bias
            s_s[slot, QBLK:, :kw] = _dot_nt(jnp.where(head0, 0.0, qb).astype(BF16), kb) + bias

        def weigh(idx, slot, p=p, kw=kw, rows_of=rows_of):
            qrows, krows, _ = rows_of(idx)
            s = s_s[slot, :, :kw]
            m = jnp.max(s, axis=-1, keepdims=True)
            e = jnp.exp2(s - m).astype(BF16)
            o_a = _dot(e[:QBLK], va_s[krows, :].astype(BF16))
            o_b = _dot(e[QBLK:], vb_s[krows, :].astype(BF16))
            oacc_s[p, qrows, :] = jnp.where(head0, o_a, o_b)
            den_s[p, qrows, :] = jnp.where(head0, o_b, o_a)
            max_s[p, qrows, :] = jnp.where(head0, jnp.broadcast_to(m[:QBLK], (QBLK, LANES)),
                                           jnp.broadcast_to(m[QBLK:], (QBLK, LANES)))

        patterns.append((scores, weigh, d * nblk))

    def pair(weigh, i0, cur, ahead):
        nxt = 2 - cur
        if ahead is not None:
            ahead[0](ahead[1], nxt)
        weigh(i0, cur)
        if ahead is not None:
            ahead[0](ahead[1] + 1, nxt + 1)
        weigh(i0 + 1, cur + 1)

    patterns[0][0](0, 0)
    patterns[0][0](1, 1)
    for which, (scores, weigh, n_blocks) in enumerate(patterns):
        def quad(j, carry, scores=scores, weigh=weigh):
            pair(weigh, 4 * j, 0, (scores, 4 * j + 2))
            pair(weigh, 4 * j + 2, 2, (scores, 4 * j + 4))
            return carry

        lax.fori_loop(0, n_blocks // 4 - 1, quad, 0)
        pair(weigh, n_blocks - 4, 0, (scores, n_blocks - 2))
        following = (patterns[which + 1][0], 0) if which + 1 < len(patterns) else None
        pair(weigh, n_blocks - 2, 2, following)

    def combine(i, carry):
        rows = pl.ds(pl.multiple_of(i * NORM_ROWS, NORM_ROWS), NORM_ROWS)
        m0, m1, m2 = max_s[0, rows, :], max_s[1, rows, :], max_s[2, rows, :]
        m = jnp.maximum(jnp.maximum(m0, m1), m2)
        w0, w1, w2 = jnp.exp2(m0 - m), jnp.exp2(m1 - m), jnp.exp2(m2 - m)
        num = w0 * oacc_s[0, rows, :] + w1 * oacc_s[1, rows, :] + w2 * oacc_s[2, rows, :]
        d0, d1, d2 = (pltpu.roll(den_s[p, rows, :], HEAD_DIM, 1) for p in range(3))
        o_ref[0, rows, :] = (num / (w0 * d0 + w1 * d1 + w2 * d2)).astype(o_ref.dtype)
        return carry

    lax.fori_loop(0, seq // NORM_ROWS, combine, 0)


def _band_bias():
    i = np.arange(QBLK)[:, None]
    j = np.arange(KWIN)[None, :]
    off = np.arange(3)[:, None, None] * BAND_HALF
    return np.where(np.abs(j - i - off) <= BAND_HALF, 0.0, NEG_INF).astype(np.float32)


def _attention(proj3d):
    bsz, seq, _ = proj3d.shape
    n_pairs = ATTN_WIDTH // LANES

    def qspec(p):
        return pl.BlockSpec((1, seq, LANES),
                            lambda b, hp, p=p: (b, 0, PF_Q // LANES + p * n_pairs + hp))

    seq_tile = lambda: pltpu.VMEM((seq, LANES), F32)
    pat_tile = lambda: pltpu.VMEM((len(DILATIONS), seq, LANES), F32)
    return pl.pallas_call(
        _attn_kernel,
        grid=(bsz, n_pairs),
        in_specs=[
            qspec(0), qspec(1), qspec(2),
            pl.BlockSpec((1, seq, LANES), lambda b, hp: (b, 0, PF_K // LANES + hp)),
            pl.BlockSpec((1, seq, LANES), lambda b, hp: (b, 0, PF_V // LANES + hp)),
            pl.BlockSpec((3, QBLK, KWIN), lambda b, hp: (0, 0, 0)),
        ],
        out_specs=pl.BlockSpec((1, seq, LANES), lambda b, hp: (b, 0, hp)),
        out_shape=jax.ShapeDtypeStruct((bsz, seq, ATTN_WIDTH), BF16),
        scratch_shapes=[
            seq_tile(), seq_tile(),
            pat_tile(), pat_tile(), pat_tile(),
            pltpu.VMEM((4, 2 * QBLK, KWIN), F32),
        ],
        compiler_params=pltpu.CompilerParams(
            dimension_semantics=("arbitrary", "arbitrary"), vmem_limit_bytes=VMEM_LIMIT),
        name="dilated_attention",
    )(proj3d, proj3d, proj3d, proj3d, proj3d, _band_bias())


GP = 2
N_PAIRS = SSM_GROUPS // GP
PG = GP * SSM_GROUP
PW = CHUNK * PG
SW = 4 * LANES
LAG_REP = LANES // PG


def _s5_prep_kernel(lre_ref, lim_ref, ldt_ref, bre_ref, bim_ref, cre_ref, cim_ref,
                    toep_ref, win_ref, wout_ref, a_ref, wf32_s):
    four = lambda ref: jnp.concatenate([ref[0, 0], ref[0, 0], ref[1, 0], ref[1, 0]], axis=1)
    lr = jnp.minimum(four(lre_ref), -1e-4)
    li = four(lim_ref)
    dt = jnp.exp(four(ldt_ref))
    col = lax.broadcasted_iota(jnp.int32, (1, SW), 1)
    is_re = (col // LANES) % 2 == 0
    is_fwd = col < 2 * LANES

    def power(k):
        mag = jnp.exp(k * (lr * dt))
        return mag * jnp.cos(k * (li * dt)), mag * jnp.sin(k * (li * dt))

    a_re, a_im = power(1.0)
    nr, ni, mag2 = a_re - 1.0, a_im, lr * lr + li * li
    f_re, f_im = (nr * lr + ni * li) / mag2, (ni * lr - nr * li) / mag2

    row_g = lax.broadcasted_iota(jnp.int32, (PG, SW), 0) // SSM_GROUP
    col_g = (lax.broadcasted_iota(jnp.int32, (PG, SW), 1) % LANES) // SSM_STATE
    diag = row_g == col_g
    b_re = jnp.where(diag, four(bre_ref), 0.0)
    b_im = jnp.where(diag, four(bim_ref), 0.0)
    c_re = jnp.where(diag, four(cre_ref), 0.0)
    c_im = jnp.where(diag, four(cim_ref), 0.0)
    bb_re = f_re * b_re - f_im * b_im
    bb_im = f_re * b_im + f_im * b_re

    step = lax.broadcasted_iota(jnp.int32, (CHUNK, SW), 0)
    k_in = jnp.where(is_fwd, CHUNK - 1 - step, step).astype(F32)
    k_out = jnp.where(is_fwd, step + 1, CHUNK - step).astype(F32)
    in_re, in_im = power(k_in)
    out_re, out_im = power(k_out)
    x1, x2 = jnp.where(is_re, in_re, in_im), jnp.where(is_re, -in_im, in_re)
    y1, y2 = jnp.where(is_re, out_re, -out_im), jnp.where(is_re, -out_im, -out_re)
    for s in range(CHUNK):
        rows = slice(s * PG, (s + 1) * PG)
        w = bb_re * x1[s:s + 1] + bb_im * x2[s:s + 1]
        wf32_s[rows, :] = w
        win_ref[0, rows, :] = w.astype(BF16)
        wout_ref[0, rows, :] = (c_re * y1[s:s + 1] + c_im * y2[s:s + 1]).astype(BF16)

    a16_re, a16_im = power(float(CHUNK))
    a_ref[0] = jnp.broadcast_to(jnp.where(is_re, a16_re, a16_im), a_ref.shape[1:])

    c_cat = jnp.where(is_re, c_re, -c_im)
    c_rep = jnp.concatenate([c_cat] * LAG_REP, axis=0)
    half = 2 * LANES
    lag_f = _dot_nt_f32(wf32_s[:, :half], c_rep[:, :half])
    lag_b = _dot_nt_f32(wf32_s[:, half:], c_rep[:, half:])
    row_t = lax.broadcasted_iota(jnp.int32, (PW, LANES), 0) // PG
    lane_q = lax.broadcasted_iota(jnp.int32, (PW, LANES), 1) // PG
    zeros = lambda n: jnp.zeros((n * PG, LANES), F32)
    for q in range(PW // LANES):
        tile = jnp.zeros((PW, LANES), F32)
        for u in range(LAG_REP):
            t = q * LAG_REP + u
            up = CHUNK - 1 - t
            sh_f = lag_f if up == 0 else jnp.concatenate([lag_f[up * PG:], zeros(up)], axis=0)
            sh_b = lag_b if t == 0 else jnp.concatenate([zeros(t), lag_b[:(CHUNK - t) * PG]], axis=0)
            col_t = jnp.where(row_t <= t, sh_f, 0.0) + jnp.where(row_t >= t, sh_b, 0.0)
            tile = jnp.where(lane_q == u, col_t, tile)
        toep_ref[0, :, q * LANES:(q + 1) * LANES] = tile.astype(BF16)


def _s5_prep(lam_re, lam_im, log_dt, b_re, b_im, c_re, c_im):
    lanes = lambda t: t.reshape(2, N_PAIRS, 1, LANES)
    dt_b = jnp.broadcast_to(log_dt[..., None], lam_re.shape)

    def tiles(t_ghp):
        t = jnp.broadcast_to(t_ghp[:, :, :, None, :], t_ghp.shape[:3] + (GP, SSM_STATE))
        return t.reshape(2, N_PAIRS, PG, LANES)

    row = pl.BlockSpec((2, 1, 1, LANES), lambda i: (0, i, 0, 0))
    mat = pl.BlockSpec((2, 1, PG, LANES), lambda i: (0, i, 0, 0))
    out = lambda r: pl.BlockSpec((1, r, SW), lambda i: (i, 0, 0))
    return pl.pallas_call(
        _s5_prep_kernel,
        grid=(N_PAIRS,),
        in_specs=[row, row, row, mat, mat, mat, mat],
        out_specs=[out(PW), out(PW), out(PW), out(8)],
        out_shape=[jax.ShapeDtypeStruct((N_PAIRS, PW, PW), BF16),
                   jax.ShapeDtypeStruct((N_PAIRS, PW, SW), BF16),
                   jax.ShapeDtypeStruct((N_PAIRS, PW, SW), BF16),
                   jax.ShapeDtypeStruct((N_PAIRS, 8, SW), F32)],
        scratch_shapes=[pltpu.VMEM((PW, SW), F32)],
        compiler_params=pltpu.CompilerParams(dimension_semantics=("arbitrary",)),
        name="s5_prep",
    )(lanes(lam_re), lanes(lam_im), lanes(dt_b), tiles(b_re.transpose(0, 1, 3, 2)),
      tiles(b_im.transpose(0, 1, 3, 2)), tiles(c_re), tiles(c_im))


SCAN_UNROLL = 4
GPS = LANES // PG
REGROUP_CHUNKS = 16


def _s5_kernel(u_ref, d_ref, toep_ref, win_ref, wout_ref, a_ref, y_ref, x_s, p_s, st_s):
    nc = u_ref.shape[1]
    bsz = u_ref.shape[2] // CHUNK
    piece = REGROUP_CHUNKS * bsz
    pp = pl.program_id(1)
    slot = lax.broadcasted_iota(jnp.int32, (piece, LANES), 1) // PG
    time_rows = lambda t: pl.ds(pl.multiple_of(t * bsz, bsz), bsz)

    def gather(cb, carry):
        chunks = pl.ds(pl.multiple_of(cb * REGROUP_CHUNKS, REGROUP_CHUNKS), REGROUP_CHUNKS)
        rows = pl.ds(pl.multiple_of(cb * piece, piece), piece)
        for q in range(CHUNK // GPS):
            acc = None
            for r in range(GPS):
                k = (pp + r) % GPS
                v = u_ref[0, chunks, time_rows(q * GPS + k), :].reshape(piece, LANES)
                v = v if r == 0 else pltpu.roll(v, r * PG, 1)
                acc = v if acc is None else jnp.where(slot == k, v, acc)
            x_s[rows, q * LANES:(q + 1) * LANES] = acc.astype(BF16)
        return carry

    lax.fori_loop(0, nc // REGROUP_CHUNKS, gather, 0)

    x = x_s[...]
    p_s[...] = _dot(x, win_ref[0])
    a_fr, a_fi, a_br, a_bi = (a_ref[0, :, q * LANES:(q + 1) * LANES] for q in range(4))
    tile = lambda q: slice(q * LANES, (q + 1) * LANES)

    def scan(c, carry):
        s_fr, s_fi, s_br, s_bi = carry
        rf = pl.ds(pl.multiple_of(c * bsz, bsz), bsz)
        rb = pl.ds(pl.multiple_of((nc - 1 - c) * bsz, bsz), bsz)
        st_s[rf, tile(0)] = s_fr
        st_s[rf, tile(1)] = s_fi
        st_s[rb, tile(2)] = s_br
        st_s[rb, tile(3)] = s_bi
        return (a_fr * s_fr - a_fi * s_fi + p_s[rf, tile(0)],
                a_fr * s_fi + a_fi * s_fr + p_s[rf, tile(1)],
                a_br * s_br - a_bi * s_bi + p_s[rb, tile(2)],
                a_br * s_bi + a_bi * s_br + p_s[rb, tile(3)])

    zero = jnp.zeros((bsz, LANES), F32)
    lax.fori_loop(0, nc, scan, (zero,) * 4, unroll=SCAN_UNROLL)
    p_s[...] = _dot(x, toep_ref[0]) + _dot_nt(st_s[...].astype(BF16), wout_ref[0])

    mine = (slot == pp).reshape(1, REGROUP_CHUNKS, bsz, LANES)
    d_skip = d_ref[0]

    def scatter(cb, carry):
        chunks = pl.ds(pl.multiple_of(cb * REGROUP_CHUNKS, REGROUP_CHUNKS), REGROUP_CHUNKS)
        rows = pl.ds(pl.multiple_of(cb * piece, piece), piece)
        for q in range(CHUNK // GPS):
            w = p_s[rows, q * LANES:(q + 1) * LANES]
            for r in range(GPS):
                k = (pp + GPS - r) % GPS
                e = w if r == 0 else pltpu.roll(w, r * PG, 1)
                where = (pl.ds(0, 1), chunks, time_rows(q * GPS + k), slice(None))
                e = e.reshape(1, REGROUP_CHUNKS, bsz, LANES) + d_skip * u_ref[where]
                pltpu.store(y_ref.at[where], e, mask=mine)
        return carry

    lax.fori_loop(0, nc // REGROUP_CHUNKS, scatter, 0)


def _s5(u_tb, d_tiles, toep, w_in, w_out, a_pow, bsz):
    n_tiles, nc, cb, _ = u_tb.shape
    rows = nc * bsz
    tile = pl.BlockSpec((1, nc, cb, LANES), lambda j, p: (j, 0, 0, 0))
    pair = lambda r, c: pl.BlockSpec((1, r, c), lambda j, p: (j * GPS + p, 0, 0))
    return pl.pallas_call(
        _s5_kernel,
        grid=(n_tiles, GPS),
        in_specs=[tile, pl.BlockSpec((1, 1, LANES), lambda j, p: (j, 0, 0)),
                  pair(PW, PW), pair(PW, SW), pair(PW, SW), pair(bsz, SW)],
        out_specs=tile,
        out_shape=jax.ShapeDtypeStruct(u_tb.shape, F32),
        scratch_shapes=[pltpu.VMEM((rows, PW), BF16),
                        pltpu.VMEM((rows, SW), F32),
                        pltpu.VMEM((rows, SW), F32)],
        compiler_params=pltpu.CompilerParams(
            dimension_semantics=("arbitrary", "arbitrary"), vmem_limit_bytes=VMEM_LIMIT),
        name="s5_chunked",
    )(u_tb, d_tiles, toep, w_in, w_out, a_pow)


TAIL_TT = 512


def _sigmoid(v):
    return 0.5 * jnp.tanh(0.5 * v) + 0.5


def _tail_kernel(x_ref, ytb_ref, za_ref, at_ref, zb_ref, g_ref, wg_ref, bg_ref,
                 wps_ref, wpa_ref, wo_ref, o_ref):
    bsz = ytb_ref.shape[1] // TAIL_TT
    batch = pl.program_id(0) % bsz
    y_s5 = jnp.concatenate([ytb_ref[lt, pl.ds(batch, TAIL_TT, stride=bsz), :]
                            for lt in range(SSM_WIDTH // LANES)], axis=1)
    ys = jax.nn.gelu(y_s5).astype(BF16)
    glu = _dot(ys, wg_ref[...]) + bg_ref[...]
    z_a = za_ref[...].astype(F32)
    a_in = glu[:, :SSM_WIDTH] * _sigmoid(glu[:, SSM_WIDTH:]) * (z_a * _sigmoid(z_a))
    y_a = _dot(a_in.astype(BF16), wps_ref[...])
    z_b = zb_ref[...].astype(F32)
    y_b = _dot((at_ref[...].astype(F32) * (z_b * _sigmoid(z_b))).astype(BF16), wpa_ref[...])
    g = g_ref[...].astype(F32)
    mix = _sigmoid(g[:, :D_MODEL]) * y_a + _sigmoid(g[:, D_MODEL:]) * y_b
    o_ref[...] = x_ref[...] + _dot(mix.astype(BF16), wo_ref[...])


def _tail(x2d, y_tb, attn2d, pz2d, w_glu, b_glu, w_ps, w_pa, w_out, bsz, seq):
    t = x2d.shape[0]
    tile = _token_tile(bsz, seq // TAIL_TT)
    row = lambda w, c: pl.BlockSpec((TAIL_TT, w), lambda i, c=c: (tile(i), c))
    const = lambda shape: pl.BlockSpec(shape, lambda i: (0, 0))
    return pl.pallas_call(
        _tail_kernel,
        grid=(t // TAIL_TT,),
        in_specs=[
            row(D_MODEL, 0),
            pl.BlockSpec((SSM_WIDTH // LANES, TAIL_TT * bsz, LANES), lambda i: (0, i // bsz, 0)),
            row(SSM_WIDTH, PZ_ZA // SSM_WIDTH),
            row(ATTN_WIDTH, 0),
            row(ATTN_WIDTH, PZ_ZB // ATTN_WIDTH),
            row(2 * D_MODEL, PZ_G // (2 * D_MODEL)),
            const((SSM_WIDTH, 2 * SSM_WIDTH)), const((1, 2 * SSM_WIDTH)),
            const((SSM_WIDTH, D_MODEL)), const((ATTN_WIDTH, D_MODEL)), const((D_MODEL, D_MODEL)),
        ],
        out_specs=pl.BlockSpec((TAIL_TT, D_MODEL), lambda i: (tile(i), 0)),
        out_shape=jax.ShapeDtypeStruct((t, D_MODEL), F32),
        compiler_params=pltpu.CompilerParams(
            dimension_semantics=("arbitrary",), vmem_limit_bytes=VMEM_LIMIT),
        name="tail",
    )(x2d, y_tb, pz2d, attn2d, pz2d, pz2d, w_glu, b_glu, w_ps, w_pa, w_out)


def _rope_tables(seq):
    half = ROPE_DIM // 2
    inv = ROPE_THETA ** (-np.arange(0, ROPE_DIM, 2, dtype=np.float64) / ROPE_DIM)
    ang = np.arange(seq, dtype=np.float64)[:, None] * inv[None, :]
    cos, sin = np.cos(ang).astype(np.float32), np.sin(ang).astype(np.float32)
    zeros = np.zeros((seq, HEAD_DIM - ROPE_DIM), np.float32)
    z8 = np.zeros((seq, half), np.float32)
    cos_h = np.concatenate([cos, cos, np.ones_like(zeros)], axis=1)
    sa_h = np.concatenate([-sin, z8, zeros], axis=1)
    sb_h = np.concatenate([z8, sin, zeros], axis=1)
    two = lambda t: np.concatenate([t, t], axis=1).astype(np.float32)
    return two(cos_h), two(sa_h), two(sb_h)


def kernel(x, norm_w, w_in, b_gate, q_norm_w, k_norm_w, ssm_lam_re, ssm_lam_im, ssm_log_dt,
           ssm_b_re, ssm_b_im, ssm_c_re, ssm_c_im, ssm_d, w_glu, b_glu,
           w_proj_ssm, w_proj_attn, w_out):
    bsz, seq, _ = x.shape
    depth = norm_w.shape[0]
    cosf, sa, sb = _rope_tables(seq)
    ones_blk = jnp.asarray(np.kron(np.eye(2 * LANES // HEAD_DIM, dtype=np.float32),
                                   np.full((HEAD_DIM, HEAD_DIM), 1.0 / HEAD_DIM, np.float32)), BF16)
    for layer in range(depth):
        x2d = x.reshape(bsz * seq, D_MODEL)
        q_gain = q_norm_w[layer].astype(F32) * (LOG2E * HEAD_DIM ** -0.5)
        qk_w_row = jnp.concatenate([jnp.tile(q_gain, len(DILATIONS) * ATTN_SLOTS),
                                    jnp.tile(k_norm_w[layer].astype(F32), ATTN_SLOTS)])[None, :]
        pf2d, pz2d, u_tb = _in_proj(
            x2d, norm_w[layer][None, :].astype(F32), w_in[layer].astype(BF16),
            b_gate[layer][None, :].astype(F32), cosf, sa, sb, qk_w_row, ones_blk, bsz, seq)
        attn = _attention(pf2d.reshape(bsz, seq, PF_WIDTH))

        toep, s5_in, s5_out, a_pow = _s5_prep(
            ssm_lam_re[layer].astype(F32), ssm_lam_im[layer].astype(F32),
            ssm_log_dt[layer].astype(F32), ssm_b_re[layer].astype(F32),
            ssm_b_im[layer].astype(F32), ssm_c_re[layer].astype(F32),
            ssm_c_im[layer].astype(F32))
        n_tiles = SSM_WIDTH // LANES
        y_tb = _s5(u_tb.reshape(n_tiles, seq // CHUNK, CHUNK * bsz, LANES),
                   ssm_d[layer].astype(F32).reshape(n_tiles, 1, LANES), toep, s5_in, s5_out,
                   a_pow, bsz)

        out2d = _tail(x2d, y_tb.reshape(n_tiles, seq * bsz, LANES),
                      attn.reshape(bsz * seq, ATTN_WIDTH), pz2d,
                      w_glu[layer].astype(BF16), b_glu[layer][None, :].astype(F32),
                      w_proj_ssm[layer].astype(BF16), w_proj_attn[layer].astype(BF16),
                      w_out[layer].astype(BF16), bsz, seq)
        x = out2d.reshape(bsz, seq, D_MODEL)
    return x
```

```python
import math

import jax
import jax.numpy as jnp
import numpy as np
from jax import lax
from jax.experimental import pallas as pl
from jax.experimental.pallas import tpu as pltpu

F32 = jnp.float32
BF16 = jnp.bfloat16

D_MODEL = 1024
SSM_WIDTH = 512
SSM_GROUP = 16
SSM_GROUPS = 32
SSM_STATE = 64
HEAD_DIM = 64
ATTN_SLOTS = 8
ATTN_WIDTH = 512
DILATIONS = (1, 4, 16)
BAND_HALF = 64
ROPE_THETA = 500000.0
ROPE_DIM = 16
EPS = 1e-6
NEG_INF = -1e30
IN_WIDTH = 6144
COL_U, COL_ZA, COL_Q, COL_K, COL_V, COL_ZB, COL_G = 0, 512, 1024, 2560, 3072, 3584, 4096

LANES = 128
VMEM_LIMIT = 56 * 1024 * 1024

CHUNK = 16


def _dot(a, b):
    return jnp.dot(a, b, preferred_element_type=F32)


def _dot_nt(a, b):
    return lax.dot_general(a, b, (((1,), (1,)), ((), ())), preferred_element_type=F32)


def _dot_nt_f32(a, b):
    a_hi, b_hi = a.astype(BF16), b.astype(BF16)
    a_lo = (a - a_hi.astype(F32)).astype(BF16)
    b_lo = (b - b_hi.astype(F32)).astype(BF16)
    return _dot_nt(a_hi, b_hi) + (_dot_nt(a_hi, b_lo) + _dot_nt(a_lo, b_hi))


IN_TT = 512
IN_TN = 512
QK_ROWS = 256
LOG2E = math.log2(math.e)
PF_Q, PF_K, PF_V, PF_WIDTH = 0, 1536, 2048, 2560
PZ_G, PZ_ZA, PZ_ZB, PZ_WIDTH = 0, 2048, 2560, 3072
_Q_COLS = tuple((COL_Q + o, PF_Q + o) for o in range(0, COL_K - COL_Q, IN_TN))
_G_COLS = tuple((COL_G + o, PZ_G + o) for o in range(0, IN_WIDTH - COL_G, IN_TN))
F32_DEST = dict(((COL_K, PF_K), (COL_V, PF_V)) + _Q_COLS)
BF16_DEST = dict(((COL_ZA, PZ_ZA), (COL_ZB, PZ_ZB)) + _G_COLS)


def _in_proj_kernel(x_ref, nw_ref, w_ref, b_ref, cos_ref, sa_ref, sb_ref, qkw_ref, ones_ref,
                    pf_ref, pz_ref, utb_ref):
    bsz = utb_ref.shape[1] // IN_TT
    batch = pl.program_id(0) % bsz
    x = x_ref[...]
    var = jnp.mean(x * x, axis=-1, keepdims=True)
    h = (x * lax.rsqrt(var + EPS) * nw_ref[...]).astype(BF16)
    ones_blk = ones_ref[...]
    for c0 in range(0, IN_WIDTH, IN_TN):
        acc = _dot(h, w_ref[:, c0:c0 + IN_TN])
        if c0 in BF16_DEST:
            if c0 >= COL_G:
                acc = acc + b_ref[:, c0 - COL_G:c0 - COL_G + IN_TN]
            pz_ref[:, BF16_DEST[c0]:BF16_DEST[c0] + IN_TN] = acc.astype(BF16)
            continue
        if c0 == COL_U:
            for lt in range(SSM_WIDTH // LANES):
                utb_ref[lt, pl.ds(batch, IN_TT, stride=bsz), :] = acc[:, lt * LANES:(lt + 1) * LANES]
            continue
        d0 = F32_DEST[c0]
        if not COL_Q <= c0 < COL_V:
            pf_ref[:, d0:d0 + IN_TN] = acc
            continue
        for r0 in range(0, IN_TT, QK_ROWS):
            rows = slice(r0, r0 + QK_ROWS)
            for t2 in range(0, IN_TN, 2 * LANES):
                xq2 = acc[rows, t2:t2 + 2 * LANES]
                ms2 = _dot((xq2 * xq2).astype(BF16), ones_blk)
                xn2 = xq2 * lax.rsqrt(ms2 + EPS) * qkw_ref[:, c0 - COL_Q + t2:
                                                            c0 - COL_Q + t2 + 2 * LANES]
                for t in (0, LANES):
                    xn = xn2[:, t:t + LANES]
                    pf_ref[rows, d0 + t2 + t:d0 + t2 + t + LANES] = (
                        xn * cos_ref[rows, :]
                        + pltpu.roll(xn, LANES - ROPE_DIM // 2, 1) * sa_ref[rows, :]
                        + pltpu.roll(xn, ROPE_DIM // 2, 1) * sb_ref[rows, :])


def _token_tile(bsz, tiles_per_seq):
    return lambda i: (i % bsz) * tiles_per_seq + i // bsz


def _in_proj(x2d, norm_w, w_in_bf16, b_gate_row, cosf, sa, sb, qk_w_row, ones_blk, bsz, seq):
    t = x2d.shape[0]
    tile = _token_tile(bsz, seq // IN_TT)
    const = lambda shape: pl.BlockSpec(shape, lambda i: (0, 0))
    rope = lambda: pl.BlockSpec((IN_TT, LANES), lambda i: (i // bsz, 0))
    return pl.pallas_call(
        _in_proj_kernel,
        grid=(t // IN_TT,),
        in_specs=[
            pl.BlockSpec((IN_TT, D_MODEL), lambda i: (tile(i), 0)),
            const((1, D_MODEL)),
            pl.BlockSpec((D_MODEL, IN_WIDTH), lambda i: (0, 0), pipeline_mode=pl.Buffered(1)),
            const((1, IN_WIDTH - COL_G)),
            rope(), rope(), rope(),
            const((1, COL_V - COL_Q)),
            const((2 * LANES, 2 * LANES)),
        ],
        out_specs=[
            pl.BlockSpec((IN_TT, PF_WIDTH), lambda i: (tile(i), 0)),
            pl.BlockSpec((IN_TT, PZ_WIDTH), lambda i: (tile(i), 0)),
            pl.BlockSpec((SSM_WIDTH // LANES, IN_TT * bsz, LANES), lambda i: (0, i // bsz, 0)),
        ],
        out_shape=[
            jax.ShapeDtypeStruct((t, PF_WIDTH), F32),
            jax.ShapeDtypeStruct((t, PZ_WIDTH), BF16),
            jax.ShapeDtypeStruct((SSM_WIDTH // LANES, t, LANES), F32),
        ],
        compiler_params=pltpu.CompilerParams(
            dimension_semantics=("arbitrary",), vmem_limit_bytes=VMEM_LIMIT),
        name="in_proj",
    )(x2d, norm_w, w_in_bf16, b_gate_row, cosf, sa, sb, qk_w_row, ones_blk)


QBLK = 128
KWIN = QBLK + 2 * BAND_HALF
NORM_ROWS = 256
MERGE_ROWS = 64
MERGE_UNROLL = 8


def _attn_kernel(q0_ref, q1_ref, q2_ref, k_ref, v_ref, bias_ref, o_ref,
                 va_s, vb_s, oacc_s, den_s, max_s, s_s):
    seq = k_ref.shape[1]
    q_refs = (q0_ref, q1_ref, q2_ref)
    head0 = lax.broadcasted_iota(jnp.int32, (QBLK, LANES), 1) < HEAD_DIM
    head0_rows = lax.broadcasted_iota(jnp.int32, (NORM_ROWS, LANES), 1) < HEAD_DIM

    def prep(i, carry):
        rows = pl.ds(pl.multiple_of(i * NORM_ROWS, NORM_ROWS), NORM_ROWS)
        v = v_ref[0, rows, :]
        va_s[rows, :] = jnp.where(head0_rows, v, 1.0)
        vb_s[rows, :] = jnp.where(head0_rows, 1.0, v)
        return carry

    lax.fori_loop(0, seq // NORM_ROWS, prep, 0)

    patterns = []
    for p, d in enumerate(DILATIONS):
        n = seq // d
        nblk = n // QBLK
        kw = min(n, KWIN)

        def rows_of(idx, d=d, n=n, nblk=nblk, kw=kw):
            r = idx // nblk
            q0 = (idx % nblk) * QBLK
            ks = jnp.clip(q0 - BAND_HALF, 0, n - kw)
            return (pl.ds(r + d * q0, QBLK, stride=d), pl.ds(r + d * ks, kw, stride=d),
                    (q0 - ks) // BAND_HALF)

        def scores(idx, slot, p=p, kw=kw, rows_of=rows_of):
            qrows, krows, case = rows_of(idx)
            qb = q_refs[p][0, qrows, :]
            kb = k_ref[0, krows, :].astype(BF16)
            bias = bias_ref[case, :, :kw]
            s_s[slot, :QBLK, :kw] = _dot_nt(jnp.where(head0, qb, 0.0).astype(BF16), kb) + bias
            s_s[slot, QBLK:, :kw] = _dot_nt(jnp.where(head0, 0.0, qb).astype(BF16), kb) + bias

        def weigh(idx, slot, p=p, kw=kw, rows_of=rows_of):
            qrows, krows, _ = rows_of(idx)
            s = s_s[slot, :, :kw]
            m = jnp.max(s, axis=-1, keepdims=True)
            e = jnp.exp2(s - m).astype(BF16)
            o_a = _dot(e[:QBLK], va_s[krows, :].astype(BF16))
            o_b = _dot(e[QBLK:], vb_s[krows, :].astype(BF16))
            oacc_s[p, qrows, :] = jnp.where(head0, o_a, o_b)
            den_s[p, qrows, :] = jnp.where(head0, o_b, o_a)
            max_s[p, qrows, :] = jnp.where(head0, jnp.broadcast_to(m[:QBLK], (QBLK, LANES)),
                                           jnp.broadcast_to(m[QBLK:], (QBLK, LANES)))

        patterns.append((scores, weigh, d * nblk))

    def pair(weigh, i0, cur, ahead):
        nxt = 2 - cur
        if ahead is not None:
            ahead[0](ahead[1], nxt)
        weigh(i0, cur)
        if ahead is not None:
            ahead[0](ahead[1] + 1, nxt + 1)
        weigh(i0 + 1, cur + 1)

    patterns[0][0](0, 0)
    patterns[0][0](1, 1)
    for which, (scores, weigh, n_blocks) in enumerate(patterns):
        def quad(j, carry, scores=scores, weigh=weigh):
            pair(weigh, 4 * j, 0, (scores, 4 * j + 2))
            pair(weigh, 4 * j + 2, 2, (scores, 4 * j + 4))
            return carry

        lax.fori_loop(0, n_blocks // 4 - 1, quad, 0)
        pair(weigh, n_blocks - 4, 0, (scores, n_blocks - 2))
        following = (patterns[which + 1][0], 0) if which + 1 < len(patterns) else None
        pair(weigh, n_blocks - 2, 2, following)

    def combine(i, carry):
        rows = pl.ds(pl.multiple_of(i * MERGE_ROWS, MERGE_ROWS), MERGE_ROWS)
        m0, m1, m2 = max_s[0, rows, :], max_s[1, rows, :], max_s[2, rows, :]
        m = jnp.maximum(jnp.maximum(m0, m1), m2)
        w0, w1, w2 = jnp.exp2(m0 - m), jnp.exp2(m1 - m), jnp.exp2(m2 - m)
        num = w0 * oacc_s[0, rows, :] + w1 * oacc_s[1, rows, :] + w2 * oacc_s[2, rows, :]
        d0, d1, d2 = (pltpu.roll(den_s[p, rows, :], HEAD_DIM, 1) for p in range(3))
        o_ref[0, rows, :] = (num / (w0 * d0 + w1 * d1 + w2 * d2)).astype(o_ref.dtype)
        return carry

    lax.fori_loop(0, seq // MERGE_ROWS, combine, 0, unroll=MERGE_UNROLL)


def _band_bias():
    i = np.arange(QBLK)[:, None]
    j = np.arange(KWIN)[None, :]
    off = np.arange(3)[:, None, None] * BAND_HALF
    return np.where(np.abs(j - i - off) <= BAND_HALF, 0.0, NEG_INF).astype(np.float32)


def _attention(proj3d):
    bsz, seq, _ = proj3d.shape
    n_pairs = ATTN_WIDTH // LANES

    def qspec(p):
        return pl.BlockSpec((1, seq, LANES),
                            lambda b, hp, p=p: (b, 0, PF_Q // LANES + p * n_pairs + hp))

    seq_tile = lambda: pltpu.VMEM((seq, LANES), F32)
    pat_tile = lambda: pltpu.VMEM((len(DILATIONS), seq, LANES), F32)
    return pl.pallas_call(
        _attn_kernel,
        grid=(bsz, n_pairs),
        in_specs=[
            qspec(0), qspec(1), qspec(2),
            pl.BlockSpec((1, seq, LANES), lambda b, hp: (b, 0, PF_K // LANES + hp)),
            pl.BlockSpec((1, seq, LANES), lambda b, hp: (b, 0, PF_V // LANES + hp)),
            pl.BlockSpec((3, QBLK, KWIN), lambda b, hp: (0, 0, 0)),
        ],
        out_specs=pl.BlockSpec((1, seq, LANES), lambda b, hp: (b, 0, hp)),
        out_shape=jax.ShapeDtypeStruct((bsz, seq, ATTN_WIDTH), BF16),
        scratch_shapes=[
            seq_tile(), seq_tile(),
            pat_tile(), pat_tile(), pat_tile(),
            pltpu.VMEM((4, 2 * QBLK, KWIN), F32),
        ],
        compiler_params=pltpu.CompilerParams(
            dimension_semantics=("arbitrary", "arbitrary"), vmem_limit_bytes=VMEM_LIMIT),
        name="dilated_attention",
    )(proj3d, proj3d, proj3d, proj3d, proj3d, _band_bias())


GP = 2
N_PAIRS = SSM_GROUPS // GP
PG = GP * SSM_GROUP
PW = CHUNK * PG
SW = 4 * LANES
LAG_REP = LANES // PG


def _s5_prep_kernel(lre_ref, lim_ref, ldt_ref, bre_ref, bim_ref, cre_ref, cim_ref,
                    toep_ref, win_ref, wout_ref, a_ref, wf32_s):
    four = lambda ref: jnp.concatenate([ref[0, 0], ref[0, 0], ref[1, 0], ref[1, 0]], axis=1)
    lr = jnp.minimum(four(lre_ref), -1e-4)
    li = four(lim_ref)
    dt = jnp.exp(four(ldt_ref))
    col = lax.broadcasted_iota(jnp.int32, (1, SW), 1)
    is_re = (col // LANES) % 2 == 0
    is_fwd = col < 2 * LANES

    def power(k):
        mag = jnp.exp(k * (lr * dt))
        return mag * jnp.cos(k * (li * dt)), mag * jnp.sin(k * (li * dt))

    a_re, a_im = power(1.0)
    nr, ni, mag2 = a_re - 1.0, a_im, lr * lr + li * li
    f_re, f_im = (nr * lr + ni * li) / mag2, (ni * lr - nr * li) / mag2

    row_g = lax.broadcasted_iota(jnp.int32, (PG, SW), 0) // SSM_GROUP
    col_g = (lax.broadcasted_iota(jnp.int32, (PG, SW), 1) % LANES) // SSM_STATE
    diag = row_g == col_g
    b_re = jnp.where(diag, four(bre_ref), 0.0)
    b_im = jnp.where(diag, four(bim_ref), 0.0)
    c_re = jnp.where(diag, four(cre_ref), 0.0)
    c_im = jnp.where(diag, four(cim_ref), 0.0)
    bb_re = f_re * b_re - f_im * b_im
    bb_im = f_re * b_im + f_im * b_re

    step = lax.broadcasted_iota(jnp.int32, (CHUNK, SW), 0)
    k_in = jnp.where(is_fwd, CHUNK - 1 - step, step).astype(F32)
    k_out = jnp.where(is_fwd, step + 1, CHUNK - step).astype(F32)
    in_re, in_im = power(k_in)
    out_re, out_im = power(k_out)
    x1, x2 = jnp.where(is_re, in_re, in_im), jnp.where(is_re, -in_im, in_re)
    y1, y2 = jnp.where(is_re, out_re, -out_im), jnp.where(is_re, -out_im, -out_re)
    for s in range(CHUNK):
        rows = slice(s * PG, (s + 1) * PG)
        w = bb_re * x1[s:s + 1] + bb_im * x2[s:s + 1]
        wf32_s[rows, :] = w
        win_ref[0, rows, :] = w.astype(BF16)
        wout_ref[0, rows, :] = (c_re * y1[s:s + 1] + c_im * y2[s:s + 1]).astype(BF16)

    a16_re, a16_im = power(float(CHUNK))
    a_ref[0] = jnp.broadcast_to(jnp.where(is_re, a16_re, a16_im), a_ref.shape[1:])

    c_cat = jnp.where(is_re, c_re, -c_im)
    c_rep = jnp.concatenate([c_cat] * LAG_REP, axis=0)
    half = 2 * LANES
    lag_f = _dot_nt_f32(wf32_s[:, :half], c_rep[:, :half])
    lag_b = _dot_nt_f32(wf32_s[:, half:], c_rep[:, half:])
    row_t = lax.broadcasted_iota(jnp.int32, (PW, LANES), 0) // PG
    lane_q = lax.broadcasted_iota(jnp.int32, (PW, LANES), 1) // PG
    zeros = lambda n: jnp.zeros((n * PG, LANES), F32)
    for q in range(PW // LANES):
        tile = jnp.zeros((PW, LANES), F32)
        for u in range(LAG_REP):
            t = q * LAG_REP + u
            up = CHUNK - 1 - t
            sh_f = lag_f if up == 0 else jnp.concatenate([lag_f[up * PG:], zeros(up)], axis=0)
            sh_b = lag_b if t == 0 else jnp.concatenate([zeros(t), lag_b[:(CHUNK - t) * PG]], axis=0)
            col_t = jnp.where(row_t <= t, sh_f, 0.0) + jnp.where(row_t >= t, sh_b, 0.0)
            tile = jnp.where(lane_q == u, col_t, tile)
        toep_ref[0, :, q * LANES:(q + 1) * LANES] = tile.astype(BF16)


def _s5_prep(lam_re, lam_im, log_dt, b_re, b_im, c_re, c_im):
    lanes = lambda t: t.reshape(2, N_PAIRS, 1, LANES)
    dt_b = jnp.broadcast_to(log_dt[..., None], lam_re.shape)

    def tiles(t_ghp):
        t = jnp.broadcast_to(t_ghp[:, :, :, None, :], t_ghp.shape[:3] + (GP, SSM_STATE))
        return t.reshape(2, N_PAIRS, PG, LANES)

    row = pl.BlockSpec((2, 1, 1, LANES), lambda i: (0, i, 0, 0))
    mat = pl.BlockSpec((2, 1, PG, LANES), lambda i: (0, i, 0, 0))
    out = lambda r: pl.BlockSpec((1, r, SW), lambda i: (i, 0, 0))
    return pl.pallas_call(
        _s5_prep_kernel,
        grid=(N_PAIRS,),
        in_specs=[row, row, row, mat, mat, mat, mat],
        out_specs=[out(PW), out(PW), out(PW), out(8)],
        out_shape=[jax.ShapeDtypeStruct((N_PAIRS, PW, PW), BF16),
                   jax.ShapeDtypeStruct((N_PAIRS, PW, SW), BF16),
                   jax.ShapeDtypeStruct((N_PAIRS, PW, SW), BF16),
                   jax.ShapeDtypeStruct((N_PAIRS, 8, SW), F32)],
        scratch_shapes=[pltpu.VMEM((PW, SW), F32)],
        compiler_params=pltpu.CompilerParams(dimension_semantics=("arbitrary",)),
        name="s5_prep",
    )(lanes(lam_re), lanes(lam_im), lanes(dt_b), tiles(b_re.transpose(0, 1, 3, 2)),
      tiles(b_im.transpose(0, 1, 3, 2)), tiles(c_re), tiles(c_im))


SCAN_UNROLL = 4
GPS = LANES // PG
REGROUP_CHUNKS = 32


def _s5_kernel(u_ref, d_ref, toep_ref, win_ref, wout_ref, a_ref, y_ref, x_s, p_s, st_s):
    nc = u_ref.shape[1]
    bsz = u_ref.shape[2] // CHUNK
    piece = REGROUP_CHUNKS * bsz
    pp = pl.program_id(1)
    slot = lax.broadcasted_iota(jnp.int32, (piece, LANES), 1) // PG
    time_rows = lambda t: pl.ds(pl.multiple_of(t * bsz, bsz), bsz)

    def gather(cb, carry):
        chunks = pl.ds(pl.multiple_of(cb * REGROUP_CHUNKS, REGROUP_CHUNKS), REGROUP_CHUNKS)
        rows = pl.ds(pl.multiple_of(cb * piece, piece), piece)
        for q in range(CHUNK // GPS):
            acc = None
            for r in range(GPS):
                k = (pp + r) % GPS
                v = u_ref[0, chunks, time_rows(q * GPS + k), :].reshape(piece, LANES)
                v = v if r == 0 else pltpu.roll(v, r * PG, 1)
                acc = v if acc is None else jnp.where(slot == k, v, acc)
            x_s[rows, q * LANES:(q + 1) * LANES] = acc.astype(BF16)
        return carry

    lax.fori_loop(0, nc // REGROUP_CHUNKS, gather, 0)

    x = x_s[...]
    p_s[...] = _dot(x, win_ref[0])
    a_fr, a_fi, a_br, a_bi = (a_ref[0, :, q * LANES:(q + 1) * LANES] for q in range(4))
    tile = lambda q: slice(q * LANES, (q + 1) * LANES)

    def scan(c, carry):
        s_fr, s_fi, s_br, s_bi = carry
        rf = pl.ds(pl.multiple_of(c * bsz, bsz), bsz)
        rb = pl.ds(pl.multiple_of((nc - 1 - c) * bsz, bsz), bsz)
        st_s[rf, tile(0)] = s_fr
        st_s[rf, tile(1)] = s_fi
        st_s[rb, tile(2)] = s_br
        st_s[rb, tile(3)] = s_bi
        return (a_fr * s_fr - a_fi * s_fi + p_s[rf, tile(0)],
                a_fr * s_fi + a_fi * s_fr + p_s[rf, tile(1)],
                a_br * s_br - a_bi * s_bi + p_s[rb, tile(2)],
                a_br * s_bi + a_bi * s_br + p_s[rb, tile(3)])

    zero = jnp.zeros((bsz, LANES), F32)
    lax.fori_loop(0, nc, scan, (zero,) * 4, unroll=SCAN_UNROLL)
    p_s[...] = _dot(x, toep_ref[0]) + _dot_nt(st_s[...].astype(BF16), wout_ref[0])

    mine = (slot == pp).reshape(1, REGROUP_CHUNKS, bsz, LANES)
    d_skip = d_ref[0]

    def scatter(cb, carry):
        chunks = pl.ds(pl.multiple_of(cb * REGROUP_CHUNKS, REGROUP_CHUNKS), REGROUP_CHUNKS)
        rows = pl.ds(pl.multiple_of(cb * piece, piece), piece)
        for q in range(CHUNK // GPS):
            w = p_s[rows, q * LANES:(q + 1) * LANES]
            for r in range(GPS):
                k = (pp + GPS - r) % GPS
                e = w if r == 0 else pltpu.roll(w, r * PG, 1)
                where = (pl.ds(0, 1), chunks, time_rows(q * GPS + k), slice(None))
                e = e.reshape(1, REGROUP_CHUNKS, bsz, LANES) + d_skip * u_ref[where]
                pltpu.store(y_ref.at[where], e, mask=mine)
        return carry

    lax.fori_loop(0, nc // REGROUP_CHUNKS, scatter, 0)


def _s5(u_tb, d_tiles, toep, w_in, w_out, a_pow, bsz):
    n_tiles, nc, cb, _ = u_tb.shape
    rows = nc * bsz
    tile = pl.BlockSpec((1, nc, cb, LANES), lambda j, p: (j, 0, 0, 0))
    pair = lambda r, c: pl.BlockSpec((1, r, c), lambda j, p: (j * GPS + p, 0, 0))
    return pl.pallas_call(
        _s5_kernel,
        grid=(n_tiles, GPS),
        in_specs=[tile, pl.BlockSpec((1, 1, LANES), lambda j, p: (j, 0, 0)),
                  pair(PW, PW), pair(PW, SW), pair(PW, SW), pair(bsz, SW)],
        out_specs=tile,
        out_shape=jax.ShapeDtypeStruct(u_tb.shape, F32),
        scratch_shapes=[pltpu.VMEM((rows, PW), BF16),
                        pltpu.VMEM((rows, SW), F32),
                        pltpu.VMEM((rows, SW), F32)],
        compiler_params=pltpu.CompilerParams(
            dimension_semantics=("arbitrary", "arbitrary"), vmem_limit_bytes=VMEM_LIMIT),
        name="s5_chunked",
    )(u_tb, d_tiles, toep, w_in, w_out, a_pow)


TAIL_TT = 512


def _sigmoid(v):
    return 0.5 * jnp.tanh(0.5 * v) + 0.5


def _tail_kernel(x_ref, ytb_ref, za_ref, at_ref, zb_ref, g_ref, wg_ref, bg_ref,
                 wps_ref, wpa_ref, wo_ref, o_ref):
    bsz = ytb_ref.shape[1] // TAIL_TT
    batch = pl.program_id(0) % bsz
    y_s5 = jnp.concatenate([ytb_ref[lt, pl.ds(batch, TAIL_TT, stride=bsz), :]
                            for lt in range(SSM_WIDTH // LANES)], axis=1)
    ys = jax.nn.gelu(y_s5).astype(BF16)
    glu = _dot(ys, wg_ref[...]) + bg_ref[...]
    z_a = za_ref[...].astype(F32)
    a_in = glu[:, :SSM_WIDTH] * _sigmoid(glu[:, SSM_WIDTH:]) * (z_a * _sigmoid(z_a))
    y_a = _dot(a_in.astype(BF16), wps_ref[...])
    z_b = zb_ref[...].astype(F32)
    y_b = _dot((at_ref[...].astype(F32) * (z_b * _sigmoid(z_b))).astype(BF16), wpa_ref[...])
    g = g_ref[...].astype(F32)
    mix = _sigmoid(g[:, :D_MODEL]) * y_a + _sigmoid(g[:, D_MODEL:]) * y_b
    o_ref[...] = x_ref[...] + _dot(mix.astype(BF16), wo_ref[...])


def _tail(x2d, y_tb, attn2d, pz2d, w_glu, b_glu, w_ps, w_pa, w_out, bsz, seq):
    t = x2d.shape[0]
    tile = _token_tile(bsz, seq // TAIL_TT)
    row = lambda w, c: pl.BlockSpec((TAIL_TT, w), lambda i, c=c: (tile(i), c))
    const = lambda shape: pl.BlockSpec(shape, lambda i: (0, 0))
    return pl.pallas_call(
        _tail_kernel,
        grid=(t // TAIL_TT,),
        in_specs=[
            row(D_MODEL, 0),
            pl.BlockSpec((SSM_WIDTH // LANES, TAIL_TT * bsz, LANES), lambda i: (0, i // bsz, 0)),
            row(SSM_WIDTH, PZ_ZA // SSM_WIDTH),
            row(ATTN_WIDTH, 0),
            row(ATTN_WIDTH, PZ_ZB // ATTN_WIDTH),
            row(2 * D_MODEL, PZ_G // (2 * D_MODEL)),
            const((SSM_WIDTH, 2 * SSM_WIDTH)), const((1, 2 * SSM_WIDTH)),
            const((SSM_WIDTH, D_MODEL)), const((ATTN_WIDTH, D_MODEL)), const((D_MODEL, D_MODEL)),
        ],
        out_specs=pl.BlockSpec((TAIL_TT, D_MODEL), lambda i: (tile(i), 0)),
        out_shape=jax.ShapeDtypeStruct((t, D_MODEL), F32),
        compiler_params=pltpu.CompilerParams(
            dimension_semantics=("arbitrary",), vmem_limit_bytes=VMEM_LIMIT),
        name="tail",
    )(x2d, y_tb, pz2d, attn2d, pz2d, pz2d, w_glu, b_glu, w_ps, w_pa, w_out)


def _rope_tables(seq):
    half = ROPE_DIM // 2
    inv = ROPE_THETA ** (-np.arange(0, ROPE_DIM, 2, dtype=np.float64) / ROPE_DIM)
    ang = np.arange(seq, dtype=np.float64)[:, None] * inv[None, :]
    cos, sin = np.cos(ang).astype(np.float32), np.sin(ang).astype(np.float32)
    zeros = np.zeros((seq, HEAD_DIM - ROPE_DIM), np.float32)
    z8 = np.zeros((seq, half), np.float32)
    cos_h = np.concatenate([cos, cos, np.ones_like(zeros)], axis=1)
    sa_h = np.concatenate([-sin, z8, zeros], axis=1)
    sb_h = np.concatenate([z8, sin, zeros], axis=1)
    two = lambda t: np.concatenate([t, t], axis=1).astype(np.float32)
    return two(cos_h), two(sa_h), two(sb_h)


def kernel(x, norm_w, w_in, b_gate, q_norm_w, k_norm_w, ssm_lam_re, ssm_lam_im, ssm_log_dt,
           ssm_b_re, ssm_b_im, ssm_c_re, ssm_c_im, ssm_d, w_glu, b_glu,
           w_proj_ssm, w_proj_attn, w_out):
    bsz, seq, _ = x.shape
    depth = norm_w.shape[0]
    cosf, sa, sb = _rope_tables(seq)
    ones_blk = jnp.asarray(np.kron(np.eye(2 * LANES // HEAD_DIM, dtype=np.float32),
                                   np.full((HEAD_DIM, HEAD_DIM), 1.0 / HEAD_DIM, np.float32)), BF16)
    for layer in range(depth):
        x2d = x.reshape(bsz * seq, D_MODEL)
        q_gain = q_norm_w[layer].astype(F32) * (LOG2E * HEAD_DIM ** -0.5)
        qk_w_row = jnp.concatenate([jnp.tile(q_gain, len(DILATIONS) * ATTN_SLOTS),
                                    jnp.tile(k_norm_w[layer].astype(F32), ATTN_SLOTS)])[None, :]
        pf2d, pz2d, u_tb = _in_proj(
            x2d, norm_w[layer][None, :].astype(F32), w_in[layer].astype(BF16),
            b_gate[layer][None, :].astype(F32), cosf, sa, sb, qk_w_row, ones_blk, bsz, seq)
        attn = _attention(pf2d.reshape(bsz, seq, PF_WIDTH))

        toep, s5_in, s5_out, a_pow = _s5_prep(
            ssm_lam_re[layer].astype(F32), ssm_lam_im[layer].astype(F32),
            ssm_log_dt[layer].astype(F32), ssm_b_re[layer].astype(F32),
            ssm_b_im[layer].astype(F32), ssm_c_re[layer].astype(F32),
            ssm_c_im[layer].astype(F32))
        n_tiles = SSM_WIDTH // LANES
        y_tb = _s5(u_tb.reshape(n_tiles, seq // CHUNK, CHUNK * bsz, LANES),
                   ssm_d[layer].astype(F32).reshape(n_tiles, 1, LANES), toep, s5_in, s5_out,
                   a_pow, bsz)

        out2d = _tail(x2d, y_tb.reshape(n_tiles, seq * bsz, LANES),
                      attn.reshape(bsz * seq, ATTN_WIDTH), pz2d,
                      w_glu[layer].astype(BF16), b_glu[layer][None, :].astype(F32),
                      w_proj_ssm[layer].astype(BF16), w_proj_attn[layer].astype(BF16),
                      w_out[layer].astype(BF16), bsz, seq)
        x = out2d.reshape(bsz, seq, D_MODEL)
    return x
```

```python
import math

import jax
import jax.numpy as jnp
import numpy as np
from jax import lax
from jax.experimental import pallas as pl
from jax.experimental.pallas import tpu as pltpu

F32 = jnp.float32
BF16 = jnp.bfloat16

D_MODEL = 1024
SSM_WIDTH = 512
SSM_GROUP = 16
SSM_GROUPS = 32
SSM_STATE = 64
HEAD_DIM = 64
ATTN_SLOTS = 8
ATTN_WIDTH = 512
DILATIONS = (1, 4, 16)
BAND_HALF = 64
ROPE_THETA = 500000.0
ROPE_DIM = 16
EPS = 1e-6
NEG_INF = -1e30
IN_WIDTH = 6144
COL_U, COL_ZA, COL_Q, COL_K, COL_V, COL_ZB, COL_G = 0, 512, 1024, 2560, 3072, 3584, 4096

LANES = 128
SUBLANES = 8
VMEM_LIMIT = 56 * 1024 * 1024

CHUNK = 16


def _dot(a, b):
    return jnp.dot(a, b, preferred_element_type=F32)


def _dot_nt(a, b):
    return lax.dot_general(a, b, (((1,), (1,)), ((), ())), preferred_element_type=F32)


def _dot_nt_f32(a, b):
    a_hi, b_hi = a.astype(BF16), b.astype(BF16)
    a_lo = (a - a_hi.astype(F32)).astype(BF16)
    b_lo = (b - b_hi.astype(F32)).astype(BF16)
    return _dot_nt(a_hi, b_hi) + (_dot_nt(a_hi, b_lo) + _dot_nt(a_lo, b_hi))


def _sigmoid(v):
    return 0.5 * jnp.tanh(0.5 * v) + 0.5


IN_TT = 512
IN_TN = 512
QK_ROWS = 256
LOG2E = math.log2(math.e)
PF_Q, PF_K, PF_V, PF_WIDTH = 0, 1536, 2048, 2560
PZ_G, PZ_ZA, PZ_ZB, PZ_WIDTH = 0, 2048, 2560, 3072
_Q_COLS = tuple((COL_Q + o, PF_Q + o) for o in range(0, COL_K - COL_Q, IN_TN))
_G_COLS = tuple((COL_G + o, PZ_G + o) for o in range(0, IN_WIDTH - COL_G, IN_TN))
F32_DEST = dict(((COL_K, PF_K), (COL_V, PF_V)) + _Q_COLS)
BF16_DEST = dict(((COL_ZA, PZ_ZA), (COL_ZB, PZ_ZB)) + _G_COLS)


def _in_proj_kernel(x_ref, nw_ref, w_ref, b_ref, cos_ref, sa_ref, sb_ref, qkw_ref, ones_ref,
                    pf_ref, pz_ref, utb_ref):
    bsz = utb_ref.shape[1] // IN_TT
    batch = pl.program_id(0) % bsz
    x = x_ref[...]
    var = jnp.mean(x * x, axis=-1, keepdims=True)
    h = (x * lax.rsqrt(var + EPS) * nw_ref[...]).astype(BF16)
    ones_blk = ones_ref[...]
    for c0 in range(0, IN_WIDTH, IN_TN):
        acc = _dot(h, w_ref[:, c0:c0 + IN_TN])
        if c0 in BF16_DEST:
            if c0 >= COL_G:
                acc = acc + b_ref[:, c0 - COL_G:c0 - COL_G + IN_TN]
            pz_ref[:, BF16_DEST[c0]:BF16_DEST[c0] + IN_TN] = acc.astype(BF16)
            continue
        if c0 == COL_U:
            for lt in range(SSM_WIDTH // LANES):
                utb_ref[lt, pl.ds(batch, IN_TT, stride=bsz), :] = acc[:, lt * LANES:(lt + 1) * LANES]
            continue
        d0 = F32_DEST[c0]
        if not COL_Q <= c0 < COL_V:
            pf_ref[:, d0:d0 + IN_TN] = acc
            continue
        for r0 in range(0, IN_TT, QK_ROWS):
            rows = slice(r0, r0 + QK_ROWS)
            for t2 in range(0, IN_TN, 2 * LANES):
                xq2 = acc[rows, t2:t2 + 2 * LANES]
                ms2 = _dot((xq2 * xq2).astype(BF16), ones_blk)
                xn2 = xq2 * lax.rsqrt(ms2 + EPS) * qkw_ref[:, c0 - COL_Q + t2:
                                                            c0 - COL_Q + t2 + 2 * LANES]
                for t in (0, LANES):
                    xn = xn2[:, t:t + LANES]
                    pf_ref[rows, d0 + t2 + t:d0 + t2 + t + LANES] = (
                        xn * cos_ref[rows, :]
                        + pltpu.roll(xn, LANES - ROPE_DIM // 2, 1) * sa_ref[rows, :]
                        + pltpu.roll(xn, ROPE_DIM // 2, 1) * sb_ref[rows, :])


def _token_tile(bsz, tiles_per_seq):
    return lambda i: (i % bsz) * tiles_per_seq + i // bsz


def _in_proj(x2d, norm_w, w_in_bf16, b_gate_row, cosf, sa, sb, qk_w_row, ones_blk, bsz, seq):
    t = x2d.shape[0]
    tile = _token_tile(bsz, seq // IN_TT)
    const = lambda shape: pl.BlockSpec(shape, lambda i: (0, 0))
    rope = lambda: pl.BlockSpec((IN_TT, LANES), lambda i: (i // bsz, 0))
    return pl.pallas_call(
        _in_proj_kernel,
        grid=(t // IN_TT,),
        in_specs=[
            pl.BlockSpec((IN_TT, D_MODEL), lambda i: (tile(i), 0)),
            const((1, D_MODEL)),
            pl.BlockSpec((D_MODEL, IN_WIDTH), lambda i: (0, 0), pipeline_mode=pl.Buffered(1)),
            const((1, IN_WIDTH - COL_G)),
            rope(), rope(), rope(),
            const((1, COL_V - COL_Q)),
            const((2 * LANES, 2 * LANES)),
        ],
        out_specs=[
            pl.BlockSpec((IN_TT, PF_WIDTH), lambda i: (tile(i), 0)),
            pl.BlockSpec((IN_TT, PZ_WIDTH), lambda i: (tile(i), 0)),
            pl.BlockSpec((SSM_WIDTH // LANES, IN_TT * bsz, LANES), lambda i: (0, i // bsz, 0)),
        ],
        out_shape=[
            jax.ShapeDtypeStruct((t, PF_WIDTH), F32),
            jax.ShapeDtypeStruct((t, PZ_WIDTH), BF16),
            jax.ShapeDtypeStruct((SSM_WIDTH // LANES, t, LANES), F32),
        ],
        compiler_params=pltpu.CompilerParams(
            dimension_semantics=("arbitrary",), vmem_limit_bytes=VMEM_LIMIT),
        name="in_proj",
    )(x2d, norm_w, w_in_bf16, b_gate_row, cosf, sa, sb, qk_w_row, ones_blk)


QBLK = 128
KWIN = QBLK + 2 * BAND_HALF
NORM_ROWS = 256
MERGE_ROWS = 64
MERGE_UNROLL = 8


def _attn_kernel(q0_ref, q1_ref, q2_ref, k_ref, v_ref, bias_ref, o_ref,
                 va_s, vb_s, oacc_s, den_s, max_s, s_s):
    seq = k_ref.shape[1]
    q_refs = (q0_ref, q1_ref, q2_ref)
    head0 = lax.broadcasted_iota(jnp.int32, (QBLK, LANES), 1) < HEAD_DIM
    head0_rows = lax.broadcasted_iota(jnp.int32, (NORM_ROWS, LANES), 1) < HEAD_DIM

    def prep(i, carry):
        rows = pl.ds(pl.multiple_of(i * NORM_ROWS, NORM_ROWS), NORM_ROWS)
        v = v_ref[0, rows, :]
        va_s[rows, :] = jnp.where(head0_rows, v, 1.0)
        vb_s[rows, :] = jnp.where(head0_rows, 1.0, v)
        return carry

    lax.fori_loop(0, seq // NORM_ROWS, prep, 0)

    patterns = []
    for p, d in enumerate(DILATIONS):
        n = seq // d
        nblk = n // QBLK
        kw = min(n, KWIN)

        def rows_of(idx, d=d, n=n, nblk=nblk, kw=kw):
            r = idx // nblk
            q0 = (idx % nblk) * QBLK
            ks = jnp.clip(q0 - BAND_HALF, 0, n - kw)
            return (pl.ds(r + d * q0, QBLK, stride=d), pl.ds(r + d * ks, kw, stride=d),
                    (q0 - ks) // BAND_HALF)

        def scores(idx, slot, p=p, kw=kw, rows_of=rows_of):
            qrows, krows, case = rows_of(idx)
            qb = q_refs[p][0, qrows, :]
            kb = k_ref[0, krows, :].astype(BF16)
            bias = bias_ref[case, :, :kw]
            s_s[slot, :QBLK, :kw] = _dot_nt(jnp.where(head0, qb, 0.0).astype(BF16), kb) + bias
            s_s[slot, QBLK:, :kw] = _dot_nt(jnp.where(head0, 0.0, qb).astype(BF16), kb) + bias

        def weigh(idx, slot, p=p, kw=kw, rows_of=rows_of):
            qrows, krows, _ = rows_of(idx)
            s = s_s[slot, :, :kw]
            m = jnp.max(s, axis=-1, keepdims=True)
            e = jnp.exp2(s - m).astype(BF16)
            o_a = _dot(e[:QBLK], va_s[krows, :].astype(BF16))
            o_b = _dot(e[QBLK:], vb_s[krows, :].astype(BF16))
            oacc_s[p, qrows, :] = jnp.where(head0, o_a, o_b)
            den_s[p, qrows, :] = jnp.where(head0, o_b, o_a)
            max_s[p, qrows, :] = jnp.where(head0, jnp.broadcast_to(m[:QBLK], (QBLK, LANES)),
                                           jnp.broadcast_to(m[QBLK:], (QBLK, LANES)))

        patterns.append((scores, weigh, d * nblk))

    def pair(weigh, i0, cur, ahead):
        nxt = 2 - cur
        if ahead is not None:
            ahead[0](ahead[1], nxt)
        weigh(i0, cur)
        if ahead is not None:
            ahead[0](ahead[1] + 1, nxt + 1)
        weigh(i0 + 1, cur + 1)

    patterns[0][0](0, 0)
    patterns[0][0](1, 1)
    for which, (scores, weigh, n_blocks) in enumerate(patterns):
        def quad(j, carry, scores=scores, weigh=weigh):
            pair(weigh, 4 * j, 0, (scores, 4 * j + 2))
            pair(weigh, 4 * j + 2, 2, (scores, 4 * j + 4))
            return carry

        lax.fori_loop(0, n_blocks // 4 - 1, quad, 0, unroll=True)
        pair(weigh, n_blocks - 4, 0, (scores, n_blocks - 2))
        following = (patterns[which + 1][0], 0) if which + 1 < len(patterns) else None
        pair(weigh, n_blocks - 2, 2, following)

    def combine(i, carry):
        rows = pl.ds(pl.multiple_of(i * MERGE_ROWS, MERGE_ROWS), MERGE_ROWS)
        m0, m1, m2 = max_s[0, rows, :], max_s[1, rows, :], max_s[2, rows, :]
        m = jnp.maximum(jnp.maximum(m0, m1), m2)
        w0, w1, w2 = jnp.exp2(m0 - m), jnp.exp2(m1 - m), jnp.exp2(m2 - m)
        num = w0 * oacc_s[0, rows, :] + w1 * oacc_s[1, rows, :] + w2 * oacc_s[2, rows, :]
        d0, d1, d2 = (pltpu.roll(den_s[p, rows, :], HEAD_DIM, 1) for p in range(3))
        o_ref[0, rows, :] = (num / (w0 * d0 + w1 * d1 + w2 * d2)).astype(o_ref.dtype)
        return carry

    lax.fori_loop(0, seq // MERGE_ROWS, combine, 0, unroll=MERGE_UNROLL)


def _band_bias():
    i = np.arange(QBLK)[:, None]
    j = np.arange(KWIN)[None, :]
    off = np.arange(3)[:, None, None] * BAND_HALF
    return np.where(np.abs(j - i - off) <= BAND_HALF, 0.0, NEG_INF).astype(np.float32)


def _attention(proj3d):
    bsz, seq, _ = proj3d.shape
    n_pairs = ATTN_WIDTH // LANES

    def qspec(p):
        return pl.BlockSpec((1, seq, LANES),
                            lambda b, hp, p=p: (b, 0, PF_Q // LANES + p * n_pairs + hp))

    seq_tile = lambda: pltpu.VMEM((seq, LANES), F32)
    pat_tile = lambda: pltpu.VMEM((len(DILATIONS), seq, LANES), F32)
    return pl.pallas_call(
        _attn_kernel,
        grid=(bsz, n_pairs),
        in_specs=[
            qspec(0), qspec(1), qspec(2),
            pl.BlockSpec((1, seq, LANES), lambda b, hp: (b, 0, PF_K // LANES + hp)),
            pl.BlockSpec((1, seq, LANES), lambda b, hp: (b, 0, PF_V // LANES + hp)),
            pl.BlockSpec((3, QBLK, KWIN), lambda b, hp: (0, 0, 0)),
        ],
        out_specs=pl.BlockSpec((1, seq, LANES), lambda b, hp: (b, 0, hp)),
        out_shape=jax.ShapeDtypeStruct((bsz, seq, ATTN_WIDTH), BF16),
        scratch_shapes=[
            seq_tile(), seq_tile(),
            pat_tile(), pat_tile(), pat_tile(),
            pltpu.VMEM((4, 2 * QBLK, KWIN), F32),
        ],
        compiler_params=pltpu.CompilerParams(
            dimension_semantics=("arbitrary", "arbitrary"), vmem_limit_bytes=VMEM_LIMIT),
        name="dilated_attention",
    )(proj3d, proj3d, proj3d, proj3d, proj3d, _band_bias())


GP = 2
N_PAIRS = SSM_GROUPS // GP
PG = GP * SSM_GROUP
PW = CHUNK * PG
SW = 4 * LANES
LAG_REP = LANES // PG


def _s5_prep_kernel(lre_ref, lim_ref, ldt_ref, bre_ref, bim_ref, cre_ref, cim_ref,
                    toep_ref, win_ref, wout_ref, a_ref, wf32_s):
    four = lambda ref: jnp.concatenate([ref[0, 0], ref[0, 0], ref[1, 0], ref[1, 0]], axis=1)
    lr = jnp.minimum(four(lre_ref), -1e-4)
    li = four(lim_ref)
    dt = jnp.exp(four(ldt_ref))
    col = lax.broadcasted_iota(jnp.int32, (1, SW), 1)
    is_re = (col // LANES) % 2 == 0
    is_fwd = col < 2 * LANES

    def power(k):
        mag = jnp.exp(k * (lr * dt))
        return mag * jnp.cos(k * (li * dt)), mag * jnp.sin(k * (li * dt))

    a_re, a_im = power(1.0)
    nr, ni, mag2 = a_re - 1.0, a_im, lr * lr + li * li
    f_re, f_im = (nr * lr + ni * li) / mag2, (ni * lr - nr * li) / mag2

    row_g = lax.broadcasted_iota(jnp.int32, (PG, SW), 0) // SSM_GROUP
    col_g = (lax.broadcasted_iota(jnp.int32, (PG, SW), 1) % LANES) // SSM_STATE
    diag = row_g == col_g
    b_re = jnp.where(diag, four(bre_ref), 0.0)
    b_im = jnp.where(diag, four(bim_ref), 0.0)
    c_re = jnp.where(diag, four(cre_ref), 0.0)
    c_im = jnp.where(diag, four(cim_ref), 0.0)
    bb_re = f_re * b_re - f_im * b_im
    bb_im = f_re * b_im + f_im * b_re

    step = lax.broadcasted_iota(jnp.int32, (CHUNK, SW), 0)
    k_in = jnp.where(is_fwd, CHUNK - 1 - step, step).astype(F32)
    k_out = jnp.where(is_fwd, step + 1, CHUNK - step).astype(F32)
    in_re, in_im = power(k_in)
    out_re, out_im = power(k_out)
    x1, x2 = jnp.where(is_re, in_re, in_im), jnp.where(is_re, -in_im, in_re)
    y1, y2 = jnp.where(is_re, out_re, -out_im), jnp.where(is_re, -out_im, -out_re)
    for s in range(CHUNK):
        rows = slice(s * PG, (s + 1) * PG)
        w = bb_re * x1[s:s + 1] + bb_im * x2[s:s + 1]
        wf32_s[rows, :] = w
        win_ref[0, rows, :] = w.astype(BF16)
        wout_ref[0, rows, :] = (c_re * y1[s:s + 1] + c_im * y2[s:s + 1]).astype(BF16)

    a16_re, a16_im = power(float(CHUNK))
    a_ref[0] = jnp.broadcast_to(jnp.where(is_re, a16_re, a16_im), a_ref.shape[1:])

    c_cat = jnp.where(is_re, c_re, -c_im)
    c_rep = jnp.concatenate([c_cat] * LAG_REP, axis=0)
    half = 2 * LANES
    lag_f = _dot_nt_f32(wf32_s[:, :half], c_rep[:, :half])
    lag_b = _dot_nt_f32(wf32_s[:, half:], c_rep[:, half:])
    row_t = lax.broadcasted_iota(jnp.int32, (PW, LANES), 0) // PG
    lane_q = lax.broadcasted_iota(jnp.int32, (PW, LANES), 1) // PG
    zeros = lambda n: jnp.zeros((n * PG, LANES), F32)
    for q in range(PW // LANES):
        tile = jnp.zeros((PW, LANES), F32)
        for u in range(LAG_REP):
            t = q * LAG_REP + u
            up = CHUNK - 1 - t
            sh_f = lag_f if up == 0 else jnp.concatenate([lag_f[up * PG:], zeros(up)], axis=0)
            sh_b = lag_b if t == 0 else jnp.concatenate([zeros(t), lag_b[:(CHUNK - t) * PG]], axis=0)
            col_t = jnp.where(row_t <= t, sh_f, 0.0) + jnp.where(row_t >= t, sh_b, 0.0)
            tile = jnp.where(lane_q == u, col_t, tile)
        toep_ref[0, :, q * LANES:(q + 1) * LANES] = tile.astype(BF16)


def _s5_prep(lam_re, lam_im, log_dt, b_re, b_im, c_re, c_im):
    lanes = lambda t: t.reshape(2, N_PAIRS, 1, LANES)
    dt_b = jnp.broadcast_to(log_dt[..., None], lam_re.shape)

    def tiles(t_ghp):
        t = jnp.broadcast_to(t_ghp[:, :, :, None, :], t_ghp.shape[:3] + (GP, SSM_STATE))
        return t.reshape(2, N_PAIRS, PG, LANES)

    row = pl.BlockSpec((2, 1, 1, LANES), lambda i: (0, i, 0, 0))
    mat = pl.BlockSpec((2, 1, PG, LANES), lambda i: (0, i, 0, 0))
    out = lambda r: pl.BlockSpec((1, r, SW), lambda i: (i, 0, 0))
    return pl.pallas_call(
        _s5_prep_kernel,
        grid=(N_PAIRS,),
        in_specs=[row, row, row, mat, mat, mat, mat],
        out_specs=[out(PW), out(PW), out(PW), out(SUBLANES)],
        out_shape=[jax.ShapeDtypeStruct((N_PAIRS, PW, PW), BF16),
                   jax.ShapeDtypeStruct((N_PAIRS, PW, SW), BF16),
                   jax.ShapeDtypeStruct((N_PAIRS, PW, SW), BF16),
                   jax.ShapeDtypeStruct((N_PAIRS, SUBLANES, SW), F32)],
        scratch_shapes=[pltpu.VMEM((PW, SW), F32)],
        compiler_params=pltpu.CompilerParams(dimension_semantics=("arbitrary",)),
        name="s5_prep",
    )(lanes(lam_re), lanes(lam_im), lanes(dt_b), tiles(b_re.transpose(0, 1, 3, 2)),
      tiles(b_im.transpose(0, 1, 3, 2)), tiles(c_re), tiles(c_im))


SCAN_UNROLL = 4
GPS = LANES // PG
REGROUP_CHUNKS = 32


def _s5_kernel(u_ref, d_ref, toep_ref, win_ref, wout_ref, a_ref, y_ref, x_s, p_s, st_s):
    nc = u_ref.shape[1]
    bsz = u_ref.shape[2] // CHUNK
    piece = REGROUP_CHUNKS * bsz
    pp = pl.program_id(1)
    slot = lax.broadcasted_iota(jnp.int32, (piece, LANES), 1) // PG
    time_rows = lambda t: pl.ds(pl.multiple_of(t * bsz, bsz), bsz)

    def gather(cb, carry):
        chunks = pl.ds(pl.multiple_of(cb * REGROUP_CHUNKS, REGROUP_CHUNKS), REGROUP_CHUNKS)
        rows = pl.ds(pl.multiple_of(cb * piece, piece), piece)
        for q in range(CHUNK // GPS):
            acc = None
            for r in range(GPS):
                k = (pp + r) % GPS
                v = u_ref[0, chunks, time_rows(q * GPS + k), :].reshape(piece, LANES)
                v = v if r == 0 else pltpu.roll(v, r * PG, 1)
                acc = v if acc is None else jnp.where(slot == k, v, acc)
            x_s[rows, q * LANES:(q + 1) * LANES] = acc.astype(BF16)
        return carry

    lax.fori_loop(0, nc // REGROUP_CHUNKS, gather, 0)

    x = x_s[...]
    p_s[...] = _dot(x, win_ref[0])
    a_fr, a_fi, a_br, a_bi = (a_ref[0, :, q * LANES:(q + 1) * LANES] for q in range(4))
    tile = lambda q: slice(q * LANES, (q + 1) * LANES)

    def scan(c, carry):
        s_fr, s_fi, s_br, s_bi = carry
        rf = pl.ds(pl.multiple_of(c * bsz, bsz), bsz)
        rb = pl.ds(pl.multiple_of((nc - 1 - c) * bsz, bsz), bsz)
        st_s[rf, tile(0)] = s_fr
        st_s[rf, tile(1)] = s_fi
        st_s[rb, tile(2)] = s_br
        st_s[rb, tile(3)] = s_bi
        return (a_fr * s_fr - a_fi * s_fi + p_s[rf, tile(0)],
                a_fr * s_fi + a_fi * s_fr + p_s[rf, tile(1)],
                a_br * s_br - a_bi * s_bi + p_s[rb, tile(2)],
                a_br * s_bi + a_bi * s_br + p_s[rb, tile(3)])

    zero = jnp.zeros((bsz, LANES), F32)
    lax.fori_loop(0, nc, scan, (zero,) * 4, unroll=SCAN_UNROLL)
    p_s[...] = _dot(x, toep_ref[0]) + _dot_nt(st_s[...].astype(BF16), wout_ref[0])

    mine = (slot == pp).reshape(1, REGROUP_CHUNKS, bsz, LANES)
    d_skip = d_ref[0]

    def scatter(cb, carry):
        chunks = pl.ds(pl.multiple_of(cb * REGROUP_CHUNKS, REGROUP_CHUNKS), REGROUP_CHUNKS)
        rows = pl.ds(pl.multiple_of(cb * piece, piece), piece)
        for q in range(CHUNK // GPS):
            w = p_s[rows, q * LANES:(q + 1) * LANES]
            for r in range(GPS):
                k = (pp + GPS - r) % GPS
                e = w if r == 0 else pltpu.roll(w, r * PG, 1)
                where = (pl.ds(0, 1), chunks, time_rows(q * GPS + k), slice(None))
                e = e.reshape(1, REGROUP_CHUNKS, bsz, LANES) + d_skip * u_ref[where]
                pltpu.store(y_ref.at[where], e, mask=mine)
        return carry

    lax.fori_loop(0, nc // REGROUP_CHUNKS, scatter, 0)


def _s5(u_tb, d_tiles, toep, w_in, w_out, a_pow, bsz):
    n_tiles, nc, cb, _ = u_tb.shape
    rows = nc * bsz
    tile = pl.BlockSpec((1, nc, cb, LANES), lambda j, p: (j, 0, 0, 0))
    pair = lambda r, c: pl.BlockSpec((1, r, c), lambda j, p: (j * GPS + p, 0, 0))
    return pl.pallas_call(
        _s5_kernel,
        grid=(n_tiles, GPS),
        in_specs=[tile, pl.BlockSpec((1, 1, LANES), lambda j, p: (j, 0, 0)),
                  pair(PW, PW), pair(PW, SW), pair(PW, SW), pair(bsz, SW)],
        out_specs=tile,
        out_shape=jax.ShapeDtypeStruct(u_tb.shape, F32),
        scratch_shapes=[pltpu.VMEM((rows, PW), BF16),
                        pltpu.VMEM((rows, SW), F32),
                        pltpu.VMEM((rows, SW), F32)],
        compiler_params=pltpu.CompilerParams(
            dimension_semantics=("arbitrary", "arbitrary"), vmem_limit_bytes=VMEM_LIMIT),
        name="s5_chunked",
    )(u_tb, d_tiles, toep, w_in, w_out, a_pow)


TAIL_TT = 512


def _tail_kernel(x_ref, ytb_ref, za_ref, at_ref, zb_ref, g_ref, wg_ref, bg_ref,
                 wps_ref, wpa_ref, wo_ref, o_ref):
    bsz = ytb_ref.shape[1] // TAIL_TT
    batch = pl.program_id(0) % bsz
    y_s5 = jnp.concatenate([ytb_ref[lt, pl.ds(batch, TAIL_TT, stride=bsz), :]
                            for lt in range(SSM_WIDTH // LANES)], axis=1)
    ys = jax.nn.gelu(y_s5).astype(BF16)
    glu = _dot(ys, wg_ref[...]) + bg_ref[...]
    z_b = zb_ref[...].astype(F32)
    y_b = _dot((at_ref[...].astype(F32) * (z_b * _sigmoid(z_b))).astype(BF16), wpa_ref[...])
    z_a = za_ref[...].astype(F32)
    a_in = glu[:, :SSM_WIDTH] * _sigmoid(glu[:, SSM_WIDTH:]) * (z_a * _sigmoid(z_a))
    y_a = _dot(a_in.astype(BF16), wps_ref[...])
    g = g_ref[...].astype(F32)
    mix = _sigmoid(g[:, :D_MODEL]) * y_a + _sigmoid(g[:, D_MODEL:]) * y_b
    o_ref[...] = x_ref[...] + _dot(mix.astype(BF16), wo_ref[...])


def _tail(x2d, y_tb, attn2d, pz2d, w_glu, b_glu, w_ps, w_pa, w_out, bsz, seq):
    t = x2d.shape[0]
    tile = _token_tile(bsz, seq // TAIL_TT)
    row = lambda w, c: pl.BlockSpec((TAIL_TT, w), lambda i, c=c: (tile(i), c))
    const = lambda shape: pl.BlockSpec(shape, lambda i: (0, 0))
    return pl.pallas_call(
        _tail_kernel,
        grid=(t // TAIL_TT,),
        in_specs=[
            row(D_MODEL, 0),
            pl.BlockSpec((SSM_WIDTH // LANES, TAIL_TT * bsz, LANES), lambda i: (0, i // bsz, 0)),
            row(SSM_WIDTH, PZ_ZA // SSM_WIDTH),
            row(ATTN_WIDTH, 0),
            row(ATTN_WIDTH, PZ_ZB // ATTN_WIDTH),
            row(2 * D_MODEL, PZ_G // (2 * D_MODEL)),
            const((SSM_WIDTH, 2 * SSM_WIDTH)), const((1, 2 * SSM_WIDTH)),
            const((SSM_WIDTH, D_MODEL)), const((ATTN_WIDTH, D_MODEL)), const((D_MODEL, D_MODEL)),
        ],
        out_specs=pl.BlockSpec((TAIL_TT, D_MODEL), lambda i: (tile(i), 0)),
        out_shape=jax.ShapeDtypeStruct((t, D_MODEL), F32),
        compiler_params=pltpu.CompilerParams(
            dimension_semantics=("arbitrary",), vmem_limit_bytes=VMEM_LIMIT),
        name="tail",
    )(x2d, y_tb, pz2d, attn2d, pz2d, pz2d, w_glu, b_glu, w_ps, w_pa, w_out)


def _rope_tables(seq):
    half = ROPE_DIM // 2
    inv = ROPE_THETA ** (-np.arange(0, ROPE_DIM, 2, dtype=np.float64) / ROPE_DIM)
    ang = np.arange(seq, dtype=np.float64)[:, None] * inv[None, :]
    cos, sin = np.cos(ang).astype(np.float32), np.sin(ang).astype(np.float32)
    zeros = np.zeros((seq, HEAD_DIM - ROPE_DIM), np.float32)
    z8 = np.zeros((seq, half), np.float32)
    cos_h = np.concatenate([cos, cos, np.ones_like(zeros)], axis=1)
    sa_h = np.concatenate([-sin, z8, zeros], axis=1)
    sb_h = np.concatenate([z8, sin, zeros], axis=1)
    two = lambda t: np.concatenate([t, t], axis=1).astype(np.float32)
    return two(cos_h), two(sa_h), two(sb_h)


def kernel(x, norm_w, w_in, b_gate, q_norm_w, k_norm_w, ssm_lam_re, ssm_lam_im, ssm_log_dt,
           ssm_b_re, ssm_b_im, ssm_c_re, ssm_c_im, ssm_d, w_glu, b_glu,
           w_proj_ssm, w_proj_attn, w_out):
    bsz, seq, d_model = x.shape
    depth = norm_w.shape[0]
    assert d_model == D_MODEL and w_in.shape[-1] == IN_WIDTH
    assert bsz == SUBLANES, "S5 rows (chunk, batch) must fill whole sublane tiles"
    assert seq % IN_TT == 0 and seq % TAIL_TT == 0 and seq % CHUNK == 0
    assert all(seq % (d * QBLK) == 0 for d in DILATIONS) and (seq // QBLK) % 4 == 0
    cosf, sa, sb = _rope_tables(seq)
    ones_blk = jnp.asarray(np.kron(np.eye(2 * LANES // HEAD_DIM, dtype=np.float32),
                                   np.full((HEAD_DIM, HEAD_DIM), 1.0 / HEAD_DIM, np.float32)), BF16)
    for layer in range(depth):
        x2d = x.reshape(bsz * seq, D_MODEL)
        q_gain = q_norm_w[layer].astype(F32) * (LOG2E * HEAD_DIM ** -0.5)
        qk_w_row = jnp.concatenate([jnp.tile(q_gain, len(DILATIONS) * ATTN_SLOTS),
                                    jnp.tile(k_norm_w[layer].astype(F32), ATTN_SLOTS)])[None, :]
        pf2d, pz2d, u_tb = _in_proj(
            x2d, norm_w[layer][None, :].astype(F32), w_in[layer].astype(BF16),
            b_gate[layer][None, :].astype(F32), cosf, sa, sb, qk_w_row, ones_blk, bsz, seq)
        attn = _attention(pf2d.reshape(bsz, seq, PF_WIDTH))

        toep, s5_in, s5_out, a_pow = _s5_prep(
            ssm_lam_re[layer].astype(F32), ssm_lam_im[layer].astype(F32),
            ssm_log_dt[layer].astype(F32), ssm_b_re[layer].astype(F32),
            ssm_b_im[layer].astype(F32), ssm_c_re[layer].astype(F32),
            ssm_c_im[layer].astype(F32))
        n_tiles = SSM_WIDTH // LANES
        y_tb = _s5(u_tb.reshape(n_tiles, seq // CHUNK, CHUNK * bsz, LANES),
                   ssm_d[layer].astype(F32).reshape(n_tiles, 1, LANES), toep, s5_in, s5_out,
                   a_pow, bsz)

        out2d = _tail(x2d, y_tb.reshape(n_tiles, seq * bsz, LANES),
                      attn.reshape(bsz * seq, ATTN_WIDTH), pz2d,
                      w_glu[layer].astype(BF16), b_glu[layer][None, :].astype(F32),
                      w_proj_ssm[layer].astype(BF16), w_proj_attn[layer].astype(BF16),
                      w_out[layer].astype(BF16), bsz, seq)
        x = out2d.reshape(bsz, seq, D_MODEL)
    return x
```

```python
import math

import jax
import jax.numpy as jnp
import numpy as np
from jax import lax
from jax.experimental import pallas as pl
from jax.experimental.pallas import tpu as pltpu

F32 = jnp.float32
BF16 = jnp.bfloat16

D_MODEL = 1024
SSM_WIDTH = 512
SSM_GROUP = 16
SSM_GROUPS = 32
SSM_STATE = 64
HEAD_DIM = 64
ATTN_SLOTS = 8
ATTN_WIDTH = 512
DILATIONS = (1, 4, 16)
BAND_HALF = 64
ROPE_THETA = 500000.0
ROPE_DIM = 16
EPS = 1e-6
NEG_INF = -1e30
IN_WIDTH = 6144
COL_U, COL_ZA, COL_Q, COL_K, COL_V, COL_ZB, COL_G = 0, 512, 1024, 2560, 3072, 3584, 4096

LANES = 128
SUBLANES = 8
VMEM_LIMIT = 56 * 1024 * 1024

CHUNK = 16


def _dot(a, b):
    return jnp.dot(a, b, preferred_element_type=F32)


def _dot_nt(a, b):
    return lax.dot_general(a, b, (((1,), (1,)), ((), ())), preferred_element_type=F32)


def _dot_nt_f32(a, b):
    a_hi, b_hi = a.astype(BF16), b.astype(BF16)
    a_lo = (a - a_hi.astype(F32)).astype(BF16)
    b_lo = (b - b_hi.astype(F32)).astype(BF16)
    return _dot_nt(a_hi, b_hi) + (_dot_nt(a_hi, b_lo) + _dot_nt(a_lo, b_hi))


def _sigmoid(v):
    return 0.5 * jnp.tanh(0.5 * v) + 0.5


IN_TT = 512
IN_TN = 512
QK_ROWS = 256
LOG2E = math.log2(math.e)
PF_Q, PF_K, PF_V, PF_WIDTH = 0, 1536, 2048, 2560
PZ_G, PZ_ZA, PZ_ZB, PZ_WIDTH = 0, 2048, 2560, 3072
_Q_COLS = tuple((COL_Q + o, PF_Q + o) for o in range(0, COL_K - COL_Q, IN_TN))
_G_COLS = tuple((COL_G + o, PZ_G + o) for o in range(0, IN_WIDTH - COL_G, IN_TN))
F32_DEST = dict(((COL_K, PF_K), (COL_V, PF_V)) + _Q_COLS)
BF16_DEST = dict(((COL_ZA, PZ_ZA), (COL_ZB, PZ_ZB)) + _G_COLS)


def _in_proj_kernel(x_ref, nw_ref, w_ref, b_ref, cos_ref, sa_ref, sb_ref, qkw_ref, ones_ref,
                    pf_ref, pz_ref, utb_ref):
    bsz = utb_ref.shape[1] // IN_TT
    batch = pl.program_id(0) % bsz
    x = x_ref[...]
    var = jnp.mean(x * x, axis=-1, keepdims=True)
    h = (x * lax.rsqrt(var + EPS) * nw_ref[...]).astype(BF16)
    ones_blk = ones_ref[...]
    for c0 in range(0, IN_WIDTH, IN_TN):
        acc = _dot(h, w_ref[:, c0:c0 + IN_TN])
        if c0 in BF16_DEST:
            if c0 >= COL_G:
                acc = acc + b_ref[:, c0 - COL_G:c0 - COL_G + IN_TN]
            pz_ref[:, BF16_DEST[c0]:BF16_DEST[c0] + IN_TN] = acc.astype(BF16)
            continue
        if c0 == COL_U:
            for lt in range(SSM_WIDTH // LANES):
                utb_ref[lt, pl.ds(batch, IN_TT, stride=bsz), :] = acc[:, lt * LANES:(lt + 1) * LANES]
            continue
        d0 = F32_DEST[c0]
        if not COL_Q <= c0 < COL_V:
            pf_ref[:, d0:d0 + IN_TN] = acc
            continue
        for r0 in range(0, IN_TT, QK_ROWS):
            rows = slice(r0, r0 + QK_ROWS)
            for t2 in range(0, IN_TN, 2 * LANES):
                xq2 = acc[rows, t2:t2 + 2 * LANES]
                ms2 = _dot((xq2 * xq2).astype(BF16), ones_blk)
                xn2 = xq2 * lax.rsqrt(ms2 + EPS) * qkw_ref[:, c0 - COL_Q + t2:
                                                            c0 - COL_Q + t2 + 2 * LANES]
                for t in (0, LANES):
                    xn = xn2[:, t:t + LANES]
                    pf_ref[rows, d0 + t2 + t:d0 + t2 + t + LANES] = (
                        xn * cos_ref[rows, :]
                        + pltpu.roll(xn, LANES - ROPE_DIM // 2, 1) * sa_ref[rows, :]
                        + pltpu.roll(xn, ROPE_DIM // 2, 1) * sb_ref[rows, :])


def _token_tile(bsz, tiles_per_seq):
    return lambda i: (i % bsz) * tiles_per_seq + i // bsz


def _in_proj(x2d, norm_w, w_in_bf16, b_gate_row, cosf, sa, sb, qk_w_row, ones_blk, bsz, seq):
    t = x2d.shape[0]
    tile = _token_tile(bsz, seq // IN_TT)
    const = lambda shape: pl.BlockSpec(shape, lambda i: (0, 0))
    rope = lambda: pl.BlockSpec((IN_TT, LANES), lambda i: (i // bsz, 0))
    return pl.pallas_call(
        _in_proj_kernel,
        grid=(t // IN_TT,),
        in_specs=[
            pl.BlockSpec((IN_TT, D_MODEL), lambda i: (tile(i), 0)),
            const((1, D_MODEL)),
            pl.BlockSpec((D_MODEL, IN_WIDTH), lambda i: (0, 0), pipeline_mode=pl.Buffered(1)),
            const((1, IN_WIDTH - COL_G)),
            rope(), rope(), rope(),
            const((1, COL_V - COL_Q)),
            const((2 * LANES, 2 * LANES)),
        ],
        out_specs=[
            pl.BlockSpec((IN_TT, PF_WIDTH), lambda i: (tile(i), 0)),
            pl.BlockSpec((IN_TT, PZ_WIDTH), lambda i: (tile(i), 0)),
            pl.BlockSpec((SSM_WIDTH // LANES, IN_TT * bsz, LANES), lambda i: (0, i // bsz, 0)),
        ],
        out_shape=[
            jax.ShapeDtypeStruct((t, PF_WIDTH), F32),
            jax.ShapeDtypeStruct((t, PZ_WIDTH), BF16),
            jax.ShapeDtypeStruct((SSM_WIDTH // LANES, t, LANES), F32),
        ],
        compiler_params=pltpu.CompilerParams(
            dimension_semantics=("arbitrary",), vmem_limit_bytes=VMEM_LIMIT),
        name="in_proj",
    )(x2d, norm_w, w_in_bf16, b_gate_row, cosf, sa, sb, qk_w_row, ones_blk)


QBLK = 128
KWIN = QBLK + 2 * BAND_HALF
NORM_ROWS = 256
MERGE_ROWS = 64


def _attn_kernel(q0_ref, q1_ref, q2_ref, k_ref, v_ref, bias_ref, o_ref,
                 va_s, vb_s, oacc_s, den_s, max_s, s_s):
    seq = k_ref.shape[1]
    q_refs = (q0_ref, q1_ref, q2_ref)
    head0 = lax.broadcasted_iota(jnp.int32, (QBLK, LANES), 1) < HEAD_DIM
    head0_rows = lax.broadcasted_iota(jnp.int32, (NORM_ROWS, LANES), 1) < HEAD_DIM

    def prep(i, carry):
        rows = pl.ds(pl.multiple_of(i * NORM_ROWS, NORM_ROWS), NORM_ROWS)
        v = v_ref[0, rows, :]
        va_s[rows, :] = jnp.where(head0_rows, v, 1.0)
        vb_s[rows, :] = jnp.where(head0_rows, 1.0, v)
        return carry

    lax.fori_loop(0, seq // NORM_ROWS, prep, 0, unroll=True)

    patterns = []
    for p, d in enumerate(DILATIONS):
        n = seq // d
        nblk = n // QBLK
        kw = min(n, KWIN)

        def rows_of(idx, d=d, n=n, nblk=nblk, kw=kw):
            r = idx // nblk
            q0 = (idx % nblk) * QBLK
            ks = jnp.clip(q0 - BAND_HALF, 0, n - kw)
            return (pl.ds(r + d * q0, QBLK, stride=d), pl.ds(r + d * ks, kw, stride=d),
                    (q0 - ks) // BAND_HALF)

        def scores(idx, slot, p=p, kw=kw, rows_of=rows_of):
            qrows, krows, case = rows_of(idx)
            qb = q_refs[p][0, qrows, :]
            kb = k_ref[0, krows, :].astype(BF16)
            bias = bias_ref[case, :, :kw]
            s_s[slot, :QBLK, :kw] = _dot_nt(jnp.where(head0, qb, 0.0).astype(BF16), kb) + bias
            s_s[slot, QBLK:, :kw] = _dot_nt(jnp.where(head0, 0.0, qb).astype(BF16), kb) + bias

        def weigh(idx, slot, p=p, kw=kw, rows_of=rows_of):
            qrows, krows, _ = rows_of(idx)
            s = s_s[slot, :, :kw]
            m = jnp.max(s, axis=-1, keepdims=True)
            e = jnp.exp2(s - m).astype(BF16)
            o_a = _dot(e[:QBLK], va_s[krows, :].astype(BF16))
            o_b = _dot(e[QBLK:], vb_s[krows, :].astype(BF16))
            oacc_s[p, qrows, :] = jnp.where(head0, o_a, o_b)
            den_s[p, qrows, :] = jnp.where(head0, o_b, o_a)
            max_s[p, qrows, :] = jnp.where(head0, jnp.broadcast_to(m[:QBLK], (QBLK, LANES)),
                                           jnp.broadcast_to(m[QBLK:], (QBLK, LANES)))

        patterns.append((scores, weigh, d * nblk))

    def pair(weigh, i0, cur, ahead):
        nxt = 2 - cur
        if ahead is not None:
            ahead[0](ahead[1], nxt)
        weigh(i0, cur)
        if ahead is not None:
            ahead[0](ahead[1] + 1, nxt + 1)
        weigh(i0 + 1, cur + 1)

    patterns[0][0](0, 0)
    patterns[0][0](1, 1)
    for which, (scores, weigh, n_blocks) in enumerate(patterns):
        def quad(j, carry, scores=scores, weigh=weigh):
            pair(weigh, 4 * j, 0, (scores, 4 * j + 2))
            pair(weigh, 4 * j + 2, 2, (scores, 4 * j + 4))
            return carry

        lax.fori_loop(0, n_blocks // 4 - 1, quad, 0, unroll=True)
        pair(weigh, n_blocks - 4, 0, (scores, n_blocks - 2))
        following = (patterns[which + 1][0], 0) if which + 1 < len(patterns) else None
        pair(weigh, n_blocks - 2, 2, following)

    def combine(i, carry):
        rows = pl.ds(pl.multiple_of(i * MERGE_ROWS, MERGE_ROWS), MERGE_ROWS)
        m0, m1, m2 = max_s[0, rows, :], max_s[1, rows, :], max_s[2, rows, :]
        m = jnp.maximum(jnp.maximum(m0, m1), m2)
        w0, w1, w2 = jnp.exp2(m0 - m), jnp.exp2(m1 - m), jnp.exp2(m2 - m)
        num = w0 * oacc_s[0, rows, :] + w1 * oacc_s[1, rows, :] + w2 * oacc_s[2, rows, :]
        d0, d1, d2 = (pltpu.roll(den_s[p, rows, :], HEAD_DIM, 1) for p in range(3))
        o_ref[0, rows, :] = (num / (w0 * d0 + w1 * d1 + w2 * d2)).astype(o_ref.dtype)
        return carry

    lax.fori_loop(0, seq // MERGE_ROWS, combine, 0, unroll=True)


def _band_bias():
    i = np.arange(QBLK)[:, None]
    j = np.arange(KWIN)[None, :]
    off = np.arange(3)[:, None, None] * BAND_HALF
    return np.where(np.abs(j - i - off) <= BAND_HALF, 0.0, NEG_INF).astype(np.float32)


def _attention(proj3d):
    bsz, seq, _ = proj3d.shape
    n_pairs = ATTN_WIDTH // LANES

    def qspec(p):
        return pl.BlockSpec((1, seq, LANES),
                            lambda b, hp, p=p: (b, 0, PF_Q // LANES + p * n_pairs + hp))

    seq_tile = lambda: pltpu.VMEM((seq, LANES), F32)
    pat_tile = lambda: pltpu.VMEM((len(DILATIONS), seq, LANES), F32)
    return pl.pallas_call(
        _attn_kernel,
        grid=(bsz, n_pairs),
        in_specs=[
            qspec(0), qspec(1), qspec(2),
            pl.BlockSpec((1, seq, LANES), lambda b, hp: (b, 0, PF_K // LANES + hp)),
            pl.BlockSpec((1, seq, LANES), lambda b, hp: (b, 0, PF_V // LANES + hp)),
            pl.BlockSpec((3, QBLK, KWIN), lambda b, hp: (0, 0, 0)),
        ],
        out_specs=pl.BlockSpec((1, seq, LANES), lambda b, hp: (b, 0, hp)),
        out_shape=jax.ShapeDtypeStruct((bsz, seq, ATTN_WIDTH), BF16),
        scratch_shapes=[
            seq_tile(), seq_tile(),
            pat_tile(), pat_tile(), pat_tile(),
            pltpu.VMEM((4, 2 * QBLK, KWIN), F32),
        ],
        compiler_params=pltpu.CompilerParams(
            dimension_semantics=("arbitrary", "arbitrary"), vmem_limit_bytes=VMEM_LIMIT),
        name="dilated_attention",
    )(proj3d, proj3d, proj3d, proj3d, proj3d, _band_bias())


GP = 2
N_PAIRS = SSM_GROUPS // GP
PG = GP * SSM_GROUP
PW = CHUNK * PG
SW = 4 * LANES
LAG_REP = LANES // PG


def _s5_prep_kernel(lre_ref, lim_ref, ldt_ref, bre_ref, bim_ref, cre_ref, cim_ref,
                    toep_ref, win_ref, wout_ref, a_ref, wf32_s):
    four = lambda ref: jnp.concatenate([ref[0, 0], ref[0, 0], ref[1, 0], ref[1, 0]], axis=1)
    lr = jnp.minimum(four(lre_ref), -1e-4)
    li = four(lim_ref)
    dt = jnp.exp(four(ldt_ref))
    col = lax.broadcasted_iota(jnp.int32, (1, SW), 1)
    is_re = (col // LANES) % 2 == 0
    is_fwd = col < 2 * LANES

    def power(k):
        mag = jnp.exp(k * (lr * dt))
        return mag * jnp.cos(k * (li * dt)), mag * jnp.sin(k * (li * dt))

    a_re, a_im = power(1.0)
    nr, ni, mag2 = a_re - 1.0, a_im, lr * lr + li * li
    f_re, f_im = (nr * lr + ni * li) / mag2, (ni * lr - nr * li) / mag2

    row_g = lax.broadcasted_iota(jnp.int32, (PG, SW), 0) // SSM_GROUP
    col_g = (lax.broadcasted_iota(jnp.int32, (PG, SW), 1) % LANES) // SSM_STATE
    diag = row_g == col_g
    b_re = jnp.where(diag, four(bre_ref), 0.0)
    b_im = jnp.where(diag, four(bim_ref), 0.0)
    c_re = jnp.where(diag, four(cre_ref), 0.0)
    c_im = jnp.where(diag, four(cim_ref), 0.0)
    bb_re = f_re * b_re - f_im * b_im
    bb_im = f_re * b_im + f_im * b_re

    step = lax.broadcasted_iota(jnp.int32, (CHUNK, SW), 0)
    k_in = jnp.where(is_fwd, CHUNK - 1 - step, step).astype(F32)
    k_out = jnp.where(is_fwd, step + 1, CHUNK - step).astype(F32)
    in_re, in_im = power(k_in)
    out_re, out_im = power(k_out)
    x1, x2 = jnp.where(is_re, in_re, in_im), jnp.where(is_re, -in_im, in_re)
    y1, y2 = jnp.where(is_re, out_re, -out_im), jnp.where(is_re, -out_im, -out_re)
    for s in range(CHUNK):
        rows = slice(s * PG, (s + 1) * PG)
        w = bb_re * x1[s:s + 1] + bb_im * x2[s:s + 1]
        wf32_s[rows, :] = w
        win_ref[0, rows, :] = w.astype(BF16)
        wout_ref[0, rows, :] = (c_re * y1[s:s + 1] + c_im * y2[s:s + 1]).astype(BF16)

    a16_re, a16_im = power(float(CHUNK))
    a_ref[0] = jnp.broadcast_to(jnp.where(is_re, a16_re, a16_im), a_ref.shape[1:])

    c_cat = jnp.where(is_re, c_re, -c_im)
    c_rep = jnp.concatenate([c_cat] * LAG_REP, axis=0)
    half = 2 * LANES
    lag_f = _dot_nt_f32(wf32_s[:, :half], c_rep[:, :half])
    lag_b = _dot_nt_f32(wf32_s[:, half:], c_rep[:, half:])
    row_t = lax.broadcasted_iota(jnp.int32, (PW, LANES), 0) // PG
    lane_q = lax.broadcasted_iota(jnp.int32, (PW, LANES), 1) // PG
    zeros = lambda n: jnp.zeros((n * PG, LANES), F32)
    for q in range(PW // LANES):
        tile = jnp.zeros((PW, LANES), F32)
        for u in range(LAG_REP):
            t = q * LAG_REP + u
            up = CHUNK - 1 - t
            sh_f = lag_f if up == 0 else jnp.concatenate([lag_f[up * PG:], zeros(up)], axis=0)
            sh_b = lag_b if t == 0 else jnp.concatenate([zeros(t), lag_b[:(CHUNK - t) * PG]], axis=0)
            col_t = jnp.where(row_t <= t, sh_f, 0.0) + jnp.where(row_t >= t, sh_b, 0.0)
            tile = jnp.where(lane_q == u, col_t, tile)
        toep_ref[0, :, q * LANES:(q + 1) * LANES] = tile.astype(BF16)


def _s5_prep(lam_re, lam_im, log_dt, b_re, b_im, c_re, c_im):
    lanes = lambda t: t.reshape(2, N_PAIRS, 1, LANES)
    dt_b = jnp.broadcast_to(log_dt[..., None], lam_re.shape)

    def tiles(t_ghp):
        t = jnp.broadcast_to(t_ghp[:, :, :, None, :], t_ghp.shape[:3] + (GP, SSM_STATE))
        return t.reshape(2, N_PAIRS, PG, LANES)

    row = pl.BlockSpec((2, 1, 1, LANES), lambda i: (0, i, 0, 0))
    mat = pl.BlockSpec((2, 1, PG, LANES), lambda i: (0, i, 0, 0))
    out = lambda r: pl.BlockSpec((1, r, SW), lambda i: (i, 0, 0))
    return pl.pallas_call(
        _s5_prep_kernel,
        grid=(N_PAIRS,),
        in_specs=[row, row, row, mat, mat, mat, mat],
        out_specs=[out(PW), out(PW), out(PW), out(SUBLANES)],
        out_shape=[jax.ShapeDtypeStruct((N_PAIRS, PW, PW), BF16),
                   jax.ShapeDtypeStruct((N_PAIRS, PW, SW), BF16),
                   jax.ShapeDtypeStruct((N_PAIRS, PW, SW), BF16),
                   jax.ShapeDtypeStruct((N_PAIRS, SUBLANES, SW), F32)],
        scratch_shapes=[pltpu.VMEM((PW, SW), F32)],
        compiler_params=pltpu.CompilerParams(dimension_semantics=("arbitrary",)),
        name="s5_prep",
    )(lanes(lam_re), lanes(lam_im), lanes(dt_b), tiles(b_re.transpose(0, 1, 3, 2)),
      tiles(b_im.transpose(0, 1, 3, 2)), tiles(c_re), tiles(c_im))


SCAN_UNROLL = 4
GPS = LANES // PG
REGROUP_CHUNKS = 32


def _s5_kernel(u_ref, d_ref, toep_ref, win_ref, wout_ref, a_ref, y_ref, x_s, p_s, st_s):
    nc = u_ref.shape[1]
    bsz = u_ref.shape[2] // CHUNK
    piece = REGROUP_CHUNKS * bsz
    pp = pl.program_id(1)
    slot = lax.broadcasted_iota(jnp.int32, (piece, LANES), 1) // PG
    time_rows = lambda t: pl.ds(pl.multiple_of(t * bsz, bsz), bsz)

    def gather(cb, carry):
        chunks = pl.ds(pl.multiple_of(cb * REGROUP_CHUNKS, REGROUP_CHUNKS), REGROUP_CHUNKS)
        rows = pl.ds(pl.multiple_of(cb * piece, piece), piece)
        for q in range(CHUNK // GPS):
            acc = None
            for r in range(GPS):
                k = (pp + r) % GPS
                v = u_ref[0, chunks, time_rows(q * GPS + k), :].reshape(piece, LANES)
                v = v if r == 0 else pltpu.roll(v, r * PG, 1)
                acc = v if acc is None else jnp.where(slot == k, v, acc)
            x_s[rows, q * LANES:(q + 1) * LANES] = acc.astype(BF16)
        return carry

    lax.fori_loop(0, nc // REGROUP_CHUNKS, gather, 0, unroll=True)

    x = x_s[...]
    p_s[...] = _dot(x, win_ref[0])
    a_fr, a_fi, a_br, a_bi = (a_ref[0, :, q * LANES:(q + 1) * LANES] for q in range(4))
    tile = lambda q: slice(q * LANES, (q + 1) * LANES)

    def scan(c, carry):
        s_fr, s_fi, s_br, s_bi = carry
        rf = pl.ds(pl.multiple_of(c * bsz, bsz), bsz)
        rb = pl.ds(pl.multiple_of((nc - 1 - c) * bsz, bsz), bsz)
        st_s[rf, tile(0)] = s_fr
        st_s[rf, tile(1)] = s_fi
        st_s[rb, tile(2)] = s_br
        st_s[rb, tile(3)] = s_bi
        return (a_fr * s_fr - a_fi * s_fi + p_s[rf, tile(0)],
                a_fr * s_fi + a_fi * s_fr + p_s[rf, tile(1)],
                a_br * s_br - a_bi * s_bi + p_s[rb, tile(2)],
                a_br * s_bi + a_bi * s_br + p_s[rb, tile(3)])

    zero = jnp.zeros((bsz, LANES), F32)
    lax.fori_loop(0, nc, scan, (zero,) * 4, unroll=SCAN_UNROLL)
    p_s[...] = _dot(x, toep_ref[0]) + _dot_nt(st_s[...].astype(BF16), wout_ref[0])

    mine = (slot == pp).reshape(1, REGROUP_CHUNKS, bsz, LANES)
    d_skip = d_ref[0]

    def scatter(cb, carry):
        chunks = pl.ds(pl.multiple_of(cb * REGROUP_CHUNKS, REGROUP_CHUNKS), REGROUP_CHUNKS)
        rows = pl.ds(pl.multiple_of(cb * piece, piece), piece)
        for q in range(CHUNK // GPS):
            w = p_s[rows, q * LANES:(q + 1) * LANES]
            for r in range(GPS):
                k = (pp + GPS - r) % GPS
                e = w if r == 0 else pltpu.roll(w, r * PG, 1)
                where = (pl.ds(0, 1), chunks, time_rows(q * GPS + k), slice(None))
                e = e.reshape(1, REGROUP_CHUNKS, bsz, LANES) + d_skip * u_ref[where]
                pltpu.store(y_ref.at[where], e, mask=mine)
        return carry

    lax.fori_loop(0, nc // REGROUP_CHUNKS, scatter, 0, unroll=True)


def _s5(u_tb, d_tiles, toep, w_in, w_out, a_pow, bsz):
    n_tiles, nc, cb, _ = u_tb.shape
    rows = nc * bsz
    tile = pl.BlockSpec((1, nc, cb, LANES), lambda j, p: (j, 0, 0, 0))
    pair = lambda r, c: pl.BlockSpec((1, r, c), lambda j, p: (j * GPS + p, 0, 0))
    return pl.pallas_call(
        _s5_kernel,
        grid=(n_tiles, GPS),
        in_specs=[tile, pl.BlockSpec((1, 1, LANES), lambda j, p: (j, 0, 0)),
                  pair(PW, PW), pair(PW, SW), pair(PW, SW), pair(bsz, SW)],
        out_specs=tile,
        out_shape=jax.ShapeDtypeStruct(u_tb.shape, F32),
        scratch_shapes=[pltpu.VMEM((rows, PW), BF16),
                        pltpu.VMEM((rows, SW), F32),
                        pltpu.VMEM((rows, SW), F32)],
        compiler_params=pltpu.CompilerParams(
            dimension_semantics=("arbitrary", "arbitrary"), vmem_limit_bytes=VMEM_LIMIT),
        name="s5_chunked",
    )(u_tb, d_tiles, toep, w_in, w_out, a_pow)


TAIL_TT = 512


def _tail_kernel(x_ref, ytb_ref, za_ref, at_ref, zb_ref, g_ref, wg_ref, bg_ref,
                 wps_ref, wpa_ref, wo_ref, o_ref):
    bsz = ytb_ref.shape[1] // TAIL_TT
    batch = pl.program_id(0) % bsz
    y_s5 = jnp.concatenate([ytb_ref[lt, pl.ds(batch, TAIL_TT, stride=bsz), :]
                            for lt in range(SSM_WIDTH // LANES)], axis=1)
    ys = jax.nn.gelu(y_s5).astype(BF16)
    glu = _dot(ys, wg_ref[...]) + bg_ref[...]
    z_b = zb_ref[...].astype(F32)
    y_b = _dot((at_ref[...].astype(F32) * (z_b * _sigmoid(z_b))).astype(BF16), wpa_ref[...])
    z_a = za_ref[...].astype(F32)
    a_in = glu[:, :SSM_WIDTH] * _sigmoid(glu[:, SSM_WIDTH:]) * (z_a * _sigmoid(z_a))
    y_a = _dot(a_in.astype(BF16), wps_ref[...])
    g = g_ref[...].astype(F32)
    mix = _sigmoid(g[:, :D_MODEL]) * y_a + _sigmoid(g[:, D_MODEL:]) * y_b
    o_ref[...] = x_ref[...] + _dot(mix.astype(BF16), wo_ref[...])


def _tail(x2d, y_tb, attn2d, pz2d, w_glu, b_glu, w_ps, w_pa, w_out, bsz, seq):
    t = x2d.shape[0]
    tile = _token_tile(bsz, seq // TAIL_TT)
    row = lambda w, c: pl.BlockSpec((TAIL_TT, w), lambda i, c=c: (tile(i), c))
    const = lambda shape: pl.BlockSpec(shape, lambda i: (0, 0))
    return pl.pallas_call(
        _tail_kernel,
        grid=(t // TAIL_TT,),
        in_specs=[
            row(D_MODEL, 0),
            pl.BlockSpec((SSM_WIDTH // LANES, TAIL_TT * bsz, LANES), lambda i: (0, i // bsz, 0)),
            row(SSM_WIDTH, PZ_ZA // SSM_WIDTH),
            row(ATTN_WIDTH, 0),
            row(ATTN_WIDTH, PZ_ZB // ATTN_WIDTH),
            row(2 * D_MODEL, PZ_G // (2 * D_MODEL)),
            const((SSM_WIDTH, 2 * SSM_WIDTH)), const((1, 2 * SSM_WIDTH)),
            const((SSM_WIDTH, D_MODEL)), const((ATTN_WIDTH, D_MODEL)), const((D_MODEL, D_MODEL)),
        ],
        out_specs=pl.BlockSpec((TAIL_TT, D_MODEL), lambda i: (tile(i), 0)),
        out_shape=jax.ShapeDtypeStruct((t, D_MODEL), F32),
        compiler_params=pltpu.CompilerParams(
            dimension_semantics=("arbitrary",), vmem_limit_bytes=VMEM_LIMIT),
        name="tail",
    )(x2d, y_tb, pz2d, attn2d, pz2d, pz2d, w_glu, b_glu, w_ps, w_pa, w_out)


def _rope_tables(seq):
    half = ROPE_DIM // 2
    inv = ROPE_THETA ** (-np.arange(0, ROPE_DIM, 2, dtype=np.float64) / ROPE_DIM)
    ang = np.arange(seq, dtype=np.float64)[:, None] * inv[None, :]
    cos, sin = np.cos(ang).astype(np.float32), np.sin(ang).astype(np.float32)
    zeros = np.zeros((seq, HEAD_DIM - ROPE_DIM), np.float32)
    z8 = np.zeros((seq, half), np.float32)
    cos_h = np.concatenate([cos, cos, np.ones_like(zeros)], axis=1)
    sa_h = np.concatenate([-sin, z8, zeros], axis=1)
    sb_h = np.concatenate([z8, sin, zeros], axis=1)
    two = lambda t: np.concatenate([t, t], axis=1).astype(np.float32)
    return two(cos_h), two(sa_h), two(sb_h)


def kernel(x, norm_w, w_in, b_gate, q_norm_w, k_norm_w, ssm_lam_re, ssm_lam_im, ssm_log_dt,
           ssm_b_re, ssm_b_im, ssm_c_re, ssm_c_im, ssm_d, w_glu, b_glu,
           w_proj_ssm, w_proj_attn, w_out):
    bsz, seq, d_model = x.shape
    depth = norm_w.shape[0]
    assert d_model == D_MODEL and w_in.shape[-1] == IN_WIDTH
    assert bsz == SUBLANES, "S5 rows (chunk, batch) must fill whole sublane tiles"
    assert seq % IN_TT == 0 and seq % TAIL_TT == 0 and seq % CHUNK == 0
    assert all(seq % (d * QBLK) == 0 for d in DILATIONS) and (seq // QBLK) % 4 == 0
    cosf, sa, sb = _rope_tables(seq)
    ones_blk = jnp.asarray(np.kron(np.eye(2 * LANES // HEAD_DIM, dtype=np.float32),
                                   np.full((HEAD_DIM, HEAD_DIM), 1.0 / HEAD_DIM, np.float32)), BF16)
    for layer in range(depth):
        x2d = x.reshape(bsz * seq, D_MODEL)
        q_gain = q_norm_w[layer].astype(F32) * (LOG2E * HEAD_DIM ** -0.5)
        qk_w_row = jnp.concatenate([jnp.tile(q_gain, len(DILATIONS) * ATTN_SLOTS),
                                    jnp.tile(k_norm_w[layer].astype(F32), ATTN_SLOTS)])[None, :]
        pf2d, pz2d, u_tb = _in_proj(
            x2d, norm_w[layer][None, :].astype(F32), w_in[layer].astype(BF16),
            b_gate[layer][None, :].astype(F32), cosf, sa, sb, qk_w_row, ones_blk, bsz, seq)
        attn = _attention(pf2d.reshape(bsz, seq, PF_WIDTH))

        toep, s5_in, s5_out, a_pow = _s5_prep(
            ssm_lam_re[layer].astype(F32), ssm_lam_im[layer].astype(F32),
            ssm_log_dt[layer].astype(F32), ssm_b_re[layer].astype(F32),
            ssm_b_im[layer].astype(F32), ssm_c_re[layer].astype(F32),
            ssm_c_im[layer].astype(F32))
        n_tiles = SSM_WIDTH // LANES
        y_tb = _s5(u_tb.reshape(n_tiles, seq // CHUNK, CHUNK * bsz, LANES),
                   ssm_d[layer].astype(F32).reshape(n_tiles, 1, LANES), toep, s5_in, s5_out,
                   a_pow, bsz)

        out2d = _tail(x2d, y_tb.reshape(n_tiles, seq * bsz, LANES),
                      attn.reshape(bsz * seq, ATTN_WIDTH), pz2d,
                      w_glu[layer].astype(BF16), b_glu[layer][None, :].astype(F32),
                      w_proj_ssm[layer].astype(BF16), w_proj_attn[layer].astype(BF16),
                      w_out[layer].astype(BF16), bsz, seq)
        x = out2d.reshape(bsz, seq, D_MODEL)
    return x
```

```python
import math

import jax
import jax.numpy as jnp
import numpy as np
from jax import lax
from jax.experimental import pallas as pl
from jax.experimental.pallas import tpu as pltpu

F32 = jnp.float32
BF16 = jnp.bfloat16

D_MODEL = 1024
SSM_WIDTH = 512
SSM_GROUP = 16
SSM_GROUPS = 32
SSM_STATE = 64
HEAD_DIM = 64
ATTN_SLOTS = 8
ATTN_WIDTH = 512
DILATIONS = (1, 4, 16)
BAND_HALF = 64
ROPE_THETA = 500000.0
ROPE_DIM = 16
EPS = 1e-6
NEG_INF = -1e30
IN_WIDTH = 6144
COL_U, COL_ZA, COL_Q, COL_K, COL_V, COL_ZB, COL_G = 0, 512, 1024, 2560, 3072, 3584, 4096

LANES = 128
SUBLANES = 8
VMEM_LIMIT = 56 * 1024 * 1024

CHUNK = 16


def _dot(a, b):
    return jnp.dot(a, b, preferred_element_type=F32)


def _dot_nt(a, b):
    return lax.dot_general(a, b, (((1,), (1,)), ((), ())), preferred_element_type=F32)


def _dot_nt_f32(a, b):
    a_hi, b_hi = a.astype(BF16), b.astype(BF16)
    a_lo = (a - a_hi.astype(F32)).astype(BF16)
    b_lo = (b - b_hi.astype(F32)).astype(BF16)
    return _dot_nt(a_hi, b_hi) + (_dot_nt(a_hi, b_lo) + _dot_nt(a_lo, b_hi))


def _sigmoid(v):
    return 0.5 * jnp.tanh(0.5 * v) + 0.5


IN_TT = 512
IN_TN = 512
QK_ROWS = 256
LOG2E = math.log2(math.e)
PF_Q, PF_K, PF_V, PF_WIDTH = 0, 1536, 2048, 2560
PZ_G, PZ_ZA, PZ_ZB, PZ_WIDTH = 0, 2048, 2560, 3072
_Q_COLS = tuple((COL_Q + o, PF_Q + o) for o in range(0, COL_K - COL_Q, IN_TN))
_G_COLS = tuple((COL_G + o, PZ_G + o) for o in range(0, IN_WIDTH - COL_G, IN_TN))
F32_DEST = dict(((COL_K, PF_K), (COL_V, PF_V)) + _Q_COLS)
BF16_DEST = dict(((COL_ZA, PZ_ZA), (COL_ZB, PZ_ZB)) + _G_COLS)


def _in_proj_kernel(x_ref, nw_ref, w_ref, b_ref, cos_ref, sa_ref, sb_ref, qkw_ref, ones_ref,
                    pf_ref, pz_ref, utb_ref):
    bsz = utb_ref.shape[1] // IN_TT
    batch = pl.program_id(0) % bsz
    x = x_ref[...]
    var = jnp.mean(x * x, axis=-1, keepdims=True)
    h = (x * lax.rsqrt(var + EPS) * nw_ref[...]).astype(BF16)
    ones_blk = ones_ref[...]
    for c0 in range(0, IN_WIDTH, IN_TN):
        acc = _dot(h, w_ref[:, c0:c0 + IN_TN])
        if c0 in BF16_DEST:
            if c0 >= COL_G:
                acc = acc + b_ref[:, c0 - COL_G:c0 - COL_G + IN_TN]
            pz_ref[:, BF16_DEST[c0]:BF16_DEST[c0] + IN_TN] = acc.astype(BF16)
            continue
        if c0 == COL_U:
            for lt in range(SSM_WIDTH // LANES):
                utb_ref[lt, pl.ds(batch, IN_TT, stride=bsz), :] = acc[:, lt * LANES:(lt + 1) * LANES]
            continue
        d0 = F32_DEST[c0]
        if not COL_Q <= c0 < COL_V:
            pf_ref[:, d0:d0 + IN_TN] = acc
            continue
        for r0 in range(0, IN_TT, QK_ROWS):
            rows = slice(r0, r0 + QK_ROWS)
            for t2 in range(0, IN_TN, 2 * LANES):
                xq2 = acc[rows, t2:t2 + 2 * LANES]
                ms2 = _dot((xq2 * xq2).astype(BF16), ones_blk)
                xn2 = xq2 * lax.rsqrt(ms2 + EPS) * qkw_ref[:, c0 - COL_Q + t2:
                                                            c0 - COL_Q + t2 + 2 * LANES]
                for t in (0, LANES):
                    xn = xn2[:, t:t + LANES]
                    pf_ref[rows, d0 + t2 + t:d0 + t2 + t + LANES] = (
                        xn * cos_ref[rows, :]
                        + pltpu.roll(xn, LANES - ROPE_DIM // 2, 1) * sa_ref[rows, :]
                        + pltpu.roll(xn, ROPE_DIM // 2, 1) * sb_ref[rows, :])


def _token_tile(bsz, tiles_per_seq):
    return lambda i: (i % bsz) * tiles_per_seq + i // bsz


def _in_proj(x2d, norm_w, w_in_bf16, b_gate_row, cosf, sa, sb, qk_w_row, ones_blk, bsz, seq):
    t = x2d.shape[0]
    tile = _token_tile(bsz, seq // IN_TT)
    const = lambda shape: pl.BlockSpec(shape, lambda i: (0, 0))
    rope = lambda: pl.BlockSpec((IN_TT, LANES), lambda i: (i // bsz, 0))
    return pl.pallas_call(
        _in_proj_kernel,
        grid=(t // IN_TT,),
        in_specs=[
            pl.BlockSpec((IN_TT, D_MODEL), lambda i: (tile(i), 0)),
            const((1, D_MODEL)),
            pl.BlockSpec((D_MODEL, IN_WIDTH), lambda i: (0, 0), pipeline_mode=pl.Buffered(1)),
            const((1, IN_WIDTH - COL_G)),
            rope(), rope(), rope(),
            const((1, COL_V - COL_Q)),
            const((2 * LANES, 2 * LANES)),
        ],
        out_specs=[
            pl.BlockSpec((IN_TT, PF_WIDTH), lambda i: (tile(i), 0)),
            pl.BlockSpec((IN_TT, PZ_WIDTH), lambda i: (tile(i), 0)),
            pl.BlockSpec((SSM_WIDTH // LANES, IN_TT * bsz, LANES), lambda i: (0, i // bsz, 0)),
        ],
        out_shape=[
            jax.ShapeDtypeStruct((t, PF_WIDTH), F32),
            jax.ShapeDtypeStruct((t, PZ_WIDTH), BF16),
            jax.ShapeDtypeStruct((SSM_WIDTH // LANES, t, LANES), F32),
        ],
        compiler_params=pltpu.CompilerParams(
            dimension_semantics=("arbitrary",), vmem_limit_bytes=VMEM_LIMIT),
        name="in_proj",
    )(x2d, norm_w, w_in_bf16, b_gate_row, cosf, sa, sb, qk_w_row, ones_blk)


QBLK = 128
KWIN = QBLK + 2 * BAND_HALF
NORM_ROWS = 256
MERGE_ROWS = 64


def _attn_kernel(q0_ref, q1_ref, q2_ref, k_ref, v_ref, bias_ref, o_ref,
                 va_s, vb_s, oacc_s, den_s, max_s, s_s):
    seq = k_ref.shape[1]
    q_refs = (q0_ref, q1_ref, q2_ref)
    head0 = lax.broadcasted_iota(jnp.int32, (QBLK, LANES), 1) < HEAD_DIM
    head0_rows = lax.broadcasted_iota(jnp.int32, (NORM_ROWS, LANES), 1) < HEAD_DIM

    def prep(i, carry):
        rows = pl.ds(pl.multiple_of(i * NORM_ROWS, NORM_ROWS), NORM_ROWS)
        v = v_ref[0, rows, :]
        va_s[rows, :] = jnp.where(head0_rows, v, 1.0)
        vb_s[rows, :] = jnp.where(head0_rows, 1.0, v)
        return carry

    lax.fori_loop(0, seq // NORM_ROWS, prep, 0, unroll=True)

    patterns = []
    for p, d in enumerate(DILATIONS):
        n = seq // d
        nblk = n // QBLK
        kw = min(n, KWIN)

        def rows_of(idx, d=d, n=n, nblk=nblk, kw=kw):
            r = idx // nblk
            q0 = (idx % nblk) * QBLK
            ks = jnp.clip(q0 - BAND_HALF, 0, n - kw)
            return (pl.ds(r + d * q0, QBLK, stride=d), pl.ds(r + d * ks, kw, stride=d),
                    (q0 - ks) // BAND_HALF)

        def scores(idx, slot, p=p, kw=kw, rows_of=rows_of):
            qrows, krows, case = rows_of(idx)
            qb = q_refs[p][0, qrows, :]
            kb = k_ref[0, krows, :].astype(BF16)
            bias = bias_ref[case, :, :kw]
            s_s[slot, :QBLK, :kw] = _dot_nt(jnp.where(head0, qb, 0.0).astype(BF16), kb) + bias
            s_s[slot, QBLK:, :kw] = _dot_nt(jnp.where(head0, 0.0, qb).astype(BF16), kb) + bias

        def weigh(idx, slot, p=p, kw=kw, rows_of=rows_of):
            qrows, krows, _ = rows_of(idx)
            s = s_s[slot, :, :kw]
            m = jnp.max(s, axis=-1, keepdims=True)
            e = jnp.exp2(s - m).astype(BF16)
            o_a = _dot(e[:QBLK], va_s[krows, :].astype(BF16))
            o_b = _dot(e[QBLK:], vb_s[krows, :].astype(BF16))
            oacc_s[p, qrows, :] = jnp.where(head0, o_a, o_b)
            den_s[p, qrows, :] = jnp.where(head0, o_b, o_a)
            max_s[p, qrows, :] = jnp.where(head0, jnp.broadcast_to(m[:QBLK], (QBLK, LANES)),
                                           jnp.broadcast_to(m[QBLK:], (QBLK, LANES)))

        patterns.append((scores, weigh, d * nblk))

    def pair(weigh, i0, cur, ahead):
        nxt = 2 - cur
        if ahead is not None:
            ahead[0](ahead[1], nxt)
        weigh(i0, cur)
        if ahead is not None:
            ahead[0](ahead[1] + 1, nxt + 1)
        weigh(i0 + 1, cur + 1)

    patterns[0][0](0, 0)
    patterns[0][0](1, 1)
    for which, (scores, weigh, n_blocks) in enumerate(patterns):
        def quad(j, carry, scores=scores, weigh=weigh):
            pair(weigh, 4 * j, 0, (scores, 4 * j + 2))
            pair(weigh, 4 * j + 2, 2, (scores, 4 * j + 4))
            return carry

        lax.fori_loop(0, n_blocks // 4 - 1, quad, 0, unroll=True)
        pair(weigh, n_blocks - 4, 0, (scores, n_blocks - 2))
        following = (patterns[which + 1][0], 0) if which + 1 < len(patterns) else None
        pair(weigh, n_blocks - 2, 2, following)

    def combine(i, carry):
        rows = pl.ds(pl.multiple_of(i * MERGE_ROWS, MERGE_ROWS), MERGE_ROWS)
        m0, m1, m2 = max_s[0, rows, :], max_s[1, rows, :], max_s[2, rows, :]
        m = jnp.maximum(jnp.maximum(m0, m1), m2)
        w0, w1, w2 = jnp.exp2(m0 - m), jnp.exp2(m1 - m), jnp.exp2(m2 - m)
        num = w0 * oacc_s[0, rows, :] + w1 * oacc_s[1, rows, :] + w2 * oacc_s[2, rows, :]
        d0, d1, d2 = (pltpu.roll(den_s[p, rows, :], HEAD_DIM, 1) for p in range(3))
        o_ref[0, rows, :] = (num / (w0 * d0 + w1 * d1 + w2 * d2)).astype(o_ref.dtype)
        return carry

    lax.fori_loop(0, seq // MERGE_ROWS, combine, 0, unroll=True)


def _band_bias():
    i = np.arange(QBLK)[:, None]
    j = np.arange(KWIN)[None, :]
    off = np.arange(3)[:, None, None] * BAND_HALF
    return np.where(np.abs(j - i - off) <= BAND_HALF, 0.0, NEG_INF).astype(np.float32)


def _attention(proj3d):
    bsz, seq, _ = proj3d.shape
    n_pairs = ATTN_WIDTH // LANES

    def qspec(p):
        return pl.BlockSpec((1, seq, LANES),
                            lambda b, hp, p=p: (b, 0, PF_Q // LANES + p * n_pairs + hp))

    seq_tile = lambda: pltpu.VMEM((seq, LANES), F32)
    pat_tile = lambda: pltpu.VMEM((len(DILATIONS), seq, LANES), F32)
    return pl.pallas_call(
        _attn_kernel,
        grid=(bsz, n_pairs),
        in_specs=[
            qspec(0), qspec(1), qspec(2),
            pl.BlockSpec((1, seq, LANES), lambda b, hp: (b, 0, PF_K // LANES + hp)),
            pl.BlockSpec((1, seq, LANES), lambda b, hp: (b, 0, PF_V // LANES + hp)),
            pl.BlockSpec((3, QBLK, KWIN), lambda b, hp: (0, 0, 0)),
        ],
        out_specs=pl.BlockSpec((1, seq, LANES), lambda b, hp: (b, 0, hp)),
        out_shape=jax.ShapeDtypeStruct((bsz, seq, ATTN_WIDTH), BF16),
        scratch_shapes=[
            seq_tile(), seq_tile(),
            pat_tile(), pat_tile(), pat_tile(),
            pltpu.VMEM((4, 2 * QBLK, KWIN), F32),
        ],
        compiler_params=pltpu.CompilerParams(
            dimension_semantics=("arbitrary", "arbitrary"), vmem_limit_bytes=VMEM_LIMIT),
        name="dilated_attention",
    )(proj3d, proj3d, proj3d, proj3d, proj3d, _band_bias())


GP = 2
N_PAIRS = SSM_GROUPS // GP
PG = GP * SSM_GROUP
PW = CHUNK * PG
SW = 4 * LANES
LAG_REP = LANES // PG


def _s5_prep_kernel(lre_ref, lim_ref, ldt_ref, bre_ref, bim_ref, cre_ref, cim_ref,
                    toep_ref, win_ref, wout_ref, a_ref, wf32_s):
    four = lambda ref: jnp.concatenate([ref[0, 0], ref[0, 0], ref[1, 0], ref[1, 0]], axis=1)
    lr = jnp.minimum(four(lre_ref), -1e-4)
    li = four(lim_ref)
    dt = jnp.exp(four(ldt_ref))
    col = lax.broadcasted_iota(jnp.int32, (1, SW), 1)
    is_re = (col // LANES) % 2 == 0
    is_fwd = col < 2 * LANES

    mag = jnp.exp(lr * dt)
    pw_re = [jnp.ones_like(mag), mag * jnp.cos(li * dt)]
    pw_im = [jnp.zeros_like(mag), mag * jnp.sin(li * dt)]
    for _ in range(CHUNK - 1):
        pw_re.append(pw_re[-1] * pw_re[1] - pw_im[-1] * pw_im[1])
        pw_im.append(pw_re[-2] * pw_im[1] + pw_im[-1] * pw_re[1])

    a_re, a_im = pw_re[1], pw_im[1]
    nr, ni, mag2 = a_re - 1.0, a_im, lr * lr + li * li
    f_re, f_im = (nr * lr + ni * li) / mag2, (ni * lr - nr * li) / mag2

    row_g = lax.broadcasted_iota(jnp.int32, (PG, SW), 0) // SSM_GROUP
    col_g = (lax.broadcasted_iota(jnp.int32, (PG, SW), 1) % LANES) // SSM_STATE
    diag = row_g == col_g
    b_re = jnp.where(diag, four(bre_ref), 0.0)
    b_im = jnp.where(diag, four(bim_ref), 0.0)
    c_re = jnp.where(diag, four(cre_ref), 0.0)
    c_im = jnp.where(diag, four(cim_ref), 0.0)
    bb_re = f_re * b_re - f_im * b_im
    bb_im = f_re * b_im + f_im * b_re

    for s in range(CHUNK):
        rows = slice(s * PG, (s + 1) * PG)
        in_re = jnp.where(is_fwd, pw_re[CHUNK - 1 - s], pw_re[s])
        in_im = jnp.where(is_fwd, pw_im[CHUNK - 1 - s], pw_im[s])
        w = (bb_re * jnp.where(is_re, in_re, in_im)
             + bb_im * jnp.where(is_re, -in_im, in_re))
        wf32_s[rows, :] = w
        win_ref[0, rows, :] = w.astype(BF16)
        out_re = jnp.where(is_fwd, pw_re[s + 1], pw_re[CHUNK - s])
        out_im = jnp.where(is_fwd, pw_im[s + 1], pw_im[CHUNK - s])
        wout_ref[0, rows, :] = (c_re * jnp.where(is_re, out_re, -out_im)
                                + c_im * jnp.where(is_re, -out_im, -out_re)).astype(BF16)

    a_ref[0] = jnp.broadcast_to(jnp.where(is_re, pw_re[CHUNK], pw_im[CHUNK]), a_ref.shape[1:])

    c_cat = jnp.where(is_re, c_re, -c_im)
    c_rep = jnp.concatenate([c_cat] * LAG_REP, axis=0)
    half = 2 * LANES
    lag_f = _dot_nt_f32(wf32_s[:, :half], c_rep[:, :half])
    lag_b = _dot_nt_f32(wf32_s[:, half:], c_rep[:, half:])
    row_t = lax.broadcasted_iota(jnp.int32, (PW, LANES), 0) // PG
    lane_q = lax.broadcasted_iota(jnp.int32, (PW, LANES), 1) // PG
    zeros = lambda n: jnp.zeros((n * PG, LANES), F32)
    for q in range(PW // LANES):
        tile = jnp.zeros((PW, LANES), F32)
        for u in range(LAG_REP):
            t = q * LAG_REP + u
            up = CHUNK - 1 - t
            sh_f = lag_f if up == 0 else jnp.concatenate([lag_f[up * PG:], zeros(up)], axis=0)
            sh_b = lag_b if t == 0 else jnp.concatenate([zeros(t), lag_b[:(CHUNK - t) * PG]], axis=0)
            col_t = jnp.where(row_t <= t, sh_f, 0.0) + jnp.where(row_t >= t, sh_b, 0.0)
            tile = jnp.where(lane_q == u, col_t, tile)
        toep_ref[0, :, q * LANES:(q + 1) * LANES] = tile.astype(BF16)


def _s5_prep(lam_re, lam_im, log_dt, b_re, b_im, c_re, c_im):
    lanes = lambda t: t.reshape(2, N_PAIRS, 1, LANES)
    dt_b = jnp.broadcast_to(log_dt[..., None], lam_re.shape)

    def tiles(t_ghp):
        t = jnp.broadcast_to(t_ghp[:, :, :, None, :], t_ghp.shape[:3] + (GP, SSM_STATE))
        return t.reshape(2, N_PAIRS, PG, LANES)

    row = pl.BlockSpec((2, 1, 1, LANES), lambda i: (0, i, 0, 0))
    mat = pl.BlockSpec((2, 1, PG, LANES), lambda i: (0, i, 0, 0))
    out = lambda r: pl.BlockSpec((1, r, SW), lambda i: (i, 0, 0))
    return pl.pallas_call(
        _s5_prep_kernel,
        grid=(N_PAIRS,),
        in_specs=[row, row, row, mat, mat, mat, mat],
        out_specs=[out(PW), out(PW), out(PW), out(SUBLANES)],
        out_shape=[jax.ShapeDtypeStruct((N_PAIRS, PW, PW), BF16),
                   jax.ShapeDtypeStruct((N_PAIRS, PW, SW), BF16),
                   jax.ShapeDtypeStruct((N_PAIRS, PW, SW), BF16),
                   jax.ShapeDtypeStruct((N_PAIRS, SUBLANES, SW), F32)],
        scratch_shapes=[pltpu.VMEM((PW, SW), F32)],
        compiler_params=pltpu.CompilerParams(dimension_semantics=("arbitrary",)),
        name="s5_prep",
    )(lanes(lam_re), lanes(lam_im), lanes(dt_b), tiles(b_re.transpose(0, 1, 3, 2)),
      tiles(b_im.transpose(0, 1, 3, 2)), tiles(c_re), tiles(c_im))


SCAN_STEPS = 8
GPS = LANES // PG
REGROUP_CHUNKS = 32


def _s5_kernel(u_ref, d_ref, toep_ref, win_ref, wout_ref, a_ref, y_ref, x_s, p_s, st_s):
    nc = u_ref.shape[1]
    bsz = u_ref.shape[2] // CHUNK
    piece = REGROUP_CHUNKS * bsz
    pp = pl.program_id(1)
    slot = lax.broadcasted_iota(jnp.int32, (piece, LANES), 1) // PG
    time_rows = lambda t: pl.ds(pl.multiple_of(t * bsz, bsz), bsz)

    def gather(cb, carry):
        chunks = pl.ds(pl.multiple_of(cb * REGROUP_CHUNKS, REGROUP_CHUNKS), REGROUP_CHUNKS)
        rows = pl.ds(pl.multiple_of(cb * piece, piece), piece)
        for q in range(CHUNK // GPS):
            acc = None
            for r in range(GPS):
                k = (pp + r) % GPS
                v = u_ref[0, chunks, time_rows(q * GPS + k), :].reshape(piece, LANES)
                v = v if r == 0 else pltpu.roll(v, r * PG, 1)
                acc = v if acc is None else jnp.where(slot == k, v, acc)
            x_s[rows, q * LANES:(q + 1) * LANES] = acc.astype(BF16)
        return carry

    lax.fori_loop(0, nc // REGROUP_CHUNKS, gather, 0)

    x = x_s[...]
    p_s[...] = _dot(x, win_ref[0])
    a_fr, a_fi, a_br, a_bi = (a_ref[0, :, q * LANES:(q + 1) * LANES] for q in range(4))
    tile = lambda q: slice(q * LANES, (q + 1) * LANES)

    def scan(i, carry):
        base_f = pl.multiple_of(i * (SCAN_STEPS * bsz), SCAN_STEPS * bsz)
        base_b = pl.multiple_of((nc - (i + 1) * SCAN_STEPS) * bsz, SCAN_STEPS * bsz)
        s_fr, s_fi, s_br, s_bi = carry
        for k in range(SCAN_STEPS):
            rf = pl.ds(base_f + k * bsz, bsz)
            rb = pl.ds(base_b + (SCAN_STEPS - 1 - k) * bsz, bsz)
            st_s[rf, tile(0)] = s_fr
            st_s[rf, tile(1)] = s_fi
            st_s[rb, tile(2)] = s_br
            st_s[rb, tile(3)] = s_bi
            s_fr, s_fi, s_br, s_bi = (a_fr * s_fr - a_fi * s_fi + p_s[rf, tile(0)],
                                      a_fr * s_fi + a_fi * s_fr + p_s[rf, tile(1)],
                                      a_br * s_br - a_bi * s_bi + p_s[rb, tile(2)],
                                      a_br * s_bi + a_bi * s_br + p_s[rb, tile(3)])
        return s_fr, s_fi, s_br, s_bi

    zero = jnp.zeros((bsz, LANES), F32)
    lax.fori_loop(0, nc // SCAN_STEPS, scan, (zero,) * 4)
    p_s[...] = _dot(x, toep_ref[0]) + _dot_nt(st_s[...].astype(BF16), wout_ref[0])

    mine = (slot == pp).reshape(1, REGROUP_CHUNKS, bsz, LANES)
    d_skip = d_ref[0]

    def scatter(cb, carry):
        chunks = pl.ds(pl.multiple_of(cb * REGROUP_CHUNKS, REGROUP_CHUNKS), REGROUP_CHUNKS)
        rows = pl.ds(pl.multiple_of(cb * piece, piece), piece)
        for q in range(CHUNK // GPS):
            w = p_s[rows, q * LANES:(q + 1) * LANES]
            for r in range(GPS):
                k = (pp + GPS - r) % GPS
                e = w if r == 0 else pltpu.roll(w, r * PG, 1)
                where = (pl.ds(0, 1), chunks, time_rows(q * GPS + k), slice(None))
                e = e.reshape(1, REGROUP_CHUNKS, bsz, LANES) + d_skip * u_ref[where]
                pltpu.store(y_ref.at[where], e, mask=mine)
        return carry

    lax.fori_loop(0, nc // REGROUP_CHUNKS, scatter, 0)


def _s5(u_tb, d_tiles, toep, w_in, w_out, a_pow, bsz):
    n_tiles, nc, cb, _ = u_tb.shape
    rows = nc * bsz
    tile = pl.BlockSpec((1, nc, cb, LANES), lambda j, p: (j, 0, 0, 0))
    pair = lambda r, c: pl.BlockSpec((1, r, c), lambda j, p: (j * GPS + p, 0, 0))
    return pl.pallas_call(
        _s5_kernel,
        grid=(n_tiles, GPS),
        in_specs=[tile, pl.BlockSpec((1, 1, LANES), lambda j, p: (j, 0, 0)),
                  pair(PW, PW), pair(PW, SW), pair(PW, SW), pair(bsz, SW)],
        out_specs=tile,
        out_shape=jax.ShapeDtypeStruct(u_tb.shape, F32),
        scratch_shapes=[pltpu.VMEM((rows, PW), BF16),
                        pltpu.VMEM((rows, SW), F32),
                        pltpu.VMEM((rows, SW), F32)],
        compiler_params=pltpu.CompilerParams(
            dimension_semantics=("arbitrary", "arbitrary"), vmem_limit_bytes=VMEM_LIMIT),
        name="s5_chunked",
    )(u_tb, d_tiles, toep, w_in, w_out, a_pow)


TAIL_TT = 512


def _tail_kernel(x_ref, ytb_ref, za_ref, at_ref, zb_ref, g_ref, wg_ref, bg_ref,
                 wps_ref, wpa_ref, wo_ref, o_ref):
    bsz = ytb_ref.shape[1] // TAIL_TT
    batch = pl.program_id(0) % bsz
    y_s5 = jnp.concatenate([ytb_ref[lt, pl.ds(batch, TAIL_TT, stride=bsz), :]
                            for lt in range(SSM_WIDTH // LANES)], axis=1)
    ys = jax.nn.gelu(y_s5).astype(BF16)
    glu = _dot(ys, wg_ref[...]) + bg_ref[...]
    z_b = zb_ref[...].astype(F32)
    y_b = _dot((at_ref[...].astype(F32) * (z_b * _sigmoid(z_b))).astype(BF16), wpa_ref[...])
    z_a = za_ref[...].astype(F32)
    a_in = glu[:, :SSM_WIDTH] * _sigmoid(glu[:, SSM_WIDTH:]) * (z_a * _sigmoid(z_a))
    y_a = _dot(a_in.astype(BF16), wps_ref[...])
    g = g_ref[...].astype(F32)
    mix = _sigmoid(g[:, :D_MODEL]) * y_a + _sigmoid(g[:, D_MODEL:]) * y_b
    o_ref[...] = x_ref[...] + _dot(mix.astype(BF16), wo_ref[...])


def _tail(x2d, y_tb, attn2d, pz2d, w_glu, b_glu, w_ps, w_pa, w_out, bsz, seq):
    t = x2d.shape[0]
    tile = _token_tile(bsz, seq // TAIL_TT)
    row = lambda w, c: pl.BlockSpec((TAIL_TT, w), lambda i, c=c: (tile(i), c))
    const = lambda shape: pl.BlockSpec(shape, lambda i: (0, 0))
    return pl.pallas_call(
        _tail_kernel,
        grid=(t // TAIL_TT,),
        in_specs=[
            row(D_MODEL, 0),
            pl.BlockSpec((SSM_WIDTH // LANES, TAIL_TT * bsz, LANES), lambda i: (0, i // bsz, 0)),
            row(SSM_WIDTH, PZ_ZA // SSM_WIDTH),
            row(ATTN_WIDTH, 0),
            row(ATTN_WIDTH, PZ_ZB // ATTN_WIDTH),
            row(2 * D_MODEL, PZ_G // (2 * D_MODEL)),
            const((SSM_WIDTH, 2 * SSM_WIDTH)), const((1, 2 * SSM_WIDTH)),
            const((SSM_WIDTH, D_MODEL)), const((ATTN_WIDTH, D_MODEL)), const((D_MODEL, D_MODEL)),
        ],
        out_specs=pl.BlockSpec((TAIL_TT, D_MODEL), lambda i: (tile(i), 0)),
        out_shape=jax.ShapeDtypeStruct((t, D_MODEL), F32),
        compiler_params=pltpu.CompilerParams(
            dimension_semantics=("arbitrary",), vmem_limit_bytes=VMEM_LIMIT),
        name="tail",
    )(x2d, y_tb, pz2d, attn2d, pz2d, pz2d, w_glu, b_glu, w_ps, w_pa, w_out)


def _rope_tables(seq):
    half = ROPE_DIM // 2
    inv = ROPE_THETA ** (-np.arange(0, ROPE_DIM, 2, dtype=np.float64) / ROPE_DIM)
    ang = np.arange(seq, dtype=np.float64)[:, None] * inv[None, :]
    cos, sin = np.cos(ang).astype(np.float32), np.sin(ang).astype(np.float32)
    zeros = np.zeros((seq, HEAD_DIM - ROPE_DIM), np.float32)
    z8 = np.zeros((seq, half), np.float32)
    cos_h = np.concatenate([cos, cos, np.ones_like(zeros)], axis=1)
    sa_h = np.concatenate([-sin, z8, zeros], axis=1)
    sb_h = np.concatenate([z8, sin, zeros], axis=1)
    two = lambda t: np.concatenate([t, t], axis=1).astype(np.float32)
    return two(cos_h), two(sa_h), two(sb_h)


def kernel(x, norm_w, w_in, b_gate, q_norm_w, k_norm_w, ssm_lam_re, ssm_lam_im, ssm_log_dt,
           ssm_b_re, ssm_b_im, ssm_c_re, ssm_c_im, ssm_d, w_glu, b_glu,
           w_proj_ssm, w_proj_attn, w_out):
    bsz, seq, d_model = x.shape
    depth = norm_w.shape[0]
    assert d_model == D_MODEL and w_in.shape[-1] == IN_WIDTH
    assert bsz == SUBLANES, "S5 rows (chunk, batch) must fill whole sublane tiles"
    assert seq % IN_TT == 0 and seq % TAIL_TT == 0 and seq % CHUNK == 0
    assert all(seq % (d * QBLK) == 0 for d in DILATIONS) and (seq // QBLK) % 4 == 0
    cosf, sa, sb = _rope_tables(seq)
    ones_blk = jnp.asarray(np.kron(np.eye(2 * LANES // HEAD_DIM, dtype=np.float32),
                                   np.full((HEAD_DIM, HEAD_DIM), 1.0 / HEAD_DIM, np.float32)), BF16)
    for layer in range(depth):
        x2d = x.reshape(bsz * seq, D_MODEL)
        q_gain = q_norm_w[layer].astype(F32) * (LOG2E * HEAD_DIM ** -0.5)
        qk_w_row = jnp.concatenate([jnp.tile(q_gain, len(DILATIONS) * ATTN_SLOTS),
                                    jnp.tile(k_norm_w[layer].astype(F32), ATTN_SLOTS)])[None, :]
        pf2d, pz2d, u_tb = _in_proj(
            x2d, norm_w[layer][None, :].astype(F32), w_in[layer].astype(BF16),
            b_gate[layer][None, :].astype(F32), cosf, sa, sb, qk_w_row, ones_blk, bsz, seq)
        attn = _attention(pf2d.reshape(bsz, seq, PF_WIDTH))

        toep, s5_in, s5_out, a_pow = _s5_prep(
            ssm_lam_re[layer].astype(F32), ssm_lam_im[layer].astype(F32),
            ssm_log_dt[layer].astype(F32), ssm_b_re[layer].astype(F32),
            ssm_b_im[layer].astype(F32), ssm_c_re[layer].astype(F32),
            ssm_c_im[layer].astype(F32))
        n_tiles = SSM_WIDTH // LANES
        y_tb = _s5(u_tb.reshape(n_tiles, seq // CHUNK, CHUNK * bsz, LANES),
                   ssm_d[layer].astype(F32).reshape(n_tiles, 1, LANES), toep, s5_in, s5_out,
                   a_pow, bsz)

        out2d = _tail(x2d, y_tb.reshape(n_tiles, seq * bsz, LANES),
                      attn.reshape(bsz * seq, ATTN_WIDTH), pz2d,
                      w_glu[layer].astype(BF16), b_glu[layer][None, :].astype(F32),
                      w_proj_ssm[layer].astype(BF16), w_proj_attn[layer].astype(BF16),
                      w_out[layer].astype(BF16), bsz, seq)
        x = out2d.reshape(bsz, seq, D_MODEL)
    return x
```

```python
import math

import jax
import jax.numpy as jnp
import numpy as np
from jax import lax
from jax.experimental import pallas as pl
from jax.experimental.pallas import tpu as pltpu

F32 = jnp.float32
BF16 = jnp.bfloat16

D_MODEL = 1024
SSM_WIDTH = 512
SSM_GROUP = 16
SSM_GROUPS = 32
SSM_STATE = 64
HEAD_DIM = 64
ATTN_SLOTS = 8
ATTN_WIDTH = 512
DILATIONS = (1, 4, 16)
BAND_HALF = 64
ROPE_THETA = 500000.0
ROPE_DIM = 16
EPS = 1e-6
NEG_INF = -1e30
IN_WIDTH = 6144
COL_U, COL_ZA, COL_Q, COL_K, COL_V, COL_ZB, COL_G = 0, 512, 1024, 2560, 3072, 3584, 4096

LANES = 128
SUBLANES = 8
VMEM_LIMIT = 56 * 1024 * 1024

CHUNK = 16
GP = 2
N_PAIRS = SSM_GROUPS // GP
PG = GP * SSM_GROUP
PW = CHUNK * PG
SW = 4 * LANES
GPS = LANES // PG


def _dot(a, b):
    return jnp.dot(a, b, preferred_element_type=F32)


def _dot_nt(a, b):
    return lax.dot_general(a, b, (((1,), (1,)), ((), ())), preferred_element_type=F32)


def _dot_nt_f32(a, b):
    a_hi, b_hi = a.astype(BF16), b.astype(BF16)
    a_lo = (a - a_hi.astype(F32)).astype(BF16)
    b_lo = (b - b_hi.astype(F32)).astype(BF16)
    return _dot_nt(a_hi, b_hi) + (_dot_nt(a_hi, b_lo) + _dot_nt(a_lo, b_hi))


def _rotate_rows(tile, forward):
    residue = lax.broadcasted_iota(jnp.int32, tile.shape, 0) % GPS
    out = tile
    for k in range(1, GPS):
        out = jnp.where(residue == k, pltpu.roll(tile, (k if forward else GPS - k) * PG, 1), out)
    return out


def _sigmoid(v):
    return 0.5 * jnp.tanh(0.5 * v) + 0.5


IN_TT = 512
IN_TN = 512
QK_ROWS = 256
LOG2E = math.log2(math.e)
PF_Q, PF_K, PF_V, PF_WIDTH = 0, 1536, 2048, 2560
PZ_G, PZ_ZA, PZ_ZB, PZ_WIDTH = 0, 2048, 2560, 3072
_Q_COLS = tuple((COL_Q + o, PF_Q + o) for o in range(0, COL_K - COL_Q, IN_TN))
_G_COLS = tuple((COL_G + o, PZ_G + o) for o in range(0, IN_WIDTH - COL_G, IN_TN))
F32_DEST = dict(((COL_K, PF_K), (COL_V, PF_V)) + _Q_COLS)
BF16_DEST = dict(((COL_ZA, PZ_ZA), (COL_ZB, PZ_ZB)) + _G_COLS)


def _in_proj_kernel(x_ref, nw_ref, w_ref, b_ref, cos_ref, sa_ref, sb_ref, qkw_ref, ones_ref,
                    pf_ref, pz_ref, utb_ref):
    bsz = utb_ref.shape[1] // IN_TT
    batch = pl.program_id(0) % bsz
    x = x_ref[...]
    var = jnp.mean(x * x, axis=-1, keepdims=True)
    h = (x * lax.rsqrt(var + EPS) * nw_ref[...]).astype(BF16)
    ones_blk = ones_ref[...]
    for c0 in range(0, IN_WIDTH, IN_TN):
        acc = _dot(h, w_ref[:, c0:c0 + IN_TN])
        if c0 in BF16_DEST:
            if c0 >= COL_G:
                acc = acc + b_ref[:, c0 - COL_G:c0 - COL_G + IN_TN]
            pz_ref[:, BF16_DEST[c0]:BF16_DEST[c0] + IN_TN] = acc.astype(BF16)
            continue
        if c0 == COL_U:
            for lt in range(SSM_WIDTH // LANES):
                utb_ref[lt, pl.ds(batch, IN_TT, stride=bsz), :] = _rotate_rows(
                    acc[:, lt * LANES:(lt + 1) * LANES], forward=True)
            continue
        d0 = F32_DEST[c0]
        if not COL_Q <= c0 < COL_V:
            pf_ref[:, d0:d0 + IN_TN] = acc
            continue
        for r0 in range(0, IN_TT, QK_ROWS):
            rows = slice(r0, r0 + QK_ROWS)
            for t2 in range(0, IN_TN, 2 * LANES):
                xq2 = acc[rows, t2:t2 + 2 * LANES]
                ms2 = _dot((xq2 * xq2).astype(BF16), ones_blk)
                xn2 = xq2 * lax.rsqrt(ms2 + EPS) * qkw_ref[:, c0 - COL_Q + t2:
                                                            c0 - COL_Q + t2 + 2 * LANES]
                for t in (0, LANES):
                    xn = xn2[:, t:t + LANES]
                    pf_ref[rows, d0 + t2 + t:d0 + t2 + t + LANES] = (
                        xn * cos_ref[rows, :]
                        + pltpu.roll(xn, LANES - ROPE_DIM // 2, 1) * sa_ref[rows, :]
                        + pltpu.roll(xn, ROPE_DIM // 2, 1) * sb_ref[rows, :])


def _token_tile(bsz, tiles_per_seq):
    return lambda i: (i % bsz) * tiles_per_seq + i // bsz


def _in_proj(x2d, norm_w, w_in_bf16, b_gate_row, cosf, sa, sb, qk_w_row, ones_blk, bsz, seq):
    t = x2d.shape[0]
    tile = _token_tile(bsz, seq // IN_TT)
    const = lambda shape: pl.BlockSpec(shape, lambda i: (0, 0))
    rope = lambda: pl.BlockSpec((IN_TT, LANES), lambda i: (i // bsz, 0))
    return pl.pallas_call(
        _in_proj_kernel,
        grid=(t // IN_TT,),
        in_specs=[
            pl.BlockSpec((IN_TT, D_MODEL), lambda i: (tile(i), 0)),
            const((1, D_MODEL)),
            pl.BlockSpec((D_MODEL, IN_WIDTH), lambda i: (0, 0), pipeline_mode=pl.Buffered(1)),
            const((1, IN_WIDTH - COL_G)),
            rope(), rope(), rope(),
            const((1, COL_V - COL_Q)),
            const((2 * LANES, 2 * LANES)),
        ],
        out_specs=[
            pl.BlockSpec((IN_TT, PF_WIDTH), lambda i: (tile(i), 0)),
            pl.BlockSpec((IN_TT, PZ_WIDTH), lambda i: (tile(i), 0)),
            pl.BlockSpec((SSM_WIDTH // LANES, IN_TT * bsz, LANES), lambda i: (0, i // bsz, 0)),
        ],
        out_shape=[
            jax.ShapeDtypeStruct((t, PF_WIDTH), F32),
            jax.ShapeDtypeStruct((t, PZ_WIDTH), BF16),
            jax.ShapeDtypeStruct((SSM_WIDTH // LANES, t, LANES), F32),
        ],
        compiler_params=pltpu.CompilerParams(
            dimension_semantics=("arbitrary",), vmem_limit_bytes=VMEM_LIMIT),
        name="in_proj",
    )(x2d, norm_w, w_in_bf16, b_gate_row, cosf, sa, sb, qk_w_row, ones_blk)


QBLK = 128
KWIN = QBLK + 2 * BAND_HALF
NORM_ROWS = 256
MERGE_ROWS = 64


def _attn_kernel(q0_ref, q1_ref, q2_ref, k_ref, v_ref, bias_ref, o_ref,
                 va_s, vb_s, oacc_s, den_s, max_s, s_s):
    seq = k_ref.shape[1]
    q_refs = (q0_ref, q1_ref, q2_ref)
    head0 = lax.broadcasted_iota(jnp.int32, (QBLK, LANES), 1) < HEAD_DIM
    head0_rows = lax.broadcasted_iota(jnp.int32, (NORM_ROWS, LANES), 1) < HEAD_DIM

    def prep(i, carry):
        rows = pl.ds(pl.multiple_of(i * NORM_ROWS, NORM_ROWS), NORM_ROWS)
        v = v_ref[0, rows, :]
        va_s[rows, :] = jnp.where(head0_rows, v, 1.0)
        vb_s[rows, :] = jnp.where(head0_rows, 1.0, v)
        return carry

    lax.fori_loop(0, seq // NORM_ROWS, prep, 0, unroll=True)

    patterns = []
    for p, d in enumerate(DILATIONS):
        n = seq // d
        nblk = n // QBLK
        kw = min(n, KWIN)

        def rows_of(idx, d=d, n=n, nblk=nblk, kw=kw):
            r = idx // nblk
            q0 = (idx % nblk) * QBLK
            ks = jnp.clip(q0 - BAND_HALF, 0, n - kw)
            return (pl.ds(r + d * q0, QBLK, stride=d), pl.ds(r + d * ks, kw, stride=d),
                    (q0 - ks) // BAND_HALF)

        def scores(idx, slot, p=p, kw=kw, rows_of=rows_of):
            qrows, krows, case = rows_of(idx)
            qb = q_refs[p][0, qrows, :]
            kb = k_ref[0, krows, :].astype(BF16)
            bias = bias_ref[case, :, :kw]
            s_s[slot, :QBLK, :kw] = _dot_nt(jnp.where(head0, qb, 0.0).astype(BF16), kb) + bias
            s_s[slot, QBLK:, :kw] = _dot_nt(jnp.where(head0, 0.0, qb).astype(BF16), kb) + bias

        def weigh(idx, slot, p=p, kw=kw, rows_of=rows_of):
            qrows, krows, _ = rows_of(idx)
            s = s_s[slot, :, :kw]
            m = jnp.max(s, axis=-1, keepdims=True)
            e = jnp.exp2(s - m).astype(BF16)
            o_a = _dot(e[:QBLK], va_s[krows, :].astype(BF16))
            o_b = _dot(e[QBLK:], vb_s[krows, :].astype(BF16))
            oacc_s[p, qrows, :] = jnp.where(head0, o_a, o_b)
            den_s[p, qrows, :] = jnp.where(head0, o_b, o_a)
            max_s[p, qrows, :] = jnp.where(head0, jnp.broadcast_to(m[:QBLK], (QBLK, LANES)),
                                           jnp.broadcast_to(m[QBLK:], (QBLK, LANES)))

        patterns.append((scores, weigh, d * nblk))

    def pair(weigh, i0, cur, ahead):
        nxt = 2 - cur
        if ahead is not None:
            ahead[0](ahead[1], nxt)
        weigh(i0, cur)
        if ahead is not None:
            ahead[0](ahead[1] + 1, nxt + 1)
        weigh(i0 + 1, cur + 1)

    patterns[0][0](0, 0)
    patterns[0][0](1, 1)
    for which, (scores, weigh, n_blocks) in enumerate(patterns):
        def quad(j, carry, scores=scores, weigh=weigh):
            pair(weigh, 4 * j, 0, (scores, 4 * j + 2))
            pair(weigh, 4 * j + 2, 2, (scores, 4 * j + 4))
            return carry

        lax.fori_loop(0, n_blocks // 4 - 1, quad, 0, unroll=True)
        pair(weigh, n_blocks - 4, 0, (scores, n_blocks - 2))
        following = (patterns[which + 1][0], 0) if which + 1 < len(patterns) else None
        pair(weigh, n_blocks - 2, 2, following)

    def combine(i, carry):
        rows = pl.ds(pl.multiple_of(i * MERGE_ROWS, MERGE_ROWS), MERGE_ROWS)
        m0, m1, m2 = max_s[0, rows, :], max_s[1, rows, :], max_s[2, rows, :]
        m = jnp.maximum(jnp.maximum(m0, m1), m2)
        w0, w1, w2 = jnp.exp2(m0 - m), jnp.exp2(m1 - m), jnp.exp2(m2 - m)
        num = w0 * oacc_s[0, rows, :] + w1 * oacc_s[1, rows, :] + w2 * oacc_s[2, rows, :]
        d0, d1, d2 = (pltpu.roll(den_s[p, rows, :], HEAD_DIM, 1) for p in range(3))
        o_ref[0, rows, :] = (num / (w0 * d0 + w1 * d1 + w2 * d2)).astype(o_ref.dtype)
        return carry

    lax.fori_loop(0, seq // MERGE_ROWS, combine, 0, unroll=True)


def _band_bias():
    i = np.arange(QBLK)[:, None]
    j = np.arange(KWIN)[None, :]
    off = np.arange(3)[:, None, None] * BAND_HALF
    return np.where(np.abs(j - i - off) <= BAND_HALF, 0.0, NEG_INF).astype(np.float32)


def _attention(proj3d):
    bsz, seq, _ = proj3d.shape
    n_pairs = ATTN_WIDTH // LANES

    def qspec(p):
        return pl.BlockSpec((1, seq, LANES),
                            lambda b, hp, p=p: (b, 0, PF_Q // LANES + p * n_pairs + hp))

    seq_tile = lambda: pltpu.VMEM((seq, LANES), F32)
    pat_tile = lambda: pltpu.VMEM((len(DILATIONS), seq, LANES), F32)
    return pl.pallas_call(
        _attn_kernel,
        grid=(bsz, n_pairs),
        in_specs=[
            qspec(0), qspec(1), qspec(2),
            pl.BlockSpec((1, seq, LANES), lambda b, hp: (b, 0, PF_K // LANES + hp)),
            pl.BlockSpec((1, seq, LANES), lambda b, hp: (b, 0, PF_V // LANES + hp)),
            pl.BlockSpec((3, QBLK, KWIN), lambda b, hp: (0, 0, 0)),
        ],
        out_specs=pl.BlockSpec((1, seq, LANES), lambda b, hp: (b, 0, hp)),
        out_shape=jax.ShapeDtypeStruct((bsz, seq, ATTN_WIDTH), BF16),
        scratch_shapes=[
            seq_tile(), seq_tile(),
            pat_tile(), pat_tile(), pat_tile(),
            pltpu.VMEM((4, 2 * QBLK, KWIN), F32),
        ],
        compiler_params=pltpu.CompilerParams(
            dimension_semantics=("arbitrary", "arbitrary"), vmem_limit_bytes=VMEM_LIMIT),
        name="dilated_attention",
    )(proj3d, proj3d, proj3d, proj3d, proj3d, _band_bias())


LAG_REP = LANES // PG


def _s5_prep_kernel(lre_ref, lim_ref, ldt_ref, bre_ref, bim_ref, cre_ref, cim_ref,
                    toep_ref, win_ref, wout_ref, a_ref, wf32_s):
    four = lambda ref: jnp.concatenate([ref[0, 0], ref[0, 0], ref[1, 0], ref[1, 0]], axis=1)
    lr = jnp.minimum(four(lre_ref), -1e-4)
    li = four(lim_ref)
    dt = jnp.exp(four(ldt_ref))
    col = lax.broadcasted_iota(jnp.int32, (1, SW), 1)
    is_re = (col // LANES) % 2 == 0
    is_fwd = col < 2 * LANES

    mag = jnp.exp(lr * dt)
    pw_re = [jnp.ones_like(mag), mag * jnp.cos(li * dt)]
    pw_im = [jnp.zeros_like(mag), mag * jnp.sin(li * dt)]
    for _ in range(CHUNK - 1):
        pw_re.append(pw_re[-1] * pw_re[1] - pw_im[-1] * pw_im[1])
        pw_im.append(pw_re[-2] * pw_im[1] + pw_im[-1] * pw_re[1])

    a_re, a_im = pw_re[1], pw_im[1]
    nr, ni, mag2 = a_re - 1.0, a_im, lr * lr + li * li
    f_re, f_im = (nr * lr + ni * li) / mag2, (ni * lr - nr * li) / mag2

    row_g = lax.broadcasted_iota(jnp.int32, (PG, SW), 0) // SSM_GROUP
    col_g = (lax.broadcasted_iota(jnp.int32, (PG, SW), 1) % LANES) // SSM_STATE
    diag = row_g == col_g
    b_re = jnp.where(diag, four(bre_ref), 0.0)
    b_im = jnp.where(diag, four(bim_ref), 0.0)
    c_re = jnp.where(diag, four(cre_ref), 0.0)
    c_im = jnp.where(diag, four(cim_ref), 0.0)
    bb_re = f_re * b_re - f_im * b_im
    bb_im = f_re * b_im + f_im * b_re

    for s in range(CHUNK):
        rows = slice(s * PG, (s + 1) * PG)
        in_re = jnp.where(is_fwd, pw_re[CHUNK - 1 - s], pw_re[s])
        in_im = jnp.where(is_fwd, pw_im[CHUNK - 1 - s], pw_im[s])
        w = (bb_re * jnp.where(is_re, in_re, in_im)
             + bb_im * jnp.where(is_re, -in_im, in_re))
        wf32_s[rows, :] = w
        win_ref[0, rows, :] = w.astype(BF16)
        out_re = jnp.where(is_fwd, pw_re[s + 1], pw_re[CHUNK - s])
        out_im = jnp.where(is_fwd, pw_im[s + 1], pw_im[CHUNK - s])
        wout_ref[0, rows, :] = (c_re * jnp.where(is_re, out_re, -out_im)
                                + c_im * jnp.where(is_re, -out_im, -out_re)).astype(BF16)

    a_ref[0] = jnp.broadcast_to(jnp.where(is_re, pw_re[CHUNK], pw_im[CHUNK]), a_ref.shape[1:])

    c_cat = jnp.where(is_re, c_re, -c_im)
    c_rep = jnp.concatenate([c_cat] * LAG_REP, axis=0)
    half = 2 * LANES
    lag_f = _dot_nt_f32(wf32_s[:, :half], c_rep[:, :half])
    lag_b = _dot_nt_f32(wf32_s[:, half:], c_rep[:, half:])
    row_t = lax.broadcasted_iota(jnp.int32, (PW, LANES), 0) // PG
    lane_q = lax.broadcasted_iota(jnp.int32, (PW, LANES), 1) // PG
    zeros = lambda n: jnp.zeros((n * PG, LANES), F32)
    for q in range(PW // LANES):
        tile = jnp.zeros((PW, LANES), F32)
        for u in range(LAG_REP):
            t = q * LAG_REP + u
            up = CHUNK - 1 - t
            sh_f = lag_f if up == 0 else jnp.concatenate([lag_f[up * PG:], zeros(up)], axis=0)
            sh_b = lag_b if t == 0 else jnp.concatenate([zeros(t), lag_b[:(CHUNK - t) * PG]], axis=0)
            col_t = jnp.where(row_t <= t, sh_f, 0.0) + jnp.where(row_t >= t, sh_b, 0.0)
            tile = jnp.where(lane_q == u, col_t, tile)
        toep_ref[0, :, q * LANES:(q + 1) * LANES] = tile.astype(BF16)


def _s5_prep(lam_re, lam_im, log_dt, b_re, b_im, c_re, c_im):
    lanes = lambda t: t.reshape(2, N_PAIRS, 1, LANES)
    dt_b = jnp.broadcast_to(log_dt[..., None], lam_re.shape)

    def tiles(t_ghp):
        t = jnp.broadcast_to(t_ghp[:, :, :, None, :], t_ghp.shape[:3] + (GP, SSM_STATE))
        return t.reshape(2, N_PAIRS, PG, LANES)

    row = pl.BlockSpec((2, 1, 1, LANES), lambda i: (0, i, 0, 0))
    mat = pl.BlockSpec((2, 1, PG, LANES), lambda i: (0, i, 0, 0))
    out = lambda r: pl.BlockSpec((1, r, SW), lambda i: (i, 0, 0))
    return pl.pallas_call(
        _s5_prep_kernel,
        grid=(N_PAIRS,),
        in_specs=[row, row, row, mat, mat, mat, mat],
        out_specs=[out(PW), out(PW), out(PW), out(SUBLANES)],
        out_shape=[jax.ShapeDtypeStruct((N_PAIRS, PW, PW), BF16),
                   jax.ShapeDtypeStruct((N_PAIRS, PW, SW), BF16),
                   jax.ShapeDtypeStruct((N_PAIRS, PW, SW), BF16),
                   jax.ShapeDtypeStruct((N_PAIRS, SUBLANES, SW), F32)],
        scratch_shapes=[pltpu.VMEM((PW, SW), F32)],
        compiler_params=pltpu.CompilerParams(dimension_semantics=("arbitrary",)),
        name="s5_prep",
    )(lanes(lam_re), lanes(lam_im), lanes(dt_b), tiles(b_re.transpose(0, 1, 3, 2)),
      tiles(b_im.transpose(0, 1, 3, 2)), tiles(c_re), tiles(c_im))


SCAN_STEPS = 8
REGROUP_CHUNKS = 32


def _s5_kernel(u_ref, d_ref, toep_ref, win_ref, wout_ref, a_ref, y_ref, x_s, p_s, st_s):
    nc = u_ref.shape[1]
    bsz = u_ref.shape[2] // CHUNK
    piece = REGROUP_CHUNKS * bsz
    pp = pl.program_id(1)
    slot = lax.broadcasted_iota(jnp.int32, (piece, LANES), 1) // PG
    time_rows = lambda t: slice(t * bsz, (t + 1) * bsz)

    def gather(cb, carry):
        chunks = pl.ds(pl.multiple_of(cb * REGROUP_CHUNKS, REGROUP_CHUNKS), REGROUP_CHUNKS)
        rows = pl.ds(pl.multiple_of(cb * piece, piece), piece)
        for q in range(CHUNK // GPS):
            acc = None
            for k in range(GPS):
                v = u_ref[0, chunks, time_rows(q * GPS + k), :].reshape(piece, LANES)
                acc = v if acc is None else jnp.where(slot == (pp + k) % GPS, v, acc)
            x_s[rows, q * LANES:(q + 1) * LANES] = pltpu.roll(
                acc, ((GPS - pp) % GPS) * PG, 1).astype(BF16)
        return carry

    lax.fori_loop(0, nc // REGROUP_CHUNKS, gather, 0)

    x = x_s[...]
    p_s[...] = _dot(x, win_ref[0])
    a_fr, a_fi, a_br, a_bi = (a_ref[0, :, q * LANES:(q + 1) * LANES] for q in range(4))
    tile = lambda q: slice(q * LANES, (q + 1) * LANES)

    def scan(i, carry):
        base_f = pl.multiple_of(i * (SCAN_STEPS * bsz), SCAN_STEPS * bsz)
        base_b = pl.multiple_of((nc - (i + 1) * SCAN_STEPS) * bsz, SCAN_STEPS * bsz)
        s_fr, s_fi, s_br, s_bi = carry
        for k in range(SCAN_STEPS):
            rf = pl.ds(base_f + k * bsz, bsz)
            rb = pl.ds(base_b + (SCAN_STEPS - 1 - k) * bsz, bsz)
            st_s[rf, tile(0)] = s_fr
            st_s[rf, tile(1)] = s_fi
            st_s[rb, tile(2)] = s_br
            st_s[rb, tile(3)] = s_bi
            s_fr, s_fi, s_br, s_bi = (a_fr * s_fr - a_fi * s_fi + p_s[rf, tile(0)],
                                      a_fr * s_fi + a_fi * s_fr + p_s[rf, tile(1)],
                                      a_br * s_br - a_bi * s_bi + p_s[rb, tile(2)],
                                      a_br * s_bi + a_bi * s_br + p_s[rb, tile(3)])
        return s_fr, s_fi, s_br, s_bi

    zero = jnp.zeros((bsz, LANES), F32)
    lax.fori_loop(0, nc // SCAN_STEPS, scan, (zero,) * 4)
    p_s[...] = _dot(x, toep_ref[0]) + _dot_nt(st_s[...].astype(BF16), wout_ref[0])

    d_tile = jnp.broadcast_to(d_ref[0], (bsz, LANES))
    d_skip = [d_tile if k == 0 else pltpu.roll(d_tile, k * PG, 1) for k in range(GPS)]

    def scatter(cb, carry):
        chunks = pl.ds(pl.multiple_of(cb * REGROUP_CHUNKS, REGROUP_CHUNKS), REGROUP_CHUNKS)
        rows = pl.ds(pl.multiple_of(cb * piece, piece), piece)
        for q in range(CHUNK // GPS):
            w = pltpu.roll(p_s[rows, q * LANES:(q + 1) * LANES], pp * PG, 1)
            w = w.reshape(1, REGROUP_CHUNKS, bsz, LANES)
            for k in range(GPS):
                where = (pl.ds(0, 1), chunks, time_rows(q * GPS + k), slice(None))
                mine = (slot == (pp + k) % GPS).reshape(1, REGROUP_CHUNKS, bsz, LANES)
                pltpu.store(y_ref.at[where], w + d_skip[k] * u_ref[where], mask=mine)
        return carry

    lax.fori_loop(0, nc // REGROUP_CHUNKS, scatter, 0)


def _s5(u_tb, d_tiles, toep, w_in, w_out, a_pow, bsz):
    n_tiles, nc, cb, _ = u_tb.shape
    rows = nc * bsz
    tile = pl.BlockSpec((1, nc, cb, LANES), lambda j, p: (j, 0, 0, 0))
    pair = lambda r, c: pl.BlockSpec((1, r, c), lambda j, p: (j * GPS + p, 0, 0))
    return pl.pallas_call(
        _s5_kernel,
        grid=(n_tiles, GPS),
        in_specs=[tile, pl.BlockSpec((1, 1, LANES), lambda j, p: (j, 0, 0)),
                  pair(PW, PW), pair(PW, SW), pair(PW, SW), pair(bsz, SW)],
        out_specs=tile,
        out_shape=jax.ShapeDtypeStruct(u_tb.shape, F32),
        scratch_shapes=[pltpu.VMEM((rows, PW), BF16),
                        pltpu.VMEM((rows, SW), F32),
                        pltpu.VMEM((rows, SW), F32)],
        compiler_params=pltpu.CompilerParams(
            dimension_semantics=("arbitrary", "arbitrary"), vmem_limit_bytes=VMEM_LIMIT),
        name="s5_chunked",
    )(u_tb, d_tiles, toep, w_in, w_out, a_pow)


TAIL_TT = 512


def _tail_kernel(x_ref, ytb_ref, za_ref, at_ref, zb_ref, g_ref, wg_ref, bg_ref,
                 wps_ref, wpa_ref, wo_ref, o_ref):
    bsz = ytb_ref.shape[1] // TAIL_TT
    batch = pl.program_id(0) % bsz
    y_s5 = jnp.concatenate(
        [_rotate_rows(ytb_ref[lt, pl.ds(batch, TAIL_TT, stride=bsz), :], forward=False)
         for lt in range(SSM_WIDTH // LANES)], axis=1)
    ys = jax.nn.gelu(y_s5).astype(BF16)
    glu = _dot(ys, wg_ref[...]) + bg_ref[...]
    z_b = zb_ref[...].astype(F32)
    y_b = _dot((at_ref[...].astype(F32) * (z_b * _sigmoid(z_b))).astype(BF16), wpa_ref[...])
    z_a = za_ref[...].astype(F32)
    a_in = glu[:, :SSM_WIDTH] * _sigmoid(glu[:, SSM_WIDTH:]) * (z_a * _sigmoid(z_a))
    y_a = _dot(a_in.astype(BF16), wps_ref[...])
    g = g_ref[...].astype(F32)
    mix = _sigmoid(g[:, :D_MODEL]) * y_a + _sigmoid(g[:, D_MODEL:]) * y_b
    o_ref[...] = x_ref[...] + _dot(mix.astype(BF16), wo_ref[...])


def _tail(x2d, y_tb, attn2d, pz2d, w_glu, b_glu, w_ps, w_pa, w_out, bsz, seq):
    t = x2d.shape[0]
    tile = _token_tile(bsz, seq // TAIL_TT)
    row = lambda w, c: pl.BlockSpec((TAIL_TT, w), lambda i, c=c: (tile(i), c))
    const = lambda shape: pl.BlockSpec(shape, lambda i: (0, 0))
    return pl.pallas_call(
        _tail_kernel,
        grid=(t // TAIL_TT,),
        in_specs=[
            row(D_MODEL, 0),
            pl.BlockSpec((SSM_WIDTH // LANES, TAIL_TT * bsz, LANES), lambda i: (0, i // bsz, 0)),
            row(SSM_WIDTH, PZ_ZA // SSM_WIDTH),
            row(ATTN_WIDTH, 0),
            row(ATTN_WIDTH, PZ_ZB // ATTN_WIDTH),
            row(2 * D_MODEL, PZ_G // (2 * D_MODEL)),
            const((SSM_WIDTH, 2 * SSM_WIDTH)), const((1, 2 * SSM_WIDTH)),
            const((SSM_WIDTH, D_MODEL)), const((ATTN_WIDTH, D_MODEL)), const((D_MODEL, D_MODEL)),
        ],
        out_specs=pl.BlockSpec((TAIL_TT, D_MODEL), lambda i: (tile(i), 0)),
        out_shape=jax.ShapeDtypeStruct((t, D_MODEL), F32),
        compiler_params=pltpu.CompilerParams(
            dimension_semantics=("arbitrary",), vmem_limit_bytes=VMEM_LIMIT),
        name="tail",
    )(x2d, y_tb, pz2d, attn2d, pz2d, pz2d, w_glu, b_glu, w_ps, w_pa, w_out)


def _rope_tables(seq):
    half = ROPE_DIM // 2
    inv = ROPE_THETA ** (-np.arange(0, ROPE_DIM, 2, dtype=np.float64) / ROPE_DIM)
    ang = np.arange(seq, dtype=np.float64)[:, None] * inv[None, :]
    cos, sin = np.cos(ang).astype(np.float32), np.sin(ang).astype(np.float32)
    zeros = np.zeros((seq, HEAD_DIM - ROPE_DIM), np.float32)
    z8 = np.zeros((seq, half), np.float32)
    cos_h = np.concatenate([cos, cos, np.ones_like(zeros)], axis=1)
    sa_h = np.concatenate([-sin, z8, zeros], axis=1)
    sb_h = np.concatenate([z8, sin, zeros], axis=1)
    two = lambda t: np.concatenate([t, t], axis=1).astype(np.float32)
    return two(cos_h), two(sa_h), two(sb_h)


def kernel(x, norm_w, w_in, b_gate, q_norm_w, k_norm_w, ssm_lam_re, ssm_lam_im, ssm_log_dt,
           ssm_b_re, ssm_b_im, ssm_c_re, ssm_c_im, ssm_d, w_glu, b_glu,
           w_proj_ssm, w_proj_attn, w_out):
    bsz, seq, d_model = x.shape
    depth = norm_w.shape[0]
    assert d_model == D_MODEL and w_in.shape[-1] == IN_WIDTH
    assert bsz == SUBLANES, "S5 rows (chunk, batch) must fill whole sublane tiles"
    assert seq % IN_TT == 0 and seq % TAIL_TT == 0 and seq % CHUNK == 0
    assert IN_TT % GPS == 0 and TAIL_TT % GPS == 0 and CHUNK % GPS == 0
    assert all(seq % (d * QBLK) == 0 for d in DILATIONS) and (seq // QBLK) % 4 == 0
    cosf, sa, sb = _rope_tables(seq)
    ones_blk = jnp.asarray(np.kron(np.eye(2 * LANES // HEAD_DIM, dtype=np.float32),
                                   np.full((HEAD_DIM, HEAD_DIM), 1.0 / HEAD_DIM, np.float32)), BF16)
    for layer in range(depth):
        x2d = x.reshape(bsz * seq, D_MODEL)
        q_gain = q_norm_w[layer].astype(F32) * (LOG2E * HEAD_DIM ** -0.5)
        qk_w_row = jnp.concatenate([jnp.tile(q_gain, len(DILATIONS) * ATTN_SLOTS),
                                    jnp.tile(k_norm_w[layer].astype(F32), ATTN_SLOTS)])[None, :]
        pf2d, pz2d, u_tb = _in_proj(
            x2d, norm_w[layer][None, :].astype(F32), w_in[layer].astype(BF16),
            b_gate[layer][None, :].astype(F32), cosf, sa, sb, qk_w_row, ones_blk, bsz, seq)
        attn = _attention(pf2d.reshape(bsz, seq, PF_WIDTH))

        toep, s5_in, s5_out, a_pow = _s5_prep(
            ssm_lam_re[layer].astype(F32), ssm_lam_im[layer].astype(F32),
            ssm_log_dt[layer].astype(F32), ssm_b_re[layer].astype(F32),
            ssm_b_im[layer].astype(F32), ssm_c_re[layer].astype(F32),
            ssm_c_im[layer].astype(F32))
        n_tiles = SSM_WIDTH // LANES
        y_tb = _s5(u_tb.reshape(n_tiles, seq // CHUNK, CHUNK * bsz, LANES),
                   ssm_d[layer].astype(F32).reshape(n_tiles, 1, LANES), toep, s5_in, s5_out,
                   a_pow, bsz)

        out2d = _tail(x2d, y_tb.reshape(n_tiles, seq * bsz, LANES),
                      attn.reshape(bsz * seq, ATTN_WIDTH), pz2d,
                      w_glu[layer].astype(BF16), b_glu[layer][None, :].astype(F32),
                      w_proj_ssm[layer].astype(BF16), w_proj_attn[layer].astype(BF16),
                      w_out[layer].astype(BF16), bsz, seq)
        x = out2d.reshape(bsz, seq, D_MODEL)
    return x
```

```python
import math

import jax
import jax.numpy as jnp
import numpy as np
from jax import lax
from jax.experimental import pallas as pl
from jax.experimental.pallas import tpu as pltpu

F32 = jnp.float32
BF16 = jnp.bfloat16

D_MODEL = 1024
SSM_WIDTH = 512
SSM_GROUP = 16
SSM_GROUPS = 32
SSM_STATE = 64
HEAD_DIM = 64
ATTN_SLOTS = 8
ATTN_WIDTH = 512
DILATIONS = (1, 4, 16)
BAND_HALF = 64
ROPE_THETA = 500000.0
ROPE_DIM = 16
EPS = 1e-6
NEG_INF = -1e30
IN_WIDTH = 6144
COL_U, COL_ZA, COL_Q, COL_K, COL_V, COL_ZB, COL_G = 0, 512, 1024, 2560, 3072, 3584, 4096

LANES = 128
SUBLANES = 8
VMEM_LIMIT = 56 * 1024 * 1024

CHUNK = 16
GP = 2
N_PAIRS = SSM_GROUPS // GP
PG = GP * SSM_GROUP
PW = CHUNK * PG
SW = 4 * LANES
GPS = LANES // PG


def _dot(a, b):
    return jnp.dot(a, b, preferred_element_type=F32)


def _dot_nt(a, b):
    return lax.dot_general(a, b, (((1,), (1,)), ((), ())), preferred_element_type=F32)


def _dot_nt_f32(a, b):
    a_hi, b_hi = a.astype(BF16), b.astype(BF16)
    a_lo = (a - a_hi.astype(F32)).astype(BF16)
    b_lo = (b - b_hi.astype(F32)).astype(BF16)
    return _dot_nt(a_hi, b_hi) + (_dot_nt(a_hi, b_lo) + _dot_nt(a_lo, b_hi))


def _rotate_rows(tile):
    residue = lax.broadcasted_iota(jnp.int32, tile.shape, 0) % GPS
    out = tile
    for k in range(1, GPS):
        out = jnp.where(residue == k, pltpu.roll(tile, k * PG, 1), out)
    return out


def _sigmoid(v):
    return 0.5 * jnp.tanh(0.5 * v) + 0.5


IN_TT = 512
IN_TN = 512
QK_ROWS = 256
LOG2E = math.log2(math.e)
PF_Q, PF_K, PF_V, PF_WIDTH = 0, 1536, 2048, 2560
PZ_G, PZ_ZA, PZ_ZB, PZ_WIDTH = 0, 2048, 2560, 3072
_Q_COLS = tuple((COL_Q + o, PF_Q + o) for o in range(0, COL_K - COL_Q, IN_TN))
_G_COLS = tuple((COL_G + o, PZ_G + o) for o in range(0, IN_WIDTH - COL_G, IN_TN))
F32_DEST = dict(((COL_K, PF_K), (COL_V, PF_V)) + _Q_COLS)
BF16_DEST = dict(((COL_ZA, PZ_ZA), (COL_ZB, PZ_ZB)) + _G_COLS)


def _in_proj_kernel(x_ref, nw_ref, w_ref, b_ref, cos_ref, sa_ref, sb_ref, qkw_ref, ones_ref,
                    pf_ref, pz_ref, utb_ref):
    bsz = utb_ref.shape[1] // IN_TT
    batch = pl.program_id(0) % bsz
    x = x_ref[...]
    var = jnp.mean(x * x, axis=-1, keepdims=True)
    h = (x * lax.rsqrt(var + EPS) * nw_ref[...]).astype(BF16)
    ones_blk = ones_ref[...]
    for c0 in range(0, IN_WIDTH, IN_TN):
        acc = _dot(h, w_ref[:, c0:c0 + IN_TN])
        if c0 in BF16_DEST:
            if c0 >= COL_G:
                acc = acc + b_ref[:, c0 - COL_G:c0 - COL_G + IN_TN]
            pz_ref[:, BF16_DEST[c0]:BF16_DEST[c0] + IN_TN] = acc.astype(BF16)
            continue
        if c0 == COL_U:
            for lt in range(SSM_WIDTH // LANES):
                utb_ref[lt, pl.ds(batch, IN_TT, stride=bsz), :] = _rotate_rows(
                    acc[:, lt * LANES:(lt + 1) * LANES])
            continue
        d0 = F32_DEST[c0]
        if not COL_Q <= c0 < COL_V:
            pf_ref[:, d0:d0 + IN_TN] = acc
            continue
        for r0 in range(0, IN_TT, QK_ROWS):
            rows = slice(r0, r0 + QK_ROWS)
            for t2 in range(0, IN_TN, 2 * LANES):
                xq2 = acc[rows, t2:t2 + 2 * LANES]
                ms2 = _dot((xq2 * xq2).astype(BF16), ones_blk)
                xn2 = xq2 * lax.rsqrt(ms2 + EPS) * qkw_ref[:, c0 - COL_Q + t2:
                                                            c0 - COL_Q + t2 + 2 * LANES]
                for t in (0, LANES):
                    xn = xn2[:, t:t + LANES]
                    pf_ref[rows, d0 + t2 + t:d0 + t2 + t + LANES] = (
                        xn * cos_ref[rows, :]
                        + pltpu.roll(xn, LANES - ROPE_DIM // 2, 1) * sa_ref[rows, :]
                        + pltpu.roll(xn, ROPE_DIM // 2, 1) * sb_ref[rows, :])


def _token_tile(bsz, tiles_per_seq):
    return lambda i: (i % bsz) * tiles_per_seq + i // bsz


def _in_proj(x2d, norm_w, w_in_bf16, b_gate_row, cosf, sa, sb, qk_w_row, ones_blk, bsz, seq):
    t = x2d.shape[0]
    tile = _token_tile(bsz, seq // IN_TT)
    const = lambda shape: pl.BlockSpec(shape, lambda i: (0, 0))
    rope = lambda: pl.BlockSpec((IN_TT, LANES), lambda i: (i // bsz, 0))
    return pl.pallas_call(
        _in_proj_kernel,
        grid=(t // IN_TT,),
        in_specs=[
            pl.BlockSpec((IN_TT, D_MODEL), lambda i: (tile(i), 0)),
            const((1, D_MODEL)),
            pl.BlockSpec((D_MODEL, IN_WIDTH), lambda i: (0, 0), pipeline_mode=pl.Buffered(1)),
            const((1, IN_WIDTH - COL_G)),
            rope(), rope(), rope(),
            const((1, COL_V - COL_Q)),
            const((2 * LANES, 2 * LANES)),
        ],
        out_specs=[
            pl.BlockSpec((IN_TT, PF_WIDTH), lambda i: (tile(i), 0)),
            pl.BlockSpec((IN_TT, PZ_WIDTH), lambda i: (tile(i), 0)),
            pl.BlockSpec((SSM_WIDTH // LANES, IN_TT * bsz, LANES), lambda i: (0, i // bsz, 0)),
        ],
        out_shape=[
            jax.ShapeDtypeStruct((t, PF_WIDTH), F32),
            jax.ShapeDtypeStruct((t, PZ_WIDTH), BF16),
            jax.ShapeDtypeStruct((SSM_WIDTH // LANES, t, LANES), F32),
        ],
        compiler_params=pltpu.CompilerParams(
            dimension_semantics=("arbitrary",), vmem_limit_bytes=VMEM_LIMIT),
        name="in_proj",
    )(x2d, norm_w, w_in_bf16, b_gate_row, cosf, sa, sb, qk_w_row, ones_blk)


QBLK = 128
KWIN = QBLK + 2 * BAND_HALF
NORM_ROWS = 256
MERGE_ROWS = 64


def _attn_kernel(q0_ref, q1_ref, q2_ref, k_ref, v_ref, bias_ref, o_ref,
                 va_s, vb_s, oacc_s, den_s, max_s, s_s):
    seq = k_ref.shape[1]
    q_refs = (q0_ref, q1_ref, q2_ref)
    head0 = lax.broadcasted_iota(jnp.int32, (QBLK, LANES), 1) < HEAD_DIM
    head0_rows = lax.broadcasted_iota(jnp.int32, (NORM_ROWS, LANES), 1) < HEAD_DIM

    def prep(i, carry):
        rows = pl.ds(pl.multiple_of(i * NORM_ROWS, NORM_ROWS), NORM_ROWS)
        v = v_ref[0, rows, :]
        va_s[rows, :] = jnp.where(head0_rows, v, 1.0)
        vb_s[rows, :] = jnp.where(head0_rows, 1.0, v)
        return carry

    lax.fori_loop(0, seq // NORM_ROWS, prep, 0, unroll=True)

    patterns = []
    for p, d in enumerate(DILATIONS):
        n = seq // d
        nblk = n // QBLK
        kw = min(n, KWIN)

        def rows_of(idx, d=d, n=n, nblk=nblk, kw=kw):
            r = idx // nblk
            q0 = (idx % nblk) * QBLK
            ks = jnp.clip(q0 - BAND_HALF, 0, n - kw)
            return (pl.ds(r + d * q0, QBLK, stride=d), pl.ds(r + d * ks, kw, stride=d),
                    (q0 - ks) // BAND_HALF)

        def scores(idx, slot, p=p, kw=kw, rows_of=rows_of):
            qrows, krows, case = rows_of(idx)
            qb = q_refs[p][0, qrows, :]
            kb = k_ref[0, krows, :].astype(BF16)
            bias = bias_ref[case, :, :kw]
            s_s[slot, :QBLK, :kw] = _dot_nt(jnp.where(head0, qb, 0.0).astype(BF16), kb) + bias
            s_s[slot, QBLK:, :kw] = _dot_nt(jnp.where(head0, 0.0, qb).astype(BF16), kb) + bias

        def weigh(idx, slot, p=p, kw=kw, rows_of=rows_of):
            qrows, krows, _ = rows_of(idx)
            s = s_s[slot, :, :kw]
            m = jnp.max(s, axis=-1, keepdims=True)
            e = jnp.exp2(s - m).astype(BF16)
            o_a = _dot(e[:QBLK], va_s[krows, :].astype(BF16))
            o_b = _dot(e[QBLK:], vb_s[krows, :].astype(BF16))
            oacc_s[p, qrows, :] = jnp.where(head0, o_a, o_b)
            den_s[p, qrows, :] = jnp.where(head0, o_b, o_a)
            max_s[p, qrows, :] = jnp.where(head0, jnp.broadcast_to(m[:QBLK], (QBLK, LANES)),
                                           jnp.broadcast_to(m[QBLK:], (QBLK, LANES)))

        patterns.append((scores, weigh, d * nblk))

    def pair(weigh, i0, cur, ahead):
        nxt = 2 - cur
        if ahead is not None:
            ahead[0](ahead[1], nxt)
        weigh(i0, cur)
        if ahead is not None:
            ahead[0](ahead[1] + 1, nxt + 1)
        weigh(i0 + 1, cur + 1)

    patterns[0][0](0, 0)
    patterns[0][0](1, 1)
    for which, (scores, weigh, n_blocks) in enumerate(patterns):
        def quad(j, carry, scores=scores, weigh=weigh):
            pair(weigh, 4 * j, 0, (scores, 4 * j + 2))
            pair(weigh, 4 * j + 2, 2, (scores, 4 * j + 4))
            return carry

        lax.fori_loop(0, n_blocks // 4 - 1, quad, 0, unroll=True)
        pair(weigh, n_blocks - 4, 0, (scores, n_blocks - 2))
        following = (patterns[which + 1][0], 0) if which + 1 < len(patterns) else None
        pair(weigh, n_blocks - 2, 2, following)

    def combine(i, carry):
        rows = pl.ds(pl.multiple_of(i * MERGE_ROWS, MERGE_ROWS), MERGE_ROWS)
        m0, m1, m2 = max_s[0, rows, :], max_s[1, rows, :], max_s[2, rows, :]
        m = jnp.maximum(jnp.maximum(m0, m1), m2)
        w0, w1, w2 = jnp.exp2(m0 - m), jnp.exp2(m1 - m), jnp.exp2(m2 - m)
        num = w0 * oacc_s[0, rows, :] + w1 * oacc_s[1, rows, :] + w2 * oacc_s[2, rows, :]
        d0, d1, d2 = (pltpu.roll(den_s[p, rows, :], HEAD_DIM, 1) for p in range(3))
        o_ref[0, rows, :] = (num / (w0 * d0 + w1 * d1 + w2 * d2)).astype(o_ref.dtype)
        return carry

    lax.fori_loop(0, seq // MERGE_ROWS, combine, 0, unroll=True)


def _band_bias():
    i = np.arange(QBLK)[:, None]
    j = np.arange(KWIN)[None, :]
    off = np.arange(3)[:, None, None] * BAND_HALF
    return np.where(np.abs(j - i - off) <= BAND_HALF, 0.0, NEG_INF).astype(np.float32)


def _attention(proj3d):
    bsz, seq, _ = proj3d.shape
    n_pairs = ATTN_WIDTH // LANES

    def qspec(p):
        return pl.BlockSpec((1, seq, LANES),
                            lambda b, hp, p=p: (b, 0, PF_Q // LANES + p * n_pairs + hp))

    seq_tile = lambda: pltpu.VMEM((seq, LANES), F32)
    pat_tile = lambda: pltpu.VMEM((len(DILATIONS), seq, LANES), F32)
    return pl.pallas_call(
        _attn_kernel,
        grid=(bsz, n_pairs),
        in_specs=[
            qspec(0), qspec(1), qspec(2),
            pl.BlockSpec((1, seq, LANES), lambda b, hp: (b, 0, PF_K // LANES + hp)),
            pl.BlockSpec((1, seq, LANES), lambda b, hp: (b, 0, PF_V // LANES + hp)),
            pl.BlockSpec((3, QBLK, KWIN), lambda b, hp: (0, 0, 0)),
        ],
        out_specs=pl.BlockSpec((1, seq, LANES), lambda b, hp: (b, 0, hp)),
        out_shape=jax.ShapeDtypeStruct((bsz, seq, ATTN_WIDTH), BF16),
        scratch_shapes=[
            seq_tile(), seq_tile(),
            pat_tile(), pat_tile(), pat_tile(),
            pltpu.VMEM((4, 2 * QBLK, KWIN), F32),
        ],
        compiler_params=pltpu.CompilerParams(
            dimension_semantics=("arbitrary", "arbitrary"), vmem_limit_bytes=VMEM_LIMIT),
        name="dilated_attention",
    )(proj3d, proj3d, proj3d, proj3d, proj3d, _band_bias())


LAG_REP = LANES // PG


def _s5_prep_kernel(lre_ref, lim_ref, ldt_ref, bre_ref, bim_ref, cre_ref, cim_ref,
                    toep_ref, win_ref, wout_ref, a_ref, wf32_s):
    four = lambda ref: jnp.concatenate([ref[0, 0], ref[0, 0], ref[1, 0], ref[1, 0]], axis=1)
    lr = jnp.minimum(four(lre_ref), -1e-4)
    li = four(lim_ref)
    dt = jnp.exp(four(ldt_ref))
    col = lax.broadcasted_iota(jnp.int32, (1, SW), 1)
    is_re = (col // LANES) % 2 == 0
    is_fwd = col < 2 * LANES

    mag = jnp.exp(lr * dt)
    pw_re = [jnp.ones_like(mag), mag * jnp.cos(li * dt)]
    pw_im = [jnp.zeros_like(mag), mag * jnp.sin(li * dt)]
    for _ in range(CHUNK - 1):
        pw_re.append(pw_re[-1] * pw_re[1] - pw_im[-1] * pw_im[1])
        pw_im.append(pw_re[-2] * pw_im[1] + pw_im[-1] * pw_re[1])

    a_re, a_im = pw_re[1], pw_im[1]
    nr, ni, mag2 = a_re - 1.0, a_im, lr * lr + li * li
    f_re, f_im = (nr * lr + ni * li) / mag2, (ni * lr - nr * li) / mag2

    row_g = lax.broadcasted_iota(jnp.int32, (PG, SW), 0) // SSM_GROUP
    col_g = (lax.broadcasted_iota(jnp.int32, (PG, SW), 1) % LANES) // SSM_STATE
    diag = row_g == col_g
    b_re = jnp.where(diag, four(bre_ref), 0.0)
    b_im = jnp.where(diag, four(bim_ref), 0.0)
    c_re = jnp.where(diag, four(cre_ref), 0.0)
    c_im = jnp.where(diag, four(cim_ref), 0.0)
    bb_re = f_re * b_re - f_im * b_im
    bb_im = f_re * b_im + f_im * b_re

    for s in range(CHUNK):
        rows = slice(s * PG, (s + 1) * PG)
        in_re = jnp.where(is_fwd, pw_re[CHUNK - 1 - s], pw_re[s])
        in_im = jnp.where(is_fwd, pw_im[CHUNK - 1 - s], pw_im[s])
        w = (bb_re * jnp.where(is_re, in_re, in_im)
             + bb_im * jnp.where(is_re, -in_im, in_re))
        wf32_s[rows, :] = w
        win_ref[0, rows, :] = w.astype(BF16)
        out_re = jnp.where(is_fwd, pw_re[s + 1], pw_re[CHUNK - s])
        out_im = jnp.where(is_fwd, pw_im[s + 1], pw_im[CHUNK - s])
        wout_ref[0, rows, :] = (c_re * jnp.where(is_re, out_re, -out_im)
                                + c_im * jnp.where(is_re, -out_im, -out_re)).astype(BF16)

    a_ref[0] = jnp.broadcast_to(jnp.where(is_re, pw_re[CHUNK], pw_im[CHUNK]), a_ref.shape[1:])

    c_cat = jnp.where(is_re, c_re, -c_im)
    c_rep = jnp.concatenate([c_cat] * LAG_REP, axis=0)
    half = 2 * LANES
    lag_f = _dot_nt_f32(wf32_s[:, :half], c_rep[:, :half])
    lag_b = _dot_nt_f32(wf32_s[:, half:], c_rep[:, half:])
    row_t = lax.broadcasted_iota(jnp.int32, (PW, LANES), 0) // PG
    lane_q = lax.broadcasted_iota(jnp.int32, (PW, LANES), 1) // PG
    zeros = lambda n: jnp.zeros((n * PG, LANES), F32)
    for q in range(PW // LANES):
        tile = jnp.zeros((PW, LANES), F32)
        for u in range(LAG_REP):
            t = q * LAG_REP + u
            up = CHUNK - 1 - t
            sh_f = lag_f if up == 0 else jnp.concatenate([lag_f[up * PG:], zeros(up)], axis=0)
            sh_b = lag_b if t == 0 else jnp.concatenate([zeros(t), lag_b[:(CHUNK - t) * PG]], axis=0)
            col_t = jnp.where(row_t <= t, sh_f, 0.0) + jnp.where(row_t >= t, sh_b, 0.0)
            tile = jnp.where(lane_q == u, col_t, tile)
        toep_ref[0, :, q * LANES:(q + 1) * LANES] = tile.astype(BF16)


def _s5_prep(lam_re, lam_im, log_dt, b_re, b_im, c_re, c_im):
    lanes = lambda t: t.reshape(2, N_PAIRS, 1, LANES)
    dt_b = jnp.broadcast_to(log_dt[..., None], lam_re.shape)

    def tiles(t_ghp):
        t = jnp.broadcast_to(t_ghp[:, :, :, None, :], t_ghp.shape[:3] + (GP, SSM_STATE))
        return t.reshape(2, N_PAIRS, PG, LANES)

    row = pl.BlockSpec((2, 1, 1, LANES), lambda i: (0, i, 0, 0))
    mat = pl.BlockSpec((2, 1, PG, LANES), lambda i: (0, i, 0, 0))
    out = lambda r: pl.BlockSpec((1, r, SW), lambda i: (i, 0, 0))
    return pl.pallas_call(
        _s5_prep_kernel,
        grid=(N_PAIRS,),
        in_specs=[row, row, row, mat, mat, mat, mat],
        out_specs=[out(PW), out(PW), out(PW), out(SUBLANES)],
        out_shape=[jax.ShapeDtypeStruct((N_PAIRS, PW, PW), BF16),
                   jax.ShapeDtypeStruct((N_PAIRS, PW, SW), BF16),
                   jax.ShapeDtypeStruct((N_PAIRS, PW, SW), BF16),
                   jax.ShapeDtypeStruct((N_PAIRS, SUBLANES, SW), F32)],
        scratch_shapes=[pltpu.VMEM((PW, SW), F32)],
        compiler_params=pltpu.CompilerParams(dimension_semantics=("arbitrary",)),
        name="s5_prep",
    )(lanes(lam_re), lanes(lam_im), lanes(dt_b), tiles(b_re.transpose(0, 1, 3, 2)),
      tiles(b_im.transpose(0, 1, 3, 2)), tiles(c_re), tiles(c_im))


SCAN_STEPS = 8
REGROUP_CHUNKS = 32


def _s5_kernel(u_ref, d_ref, toep_ref, win_ref, wout_ref, a_ref, y_ref, x_s, p_s, st_s):
    nc = u_ref.shape[1]
    bsz = u_ref.shape[2] // CHUNK
    piece = REGROUP_CHUNKS * bsz
    pp = pl.program_id(1)
    slot = lax.broadcasted_iota(jnp.int32, (piece, LANES), 1) // PG
    time_rows = lambda t: slice(t * bsz, (t + 1) * bsz)

    def gather(cb, carry):
        chunks = pl.ds(pl.multiple_of(cb * REGROUP_CHUNKS, REGROUP_CHUNKS), REGROUP_CHUNKS)
        rows = pl.ds(pl.multiple_of(cb * piece, piece), piece)
        for q in range(CHUNK // GPS):
            acc = None
            for k in range(GPS):
                v = u_ref[0, chunks, time_rows(q * GPS + k), :].reshape(piece, LANES)
                acc = v if acc is None else jnp.where(slot == (pp + k) % GPS, v, acc)
            x_s[rows, q * LANES:(q + 1) * LANES] = pltpu.roll(
                acc, ((GPS - pp) % GPS) * PG, 1).astype(BF16)
        return carry

    lax.fori_loop(0, nc // REGROUP_CHUNKS, gather, 0)

    x = x_s[...]
    p_s[...] = _dot(x, win_ref[0])
    a_fr, a_fi, a_br, a_bi = (a_ref[0, :, q * LANES:(q + 1) * LANES] for q in range(4))
    tile = lambda q: slice(q * LANES, (q + 1) * LANES)

    def scan(i, carry):
        base_f = pl.multiple_of(i * (SCAN_STEPS * bsz), SCAN_STEPS * bsz)
        base_b = pl.multiple_of((nc - (i + 1) * SCAN_STEPS) * bsz, SCAN_STEPS * bsz)
        s_fr, s_fi, s_br, s_bi = carry
        for k in range(SCAN_STEPS):
            rf = pl.ds(base_f + k * bsz, bsz)
            rb = pl.ds(base_b + (SCAN_STEPS - 1 - k) * bsz, bsz)
            st_s[rf, tile(0)] = s_fr
            st_s[rf, tile(1)] = s_fi
            st_s[rb, tile(2)] = s_br
            st_s[rb, tile(3)] = s_bi
            s_fr, s_fi, s_br, s_bi = (a_fr * s_fr - a_fi * s_fi + p_s[rf, tile(0)],
                                      a_fr * s_fi + a_fi * s_fr + p_s[rf, tile(1)],
                                      a_br * s_br - a_bi * s_bi + p_s[rb, tile(2)],
                                      a_br * s_bi + a_bi * s_br + p_s[rb, tile(3)])
        return s_fr, s_fi, s_br, s_bi

    zero = jnp.zeros((bsz, LANES), F32)
    lax.fori_loop(0, nc // SCAN_STEPS, scan, (zero,) * 4)
    p_s[...] = _dot(x, toep_ref[0]) + _dot_nt(st_s[...].astype(BF16), wout_ref[0])

    d_tile = jnp.broadcast_to(d_ref[0], (bsz, LANES))
    d_skip = [d_tile if k == 0 else pltpu.roll(d_tile, k * PG, 1) for k in range(GPS)]

    def scatter(cb, carry):
        chunks = pl.ds(pl.multiple_of(cb * REGROUP_CHUNKS, REGROUP_CHUNKS), REGROUP_CHUNKS)
        rows = pl.ds(pl.multiple_of(cb * piece, piece), piece)
        for q in range(CHUNK // GPS):
            w = pltpu.roll(p_s[rows, q * LANES:(q + 1) * LANES], pp * PG, 1)
            w = w.reshape(1, REGROUP_CHUNKS, bsz, LANES)
            for k in range(GPS):
                u = u_ref[pl.ds(0, 1), chunks, time_rows(q * GPS + k), :]
                mine = (slot == (pp + k) % GPS).reshape(1, 1, REGROUP_CHUNKS, bsz, LANES)
                where = (pl.ds(0, 1), pl.ds(cb * GPS + k, 1), slice(None), time_rows(q), slice(None))
                pltpu.store(y_ref.at[where], (w + d_skip[k] * u)[None], mask=mine)
        return carry

    lax.fori_loop(0, nc // REGROUP_CHUNKS, scatter, 0)


def _s5(u_tb, d_tiles, toep, w_in, w_out, a_pow, bsz):
    n_tiles, nc, cb, _ = u_tb.shape
    rows = nc * bsz
    tile = pl.BlockSpec((1, nc, cb, LANES), lambda j, p: (j, 0, 0, 0))
    y_shape = (n_tiles, nc // REGROUP_CHUNKS * GPS, REGROUP_CHUNKS, cb // GPS, LANES)
    pair = lambda r, c: pl.BlockSpec((1, r, c), lambda j, p: (j * GPS + p, 0, 0))
    return pl.pallas_call(
        _s5_kernel,
        grid=(n_tiles, GPS),
        in_specs=[tile, pl.BlockSpec((1, 1, LANES), lambda j, p: (j, 0, 0)),
                  pair(PW, PW), pair(PW, SW), pair(PW, SW), pair(bsz, SW)],
        out_specs=pl.BlockSpec((1,) + y_shape[1:], lambda j, p: (j, 0, 0, 0, 0)),
        out_shape=jax.ShapeDtypeStruct(y_shape, F32),
        scratch_shapes=[pltpu.VMEM((rows, PW), BF16),
                        pltpu.VMEM((rows, SW), F32),
                        pltpu.VMEM((rows, SW), F32)],
        compiler_params=pltpu.CompilerParams(
            dimension_semantics=("arbitrary", "arbitrary"), vmem_limit_bytes=VMEM_LIMIT),
        name="s5_chunked",
    )(u_tb, d_tiles, toep, w_in, w_out, a_pow)


TAIL_TT = 512


def _tail_kernel(x_ref, ytb_ref, za_ref, at_ref, zb_ref, g_ref, wg_ref, bg_ref,
                 wps_ref, wpa_ref, wo_ref, o_ref, ys_s):
    bsz = ytb_ref.shape[1] // TAIL_TT
    batch = pl.program_id(0) % bsz
    per_class = TAIL_TT // GPS
    for lt in range(SSM_WIDTH // LANES):
        for k in range(GPS):
            piece = ytb_ref[lt, pl.ds(k * per_class * bsz + batch, per_class, stride=bsz), :]
            piece = piece if k == 0 else pltpu.roll(piece, (GPS - k) * PG, 1)
            ys_s[lt, pl.ds(k, per_class, stride=GPS), :] = piece
    y_s5 = jnp.concatenate([ys_s[lt] for lt in range(SSM_WIDTH // LANES)], axis=1)
    ys = jax.nn.gelu(y_s5).astype(BF16)
    glu = _dot(ys, wg_ref[...]) + bg_ref[...]
    z_b = zb_ref[...].astype(F32)
    y_b = _dot((at_ref[...].astype(F32) * (z_b * _sigmoid(z_b))).astype(BF16), wpa_ref[...])
    z_a = za_ref[...].astype(F32)
    a_in = glu[:, :SSM_WIDTH] * _sigmoid(glu[:, SSM_WIDTH:]) * (z_a * _sigmoid(z_a))
    y_a = _dot(a_in.astype(BF16), wps_ref[...])
    g = g_ref[...].astype(F32)
    mix = _sigmoid(g[:, :D_MODEL]) * y_a + _sigmoid(g[:, D_MODEL:]) * y_b
    o_ref[...] = x_ref[...] + _dot(mix.astype(BF16), wo_ref[...])


def _tail(x2d, y_tb, attn2d, pz2d, w_glu, b_glu, w_ps, w_pa, w_out, bsz, seq):
    t = x2d.shape[0]
    tile = _token_tile(bsz, seq // TAIL_TT)
    row = lambda w, c: pl.BlockSpec((TAIL_TT, w), lambda i, c=c: (tile(i), c))
    const = lambda shape: pl.BlockSpec(shape, lambda i: (0, 0))
    return pl.pallas_call(
        _tail_kernel,
        grid=(t // TAIL_TT,),
        in_specs=[
            row(D_MODEL, 0),
            pl.BlockSpec((SSM_WIDTH // LANES, TAIL_TT * bsz, LANES), lambda i: (0, i // bsz, 0)),
            row(SSM_WIDTH, PZ_ZA // SSM_WIDTH),
            row(ATTN_WIDTH, 0),
            row(ATTN_WIDTH, PZ_ZB // ATTN_WIDTH),
            row(2 * D_MODEL, PZ_G // (2 * D_MODEL)),
            const((SSM_WIDTH, 2 * SSM_WIDTH)), const((1, 2 * SSM_WIDTH)),
            const((SSM_WIDTH, D_MODEL)), const((ATTN_WIDTH, D_MODEL)), const((D_MODEL, D_MODEL)),
        ],
        out_specs=pl.BlockSpec((TAIL_TT, D_MODEL), lambda i: (tile(i), 0)),
        out_shape=jax.ShapeDtypeStruct((t, D_MODEL), F32),
        scratch_shapes=[pltpu.VMEM((SSM_WIDTH // LANES, TAIL_TT, LANES), F32)],
        compiler_params=pltpu.CompilerParams(
            dimension_semantics=("arbitrary",), vmem_limit_bytes=VMEM_LIMIT),
        name="tail",
    )(x2d, y_tb, pz2d, attn2d, pz2d, pz2d, w_glu, b_glu, w_ps, w_pa, w_out)


def _rope_tables(seq):
    half = ROPE_DIM // 2
    inv = ROPE_THETA ** (-np.arange(0, ROPE_DIM, 2, dtype=np.float64) / ROPE_DIM)
    ang = np.arange(seq, dtype=np.float64)[:, None] * inv[None, :]
    cos, sin = np.cos(ang).astype(np.float32), np.sin(ang).astype(np.float32)
    zeros = np.zeros((seq, HEAD_DIM - ROPE_DIM), np.float32)
    z8 = np.zeros((seq, half), np.float32)
    cos_h = np.concatenate([cos, cos, np.ones_like(zeros)], axis=1)
    sa_h = np.concatenate([-sin, z8, zeros], axis=1)
    sb_h = np.concatenate([z8, sin, zeros], axis=1)
    two = lambda t: np.concatenate([t, t], axis=1).astype(np.float32)
    return two(cos_h), two(sa_h), two(sb_h)


def kernel(x, norm_w, w_in, b_gate, q_norm_w, k_norm_w, ssm_lam_re, ssm_lam_im, ssm_log_dt,
           ssm_b_re, ssm_b_im, ssm_c_re, ssm_c_im, ssm_d, w_glu, b_glu,
           w_proj_ssm, w_proj_attn, w_out):
    bsz, seq, d_model = x.shape
    depth = norm_w.shape[0]
    assert d_model == D_MODEL and w_in.shape[-1] == IN_WIDTH
    assert bsz == SUBLANES, "S5 rows (chunk, batch) must fill whole sublane tiles"
    assert seq % IN_TT == 0 and seq % TAIL_TT == 0 and seq % CHUNK == 0
    assert IN_TT % GPS == 0 and CHUNK % GPS == 0 and TAIL_TT == REGROUP_CHUNKS * CHUNK
    assert all(seq % (d * QBLK) == 0 for d in DILATIONS) and (seq // QBLK) % 4 == 0
    cosf, sa, sb = _rope_tables(seq)
    ones_blk = jnp.asarray(np.kron(np.eye(2 * LANES // HEAD_DIM, dtype=np.float32),
                                   np.full((HEAD_DIM, HEAD_DIM), 1.0 / HEAD_DIM, np.float32)), BF16)
    for layer in range(depth):
        x2d = x.reshape(bsz * seq, D_MODEL)
        q_gain = q_norm_w[layer].astype(F32) * (LOG2E * HEAD_DIM ** -0.5)
        qk_w_row = jnp.concatenate([jnp.tile(q_gain, len(DILATIONS) * ATTN_SLOTS),
                                    jnp.tile(k_norm_w[layer].astype(F32), ATTN_SLOTS)])[None, :]
        pf2d, pz2d, u_tb = _in_proj(
            x2d, norm_w[layer][None, :].astype(F32), w_in[layer].astype(BF16),
            b_gate[layer][None, :].astype(F32), cosf, sa, sb, qk_w_row, ones_blk, bsz, seq)
        attn = _attention(pf2d.reshape(bsz, seq, PF_WIDTH))

        toep, s5_in, s5_out, a_pow = _s5_prep(
            ssm_lam_re[layer].astype(F32), ssm_lam_im[layer].astype(F32),
            ssm_log_dt[layer].astype(F32), ssm_b_re[layer].astype(F32),
            ssm_b_im[layer].astype(F32), ssm_c_re[layer].astype(F32),
            ssm_c_im[layer].astype(F32))
        n_tiles = SSM_WIDTH // LANES
        y_tb = _s5(u_tb.reshape(n_tiles, seq // CHUNK, CHUNK * bsz, LANES),
                   ssm_d[layer].astype(F32).reshape(n_tiles, 1, LANES), toep, s5_in, s5_out,
                   a_pow, bsz)

        out2d = _tail(x2d, y_tb.reshape(n_tiles, seq * bsz, LANES),
                      attn.reshape(bsz * seq, ATTN_WIDTH), pz2d,
                      w_glu[layer].astype(BF16), b_glu[layer][None, :].astype(F32),
                      w_proj_ssm[layer].astype(BF16), w_proj_attn[layer].astype(BF16),
                      w_out[layer].astype(BF16), bsz, seq)
        x = out2d.reshape(bsz, seq, D_MODEL)
    return x
```

```python
import math

import jax
import jax.numpy as jnp
import numpy as np
from jax import lax
from jax.experimental import pallas as pl
from jax.experimental.pallas import tpu as pltpu

F32 = jnp.float32
BF16 = jnp.bfloat16

D_MODEL = 1024
SSM_WIDTH = 512
SSM_GROUP = 16
SSM_GROUPS = 32
SSM_STATE = 64
HEAD_DIM = 64
ATTN_SLOTS = 8
ATTN_WIDTH = 512
DILATIONS = (1, 4, 16)
BAND_HALF = 64
ROPE_THETA = 500000.0
ROPE_DIM = 16
EPS = 1e-6
NEG_INF = -1e30
IN_WIDTH = 6144
COL_U, COL_ZA, COL_Q, COL_K, COL_V, COL_ZB, COL_G = 0, 512, 1024, 2560, 3072, 3584, 4096

LANES = 128
SUBLANES = 8
VMEM_LIMIT = 56 * 1024 * 1024

CHUNK = 16
GP = 2
N_PAIRS = SSM_GROUPS // GP
PG = GP * SSM_GROUP
PW = CHUNK * PG
SW = 4 * LANES
GPS = LANES // PG


def _dot(a, b):
    return jnp.dot(a, b, preferred_element_type=F32)


def _dot_nt(a, b):
    return lax.dot_general(a, b, (((1,), (1,)), ((), ())), preferred_element_type=F32)


def _dot_nt_f32(a, b):
    a_hi, b_hi = a.astype(BF16), b.astype(BF16)
    a_lo = (a - a_hi.astype(F32)).astype(BF16)
    b_lo = (b - b_hi.astype(F32)).astype(BF16)
    return _dot_nt(a_hi, b_hi) + (_dot_nt(a_hi, b_lo) + _dot_nt(a_lo, b_hi))


def _rotate_rows(tile, forward):
    residue = lax.broadcasted_iota(jnp.int32, tile.shape, 0) % GPS
    out = tile
    for k in range(1, GPS):
        out = jnp.where(residue == k, pltpu.roll(tile, (k if forward else GPS - k) * PG, 1), out)
    return out


def _sigmoid(v):
    return 0.5 * jnp.tanh(0.5 * v) + 0.5


IN_TT = 512
IN_TN = 512
QK_ROWS = 256
LOG2E = math.log2(math.e)
PF_Q, PF_K, PF_V, PF_WIDTH = 0, 1536, 2048, 2560
PZ_G, PZ_ZA, PZ_ZB, PZ_WIDTH = 0, 2048, 2560, 3072
_Q_COLS = tuple((COL_Q + o, PF_Q + o) for o in range(0, COL_K - COL_Q, IN_TN))
_G_COLS = tuple((COL_G + o, PZ_G + o) for o in range(0, IN_WIDTH - COL_G, IN_TN))
F32_DEST = dict(((COL_K, PF_K), (COL_V, PF_V)) + _Q_COLS)
BF16_DEST = dict(((COL_ZA, PZ_ZA), (COL_ZB, PZ_ZB)) + _G_COLS)


def _in_proj_kernel(x_ref, nw_ref, w_ref, b_ref, cos_ref, sa_ref, sb_ref, qkw_ref, ones_ref,
                    pf_ref, pz_ref, utb_ref):
    bsz = utb_ref.shape[1] // IN_TT
    batch = pl.program_id(0) % bsz
    x = x_ref[...]
    var = jnp.mean(x * x, axis=-1, keepdims=True)
    h = (x * lax.rsqrt(var + EPS) * nw_ref[...]).astype(BF16)
    ones_blk = ones_ref[...]
    for c0 in range(0, IN_WIDTH, IN_TN):
        acc = _dot(h, w_ref[:, c0:c0 + IN_TN])
        if c0 in BF16_DEST:
            if c0 >= COL_G:
                acc = acc + b_ref[:, c0 - COL_G:c0 - COL_G + IN_TN]
            pz_ref[:, BF16_DEST[c0]:BF16_DEST[c0] + IN_TN] = acc.astype(BF16)
            continue
        if c0 == COL_U:
            for lt in range(SSM_WIDTH // LANES):
                utb_ref[lt, pl.ds(batch, IN_TT, stride=bsz), :] = _rotate_rows(
                    acc[:, lt * LANES:(lt + 1) * LANES], forward=True)
            continue
        d0 = F32_DEST[c0]
        if not COL_Q <= c0 < COL_V:
            pf_ref[:, d0:d0 + IN_TN] = acc
            continue
        for r0 in range(0, IN_TT, QK_ROWS):
            rows = slice(r0, r0 + QK_ROWS)
            for t2 in range(0, IN_TN, 2 * LANES):
                xq2 = acc[rows, t2:t2 + 2 * LANES]
                ms2 = _dot((xq2 * xq2).astype(BF16), ones_blk)
                xn2 = xq2 * lax.rsqrt(ms2 + EPS) * qkw_ref[:, c0 - COL_Q + t2:
                                                            c0 - COL_Q + t2 + 2 * LANES]
                for t in (0, LANES):
                    xn = xn2[:, t:t + LANES]
                    pf_ref[rows, d0 + t2 + t:d0 + t2 + t + LANES] = (
                        xn * cos_ref[rows, :]
                        + pltpu.roll(xn, LANES - ROPE_DIM // 2, 1) * sa_ref[rows, :]
                        + pltpu.roll(xn, ROPE_DIM // 2, 1) * sb_ref[rows, :])


def _token_tile(bsz, tiles_per_seq):
    return lambda i: (i % bsz) * tiles_per_seq + i // bsz


def _in_proj(x2d, norm_w, w_in_bf16, b_gate_row, cosf, sa, sb, qk_w_row, ones_blk, bsz, seq):
    t = x2d.shape[0]
    tile = _token_tile(bsz, seq // IN_TT)
    const = lambda shape: pl.BlockSpec(shape, lambda i: (0, 0))
    rope = lambda: pl.BlockSpec((IN_TT, LANES), lambda i: (i // bsz, 0))
    return pl.pallas_call(
        _in_proj_kernel,
        grid=(t // IN_TT,),
        in_specs=[
            pl.BlockSpec((IN_TT, D_MODEL), lambda i: (tile(i), 0)),
            const((1, D_MODEL)),
            pl.BlockSpec((D_MODEL, IN_WIDTH), lambda i: (0, 0), pipeline_mode=pl.Buffered(1)),
            const((1, IN_WIDTH - COL_G)),
            rope(), rope(), rope(),
            const((1, COL_V - COL_Q)),
            const((2 * LANES, 2 * LANES)),
        ],
        out_specs=[
            pl.BlockSpec((IN_TT, PF_WIDTH), lambda i: (tile(i), 0)),
            pl.BlockSpec((IN_TT, PZ_WIDTH), lambda i: (tile(i), 0)),
            pl.BlockSpec((SSM_WIDTH // LANES, IN_TT * bsz, LANES), lambda i: (0, i // bsz, 0)),
        ],
        out_shape=[
            jax.ShapeDtypeStruct((t, PF_WIDTH), F32),
            jax.ShapeDtypeStruct((t, PZ_WIDTH), BF16),
            jax.ShapeDtypeStruct((SSM_WIDTH // LANES, t, LANES), F32),
        ],
        compiler_params=pltpu.CompilerParams(
            dimension_semantics=("arbitrary",), vmem_limit_bytes=VMEM_LIMIT),
        name="in_proj",
    )(x2d, norm_w, w_in_bf16, b_gate_row, cosf, sa, sb, qk_w_row, ones_blk)


QBLK = 128
KWIN = QBLK + 2 * BAND_HALF
NORM_ROWS = 256
MERGE_ROWS = 64


def _attn_kernel(q0_ref, q1_ref, q2_ref, k_ref, v_ref, bias_ref, o_ref,
                 va_s, vb_s, oacc_s, den_s, max_s, s_s):
    seq = k_ref.shape[1]
    q_refs = (q0_ref, q1_ref, q2_ref)
    head0 = lax.broadcasted_iota(jnp.int32, (QBLK, LANES), 1) < HEAD_DIM
    head0_rows = lax.broadcasted_iota(jnp.int32, (NORM_ROWS, LANES), 1) < HEAD_DIM

    def prep(i, carry):
        rows = pl.ds(pl.multiple_of(i * NORM_ROWS, NORM_ROWS), NORM_ROWS)
        v = v_ref[0, rows, :]
        va_s[rows, :] = jnp.where(head0_rows, v, 1.0)
        vb_s[rows, :] = jnp.where(head0_rows, 1.0, v)
        return carry

    lax.fori_loop(0, seq // NORM_ROWS, prep, 0, unroll=True)

    patterns = []
    for p, d in enumerate(DILATIONS):
        n = seq // d
        nblk = n // QBLK
        kw = min(n, KWIN)

        def rows_of(idx, d=d, n=n, nblk=nblk, kw=kw):
            r = idx // nblk
            q0 = (idx % nblk) * QBLK
            ks = jnp.clip(q0 - BAND_HALF, 0, n - kw)
            return (pl.ds(r + d * q0, QBLK, stride=d), pl.ds(r + d * ks, kw, stride=d),
                    (q0 - ks) // BAND_HALF)

        def scores(idx, slot, p=p, kw=kw, rows_of=rows_of):
            qrows, krows, case = rows_of(idx)
            qb = q_refs[p][0, qrows, :]
            kb = k_ref[0, krows, :].astype(BF16)
            bias = bias_ref[case, :, :kw]
            s_s[slot, :QBLK, :kw] = _dot_nt(jnp.where(head0, qb, 0.0).astype(BF16), kb) + bias
            s_s[slot, QBLK:, :kw] = _dot_nt(jnp.where(head0, 0.0, qb).astype(BF16), kb) + bias

        def weigh(idx, slot, p=p, kw=kw, rows_of=rows_of):
            qrows, krows, _ = rows_of(idx)
            s = s_s[slot, :, :kw]
            m = jnp.max(s, axis=-1, keepdims=True)
            e = jnp.exp2(s - m).astype(BF16)
            o_a = _dot(e[:QBLK], va_s[krows, :].astype(BF16))
            o_b = _dot(e[QBLK:], vb_s[krows, :].astype(BF16))
            oacc_s[p, qrows, :] = jnp.where(head0, o_a, o_b)
            den_s[p, qrows, :] = jnp.where(head0, o_b, o_a)
            max_s[p, qrows, :] = jnp.where(head0, jnp.broadcast_to(m[:QBLK], (QBLK, LANES)),
                                           jnp.broadcast_to(m[QBLK:], (QBLK, LANES)))

        patterns.append((scores, weigh, d * nblk))

    def pair(weigh, i0, cur, ahead):
        nxt = 2 - cur
        if ahead is not None:
            ahead[0](ahead[1], nxt)
        weigh(i0, cur)
        if ahead is not None:
            ahead[0](ahead[1] + 1, nxt + 1)
        weigh(i0 + 1, cur + 1)

    patterns[0][0](0, 0)
    patterns[0][0](1, 1)
    for which, (scores, weigh, n_blocks) in enumerate(patterns):
        def quad(j, carry, scores=scores, weigh=weigh):
            pair(weigh, 4 * j, 0, (scores, 4 * j + 2))
            pair(weigh, 4 * j + 2, 2, (scores, 4 * j + 4))
            return carry

        lax.fori_loop(0, n_blocks // 4 - 1, quad, 0, unroll=True)
        pair(weigh, n_blocks - 4, 0, (scores, n_blocks - 2))
        following = (patterns[which + 1][0], 0) if which + 1 < len(patterns) else None
        pair(weigh, n_blocks - 2, 2, following)

    def combine(i, carry):
        rows = pl.ds(pl.multiple_of(i * MERGE_ROWS, MERGE_ROWS), MERGE_ROWS)
        m0, m1, m2 = max_s[0, rows, :], max_s[1, rows, :], max_s[2, rows, :]
        m = jnp.maximum(jnp.maximum(m0, m1), m2)
        w0, w1, w2 = jnp.exp2(m0 - m), jnp.exp2(m1 - m), jnp.exp2(m2 - m)
        num = w0 * oacc_s[0, rows, :] + w1 * oacc_s[1, rows, :] + w2 * oacc_s[2, rows, :]
        d0, d1, d2 = (pltpu.roll(den_s[p, rows, :], HEAD_DIM, 1) for p in range(3))
        o_ref[0, rows, :] = (num / (w0 * d0 + w1 * d1 + w2 * d2)).astype(o_ref.dtype)
        return carry

    lax.fori_loop(0, seq // MERGE_ROWS, combine, 0, unroll=True)


def _band_bias():
    i = np.arange(QBLK)[:, None]
    j = np.arange(KWIN)[None, :]
    off = np.arange(3)[:, None, None] * BAND_HALF
    return np.where(np.abs(j - i - off) <= BAND_HALF, 0.0, NEG_INF).astype(np.float32)


def _attention(proj3d):
    bsz, seq, _ = proj3d.shape
    n_pairs = ATTN_WIDTH // LANES

    def qspec(p):
        return pl.BlockSpec((1, seq, LANES),
                            lambda b, hp, p=p: (b, 0, PF_Q // LANES + p * n_pairs + hp))

    seq_tile = lambda: pltpu.VMEM((seq, LANES), F32)
    pat_tile = lambda: pltpu.VMEM((len(DILATIONS), seq, LANES), F32)
    return pl.pallas_call(
        _attn_kernel,
        grid=(bsz, n_pairs),
        in_specs=[
            qspec(0), qspec(1), qspec(2),
            pl.BlockSpec((1, seq, LANES), lambda b, hp: (b, 0, PF_K // LANES + hp)),
            pl.BlockSpec((1, seq, LANES), lambda b, hp: (b, 0, PF_V // LANES + hp)),
            pl.BlockSpec((3, QBLK, KWIN), lambda b, hp: (0, 0, 0)),
        ],
        out_specs=pl.BlockSpec((1, seq, LANES), lambda b, hp: (b, 0, hp)),
        out_shape=jax.ShapeDtypeStruct((bsz, seq, ATTN_WIDTH), BF16),
        scratch_shapes=[
            seq_tile(), seq_tile(),
            pat_tile(), pat_tile(), pat_tile(),
            pltpu.VMEM((4, 2 * QBLK, KWIN), F32),
        ],
        compiler_params=pltpu.CompilerParams(
            dimension_semantics=("arbitrary", "arbitrary"), vmem_limit_bytes=VMEM_LIMIT),
        name="dilated_attention",
    )(proj3d, proj3d, proj3d, proj3d, proj3d, _band_bias())


LAG_REP = LANES // PG


def _s5_prep_kernel(lre_ref, lim_ref, ldt_ref, bre_ref, bim_ref, cre_ref, cim_ref,
                    toep_ref, win_ref, wout_ref, a_ref, wf32_s):
    four = lambda ref: jnp.concatenate([ref[0, 0], ref[0, 0], ref[1, 0], ref[1, 0]], axis=1)
    lr = jnp.minimum(four(lre_ref), -1e-4)
    li = four(lim_ref)
    dt = jnp.exp(four(ldt_ref))
    col = lax.broadcasted_iota(jnp.int32, (1, SW), 1)
    is_re = (col // LANES) % 2 == 0
    is_fwd = col < 2 * LANES

    mag = jnp.exp(lr * dt)
    pw_re = [jnp.ones_like(mag), mag * jnp.cos(li * dt)]
    pw_im = [jnp.zeros_like(mag), mag * jnp.sin(li * dt)]
    for _ in range(CHUNK - 1):
        pw_re.append(pw_re[-1] * pw_re[1] - pw_im[-1] * pw_im[1])
        pw_im.append(pw_re[-2] * pw_im[1] + pw_im[-1] * pw_re[1])

    a_re, a_im = pw_re[1], pw_im[1]
    nr, ni, mag2 = a_re - 1.0, a_im, lr * lr + li * li
    f_re, f_im = (nr * lr + ni * li) / mag2, (ni * lr - nr * li) / mag2

    row_g = lax.broadcasted_iota(jnp.int32, (PG, SW), 0) // SSM_GROUP
    col_g = (lax.broadcasted_iota(jnp.int32, (PG, SW), 1) % LANES) // SSM_STATE
    diag = row_g == col_g
    b_re = jnp.where(diag, four(bre_ref), 0.0)
    b_im = jnp.where(diag, four(bim_ref), 0.0)
    c_re = jnp.where(diag, four(cre_ref), 0.0)
    c_im = jnp.where(diag, four(cim_ref), 0.0)
    bb_re = f_re * b_re - f_im * b_im
    bb_im = f_re * b_im + f_im * b_re

    for s in range(CHUNK):
        rows = slice(s * PG, (s + 1) * PG)
        in_re = jnp.where(is_fwd, pw_re[CHUNK - 1 - s], pw_re[s])
        in_im = jnp.where(is_fwd, pw_im[CHUNK - 1 - s], pw_im[s])
        w = (bb_re * jnp.where(is_re, in_re, in_im)
             + bb_im * jnp.where(is_re, -in_im, in_re))
        wf32_s[rows, :] = w
        win_ref[0, rows, :] = w.astype(BF16)
        out_re = jnp.where(is_fwd, pw_re[s + 1], pw_re[CHUNK - s])
        out_im = jnp.where(is_fwd, pw_im[s + 1], pw_im[CHUNK - s])
        wout_ref[0, rows, :] = (c_re * jnp.where(is_re, out_re, -out_im)
                                + c_im * jnp.where(is_re, -out_im, -out_re)).astype(BF16)

    a_ref[0] = jnp.broadcast_to(jnp.where(is_re, pw_re[CHUNK], pw_im[CHUNK]), a_ref.shape[1:])

    c_cat = jnp.where(is_re, c_re, -c_im)
    c_rep = jnp.concatenate([c_cat] * LAG_REP, axis=0)
    half = 2 * LANES
    lag_f = _dot_nt_f32(wf32_s[:, :half], c_rep[:, :half])
    lag_b = _dot_nt_f32(wf32_s[:, half:], c_rep[:, half:])
    row_t = lax.broadcasted_iota(jnp.int32, (PW, LANES), 0) // PG
    lane_q = lax.broadcasted_iota(jnp.int32, (PW, LANES), 1) // PG
    zeros = lambda n: jnp.zeros((n * PG, LANES), F32)
    for q in range(PW // LANES):
        tile = jnp.zeros((PW, LANES), F32)
        for u in range(LAG_REP):
            t = q * LAG_REP + u
            up = CHUNK - 1 - t
            sh_f = lag_f if up == 0 else jnp.concatenate([lag_f[up * PG:], zeros(up)], axis=0)
            sh_b = lag_b if t == 0 else jnp.concatenate([zeros(t), lag_b[:(CHUNK - t) * PG]], axis=0)
            col_t = jnp.where(row_t <= t, sh_f, 0.0) + jnp.where(row_t >= t, sh_b, 0.0)
            tile = jnp.where(lane_q == u, col_t, tile)
        toep_ref[0, :, q * LANES:(q + 1) * LANES] = tile.astype(BF16)


def _s5_prep(lam_re, lam_im, log_dt, b_re, b_im, c_re, c_im):
    lanes = lambda t: t.reshape(2, N_PAIRS, 1, LANES)
    dt_b = jnp.broadcast_to(log_dt[..., None], lam_re.shape)

    def tiles(t_ghp):
        t = jnp.broadcast_to(t_ghp[:, :, :, None, :], t_ghp.shape[:3] + (GP, SSM_STATE))
        return t.reshape(2, N_PAIRS, PG, LANES)

    row = pl.BlockSpec((2, 1, 1, LANES), lambda i: (0, i, 0, 0))
    mat = pl.BlockSpec((2, 1, PG, LANES), lambda i: (0, i, 0, 0))
    out = lambda r: pl.BlockSpec((1, r, SW), lambda i: (i, 0, 0))
    return pl.pallas_call(
        _s5_prep_kernel,
        grid=(N_PAIRS,),
        in_specs=[row, row, row, mat, mat, mat, mat],
        out_specs=[out(PW), out(PW), out(PW), out(SUBLANES)],
        out_shape=[jax.ShapeDtypeStruct((N_PAIRS, PW, PW), BF16),
                   jax.ShapeDtypeStruct((N_PAIRS, PW, SW), BF16),
                   jax.ShapeDtypeStruct((N_PAIRS, PW, SW), BF16),
                   jax.ShapeDtypeStruct((N_PAIRS, SUBLANES, SW), F32)],
        scratch_shapes=[pltpu.VMEM((PW, SW), F32)],
        compiler_params=pltpu.CompilerParams(dimension_semantics=("arbitrary",)),
        name="s5_prep",
    )(lanes(lam_re), lanes(lam_im), lanes(dt_b), tiles(b_re.transpose(0, 1, 3, 2)),
      tiles(b_im.transpose(0, 1, 3, 2)), tiles(c_re), tiles(c_im))


SCAN_STEPS = 8
REGROUP_CHUNKS = 32


def _s5_kernel(u_ref, d_ref, toep_ref, win_ref, wout_ref, a_ref, y_ref, x_s, p_s, st_s):
    nc = u_ref.shape[1]
    bsz = u_ref.shape[2] // CHUNK
    piece = REGROUP_CHUNKS * bsz
    pp = pl.program_id(1)
    slot = lax.broadcasted_iota(jnp.int32, (piece, LANES), 1) // PG
    time_rows = lambda t: slice(t * bsz, (t + 1) * bsz)

    def gather(cb, carry):
        chunks = pl.ds(pl.multiple_of(cb * REGROUP_CHUNKS, REGROUP_CHUNKS), REGROUP_CHUNKS)
        rows = pl.ds(pl.multiple_of(cb * piece, piece), piece)
        for q in range(CHUNK // GPS):
            acc = None
            for k in range(GPS):
                v = u_ref[0, chunks, time_rows(q * GPS + k), :].reshape(piece, LANES)
                acc = v if acc is None else jnp.where(slot == (pp + k) % GPS, v, acc)
            x_s[rows, q * LANES:(q + 1) * LANES] = pltpu.roll(
                acc, ((GPS - pp) % GPS) * PG, 1).astype(BF16)
        return carry

    lax.fori_loop(0, nc // REGROUP_CHUNKS, gather, 0)

    x = x_s[...]
    p_s[...] = _dot(x, win_ref[0])
    a_fr, a_fi, a_br, a_bi = (a_ref[0, :, q * LANES:(q + 1) * LANES] for q in range(4))
    tile = lambda q: slice(q * LANES, (q + 1) * LANES)

    def scan(i, carry):
        base_f = pl.multiple_of(i * (SCAN_STEPS * bsz), SCAN_STEPS * bsz)
        base_b = pl.multiple_of((nc - (i + 1) * SCAN_STEPS) * bsz, SCAN_STEPS * bsz)
        s_fr, s_fi, s_br, s_bi = carry
        for k in range(SCAN_STEPS):
            rf = pl.ds(base_f + k * bsz, bsz)
            rb = pl.ds(base_b + (SCAN_STEPS - 1 - k) * bsz, bsz)
            st_s[rf, tile(0)] = s_fr
            st_s[rf, tile(1)] = s_fi
            st_s[rb, tile(2)] = s_br
            st_s[rb, tile(3)] = s_bi
            s_fr, s_fi, s_br, s_bi = (a_fr * s_fr - a_fi * s_fi + p_s[rf, tile(0)],
                                      a_fr * s_fi + a_fi * s_fr + p_s[rf, tile(1)],
                                      a_br * s_br - a_bi * s_bi + p_s[rb, tile(2)],
                                      a_br * s_bi + a_bi * s_br + p_s[rb, tile(3)])
        return s_fr, s_fi, s_br, s_bi

    zero = jnp.zeros((bsz, LANES), F32)
    lax.fori_loop(0, nc // SCAN_STEPS, scan, (zero,) * 4)
    p_s[...] = _dot(x, toep_ref[0]) + _dot_nt(st_s[...].astype(BF16), wout_ref[0])

    d_tile = jnp.broadcast_to(d_ref[0], (bsz, LANES))
    d_skip = [d_tile if k == 0 else pltpu.roll(d_tile, k * PG, 1) for k in range(GPS)]

    def scatter(cb, carry):
        chunks = pl.ds(pl.multiple_of(cb * REGROUP_CHUNKS, REGROUP_CHUNKS), REGROUP_CHUNKS)
        rows = pl.ds(pl.multiple_of(cb * piece, piece), piece)
        for q in range(CHUNK // GPS):
            w = pltpu.roll(p_s[rows, q * LANES:(q + 1) * LANES], pp * PG, 1)
            w = w.reshape(1, REGROUP_CHUNKS, bsz, LANES)
            for k in range(GPS):
                where = (pl.ds(0, 1), chunks, time_rows(q * GPS + k), slice(None))
                mine = (slot == (pp + k) % GPS).reshape(1, REGROUP_CHUNKS, bsz, LANES)
                pltpu.store(y_ref.at[where], w + d_skip[k] * u_ref[where], mask=mine)
        return carry

    lax.fori_loop(0, nc // REGROUP_CHUNKS, scatter, 0)


def _s5(u_tb, d_tiles, toep, w_in, w_out, a_pow, bsz):
    n_tiles, nc, cb, _ = u_tb.shape
    rows = nc * bsz
    tile = pl.BlockSpec((1, nc, cb, LANES), lambda j, p: (j, 0, 0, 0))
    pair = lambda r, c: pl.BlockSpec((1, r, c), lambda j, p: (j * GPS + p, 0, 0))
    return pl.pallas_call(
        _s5_kernel,
        grid=(n_tiles, GPS),
        in_specs=[tile, pl.BlockSpec((1, 1, LANES), lambda j, p: (j, 0, 0)),
                  pair(PW, PW), pair(PW, SW), pair(PW, SW), pair(bsz, SW)],
        out_specs=tile,
        out_shape=jax.ShapeDtypeStruct(u_tb.shape, F32),
        scratch_shapes=[pltpu.VMEM((rows, PW), BF16),
                        pltpu.VMEM((rows, SW), F32),
                        pltpu.VMEM((rows, SW), F32)],
        compiler_params=pltpu.CompilerParams(
            dimension_semantics=("arbitrary", "arbitrary"), vmem_limit_bytes=VMEM_LIMIT),
        name="s5_chunked",
    )(u_tb, d_tiles, toep, w_in, w_out, a_pow)


TAIL_TT = 512


def _tail_kernel(x_ref, ytb_ref, za_ref, at_ref, zb_ref, g_ref, wg32_ref, bg_ref,
                 wps32_ref, wpa32_ref, wo32_ref, o_ref, wg_ref, wps_ref, wpa_ref, wo_ref):
    @pl.when(pl.program_id(0) == 0)
    def _cast_weights():
        for src, dst in ((wg32_ref, wg_ref), (wps32_ref, wps_ref), (wpa32_ref, wpa_ref),
                         (wo32_ref, wo_ref)):
            dst[...] = src[...].astype(BF16)

    bsz = ytb_ref.shape[1] // TAIL_TT
    batch = pl.program_id(0) % bsz
    y_s5 = jnp.concatenate(
        [_rotate_rows(ytb_ref[lt, pl.ds(batch, TAIL_TT, stride=bsz), :], forward=False)
         for lt in range(SSM_WIDTH // LANES)], axis=1)
    ys = jax.nn.gelu(y_s5).astype(BF16)
    glu = _dot(ys, wg_ref[...]) + bg_ref[...]
    z_b = zb_ref[...].astype(F32)
    y_b = _dot((at_ref[...].astype(F32) * (z_b * _sigmoid(z_b))).astype(BF16), wpa_ref[...])
    z_a = za_ref[...].astype(F32)
    a_in = glu[:, :SSM_WIDTH] * _sigmoid(glu[:, SSM_WIDTH:]) * (z_a * _sigmoid(z_a))
    y_a = _dot(a_in.astype(BF16), wps_ref[...])
    g = g_ref[...].astype(F32)
    mix = _sigmoid(g[:, :D_MODEL]) * y_a + _sigmoid(g[:, D_MODEL:]) * y_b
    o_ref[...] = x_ref[...] + _dot(mix.astype(BF16), wo_ref[...])


def _tail(x2d, y_tb, attn2d, pz2d, w_glu, b_glu, w_ps, w_pa, w_out, bsz, seq):
    t = x2d.shape[0]
    tile = _token_tile(bsz, seq // TAIL_TT)
    row = lambda w, c: pl.BlockSpec((TAIL_TT, w), lambda i, c=c: (tile(i), c))
    const = lambda shape: pl.BlockSpec(shape, lambda i: (0, 0))
    weight = lambda shape: pl.BlockSpec(shape, lambda i: (0, 0), pipeline_mode=pl.Buffered(1))
    return pl.pallas_call(
        _tail_kernel,
        grid=(t // TAIL_TT,),
        in_specs=[
            row(D_MODEL, 0),
            pl.BlockSpec((SSM_WIDTH // LANES, TAIL_TT * bsz, LANES), lambda i: (0, i // bsz, 0)),
            row(SSM_WIDTH, PZ_ZA // SSM_WIDTH),
            row(ATTN_WIDTH, 0),
            row(ATTN_WIDTH, PZ_ZB // ATTN_WIDTH),
            row(2 * D_MODEL, PZ_G // (2 * D_MODEL)),
            weight((SSM_WIDTH, 2 * SSM_WIDTH)), const((1, 2 * SSM_WIDTH)),
            weight((SSM_WIDTH, D_MODEL)), weight((ATTN_WIDTH, D_MODEL)), weight((D_MODEL, D_MODEL)),
        ],
        out_specs=pl.BlockSpec((TAIL_TT, D_MODEL), lambda i: (tile(i), 0)),
        out_shape=jax.ShapeDtypeStruct((t, D_MODEL), F32),
        scratch_shapes=[pltpu.VMEM((SSM_WIDTH, 2 * SSM_WIDTH), BF16),
                        pltpu.VMEM((SSM_WIDTH, D_MODEL), BF16),
                        pltpu.VMEM((ATTN_WIDTH, D_MODEL), BF16),
                        pltpu.VMEM((D_MODEL, D_MODEL), BF16)],
        compiler_params=pltpu.CompilerParams(
            dimension_semantics=("arbitrary",), vmem_limit_bytes=VMEM_LIMIT),
        name="tail",
    )(x2d, y_tb, pz2d, attn2d, pz2d, pz2d, w_glu, b_glu, w_ps, w_pa, w_out)


def _rope_tables(seq):
    half = ROPE_DIM // 2
    inv = ROPE_THETA ** (-np.arange(0, ROPE_DIM, 2, dtype=np.float64) / ROPE_DIM)
    ang = np.arange(seq, dtype=np.float64)[:, None] * inv[None, :]
    cos, sin = np.cos(ang).astype(np.float32), np.sin(ang).astype(np.float32)
    zeros = np.zeros((seq, HEAD_DIM - ROPE_DIM), np.float32)
    z8 = np.zeros((seq, half), np.float32)
    cos_h = np.concatenate([cos, cos, np.ones_like(zeros)], axis=1)
    sa_h = np.concatenate([-sin, z8, zeros], axis=1)
    sb_h = np.concatenate([z8, sin, zeros], axis=1)
    two = lambda t: np.concatenate([t, t], axis=1).astype(np.float32)
    return two(cos_h), two(sa_h), two(sb_h)


def kernel(x, norm_w, w_in, b_gate, q_norm_w, k_norm_w, ssm_lam_re, ssm_lam_im, ssm_log_dt,
           ssm_b_re, ssm_b_im, ssm_c_re, ssm_c_im, ssm_d, w_glu, b_glu,
           w_proj_ssm, w_proj_attn, w_out):
    bsz, seq, d_model = x.shape
    depth = norm_w.shape[0]
    assert d_model == D_MODEL and w_in.shape[-1] == IN_WIDTH
    assert bsz == SUBLANES, "S5 rows (chunk, batch) must fill whole sublane tiles"
    assert seq % IN_TT == 0 and seq % TAIL_TT == 0 and seq % CHUNK == 0
    assert IN_TT % GPS == 0 and TAIL_TT % GPS == 0 and CHUNK % GPS == 0
    assert all(seq % (d * QBLK) == 0 for d in DILATIONS) and (seq // QBLK) % 4 == 0
    cosf, sa, sb = _rope_tables(seq)
    ones_blk = jnp.asarray(np.kron(np.eye(2 * LANES // HEAD_DIM, dtype=np.float32),
                                   np.full((HEAD_DIM, HEAD_DIM), 1.0 / HEAD_DIM, np.float32)), BF16)
    for layer in range(depth):
        x2d = x.reshape(bsz * seq, D_MODEL)
        q_gain = q_norm_w[layer].astype(F32) * (LOG2E * HEAD_DIM ** -0.5)
        qk_w_row = jnp.concatenate([jnp.tile(q_gain, len(DILATIONS) * ATTN_SLOTS),
                                    jnp.tile(k_norm_w[layer].astype(F32), ATTN_SLOTS)])[None, :]
        pf2d, pz2d, u_tb = _in_proj(
            x2d, norm_w[layer][None, :].astype(F32), w_in[layer].astype(BF16),
            b_gate[layer][None, :].astype(F32), cosf, sa, sb, qk_w_row, ones_blk, bsz, seq)
        attn = _attention(pf2d.reshape(bsz, seq, PF_WIDTH))

        toep, s5_in, s5_out, a_pow = _s5_prep(
            ssm_lam_re[layer].astype(F32), ssm_lam_im[layer].astype(F32),
            ssm_log_dt[layer].astype(F32), ssm_b_re[layer].astype(F32),
            ssm_b_im[layer].astype(F32), ssm_c_re[layer].astype(F32),
            ssm_c_im[layer].astype(F32))
        n_tiles = SSM_WIDTH // LANES
        y_tb = _s5(u_tb.reshape(n_tiles, seq // CHUNK, CHUNK * bsz, LANES),
                   ssm_d[layer].astype(F32).reshape(n_tiles, 1, LANES), toep, s5_in, s5_out,
                   a_pow, bsz)

        out2d = _tail(x2d, y_tb.reshape(n_tiles, seq * bsz, LANES),
                      attn.reshape(bsz * seq, ATTN_WIDTH), pz2d,
                      w_glu[layer].astype(F32), b_glu[layer][None, :].astype(F32),
                      w_proj_ssm[layer].astype(F32), w_proj_attn[layer].astype(F32),
                      w_out[layer].astype(F32), bsz, seq)
        x = out2d.reshape(bsz, seq, D_MODEL)
    return x
```

```python
import math

import jax
import jax.numpy as jnp
import numpy as np
from jax import lax
from jax.experimental import pallas as pl
from jax.experimental.pallas import tpu as pltpu

F32 = jnp.float32
BF16 = jnp.bfloat16

D_MODEL = 1024
SSM_WIDTH = 512
SSM_GROUP = 16
SSM_GROUPS = 32
SSM_STATE = 64
HEAD_DIM = 64
ATTN_SLOTS = 8
ATTN_WIDTH = 512
DILATIONS = (1, 4, 16)
BAND_HALF = 64
ROPE_THETA = 500000.0
ROPE_DIM = 16
EPS = 1e-6
NEG_INF = -1e30
IN_WIDTH = 6144
COL_U, COL_ZA, COL_Q, COL_K, COL_V, COL_ZB, COL_G = 0, 512, 1024, 2560, 3072, 3584, 4096

LANES = 128
SUBLANES = 8
VMEM_LIMIT = 56 * 1024 * 1024

CHUNK = 16
GP = 2
N_PAIRS = SSM_GROUPS // GP
PG = GP * SSM_GROUP
PW = CHUNK * PG
SW = 4 * LANES
GPS = LANES // PG


def _dot(a, b):
    return jnp.dot(a, b, preferred_element_type=F32)


def _dot_nt(a, b):
    return lax.dot_general(a, b, (((1,), (1,)), ((), ())), preferred_element_type=F32)


def _dot_nt_f32(a, b):
    a_hi, b_hi = a.astype(BF16), b.astype(BF16)
    a_lo = (a - a_hi.astype(F32)).astype(BF16)
    b_lo = (b - b_hi.astype(F32)).astype(BF16)
    return _dot_nt(a_hi, b_hi) + (_dot_nt(a_hi, b_lo) + _dot_nt(a_lo, b_hi))


def _rotate_rows(tile, forward):
    residue = lax.broadcasted_iota(jnp.int32, tile.shape, 0) % GPS
    out = tile
    for k in range(1, GPS):
        out = jnp.where(residue == k, pltpu.roll(tile, (k if forward else GPS - k) * PG, 1), out)
    return out


def _sigmoid(v):
    return 0.5 * jnp.tanh(0.5 * v) + 0.5


IN_TT = 512
IN_TN = 512
QK_ROWS = 256
LOG2E = math.log2(math.e)
PF_Q, PF_K, PF_V, PF_WIDTH = 0, 1536, 2048, 2560
PZ_G, PZ_ZA, PZ_ZB, PZ_WIDTH = 0, 2048, 2560, 3072
_Q_COLS = tuple((COL_Q + o, PF_Q + o) for o in range(0, COL_K - COL_Q, IN_TN))
_G_COLS = tuple((COL_G + o, PZ_G + o) for o in range(0, IN_WIDTH - COL_G, IN_TN))
F32_DEST = dict(((COL_K, PF_K), (COL_V, PF_V)) + _Q_COLS)
BF16_DEST = dict(((COL_ZA, PZ_ZA), (COL_ZB, PZ_ZB)) + _G_COLS)


def _in_proj_kernel(x_ref, nw_ref, w_ref, b_ref, cos_ref, sa_ref, sb_ref, qkw_ref, ones_ref,
                    pf_ref, pz_ref, utb_ref):
    bsz = utb_ref.shape[1] // IN_TT
    batch = pl.program_id(0) % bsz
    x = x_ref[...]
    var = jnp.mean(x * x, axis=-1, keepdims=True)
    h = (x * lax.rsqrt(var + EPS) * nw_ref[...]).astype(BF16)
    ones_blk = ones_ref[...]
    for c0 in range(0, IN_WIDTH, IN_TN):
        acc = _dot(h, w_ref[:, c0:c0 + IN_TN])
        if c0 in BF16_DEST:
            if c0 >= COL_G:
                acc = acc + b_ref[:, c0 - COL_G:c0 - COL_G + IN_TN]
            pz_ref[:, BF16_DEST[c0]:BF16_DEST[c0] + IN_TN] = acc.astype(BF16)
            continue
        if c0 == COL_U:
            for lt in range(SSM_WIDTH // LANES):
                utb_ref[lt, pl.ds(batch, IN_TT, stride=bsz), :] = _rotate_rows(
                    acc[:, lt * LANES:(lt + 1) * LANES], forward=True)
            continue
        d0 = F32_DEST[c0]
        if not COL_Q <= c0 < COL_V:
            pf_ref[:, d0:d0 + IN_TN] = acc
            continue
        for r0 in range(0, IN_TT, QK_ROWS):
            rows = slice(r0, r0 + QK_ROWS)
            for t2 in range(0, IN_TN, 2 * LANES):
                xq2 = acc[rows, t2:t2 + 2 * LANES]
                ms2 = _dot((xq2 * xq2).astype(BF16), ones_blk)
                xn2 = xq2 * lax.rsqrt(ms2 + EPS) * qkw_ref[:, c0 - COL_Q + t2:
                                                            c0 - COL_Q + t2 + 2 * LANES]
                for t in (0, LANES):
                    xn = xn2[:, t:t + LANES]
                    pf_ref[rows, d0 + t2 + t:d0 + t2 + t + LANES] = (
                        xn * cos_ref[rows, :]
                        + pltpu.roll(xn, LANES - ROPE_DIM // 2, 1) * sa_ref[rows, :]
                        + pltpu.roll(xn, ROPE_DIM // 2, 1) * sb_ref[rows, :])


def _token_tile(bsz, tiles_per_seq):
    return lambda i: (i % bsz) * tiles_per_seq + i // bsz


def _in_proj(x2d, norm_w, w_in_bf16, b_gate_row, cosf, sa, sb, qk_w_row, ones_blk, bsz, seq):
    t = x2d.shape[0]
    tile = _token_tile(bsz, seq // IN_TT)
    const = lambda shape: pl.BlockSpec(shape, lambda i: (0, 0))
    rope = lambda: pl.BlockSpec((IN_TT, LANES), lambda i: (i // bsz, 0))
    return pl.pallas_call(
        _in_proj_kernel,
        grid=(t // IN_TT,),
        in_specs=[
            pl.BlockSpec((IN_TT, D_MODEL), lambda i: (tile(i), 0)),
            const((1, D_MODEL)),
            pl.BlockSpec((D_MODEL, IN_WIDTH), lambda i: (0, 0), pipeline_mode=pl.Buffered(1)),
            const((1, IN_WIDTH - COL_G)),
            rope(), rope(), rope(),
            const((1, COL_V - COL_Q)),
            const((2 * LANES, 2 * LANES)),
        ],
        out_specs=[
            pl.BlockSpec((IN_TT, PF_WIDTH), lambda i: (tile(i), 0)),
            pl.BlockSpec((IN_TT, PZ_WIDTH), lambda i: (tile(i), 0)),
            pl.BlockSpec((SSM_WIDTH // LANES, IN_TT * bsz, LANES), lambda i: (0, i // bsz, 0)),
        ],
        out_shape=[
            jax.ShapeDtypeStruct((t, PF_WIDTH), F32),
            jax.ShapeDtypeStruct((t, PZ_WIDTH), BF16),
            jax.ShapeDtypeStruct((SSM_WIDTH // LANES, t, LANES), F32),
        ],
        compiler_params=pltpu.CompilerParams(
            dimension_semantics=("arbitrary",), vmem_limit_bytes=VMEM_LIMIT),
        name="in_proj",
    )(x2d, norm_w, w_in_bf16, b_gate_row, cosf, sa, sb, qk_w_row, ones_blk)


QBLK = 128
KWIN = QBLK + 2 * BAND_HALF
NORM_ROWS = 256
MERGE_ROWS = 64


def _attn_kernel(q0_ref, q1_ref, q2_ref, k_ref, v_ref, bias_ref, o_ref,
                 va_s, vb_s, oacc_s, den_s, max_s, s_s):
    seq = k_ref.shape[1]
    q_refs = (q0_ref, q1_ref, q2_ref)
    head0 = lax.broadcasted_iota(jnp.int32, (QBLK, LANES), 1) < HEAD_DIM
    head0_rows = lax.broadcasted_iota(jnp.int32, (NORM_ROWS, LANES), 1) < HEAD_DIM

    def prep(i, carry):
        rows = pl.ds(pl.multiple_of(i * NORM_ROWS, NORM_ROWS), NORM_ROWS)
        v = v_ref[0, rows, :]
        va_s[rows, :] = jnp.where(head0_rows, v, 1.0)
        vb_s[rows, :] = jnp.where(head0_rows, 1.0, v)
        return carry

    lax.fori_loop(0, seq // NORM_ROWS, prep, 0, unroll=True)

    patterns = []
    for p, d in enumerate(DILATIONS):
        n = seq // d
        nblk = n // QBLK
        kw = min(n, KWIN)

        def rows_of(idx, d=d, n=n, nblk=nblk, kw=kw):
            r = idx // nblk
            q0 = (idx % nblk) * QBLK
            ks = jnp.clip(q0 - BAND_HALF, 0, n - kw)
            return (pl.ds(r + d * q0, QBLK, stride=d), pl.ds(r + d * ks, kw, stride=d),
                    (q0 - ks) // BAND_HALF)

        def scores(idx, slot, p=p, kw=kw, rows_of=rows_of):
            qrows, krows, case = rows_of(idx)
            qb = q_refs[p][0, qrows, :]
            kb = k_ref[0, krows, :].astype(BF16)
            bias = bias_ref[case, :, :kw]
            s_s[slot, :QBLK, :kw] = _dot_nt(jnp.where(head0, qb, 0.0).astype(BF16), kb) + bias
            s_s[slot, QBLK:, :kw] = _dot_nt(jnp.where(head0, 0.0, qb).astype(BF16), kb) + bias

        def weigh(idx, slot, p=p, kw=kw, rows_of=rows_of):
            qrows, krows, _ = rows_of(idx)
            s = s_s[slot, :, :kw]
            m = jnp.max(s, axis=-1, keepdims=True)
            e = jnp.exp2(s - m).astype(BF16)
            o_a = _dot(e[:QBLK], va_s[krows, :].astype(BF16))
            o_b = _dot(e[QBLK:], vb_s[krows, :].astype(BF16))
            oacc_s[p, qrows, :] = jnp.where(head0, o_a, o_b)
            den_s[p, qrows, :] = jnp.where(head0, o_b, o_a)
            max_s[p, qrows, :] = jnp.where(head0, jnp.broadcast_to(m[:QBLK], (QBLK, LANES)),
                                           jnp.broadcast_to(m[QBLK:], (QBLK, LANES)))

        patterns.append((scores, weigh, d * nblk))

    def pair(weigh, i0, cur, ahead):
        nxt = 2 - cur
        if ahead is not None:
            ahead[0](ahead[1], nxt)
        weigh(i0, cur)
        if ahead is not None:
            ahead[0](ahead[1] + 1, nxt + 1)
        weigh(i0 + 1, cur + 1)

    patterns[0][0](0, 0)
    patterns[0][0](1, 1)
    for which, (scores, weigh, n_blocks) in enumerate(patterns):
        def quad(j, carry, scores=scores, weigh=weigh):
            pair(weigh, 4 * j, 0, (scores, 4 * j + 2))
            pair(weigh, 4 * j + 2, 2, (scores, 4 * j + 4))
            return carry

        lax.fori_loop(0, n_blocks // 4 - 1, quad, 0, unroll=True)
        pair(weigh, n_blocks - 4, 0, (scores, n_blocks - 2))
        following = (patterns[which + 1][0], 0) if which + 1 < len(patterns) else None
        pair(weigh, n_blocks - 2, 2, following)

    def combine(i, carry):
        rows = pl.ds(pl.multiple_of(i * MERGE_ROWS, MERGE_ROWS), MERGE_ROWS)
        m0, m1, m2 = max_s[0, rows, :], max_s[1, rows, :], max_s[2, rows, :]
        m = jnp.maximum(jnp.maximum(m0, m1), m2)
        w0, w1, w2 = jnp.exp2(m0 - m), jnp.exp2(m1 - m), jnp.exp2(m2 - m)
        num = w0 * oacc_s[0, rows, :] + w1 * oacc_s[1, rows, :] + w2 * oacc_s[2, rows, :]
        d0, d1, d2 = (pltpu.roll(den_s[p, rows, :], HEAD_DIM, 1) for p in range(3))
        o_ref[0, rows, :] = (num / (w0 * d0 + w1 * d1 + w2 * d2)).astype(o_ref.dtype)
        return carry

    lax.fori_loop(0, seq // MERGE_ROWS, combine, 0, unroll=True)


def _band_bias():
    i = np.arange(QBLK)[:, None]
    j = np.arange(KWIN)[None, :]
    off = np.arange(3)[:, None, None] * BAND_HALF
    return np.where(np.abs(j - i - off) <= BAND_HALF, 0.0, NEG_INF).astype(np.float32)


def _attention(proj3d):
    bsz, seq, _ = proj3d.shape
    n_pairs = ATTN_WIDTH // LANES

    def qspec(p):
        return pl.BlockSpec((1, seq, LANES),
                            lambda b, hp, p=p: (b, 0, PF_Q // LANES + p * n_pairs + hp))

    seq_tile = lambda: pltpu.VMEM((seq, LANES), F32)
    pat_tile = lambda: pltpu.VMEM((len(DILATIONS), seq, LANES), F32)
    return pl.pallas_call(
        _attn_kernel,
        grid=(bsz, n_pairs),
        in_specs=[
            qspec(0), qspec(1), qspec(2),
            pl.BlockSpec((1, seq, LANES), lambda b, hp: (b, 0, PF_K // LANES + hp)),
            pl.BlockSpec((1, seq, LANES), lambda b, hp: (b, 0, PF_V // LANES + hp)),
            pl.BlockSpec((3, QBLK, KWIN), lambda b, hp: (0, 0, 0)),
        ],
        out_specs=pl.BlockSpec((1, seq, LANES), lambda b, hp: (b, 0, hp)),
        out_shape=jax.ShapeDtypeStruct((bsz, seq, ATTN_WIDTH), BF16),
        scratch_shapes=[
            seq_tile(), seq_tile(),
            pat_tile(), pat_tile(), pat_tile(),
            pltpu.VMEM((4, 2 * QBLK, KWIN), F32),
        ],
        compiler_params=pltpu.CompilerParams(
            dimension_semantics=("arbitrary", "arbitrary"), vmem_limit_bytes=VMEM_LIMIT),
        name="dilated_attention",
    )(proj3d, proj3d, proj3d, proj3d, proj3d, _band_bias())


LAG_REP = LANES // PG


def _s5_prep_kernel(lre_ref, lim_ref, ldt_ref, bre_ref, bim_ref, cre_ref, cim_ref,
                    toep_ref, win_ref, wout_ref, a_ref, wf32_s):
    four = lambda ref: jnp.concatenate([ref[0, 0], ref[0, 0], ref[1, 0], ref[1, 0]], axis=1)
    lr = jnp.minimum(four(lre_ref), -1e-4)
    li = four(lim_ref)
    dt = jnp.exp(four(ldt_ref))
    col = lax.broadcasted_iota(jnp.int32, (1, SW), 1)
    is_re = (col // LANES) % 2 == 0
    is_fwd = col < 2 * LANES

    mag = jnp.exp(lr * dt)
    pw_re = [jnp.ones_like(mag), mag * jnp.cos(li * dt)]
    pw_im = [jnp.zeros_like(mag), mag * jnp.sin(li * dt)]
    for _ in range(CHUNK - 1):
        pw_re.append(pw_re[-1] * pw_re[1] - pw_im[-1] * pw_im[1])
        pw_im.append(pw_re[-2] * pw_im[1] + pw_im[-1] * pw_re[1])

    a_re, a_im = pw_re[1], pw_im[1]
    nr, ni, mag2 = a_re - 1.0, a_im, lr * lr + li * li
    f_re, f_im = (nr * lr + ni * li) / mag2, (ni * lr - nr * li) / mag2

    row_g = lax.broadcasted_iota(jnp.int32, (PG, SW), 0) // SSM_GROUP
    col_g = (lax.broadcasted_iota(jnp.int32, (PG, SW), 1) % LANES) // SSM_STATE
    diag = row_g == col_g
    wide = lambda ref: jnp.concatenate([ref[0, 0]] * (2 * GP) + [ref[1, 0]] * (2 * GP), axis=1)
    b_re = jnp.where(diag, wide(bre_ref), 0.0)
    b_im = jnp.where(diag, wide(bim_ref), 0.0)
    c_re = jnp.where(diag, wide(cre_ref), 0.0)
    c_im = jnp.where(diag, wide(cim_ref), 0.0)
    bb_re = f_re * b_re - f_im * b_im
    bb_im = f_re * b_im + f_im * b_re

    for s in range(CHUNK):
        rows = slice(s * PG, (s + 1) * PG)
        in_re = jnp.where(is_fwd, pw_re[CHUNK - 1 - s], pw_re[s])
        in_im = jnp.where(is_fwd, pw_im[CHUNK - 1 - s], pw_im[s])
        w = (bb_re * jnp.where(is_re, in_re, in_im)
             + bb_im * jnp.where(is_re, -in_im, in_re))
        wf32_s[rows, :] = w
        win_ref[0, rows, :] = w.astype(BF16)
        out_re = jnp.where(is_fwd, pw_re[s + 1], pw_re[CHUNK - s])
        out_im = jnp.where(is_fwd, pw_im[s + 1], pw_im[CHUNK - s])
        wout_ref[0, rows, :] = (c_re * jnp.where(is_re, out_re, -out_im)
                                + c_im * jnp.where(is_re, -out_im, -out_re)).astype(BF16)

    a_ref[0] = jnp.broadcast_to(jnp.where(is_re, pw_re[CHUNK], pw_im[CHUNK]), a_ref.shape[1:])

    c_cat = jnp.where(is_re, c_re, -c_im)
    c_rep = jnp.concatenate([c_cat] * LAG_REP, axis=0)
    half = 2 * LANES
    lag_f = _dot_nt_f32(wf32_s[:, :half], c_rep[:, :half])
    lag_b = _dot_nt_f32(wf32_s[:, half:], c_rep[:, half:])
    row_t = lax.broadcasted_iota(jnp.int32, (PW, LANES), 0) // PG
    lane_q = lax.broadcasted_iota(jnp.int32, (PW, LANES), 1) // PG
    zeros = lambda n: jnp.zeros((n * PG, LANES), F32)
    for q in range(PW // LANES):
        tile = jnp.zeros((PW, LANES), F32)
        for u in range(LAG_REP):
            t = q * LAG_REP + u
            up = CHUNK - 1 - t
            sh_f = lag_f if up == 0 else jnp.concatenate([lag_f[up * PG:], zeros(up)], axis=0)
            sh_b = lag_b if t == 0 else jnp.concatenate([zeros(t), lag_b[:(CHUNK - t) * PG]], axis=0)
            col_t = jnp.where(row_t <= t, sh_f, 0.0) + jnp.where(row_t >= t, sh_b, 0.0)
            tile = jnp.where(lane_q == u, col_t, tile)
        toep_ref[0, :, q * LANES:(q + 1) * LANES] = tile.astype(BF16)


def _s5_prep(lam_re, lam_im, log_dt, b_re, b_im, c_re, c_im):
    lanes = lambda t: t.reshape(2, N_PAIRS, 1, LANES)
    dt_b = jnp.broadcast_to(log_dt[..., None], lam_re.shape)

    tiles = lambda t_ghp: t_ghp.reshape(2, N_PAIRS, PG, SSM_STATE)

    row = pl.BlockSpec((2, 1, 1, LANES), lambda i: (0, i, 0, 0))
    mat = pl.BlockSpec((2, 1, PG, SSM_STATE), lambda i: (0, i, 0, 0))
    out = lambda r: pl.BlockSpec((1, r, SW), lambda i: (i, 0, 0))
    return pl.pallas_call(
        _s5_prep_kernel,
        grid=(N_PAIRS,),
        in_specs=[row, row, row, mat, mat, mat, mat],
        out_specs=[out(PW), out(PW), out(PW), out(SUBLANES)],
        out_shape=[jax.ShapeDtypeStruct((N_PAIRS, PW, PW), BF16),
                   jax.ShapeDtypeStruct((N_PAIRS, PW, SW), BF16),
                   jax.ShapeDtypeStruct((N_PAIRS, PW, SW), BF16),
                   jax.ShapeDtypeStruct((N_PAIRS, SUBLANES, SW), F32)],
        scratch_shapes=[pltpu.VMEM((PW, SW), F32)],
        compiler_params=pltpu.CompilerParams(dimension_semantics=("arbitrary",)),
        name="s5_prep",
    )(lanes(lam_re), lanes(lam_im), lanes(dt_b), tiles(b_re.transpose(0, 1, 3, 2)),
      tiles(b_im.transpose(0, 1, 3, 2)), tiles(c_re), tiles(c_im))


SCAN_STEPS = 8
REGROUP_CHUNKS = 32


def _s5_kernel(u_ref, d_ref, toep_ref, win_ref, wout_ref, a_ref, y_ref, x_s, p_s, st_s):
    nc = u_ref.shape[1]
    bsz = u_ref.shape[2] // CHUNK
    piece = REGROUP_CHUNKS * bsz
    pp = pl.program_id(1)
    slot = lax.broadcasted_iota(jnp.int32, (piece, LANES), 1) // PG
    time_rows = lambda t: slice(t * bsz, (t + 1) * bsz)

    def gather(cb, carry):
        chunks = pl.ds(pl.multiple_of(cb * REGROUP_CHUNKS, REGROUP_CHUNKS), REGROUP_CHUNKS)
        rows = pl.ds(pl.multiple_of(cb * piece, piece), piece)
        for q in range(CHUNK // GPS):
            acc = None
            for k in range(GPS):
                v = u_ref[0, chunks, time_rows(q * GPS + k), :].reshape(piece, LANES)
                acc = v if acc is None else jnp.where(slot == (pp + k) % GPS, v, acc)
            x_s[rows, q * LANES:(q + 1) * LANES] = pltpu.roll(
                acc, ((GPS - pp) % GPS) * PG, 1).astype(BF16)
        return carry

    lax.fori_loop(0, nc // REGROUP_CHUNKS, gather, 0)

    x = x_s[...]
    p_s[...] = _dot(x, win_ref[0])
    a_fr, a_fi, a_br, a_bi = (a_ref[0, :, q * LANES:(q + 1) * LANES] for q in range(4))
    tile = lambda q: slice(q * LANES, (q + 1) * LANES)

    def scan(i, carry):
        base_f = pl.multiple_of(i * (SCAN_STEPS * bsz), SCAN_STEPS * bsz)
        base_b = pl.multiple_of((nc - (i + 1) * SCAN_STEPS) * bsz, SCAN_STEPS * bsz)
        s_fr, s_fi, s_br, s_bi = carry
        for k in range(SCAN_STEPS):
            rf = pl.ds(base_f + k * bsz, bsz)
            rb = pl.ds(base_b + (SCAN_STEPS - 1 - k) * bsz, bsz)
            st_s[rf, tile(0)] = s_fr
            st_s[rf, tile(1)] = s_fi
            st_s[rb, tile(2)] = s_br
            st_s[rb, tile(3)] = s_bi
            s_fr, s_fi, s_br, s_bi = (a_fr * s_fr - a_fi * s_fi + p_s[rf, tile(0)],
                                      a_fr * s_fi + a_fi * s_fr + p_s[rf, tile(1)],
                                      a_br * s_br - a_bi * s_bi + p_s[rb, tile(2)],
                                      a_br * s_bi + a_bi * s_br + p_s[rb, tile(3)])
        return s_fr, s_fi, s_br, s_bi

    zero = jnp.zeros((bsz, LANES), F32)
    lax.fori_loop(0, nc // SCAN_STEPS, scan, (zero,) * 4)
    p_s[...] = _dot(x, toep_ref[0]) + _dot_nt(st_s[...].astype(BF16), wout_ref[0])

    d_tile = jnp.broadcast_to(d_ref[0], (bsz, LANES))
    d_skip = [d_tile if k == 0 else pltpu.roll(d_tile, k * PG, 1) for k in range(GPS)]

    def scatter(cb, carry):
        chunks = pl.ds(pl.multiple_of(cb * REGROUP_CHUNKS, REGROUP_CHUNKS), REGROUP_CHUNKS)
        rows = pl.ds(pl.multiple_of(cb * piece, piece), piece)
        for q in range(CHUNK // GPS):
            w = pltpu.roll(p_s[rows, q * LANES:(q + 1) * LANES], pp * PG, 1)
            w = w.reshape(1, REGROUP_CHUNKS, bsz, LANES)
            for k in range(GPS):
                where = (pl.ds(0, 1), chunks, time_rows(q * GPS + k), slice(None))
                mine = (slot == (pp + k) % GPS).reshape(1, REGROUP_CHUNKS, bsz, LANES)
                pltpu.store(y_ref.at[where], w + d_skip[k] * u_ref[where], mask=mine)
        return carry

    lax.fori_loop(0, nc // REGROUP_CHUNKS, scatter, 0)


def _s5(u_tb, d_tiles, toep, w_in, w_out, a_pow, bsz):
    n_tiles, nc, cb, _ = u_tb.shape
    rows = nc * bsz
    tile = pl.BlockSpec((1, nc, cb, LANES), lambda j, p: (j, 0, 0, 0))
    pair = lambda r, c: pl.BlockSpec((1, r, c), lambda j, p: (j * GPS + p, 0, 0))
    return pl.pallas_call(
        _s5_kernel,
        grid=(n_tiles, GPS),
        in_specs=[tile, pl.BlockSpec((1, 1, LANES), lambda j, p: (j, 0, 0)),
                  pair(PW, PW), pair(PW, SW), pair(PW, SW), pair(bsz, SW)],
        out_specs=tile,
        out_shape=jax.ShapeDtypeStruct(u_tb.shape, F32),
        scratch_shapes=[pltpu.VMEM((rows, PW), BF16),
                        pltpu.VMEM((rows, SW), F32),
                        pltpu.VMEM((rows, SW), F32)],
        compiler_params=pltpu.CompilerParams(
            dimension_semantics=("arbitrary", "arbitrary"), vmem_limit_bytes=VMEM_LIMIT),
        name="s5_chunked",
    )(u_tb, d_tiles, toep, w_in, w_out, a_pow)


TAIL_TT = 512


def _tail_kernel(x_ref, ytb_ref, za_ref, at_ref, zb_ref, g_ref, wg_ref, bg_ref,
                 wps_ref, wpa_ref, wo_ref, o_ref):
    bsz = ytb_ref.shape[1] // TAIL_TT
    batch = pl.program_id(0) % bsz
    y_s5 = jnp.concatenate(
        [_rotate_rows(ytb_ref[lt, pl.ds(batch, TAIL_TT, stride=bsz), :], forward=False)
         for lt in range(SSM_WIDTH // LANES)], axis=1)
    ys = jax.nn.gelu(y_s5).astype(BF16)
    glu = _dot(ys, wg_ref[...]) + bg_ref[...]
    z_b = zb_ref[...].astype(F32)
    y_b = _dot((at_ref[...].astype(F32) * (z_b * _sigmoid(z_b))).astype(BF16), wpa_ref[...])
    z_a = za_ref[...].astype(F32)
    a_in = glu[:, :SSM_WIDTH] * _sigmoid(glu[:, SSM_WIDTH:]) * (z_a * _sigmoid(z_a))
    y_a = _dot(a_in.astype(BF16), wps_ref[...])
    g = g_ref[...].astype(F32)
    mix = _sigmoid(g[:, :D_MODEL]) * y_a + _sigmoid(g[:, D_MODEL:]) * y_b
    o_ref[...] = x_ref[...] + _dot(mix.astype(BF16), wo_ref[...])


def _tail(x2d, y_tb, attn2d, pz2d, w_glu, b_glu, w_ps, w_pa, w_out, bsz, seq):
    t = x2d.shape[0]
    tile = _token_tile(bsz, seq // TAIL_TT)
    row = lambda w, c: pl.BlockSpec((TAIL_TT, w), lambda i, c=c: (tile(i), c))
    const = lambda shape: pl.BlockSpec(shape, lambda i: (0, 0))
    return pl.pallas_call(
        _tail_kernel,
        grid=(t // TAIL_TT,),
        in_specs=[
            row(D_MODEL, 0),
            pl.BlockSpec((SSM_WIDTH // LANES, TAIL_TT * bsz, LANES), lambda i: (0, i // bsz, 0)),
            row(SSM_WIDTH, PZ_ZA // SSM_WIDTH),
            row(ATTN_WIDTH, 0),
            row(ATTN_WIDTH, PZ_ZB // ATTN_WIDTH),
            row(2 * D_MODEL, PZ_G // (2 * D_MODEL)),
            const((SSM_WIDTH, 2 * SSM_WIDTH)), const((1, 2 * SSM_WIDTH)),
            const((SSM_WIDTH, D_MODEL)), const((ATTN_WIDTH, D_MODEL)), const((D_MODEL, D_MODEL)),
        ],
        out_specs=pl.BlockSpec((TAIL_TT, D_MODEL), lambda i: (tile(i), 0)),
        out_shape=jax.ShapeDtypeStruct((t, D_MODEL), F32),
        compiler_params=pltpu.CompilerParams(
            dimension_semantics=("arbitrary",), vmem_limit_bytes=VMEM_LIMIT),
        name="tail",
    )(x2d, y_tb, pz2d, attn2d, pz2d, pz2d, w_glu, b_glu, w_ps, w_pa, w_out)


def _rope_tables(seq):
    half = ROPE_DIM // 2
    inv = ROPE_THETA ** (-np.arange(0, ROPE_DIM, 2, dtype=np.float64) / ROPE_DIM)
    ang = np.arange(seq, dtype=np.float64)[:, None] * inv[None, :]
    cos, sin = np.cos(ang).astype(np.float32), np.sin(ang).astype(np.float32)
    zeros = np.zeros((seq, HEAD_DIM - ROPE_DIM), np.float32)
    z8 = np.zeros((seq, half), np.float32)
    cos_h = np.concatenate([cos, cos, np.ones_like(zeros)], axis=1)
    sa_h = np.concatenate([-sin, z8, zeros], axis=1)
    sb_h = np.concatenate([z8, sin, zeros], axis=1)
    two = lambda t: np.concatenate([t, t], axis=1).astype(np.float32)
    return two(cos_h), two(sa_h), two(sb_h)


def kernel(x, norm_w, w_in, b_gate, q_norm_w, k_norm_w, ssm_lam_re, ssm_lam_im, ssm_log_dt,
           ssm_b_re, ssm_b_im, ssm_c_re, ssm_c_im, ssm_d, w_glu, b_glu,
           w_proj_ssm, w_proj_attn, w_out):
    bsz, seq, d_model = x.shape
    depth = norm_w.shape[0]
    assert d_model == D_MODEL and w_in.shape[-1] == IN_WIDTH
    assert bsz == SUBLANES, "S5 rows (chunk, batch) must fill whole sublane tiles"
    assert seq % IN_TT == 0 and seq % TAIL_TT == 0 and seq % CHUNK == 0
    assert IN_TT % GPS == 0 and TAIL_TT % GPS == 0 and CHUNK % GPS == 0
    assert all(seq % (d * QBLK) == 0 for d in DILATIONS) and (seq // QBLK) % 4 == 0
    cosf, sa, sb = _rope_tables(seq)
    ones_blk = jnp.asarray(np.kron(np.eye(2 * LANES // HEAD_DIM, dtype=np.float32),
                                   np.full((HEAD_DIM, HEAD_DIM), 1.0 / HEAD_DIM, np.float32)), BF16)
    for layer in range(depth):
        x2d = x.reshape(bsz * seq, D_MODEL)
        q_gain = q_norm_w[layer].astype(F32) * (LOG2E * HEAD_DIM ** -0.5)
        qk_w_row = jnp.concatenate([jnp.tile(q_gain, len(DILATIONS) * ATTN_SLOTS),
                                    jnp.tile(k_norm_w[layer].astype(F32), ATTN_SLOTS)])[None, :]
        pf2d, pz2d, u_tb = _in_proj(
            x2d, norm_w[layer][None, :].astype(F32), w_in[layer].astype(BF16),
            b_gate[layer][None, :].astype(F32), cosf, sa, sb, qk_w_row, ones_blk, bsz, seq)
        attn = _attention(pf2d.reshape(bsz, seq, PF_WIDTH))

        toep, s5_in, s5_out, a_pow = _s5_prep(
            ssm_lam_re[layer].astype(F32), ssm_lam_im[layer].astype(F32),
            ssm_log_dt[layer].astype(F32), ssm_b_re[layer].astype(F32),
            ssm_b_im[layer].astype(F32), ssm_c_re[layer].astype(F32),
            ssm_c_im[layer].astype(F32))
        n_tiles = SSM_WIDTH // LANES
        y_tb = _s5(u_tb.reshape(n_tiles, seq // CHUNK, CHUNK * bsz, LANES),
                   ssm_d[layer].astype(F32).reshape(n_tiles, 1, LANES), toep, s5_in, s5_out,
                   a_pow, bsz)

        out2d = _tail(x2d, y_tb.reshape(n_tiles, seq * bsz, LANES),
                      attn.reshape(bsz * seq, ATTN_WIDTH), pz2d,
                      w_glu[layer].astype(BF16), b_glu[layer][None, :].astype(F32),
                      w_proj_ssm[layer].astype(BF16), w_proj_attn[layer].astype(BF16),
                      w_out[layer].astype(BF16), bsz, seq)
        x = out2d.reshape(bsz, seq, D_MODEL)
    return x
```

```python
import math

import jax
import jax.numpy as jnp
import numpy as np
from jax import lax
from jax.experimental import pallas as pl
from jax.experimental.pallas import tpu as pltpu

F32 = jnp.float32
BF16 = jnp.bfloat16

D_MODEL = 1024
SSM_WIDTH = 512
SSM_GROUP = 16
SSM_GROUPS = 32
SSM_STATE = 64
HEAD_DIM = 64
ATTN_SLOTS = 8
ATTN_WIDTH = 512
DILATIONS = (1, 4, 16)
BAND_HALF = 64
ROPE_THETA = 500000.0
ROPE_DIM = 16
EPS = 1e-6
NEG_INF = -1e30
IN_WIDTH = 6144
COL_U, COL_ZA, COL_Q, COL_K, COL_V, COL_ZB, COL_G = 0, 512, 1024, 2560, 3072, 3584, 4096

LANES = 128
SUBLANES = 8
VMEM_LIMIT = 56 * 1024 * 1024

CHUNK = 16
GP = 2
N_PAIRS = SSM_GROUPS // GP
PG = GP * SSM_GROUP
PW = CHUNK * PG
SW = 4 * LANES
GPS = LANES // PG


def _dot(a, b):
    return jnp.dot(a, b, preferred_element_type=F32)


def _dot_nt(a, b):
    return lax.dot_general(a, b, (((1,), (1,)), ((), ())), preferred_element_type=F32)


def _dot_nt_f32(a, b):
    a_hi, b_hi = a.astype(BF16), b.astype(BF16)
    a_lo = (a - a_hi.astype(F32)).astype(BF16)
    b_lo = (b - b_hi.astype(F32)).astype(BF16)
    return _dot_nt(a_hi, b_hi) + (_dot_nt(a_hi, b_lo) + _dot_nt(a_lo, b_hi))


def _rotate_rows(tile, forward):
    residue = lax.broadcasted_iota(jnp.int32, tile.shape, 0) % GPS
    out = tile
    for k in range(1, GPS):
        out = jnp.where(residue == k, pltpu.roll(tile, (k if forward else GPS - k) * PG, 1), out)
    return out


def _sigmoid(v):
    return 0.5 * jnp.tanh(0.5 * v) + 0.5


IN_TT = 512
IN_TN = 512
QK_ROWS = 256
LOG2E = math.log2(math.e)
PF_Q, PF_K, PF_V, PF_WIDTH = 0, 1536, 2048, 2560
PZ_G, PZ_ZA, PZ_ZB, PZ_WIDTH = 0, 2048, 2560, 3072
_Q_COLS = tuple((COL_Q + o, PF_Q + o) for o in range(0, COL_K - COL_Q, IN_TN))
_G_COLS = tuple((COL_G + o, PZ_G + o) for o in range(0, IN_WIDTH - COL_G, IN_TN))
F32_DEST = dict(((COL_K, PF_K), (COL_V, PF_V)) + _Q_COLS)
BF16_DEST = dict(((COL_ZA, PZ_ZA), (COL_ZB, PZ_ZB)) + _G_COLS)


def _in_proj_kernel(x_ref, nw_ref, w_ref, b_ref, cos_ref, sa_ref, sb_ref, qkw_ref, ones_ref,
                    pf_ref, pz_ref, utb_ref):
    bsz = utb_ref.shape[1] // IN_TT
    batch = pl.program_id(0) % bsz
    x = x_ref[...]
    var = jnp.mean(x * x, axis=-1, keepdims=True)
    h = (x * lax.rsqrt(var + EPS) * nw_ref[...]).astype(BF16)
    ones_blk = ones_ref[...]
    for c0 in range(0, IN_WIDTH, IN_TN):
        acc = _dot(h, w_ref[:, c0:c0 + IN_TN])
        if c0 in BF16_DEST:
            if c0 >= COL_G:
                acc = acc + b_ref[:, c0 - COL_G:c0 - COL_G + IN_TN]
            pz_ref[:, BF16_DEST[c0]:BF16_DEST[c0] + IN_TN] = acc.astype(BF16)
            continue
        if c0 == COL_U:
            for lt in range(SSM_WIDTH // LANES):
                utb_ref[lt, pl.ds(batch, IN_TT, stride=bsz), :] = _rotate_rows(
                    acc[:, lt * LANES:(lt + 1) * LANES], forward=True)
            continue
        d0 = F32_DEST[c0]
        if not COL_Q <= c0 < COL_V:
            pf_ref[:, d0:d0 + IN_TN] = acc
            continue
        for r0 in range(0, IN_TT, QK_ROWS):
            rows = slice(r0, r0 + QK_ROWS)
            for t2 in range(0, IN_TN, 2 * LANES):
                xq2 = acc[rows, t2:t2 + 2 * LANES]
                ms2 = _dot((xq2 * xq2).astype(BF16), ones_blk)
                xn2 = xq2 * lax.rsqrt(ms2 + EPS) * qkw_ref[:, c0 - COL_Q + t2:
                                                            c0 - COL_Q + t2 + 2 * LANES]
                for t in (0, LANES):
                    xn = xn2[:, t:t + LANES]
                    pf_ref[rows, d0 + t2 + t:d0 + t2 + t + LANES] = (
                        xn * cos_ref[rows, :]
                        + pltpu.roll(xn, LANES - ROPE_DIM // 2, 1) * sa_ref[rows, :]
                        + pltpu.roll(xn, ROPE_DIM // 2, 1) * sb_ref[rows, :])


def _token_tile(bsz, tiles_per_seq):
    return lambda i: (i % bsz) * tiles_per_seq + i // bsz


def _in_proj(x2d, norm_w, w_in_bf16, b_gate_row, cosf, sa, sb, qk_w_row, ones_blk, bsz, seq):
    t = x2d.shape[0]
    tile = _token_tile(bsz, seq // IN_TT)
    const = lambda shape: pl.BlockSpec(shape, lambda i: (0, 0))
    rope = lambda: pl.BlockSpec((IN_TT, LANES), lambda i: (i // bsz, 0))
    return pl.pallas_call(
        _in_proj_kernel,
        grid=(t // IN_TT,),
        in_specs=[
            pl.BlockSpec((IN_TT, D_MODEL), lambda i: (tile(i), 0)),
            const((1, D_MODEL)),
            pl.BlockSpec((D_MODEL, IN_WIDTH), lambda i: (0, 0), pipeline_mode=pl.Buffered(1)),
            const((1, IN_WIDTH - COL_G)),
            rope(), rope(), rope(),
            const((1, COL_V - COL_Q)),
            const((2 * LANES, 2 * LANES)),
        ],
        out_specs=[
            pl.BlockSpec((IN_TT, PF_WIDTH), lambda i: (tile(i), 0)),
            pl.BlockSpec((IN_TT, PZ_WIDTH), lambda i: (tile(i), 0)),
            pl.BlockSpec((SSM_WIDTH // LANES, IN_TT * bsz, LANES), lambda i: (0, i // bsz, 0)),
        ],
        out_shape=[
            jax.ShapeDtypeStruct((t, PF_WIDTH), F32),
            jax.ShapeDtypeStruct((t, PZ_WIDTH), BF16),
            jax.ShapeDtypeStruct((SSM_WIDTH // LANES, t, LANES), F32),
        ],
        compiler_params=pltpu.CompilerParams(
            dimension_semantics=("arbitrary",), vmem_limit_bytes=VMEM_LIMIT),
        name="in_proj",
    )(x2d, norm_w, w_in_bf16, b_gate_row, cosf, sa, sb, qk_w_row, ones_blk)


QBLK = 128
KWIN = QBLK + 2 * BAND_HALF
NORM_ROWS = 256
MERGE_ROWS = 64
LOOKAHEAD = 3
SCORE_SLOTS = LOOKAHEAD + 2


def _attn_kernel(q0_ref, q1_ref, q2_ref, k_ref, v_ref, bias_ref, o_ref,
                 va_s, vb_s, oacc_s, den_s, max_s, s_s):
    seq = k_ref.shape[1]
    q_refs = (q0_ref, q1_ref, q2_ref)
    head0 = lax.broadcasted_iota(jnp.int32, (QBLK, LANES), 1) < HEAD_DIM
    head0_rows = lax.broadcasted_iota(jnp.int32, (NORM_ROWS, LANES), 1) < HEAD_DIM

    def prep(i, carry):
        rows = pl.ds(pl.multiple_of(i * NORM_ROWS, NORM_ROWS), NORM_ROWS)
        v = v_ref[0, rows, :]
        va_s[rows, :] = jnp.where(head0_rows, v, 1.0)
        vb_s[rows, :] = jnp.where(head0_rows, 1.0, v)
        return carry

    lax.fori_loop(0, seq // NORM_ROWS, prep, 0, unroll=True)

    blocks = []
    for p, d in enumerate(DILATIONS):
        n = seq // d
        nblk = n // QBLK
        kw = min(n, KWIN)

        def rows_of(idx, d=d, n=n, nblk=nblk, kw=kw):
            r, q0 = idx // nblk, (idx % nblk) * QBLK
            ks = min(max(q0 - BAND_HALF, 0), n - kw)
            return (pl.ds(r + d * q0, QBLK, stride=d), pl.ds(r + d * ks, kw, stride=d),
                    (q0 - ks) // BAND_HALF)

        def scores(idx, slot, p=p, kw=kw, rows_of=rows_of):
            qrows, krows, case = rows_of(idx)
            qb = q_refs[p][0, qrows, :]
            kb = k_ref[0, krows, :].astype(BF16)
            bias = bias_ref[case, :, :kw]
            s_s[slot, :QBLK, :kw] = _dot_nt(jnp.where(head0, qb, 0.0).astype(BF16), kb) + bias
            s_s[slot, QBLK:, :kw] = _dot_nt(jnp.where(head0, 0.0, qb).astype(BF16), kb) + bias

        def weigh(idx, slot, p=p, kw=kw, rows_of=rows_of):
            qrows, krows, _ = rows_of(idx)
            s = s_s[slot, :, :kw]
            m = jnp.max(s, axis=-1, keepdims=True)
            e = jnp.exp2(s - m).astype(BF16)
            o_a = _dot(e[:QBLK], va_s[krows, :].astype(BF16))
            o_b = _dot(e[QBLK:], vb_s[krows, :].astype(BF16))
            oacc_s[p, qrows, :] = jnp.where(head0, o_a, o_b)
            den_s[p, qrows, :] = jnp.where(head0, o_b, o_a)
            max_s[p, qrows, :] = jnp.where(head0, jnp.broadcast_to(m[:QBLK], (QBLK, LANES)),
                                           jnp.broadcast_to(m[QBLK:], (QBLK, LANES)))

        blocks += [(scores, weigh, i) for i in range(d * nblk)]

    for g in range(LOOKAHEAD):
        blocks[g][0](blocks[g][2], g % SCORE_SLOTS)
    for g, (_, weigh, i) in enumerate(blocks):
        if g + LOOKAHEAD < len(blocks):
            ahead_scores, _, ahead_i = blocks[g + LOOKAHEAD]
            ahead_scores(ahead_i, (g + LOOKAHEAD) % SCORE_SLOTS)
        weigh(i, g % SCORE_SLOTS)

    def combine(i, carry):
        rows = pl.ds(pl.multiple_of(i * MERGE_ROWS, MERGE_ROWS), MERGE_ROWS)
        m0, m1, m2 = max_s[0, rows, :], max_s[1, rows, :], max_s[2, rows, :]
        m = jnp.maximum(jnp.maximum(m0, m1), m2)
        w0, w1, w2 = jnp.exp2(m0 - m), jnp.exp2(m1 - m), jnp.exp2(m2 - m)
        num = w0 * oacc_s[0, rows, :] + w1 * oacc_s[1, rows, :] + w2 * oacc_s[2, rows, :]
        d0, d1, d2 = (pltpu.roll(den_s[p, rows, :], HEAD_DIM, 1) for p in range(3))
        o_ref[0, rows, :] = (num / (w0 * d0 + w1 * d1 + w2 * d2)).astype(o_ref.dtype)
        return carry

    lax.fori_loop(0, seq // MERGE_ROWS, combine, 0, unroll=True)


def _band_bias():
    i = np.arange(QBLK)[:, None]
    j = np.arange(KWIN)[None, :]
    off = np.arange(3)[:, None, None] * BAND_HALF
    return np.where(np.abs(j - i - off) <= BAND_HALF, 0.0, NEG_INF).astype(np.float32)


def _attention(proj3d):
    bsz, seq, _ = proj3d.shape
    n_pairs = ATTN_WIDTH // LANES

    def qspec(p):
        return pl.BlockSpec((1, seq, LANES),
                            lambda b, hp, p=p: (b, 0, PF_Q // LANES + p * n_pairs + hp))

    seq_tile = lambda: pltpu.VMEM((seq, LANES), F32)
    pat_tile = lambda: pltpu.VMEM((len(DILATIONS), seq, LANES), F32)
    return pl.pallas_call(
        _attn_kernel,
        grid=(bsz, n_pairs),
        in_specs=[
            qspec(0), qspec(1), qspec(2),
            pl.BlockSpec((1, seq, LANES), lambda b, hp: (b, 0, PF_K // LANES + hp)),
            pl.BlockSpec((1, seq, LANES), lambda b, hp: (b, 0, PF_V // LANES + hp)),
            pl.BlockSpec((3, QBLK, KWIN), lambda b, hp: (0, 0, 0)),
        ],
        out_specs=pl.BlockSpec((1, seq, LANES), lambda b, hp: (b, 0, hp)),
        out_shape=jax.ShapeDtypeStruct((bsz, seq, ATTN_WIDTH), BF16),
        scratch_shapes=[
            seq_tile(), seq_tile(),
            pat_tile(), pat_tile(), pat_tile(),
            pltpu.VMEM((SCORE_SLOTS, 2 * QBLK, KWIN), F32),
        ],
        compiler_params=pltpu.CompilerParams(
            dimension_semantics=("arbitrary", "arbitrary"), vmem_limit_bytes=VMEM_LIMIT),
        name="dilated_attention",
    )(proj3d, proj3d, proj3d, proj3d, proj3d, _band_bias())


LAG_REP = LANES // PG


def _s5_prep_kernel(lre_ref, lim_ref, ldt_ref, bre_ref, bim_ref, cre_ref, cim_ref,
                    toep_ref, win_ref, wout_ref, a_ref, wf32_s):
    four = lambda ref: jnp.concatenate([ref[0, 0], ref[0, 0], ref[1, 0], ref[1, 0]], axis=1)
    lr = jnp.minimum(four(lre_ref), -1e-4)
    li = four(lim_ref)
    dt = jnp.exp(four(ldt_ref))
    col = lax.broadcasted_iota(jnp.int32, (1, SW), 1)
    is_re = (col // LANES) % 2 == 0
    is_fwd = col < 2 * LANES

    mag = jnp.exp(lr * dt)
    pw_re = [jnp.ones_like(mag), mag * jnp.cos(li * dt)]
    pw_im = [jnp.zeros_like(mag), mag * jnp.sin(li * dt)]
    for _ in range(CHUNK - 1):
        pw_re.append(pw_re[-1] * pw_re[1] - pw_im[-1] * pw_im[1])
        pw_im.append(pw_re[-2] * pw_im[1] + pw_im[-1] * pw_re[1])

    a_re, a_im = pw_re[1], pw_im[1]
    nr, ni, mag2 = a_re - 1.0, a_im, lr * lr + li * li
    f_re, f_im = (nr * lr + ni * li) / mag2, (ni * lr - nr * li) / mag2

    row_g = lax.broadcasted_iota(jnp.int32, (PG, SW), 0) // SSM_GROUP
    col_g = (lax.broadcasted_iota(jnp.int32, (PG, SW), 1) % LANES) // SSM_STATE
    diag = row_g == col_g
    wide = lambda ref: jnp.concatenate([ref[0, 0]] * (2 * GP) + [ref[1, 0]] * (2 * GP), axis=1)
    b_re = jnp.where(diag, wide(bre_ref), 0.0)
    b_im = jnp.where(diag, wide(bim_ref), 0.0)
    c_re = jnp.where(diag, wide(cre_ref), 0.0)
    c_im = jnp.where(diag, wide(cim_ref), 0.0)
    bb_re = f_re * b_re - f_im * b_im
    bb_im = f_re * b_im + f_im * b_re

    for s in range(CHUNK):
        rows = slice(s * PG, (s + 1) * PG)
        in_re = jnp.where(is_fwd, pw_re[CHUNK - 1 - s], pw_re[s])
        in_im = jnp.where(is_fwd, pw_im[CHUNK - 1 - s], pw_im[s])
        w = (bb_re * jnp.where(is_re, in_re, in_im)
             + bb_im * jnp.where(is_re, -in_im, in_re))
        wf32_s[rows, :] = w
        win_ref[0, rows, :] = w.astype(BF16)
        out_re = jnp.where(is_fwd, pw_re[s + 1], pw_re[CHUNK - s])
        out_im = jnp.where(is_fwd, pw_im[s + 1], pw_im[CHUNK - s])
        wout_ref[0, rows, :] = (c_re * jnp.where(is_re, out_re, -out_im)
                                + c_im * jnp.where(is_re, -out_im, -out_re)).astype(BF16)

    a_ref[0] = jnp.broadcast_to(jnp.where(is_re, pw_re[CHUNK], pw_im[CHUNK]), a_ref.shape[1:])

    c_cat = jnp.where(is_re, c_re, -c_im)
    c_rep = jnp.concatenate([c_cat] * LAG_REP, axis=0)
    half = 2 * LANES
    lag_f = _dot_nt_f32(wf32_s[:, :half], c_rep[:, :half])
    lag_b = _dot_nt_f32(wf32_s[:, half:], c_rep[:, half:])
    row_t = lax.broadcasted_iota(jnp.int32, (PW, LANES), 0) // PG
    lane_q = lax.broadcasted_iota(jnp.int32, (PW, LANES), 1) // PG
    zeros = lambda n: jnp.zeros((n * PG, LANES), F32)
    for q in range(PW // LANES):
        tile = jnp.zeros((PW, LANES), F32)
        for u in range(LAG_REP):
            t = q * LAG_REP + u
            up = CHUNK - 1 - t
            sh_f = lag_f if up == 0 else jnp.concatenate([lag_f[up * PG:], zeros(up)], axis=0)
            sh_b = lag_b if t == 0 else jnp.concatenate([zeros(t), lag_b[:(CHUNK - t) * PG]], axis=0)
            col_t = jnp.where(row_t <= t, sh_f, 0.0) + jnp.where(row_t >= t, sh_b, 0.0)
            tile = jnp.where(lane_q == u, col_t, tile)
        toep_ref[0, :, q * LANES:(q + 1) * LANES] = tile.astype(BF16)


def _s5_prep(lam_re, lam_im, log_dt, b_re, b_im, c_re, c_im):
    lanes = lambda t: t.reshape(2, N_PAIRS, 1, LANES)
    dt_b = jnp.broadcast_to(log_dt[..., None], lam_re.shape)

    tiles = lambda t_ghp: t_ghp.reshape(2, N_PAIRS, PG, SSM_STATE)

    row = pl.BlockSpec((2, 1, 1, LANES), lambda i: (0, i, 0, 0))
    mat = pl.BlockSpec((2, 1, PG, SSM_STATE), lambda i: (0, i, 0, 0))
    out = lambda r: pl.BlockSpec((1, r, SW), lambda i: (i, 0, 0))
    return pl.pallas_call(
        _s5_prep_kernel,
        grid=(N_PAIRS,),
        in_specs=[row, row, row, mat, mat, mat, mat],
        out_specs=[out(PW), out(PW), out(PW), out(SUBLANES)],
        out_shape=[jax.ShapeDtypeStruct((N_PAIRS, PW, PW), BF16),
                   jax.ShapeDtypeStruct((N_PAIRS, PW, SW), BF16),
                   jax.ShapeDtypeStruct((N_PAIRS, PW, SW), BF16),
                   jax.ShapeDtypeStruct((N_PAIRS, SUBLANES, SW), F32)],
        scratch_shapes=[pltpu.VMEM((PW, SW), F32)],
        compiler_params=pltpu.CompilerParams(dimension_semantics=("arbitrary",)),
        name="s5_prep",
    )(lanes(lam_re), lanes(lam_im), lanes(dt_b), tiles(b_re.transpose(0, 1, 3, 2)),
      tiles(b_im.transpose(0, 1, 3, 2)), tiles(c_re), tiles(c_im))


SCAN_STEPS = 8
REGROUP_CHUNKS = 32


def _s5_kernel(u_ref, d_ref, toep_ref, win_ref, wout_ref, a_ref, y_ref, x_s, p_s, st_s):
    nc = u_ref.shape[1]
    bsz = u_ref.shape[2] // CHUNK
    piece = REGROUP_CHUNKS * bsz
    pp = pl.program_id(1)
    slot = lax.broadcasted_iota(jnp.int32, (piece, LANES), 1) // PG
    time_rows = lambda t: slice(t * bsz, (t + 1) * bsz)

    def gather(cb, carry):
        chunks = pl.ds(pl.multiple_of(cb * REGROUP_CHUNKS, REGROUP_CHUNKS), REGROUP_CHUNKS)
        rows = pl.ds(pl.multiple_of(cb * piece, piece), piece)
        for q in range(CHUNK // GPS):
            acc = None
            for k in range(GPS):
                v = u_ref[0, chunks, time_rows(q * GPS + k), :].reshape(piece, LANES)
                acc = v if acc is None else jnp.where(slot == (pp + k) % GPS, v, acc)
            x_s[rows, q * LANES:(q + 1) * LANES] = pltpu.roll(
                acc, ((GPS - pp) % GPS) * PG, 1).astype(BF16)
        return carry

    lax.fori_loop(0, nc // REGROUP_CHUNKS, gather, 0)

    x = x_s[...]
    p_s[...] = _dot(x, win_ref[0])
    a_fr, a_fi, a_br, a_bi = (a_ref[0, :, q * LANES:(q + 1) * LANES] for q in range(4))
    tile = lambda q: slice(q * LANES, (q + 1) * LANES)

    def scan(i, carry):
        base_f = pl.multiple_of(i * (SCAN_STEPS * bsz), SCAN_STEPS * bsz)
        base_b = pl.multiple_of((nc - (i + 1) * SCAN_STEPS) * bsz, SCAN_STEPS * bsz)
        s_fr, s_fi, s_br, s_bi = carry
        for k in range(SCAN_STEPS):
            rf = pl.ds(base_f + k * bsz, bsz)
            rb = pl.ds(base_b + (SCAN_STEPS - 1 - k) * bsz, bsz)
            st_s[rf, tile(0)] = s_fr
            st_s[rf, tile(1)] = s_fi
            st_s[rb, tile(2)] = s_br
            st_s[rb, tile(3)] = s_bi
            s_fr, s_fi, s_br, s_bi = (a_fr * s_fr - a_fi * s_fi + p_s[rf, tile(0)],
                                      a_fr * s_fi + a_fi * s_fr + p_s[rf, tile(1)],
                                      a_br * s_br - a_bi * s_bi + p_s[rb, tile(2)],
                                      a_br * s_bi + a_bi * s_br + p_s[rb, tile(3)])
        return s_fr, s_fi, s_br, s_bi

    zero = jnp.zeros((bsz, LANES), F32)
    lax.fori_loop(0, nc // SCAN_STEPS, scan, (zero,) * 4)
    p_s[...] = _dot(x, toep_ref[0]) + _dot_nt(st_s[...].astype(BF16), wout_ref[0])

    d_tile = jnp.broadcast_to(d_ref[0], (bsz, LANES))
    d_skip = [d_tile if k == 0 else pltpu.roll(d_tile, k * PG, 1) for k in range(GPS)]

    def scatter(cb, carry):
        chunks = pl.ds(pl.multiple_of(cb * REGROUP_CHUNKS, REGROUP_CHUNKS), REGROUP_CHUNKS)
        rows = pl.ds(pl.multiple_of(cb * piece, piece), piece)
        for q in range(CHUNK // GPS):
            w = pltpu.roll(p_s[rows, q * LANES:(q + 1) * LANES], pp * PG, 1)
            w = w.reshape(1, REGROUP_CHUNKS, bsz, LANES)
            for k in range(GPS):
                where = (pl.ds(0, 1), chunks, time_rows(q * GPS + k), slice(None))
                mine = (slot == (pp + k) % GPS).reshape(1, REGROUP_CHUNKS, bsz, LANES)
                pltpu.store(y_ref.at[where], w + d_skip[k] * u_ref[where], mask=mine)
        return carry

    lax.fori_loop(0, nc // REGROUP_CHUNKS, scatter, 0)


def _s5(u_tb, d_tiles, toep, w_in, w_out, a_pow, bsz):
    n_tiles, nc, cb, _ = u_tb.shape
    rows = nc * bsz
    tile = pl.BlockSpec((1, nc, cb, LANES), lambda j, p: (j, 0, 0, 0))
    pair = lambda r, c: pl.BlockSpec((1, r, c), lambda j, p: (j * GPS + p, 0, 0))
    return pl.pallas_call(
        _s5_kernel,
        grid=(n_tiles, GPS),
        in_specs=[tile, pl.BlockSpec((1, 1, LANES), lambda j, p: (j, 0, 0)),
                  pair(PW, PW), pair(PW, SW), pair(PW, SW), pair(bsz, SW)],
        out_specs=tile,
        out_shape=jax.ShapeDtypeStruct(u_tb.shape, F32),
        scratch_shapes=[pltpu.VMEM((rows, PW), BF16),
                        pltpu.VMEM((rows, SW), F32),
                        pltpu.VMEM((rows, SW), F32)],
        compiler_params=pltpu.CompilerParams(
            dimension_semantics=("arbitrary", "arbitrary"), vmem_limit_bytes=VMEM_LIMIT),
        name="s5_chunked",
    )(u_tb, d_tiles, toep, w_in, w_out, a_pow)


TAIL_TT = 512


def _tail_kernel(x_ref, ytb_ref, za_ref, at_ref, zb_ref, g_ref, wg_ref, bg_ref,
                 wps_ref, wpa_ref, wo_ref, o_ref):
    bsz = ytb_ref.shape[1] // TAIL_TT
    batch = pl.program_id(0) % bsz
    y_s5 = jnp.concatenate(
        [_rotate_rows(ytb_ref[lt, pl.ds(batch, TAIL_TT, stride=bsz), :], forward=False)
         for lt in range(SSM_WIDTH // LANES)], axis=1)
    ys = jax.nn.gelu(y_s5).astype(BF16)
    glu = _dot(ys, wg_ref[...]) + bg_ref[...]
    z_b = zb_ref[...].astype(F32)
    y_b = _dot((at_ref[...].astype(F32) * (z_b * _sigmoid(z_b))).astype(BF16), wpa_ref[...])
    z_a = za_ref[...].astype(F32)
    a_in = glu[:, :SSM_WIDTH] * _sigmoid(glu[:, SSM_WIDTH:]) * (z_a * _sigmoid(z_a))
    y_a = _dot(a_in.astype(BF16), wps_ref[...])
    g = g_ref[...].astype(F32)
    mix = _sigmoid(g[:, :D_MODEL]) * y_a + _sigmoid(g[:, D_MODEL:]) * y_b
    o_ref[...] = x_ref[...] + _dot(mix.astype(BF16), wo_ref[...])


def _tail(x2d, y_tb, attn2d, pz2d, w_glu, b_glu, w_ps, w_pa, w_out, bsz, seq):
    t = x2d.shape[0]
    tile = _token_tile(bsz, seq // TAIL_TT)
    row = lambda w, c: pl.BlockSpec((TAIL_TT, w), lambda i, c=c: (tile(i), c))
    const = lambda shape: pl.BlockSpec(shape, lambda i: (0, 0))
    return pl.pallas_call(
        _tail_kernel,
        grid=(t // TAIL_TT,),
        in_specs=[
            row(D_MODEL, 0),
            pl.BlockSpec((SSM_WIDTH // LANES, TAIL_TT * bsz, LANES), lambda i: (0, i // bsz, 0)),
            row(SSM_WIDTH, PZ_ZA // SSM_WIDTH),
            row(ATTN_WIDTH, 0),
            row(ATTN_WIDTH, PZ_ZB // ATTN_WIDTH),
            row(2 * D_MODEL, PZ_G // (2 * D_MODEL)),
            const((SSM_WIDTH, 2 * SSM_WIDTH)), const((1, 2 * SSM_WIDTH)),
            const((SSM_WIDTH, D_MODEL)), const((ATTN_WIDTH, D_MODEL)), const((D_MODEL, D_MODEL)),
        ],
        out_specs=pl.BlockSpec((TAIL_TT, D_MODEL), lambda i: (tile(i), 0)),
        out_shape=jax.ShapeDtypeStruct((t, D_MODEL), F32),
        compiler_params=pltpu.CompilerParams(
            dimension_semantics=("arbitrary",), vmem_limit_bytes=VMEM_LIMIT),
        name="tail",
    )(x2d, y_tb, pz2d, attn2d, pz2d, pz2d, w_glu, b_glu, w_ps, w_pa, w_out)


def _rope_tables(seq):
    half = ROPE_DIM // 2
    inv = ROPE_THETA ** (-np.arange(0, ROPE_DIM, 2, dtype=np.float64) / ROPE_DIM)
    ang = np.arange(seq, dtype=np.float64)[:, None] * inv[None, :]
    cos, sin = np.cos(ang).astype(np.float32), np.sin(ang).astype(np.float32)
    zeros = np.zeros((seq, HEAD_DIM - ROPE_DIM), np.float32)
    z8 = np.zeros((seq, half), np.float32)
    cos_h = np.concatenate([cos, cos, np.ones_like(zeros)], axis=1)
    sa_h = np.concatenate([-sin, z8, zeros], axis=1)
    sb_h = np.concatenate([z8, sin, zeros], axis=1)
    two = lambda t: np.concatenate([t, t], axis=1).astype(np.float32)
    return two(cos_h), two(sa_h), two(sb_h)


def kernel(x, norm_w, w_in, b_gate, q_norm_w, k_norm_w, ssm_lam_re, ssm_lam_im, ssm_log_dt,
           ssm_b_re, ssm_b_im, ssm_c_re, ssm_c_im, ssm_d, w_glu, b_glu,
           w_proj_ssm, w_proj_attn, w_out):
    bsz, seq, d_model = x.shape
    depth = norm_w.shape[0]
    assert d_model == D_MODEL and w_in.shape[-1] == IN_WIDTH
    assert bsz == SUBLANES, "S5 rows (chunk, batch) must fill whole sublane tiles"
    assert seq % IN_TT == 0 and seq % TAIL_TT == 0 and seq % CHUNK == 0
    assert IN_TT % GPS == 0 and TAIL_TT % GPS == 0 and CHUNK % GPS == 0
    assert all(seq % (d * QBLK) == 0 for d in DILATIONS) and (seq // QBLK) % 4 == 0
    cosf, sa, sb = _rope_tables(seq)
    ones_blk = jnp.asarray(np.kron(np.eye(2 * LANES // HEAD_DIM, dtype=np.float32),
                                   np.full((HEAD_DIM, HEAD_DIM), 1.0 / HEAD_DIM, np.float32)), BF16)
    for layer in range(depth):
        x2d = x.reshape(bsz * seq, D_MODEL)
        q_gain = q_norm_w[layer].astype(F32) * (LOG2E * HEAD_DIM ** -0.5)
        qk_w_row = jnp.concatenate([jnp.tile(q_gain, len(DILATIONS) * ATTN_SLOTS),
                                    jnp.tile(k_norm_w[layer].astype(F32), ATTN_SLOTS)])[None, :]
        pf2d, pz2d, u_tb = _in_proj(
            x2d, norm_w[layer][None, :].astype(F32), w_in[layer].astype(BF16),
            b_gate[layer][None, :].astype(F32), cosf, sa, sb, qk_w_row, ones_blk, bsz, seq)
        attn = _attention(pf2d.reshape(bsz, seq, PF_WIDTH))

        toep, s5_in, s5_out, a_pow = _s5_prep(
            ssm_lam_re[layer].astype(F32), ssm_lam_im[layer].astype(F32),
            ssm_log_dt[layer].astype(F32), ssm_b_re[layer].astype(F32),
            ssm_b_im[layer].astype(F32), ssm_c_re[layer].astype(F32),
            ssm_c_im[layer].astype(F32))
        n_tiles = SSM_WIDTH // LANES
        y_tb = _s5(u_tb.reshape(n_tiles, seq // CHUNK, CHUNK * bsz, LANES),
                   ssm_d[layer].astype(F32).reshape(n_tiles, 1, LANES), toep, s5_in, s5_out,
                   a_pow, bsz)

        out2d = _tail(x2d, y_tb.reshape(n_tiles, seq * bsz, LANES),
                      attn.reshape(bsz * seq, ATTN_WIDTH), pz2d,
                      w_glu[layer].astype(BF16), b_glu[layer][None, :].astype(F32),
                      w_proj_ssm[layer].astype(BF16), w_proj_attn[layer].astype(BF16),
                      w_out[layer].astype(BF16), bsz, seq)
        x = out2d.reshape(bsz, seq, D_MODEL)
    return x
```

```python
import functools
import math

import jax
import jax.numpy as jnp
import numpy as np
from jax import lax
from jax.experimental import pallas as pl
from jax.experimental.pallas import tpu as pltpu

F32 = jnp.float32
BF16 = jnp.bfloat16

D_MODEL = 1024
SSM_WIDTH = 512
SSM_GROUP = 16
SSM_GROUPS = 32
SSM_STATE = 64
HEAD_DIM = 64
ATTN_SLOTS = 8
ATTN_WIDTH = 512
DILATIONS = (1, 4, 16)
BAND_HALF = 64
ROPE_THETA = 500000.0
ROPE_DIM = 16
EPS = 1e-6
NEG_INF = -1e30
IN_WIDTH = 6144
COL_U, COL_ZA, COL_Q, COL_K, COL_V, COL_ZB, COL_G = 0, 512, 1024, 2560, 3072, 3584, 4096

LANES = 128
SUBLANES = 8
VMEM_LIMIT = 56 * 1024 * 1024

CHUNK = 16
GP = 2
N_PAIRS = SSM_GROUPS // GP
PG = GP * SSM_GROUP
PW = CHUNK * PG
SW = 4 * LANES
GPS = LANES // PG


def _dot(a, b):
    return jnp.dot(a, b, preferred_element_type=F32)


def _dot_nt(a, b):
    return lax.dot_general(a, b, (((1,), (1,)), ((), ())), preferred_element_type=F32)


def _dot_nt_f32(a, b):
    a_hi, b_hi = a.astype(BF16), b.astype(BF16)
    a_lo = (a - a_hi.astype(F32)).astype(BF16)
    b_lo = (b - b_hi.astype(F32)).astype(BF16)
    return _dot_nt(a_hi, b_hi) + (_dot_nt(a_hi, b_lo) + _dot_nt(a_lo, b_hi))


def _rotate_rows(tile, forward):
    residue = lax.broadcasted_iota(jnp.int32, tile.shape, 0) % GPS
    out = tile
    for k in range(1, GPS):
        out = jnp.where(residue == k, pltpu.roll(tile, (k if forward else GPS - k) * PG, 1), out)
    return out


def _sigmoid(v):
    return 0.5 * jnp.tanh(0.5 * v) + 0.5


IN_TT = 512
IN_TN = 512
QK_ROWS = 256
LOG2E = math.log2(math.e)
PF_Q, PF_K, PF_V, PF_WIDTH = 0, 1536, 2048, 2560
PZ_G, PZ_ZA, PZ_ZB, PZ_WIDTH = 0, 2048, 2560, 3072
_Q_COLS = tuple((COL_Q + o, PF_Q + o) for o in range(0, COL_K - COL_Q, IN_TN))
_G_COLS = tuple((COL_G + o, PZ_G + o) for o in range(0, IN_WIDTH - COL_G, IN_TN))
F32_DEST = dict(((COL_K, PF_K), (COL_V, PF_V)) + _Q_COLS)
BF16_DEST = dict(((COL_ZA, PZ_ZA), (COL_ZB, PZ_ZB)) + _G_COLS)


W_STAGE_COLS = 256
W_STAGE_SLOTS = 3


def _in_proj_kernel(layer, x_ref, nw_ref, w_hbm, b_ref, cos_ref, sa_ref, sb_ref, qkw_ref,
                    ones_ref, pf_ref, pz_ref, utb_ref, w_ref, stage_ref, sem_ref):
    bsz = utb_ref.shape[1] // IN_TT
    batch = pl.program_id(0) % bsz

    def w_copy(j):
        return pltpu.make_async_copy(
            w_hbm.at[layer, :, pl.ds(j * W_STAGE_COLS, W_STAGE_COLS)],
            stage_ref.at[j % W_STAGE_SLOTS], sem_ref.at[j % W_STAGE_SLOTS])

    @pl.when(pl.program_id(0) == 0)
    def _():
        n_copies = IN_WIDTH // W_STAGE_COLS
        for j in range(W_STAGE_SLOTS):
            w_copy(j).start()
        for j in range(n_copies):
            w_copy(j).wait()
            w_ref[:, j * W_STAGE_COLS:(j + 1) * W_STAGE_COLS] = (
                stage_ref[j % W_STAGE_SLOTS].astype(BF16))
            if j + W_STAGE_SLOTS < n_copies:
                w_copy(j + W_STAGE_SLOTS).start()

    x = x_ref[...]
    var = jnp.mean(x * x, axis=-1, keepdims=True)
    h = (x * lax.rsqrt(var + EPS) * nw_ref[...]).astype(BF16)
    ones_blk = ones_ref[...]
    for c0 in range(0, IN_WIDTH, IN_TN):
        acc = _dot(h, w_ref[:, c0:c0 + IN_TN])
        if c0 in BF16_DEST:
            if c0 >= COL_G:
                acc = acc + b_ref[:, c0 - COL_G:c0 - COL_G + IN_TN]
            pz_ref[:, BF16_DEST[c0]:BF16_DEST[c0] + IN_TN] = acc.astype(BF16)
            continue
        if c0 == COL_U:
            for lt in range(SSM_WIDTH // LANES):
                utb_ref[lt, pl.ds(batch, IN_TT, stride=bsz), :] = _rotate_rows(
                    acc[:, lt * LANES:(lt + 1) * LANES], forward=True)
            continue
        d0 = F32_DEST[c0]
        if not COL_Q <= c0 < COL_V:
            pf_ref[:, d0:d0 + IN_TN] = acc
            continue
        for r0 in range(0, IN_TT, QK_ROWS):
            rows = slice(r0, r0 + QK_ROWS)
            for t2 in range(0, IN_TN, 2 * LANES):
                xq2 = acc[rows, t2:t2 + 2 * LANES]
                ms2 = _dot((xq2 * xq2).astype(BF16), ones_blk)
                xn2 = xq2 * lax.rsqrt(ms2 + EPS) * qkw_ref[:, c0 - COL_Q + t2:
                                                            c0 - COL_Q + t2 + 2 * LANES]
                for t in (0, LANES):
                    xn = xn2[:, t:t + LANES]
                    pf_ref[rows, d0 + t2 + t:d0 + t2 + t + LANES] = (
                        xn * cos_ref[rows, :]
                        + pltpu.roll(xn, LANES - ROPE_DIM // 2, 1) * sa_ref[rows, :]
                        + pltpu.roll(xn, ROPE_DIM // 2, 1) * sb_ref[rows, :])


def _token_tile(bsz, tiles_per_seq):
    return lambda i: (i % bsz) * tiles_per_seq + i // bsz


def _in_proj(x2d, norm_w, w_in, layer, b_gate_row, cosf, sa, sb, qk_w_row, ones_blk, bsz, seq):
    t = x2d.shape[0]
    tile = _token_tile(bsz, seq // IN_TT)
    const = lambda shape: pl.BlockSpec(shape, lambda i: (0, 0))
    rope = lambda: pl.BlockSpec((IN_TT, LANES), lambda i: (i // bsz, 0))
    return pl.pallas_call(
        functools.partial(_in_proj_kernel, layer),
        grid=(t // IN_TT,),
        in_specs=[
            pl.BlockSpec((IN_TT, D_MODEL), lambda i: (tile(i), 0)),
            const((1, D_MODEL)),
            pl.BlockSpec(memory_space=pl.ANY),
            const((1, IN_WIDTH - COL_G)),
            rope(), rope(), rope(),
            const((1, COL_V - COL_Q)),
            const((2 * LANES, 2 * LANES)),
        ],
        out_specs=[
            pl.BlockSpec((IN_TT, PF_WIDTH), lambda i: (tile(i), 0)),
            pl.BlockSpec((IN_TT, PZ_WIDTH), lambda i: (tile(i), 0)),
            pl.BlockSpec((SSM_WIDTH // LANES, IN_TT * bsz, LANES), lambda i: (0, i // bsz, 0)),
        ],
        out_shape=[
            jax.ShapeDtypeStruct((t, PF_WIDTH), F32),
            jax.ShapeDtypeStruct((t, PZ_WIDTH), BF16),
            jax.ShapeDtypeStruct((SSM_WIDTH // LANES, t, LANES), F32),
        ],
        scratch_shapes=[
            pltpu.VMEM((D_MODEL, IN_WIDTH), BF16),
            pltpu.VMEM((W_STAGE_SLOTS, D_MODEL, W_STAGE_COLS), w_in.dtype),
            pltpu.SemaphoreType.DMA((W_STAGE_SLOTS,)),
        ],
        compiler_params=pltpu.CompilerParams(
            dimension_semantics=("arbitrary",), vmem_limit_bytes=VMEM_LIMIT),
        name="in_proj",
    )(x2d, norm_w, w_in, b_gate_row, cosf, sa, sb, qk_w_row, ones_blk)


QBLK = 128
KWIN = QBLK + 2 * BAND_HALF
NORM_ROWS = 256
MERGE_ROWS = 64
LOOKAHEAD = 3
SCORE_SLOTS = LOOKAHEAD + 2


def _attn_kernel(q0_ref, q1_ref, q2_ref, k_ref, v_ref, bias_ref, o_ref,
                 va_s, vb_s, oacc_s, den_s, max_s, s_s):
    seq = k_ref.shape[1]
    q_refs = (q0_ref, q1_ref, q2_ref)
    head0 = lax.broadcasted_iota(jnp.int32, (QBLK, LANES), 1) < HEAD_DIM
    head0_rows = lax.broadcasted_iota(jnp.int32, (NORM_ROWS, LANES), 1) < HEAD_DIM

    def prep(i, carry):
        rows = pl.ds(pl.multiple_of(i * NORM_ROWS, NORM_ROWS), NORM_ROWS)
        v = v_ref[0, rows, :]
        va_s[rows, :] = jnp.where(head0_rows, v, 1.0)
        vb_s[rows, :] = jnp.where(head0_rows, 1.0, v)
        return carry

    lax.fori_loop(0, seq // NORM_ROWS, prep, 0, unroll=True)

    blocks = []
    for p, d in enumerate(DILATIONS):
        n = seq // d
        nblk = n // QBLK
        kw = min(n, KWIN)

        def rows_of(idx, d=d, n=n, nblk=nblk, kw=kw):
            r, q0 = idx // nblk, (idx % nblk) * QBLK
            ks = min(max(q0 - BAND_HALF, 0), n - kw)
            return (pl.ds(r + d * q0, QBLK, stride=d), pl.ds(r + d * ks, kw, stride=d),
                    (q0 - ks) // BAND_HALF)

        def scores(idx, slot, p=p, kw=kw, rows_of=rows_of):
            qrows, krows, case = rows_of(idx)
            qb = q_refs[p][0, qrows, :]
            kb = k_ref[0, krows, :].astype(BF16)
            bias = bias_ref[case, :, :kw]
            s_s[slot, :QBLK, :kw] = _dot_nt(jnp.where(head0, qb, 0.0).astype(BF16), kb) + bias
            s_s[slot, QBLK:, :kw] = _dot_nt(jnp.where(head0, 0.0, qb).astype(BF16), kb) + bias

        def weigh(idx, slot, p=p, kw=kw, rows_of=rows_of):
            qrows, krows, _ = rows_of(idx)
            s = s_s[slot, :, :kw]
            m = jnp.max(s, axis=-1, keepdims=True)
            e = jnp.exp2(s - m).astype(BF16)
            o_a = _dot(e[:QBLK], va_s[krows, :].astype(BF16))
            o_b = _dot(e[QBLK:], vb_s[krows, :].astype(BF16))
            oacc_s[p, qrows, :] = jnp.where(head0, o_a, o_b)
            den_s[p, qrows, :] = jnp.where(head0, o_b, o_a)
            max_s[p, qrows, :] = jnp.where(head0, jnp.broadcast_to(m[:QBLK], (QBLK, LANES)),
                                           jnp.broadcast_to(m[QBLK:], (QBLK, LANES)))

        blocks += [(scores, weigh, i) for i in range(d * nblk)]

    for g in range(LOOKAHEAD):
        blocks[g][0](blocks[g][2], g % SCORE_SLOTS)
    for g, (_, weigh, i) in enumerate(blocks):
        if g + LOOKAHEAD < len(blocks):
            ahead_scores, _, ahead_i = blocks[g + LOOKAHEAD]
            ahead_scores(ahead_i, (g + LOOKAHEAD) % SCORE_SLOTS)
        weigh(i, g % SCORE_SLOTS)

    def combine(i, carry):
        rows = pl.ds(pl.multiple_of(i * MERGE_ROWS, MERGE_ROWS), MERGE_ROWS)
        m0, m1, m2 = max_s[0, rows, :], max_s[1, rows, :], max_s[2, rows, :]
        m = jnp.maximum(jnp.maximum(m0, m1), m2)
        w0, w1, w2 = jnp.exp2(m0 - m), jnp.exp2(m1 - m), jnp.exp2(m2 - m)
        num = w0 * oacc_s[0, rows, :] + w1 * oacc_s[1, rows, :] + w2 * oacc_s[2, rows, :]
        d0, d1, d2 = (pltpu.roll(den_s[p, rows, :], HEAD_DIM, 1) for p in range(3))
        o_ref[0, rows, :] = (num / (w0 * d0 + w1 * d1 + w2 * d2)).astype(o_ref.dtype)
        return carry

    lax.fori_loop(0, seq // MERGE_ROWS, combine, 0, unroll=True)


def _band_bias():
    i = np.arange(QBLK)[:, None]
    j = np.arange(KWIN)[None, :]
    off = np.arange(3)[:, None, None] * BAND_HALF
    return np.where(np.abs(j - i - off) <= BAND_HALF, 0.0, NEG_INF).astype(np.float32)


def _attention(proj3d):
    bsz, seq, _ = proj3d.shape
    n_pairs = ATTN_WIDTH // LANES

    def qspec(p):
        return pl.BlockSpec((1, seq, LANES),
                            lambda b, hp, p=p: (b, 0, PF_Q // LANES + p * n_pairs + hp))

    seq_tile = lambda: pltpu.VMEM((seq, LANES), F32)
    pat_tile = lambda: pltpu.VMEM((len(DILATIONS), seq, LANES), F32)
    return pl.pallas_call(
        _attn_kernel,
        grid=(bsz, n_pairs),
        in_specs=[
            qspec(0), qspec(1), qspec(2),
            pl.BlockSpec((1, seq, LANES), lambda b, hp: (b, 0, PF_K // LANES + hp)),
            pl.BlockSpec((1, seq, LANES), lambda b, hp: (b, 0, PF_V // LANES + hp)),
            pl.BlockSpec((3, QBLK, KWIN), lambda b, hp: (0, 0, 0)),
        ],
        out_specs=pl.BlockSpec((1, seq, LANES), lambda b, hp: (b, 0, hp)),
        out_shape=jax.ShapeDtypeStruct((bsz, seq, ATTN_WIDTH), BF16),
        scratch_shapes=[
            seq_tile(), seq_tile(),
            pat_tile(), pat_tile(), pat_tile(),
            pltpu.VMEM((SCORE_SLOTS, 2 * QBLK, KWIN), F32),
        ],
        compiler_params=pltpu.CompilerParams(
            dimension_semantics=("arbitrary", "arbitrary"), vmem_limit_bytes=VMEM_LIMIT),
        name="dilated_attention",
    )(proj3d, proj3d, proj3d, proj3d, proj3d, _band_bias())


LAG_REP = LANES // PG


def _s5_prep_kernel(lre_ref, lim_ref, ldt_ref, bre_ref, bim_ref, cre_ref, cim_ref,
                    toep_ref, win_ref, wout_ref, a_ref, wf32_s):
    four = lambda ref: jnp.concatenate([ref[0, 0], ref[0, 0], ref[1, 0], ref[1, 0]], axis=1)
    lr = jnp.minimum(four(lre_ref), -1e-4)
    li = four(lim_ref)
    dt = jnp.exp(four(ldt_ref))
    col = lax.broadcasted_iota(jnp.int32, (1, SW), 1)
    is_re = (col // LANES) % 2 == 0
    is_fwd = col < 2 * LANES

    mag = jnp.exp(lr * dt)
    pw_re = [jnp.ones_like(mag), mag * jnp.cos(li * dt)]
    pw_im = [jnp.zeros_like(mag), mag * jnp.sin(li * dt)]
    for _ in range(CHUNK - 1):
        pw_re.append(pw_re[-1] * pw_re[1] - pw_im[-1] * pw_im[1])
        pw_im.append(pw_re[-2] * pw_im[1] + pw_im[-1] * pw_re[1])

    a_re, a_im = pw_re[1], pw_im[1]
    nr, ni, mag2 = a_re - 1.0, a_im, lr * lr + li * li
    f_re, f_im = (nr * lr + ni * li) / mag2, (ni * lr - nr * li) / mag2

    row_g = lax.broadcasted_iota(jnp.int32, (PG, SW), 0) // SSM_GROUP
    col_g = (lax.broadcasted_iota(jnp.int32, (PG, SW), 1) % LANES) // SSM_STATE
    diag = row_g == col_g
    wide = lambda ref: jnp.concatenate([ref[0, 0]] * (2 * GP) + [ref[1, 0]] * (2 * GP), axis=1)
    b_re = jnp.where(diag, wide(bre_ref), 0.0)
    b_im = jnp.where(diag, wide(bim_ref), 0.0)
    c_re = jnp.where(diag, wide(cre_ref), 0.0)
    c_im = jnp.where(diag, wide(cim_ref), 0.0)
    bb_re = f_re * b_re - f_im * b_im
    bb_im = f_re * b_im + f_im * b_re

    for s in range(CHUNK):
        rows = slice(s * PG, (s + 1) * PG)
        in_re = jnp.where(is_fwd, pw_re[CHUNK - 1 - s], pw_re[s])
        in_im = jnp.where(is_fwd, pw_im[CHUNK - 1 - s], pw_im[s])
        w = (bb_re * jnp.where(is_re, in_re, in_im)
             + bb_im * jnp.where(is_re, -in_im, in_re))
        wf32_s[rows, :] = w
        win_ref[0, rows, :] = w.astype(BF16)
        out_re = jnp.where(is_fwd, pw_re[s + 1], pw_re[CHUNK - s])
        out_im = jnp.where(is_fwd, pw_im[s + 1], pw_im[CHUNK - s])
        wout_ref[0, rows, :] = (c_re * jnp.where(is_re, out_re, -out_im)
                                + c_im * jnp.where(is_re, -out_im, -out_re)).astype(BF16)

    a_ref[0] = jnp.broadcast_to(jnp.where(is_re, pw_re[CHUNK], pw_im[CHUNK]), a_ref.shape[1:])

    c_cat = jnp.where(is_re, c_re, -c_im)
    c_rep = jnp.concatenate([c_cat] * LAG_REP, axis=0)
    half = 2 * LANES
    lag_f = _dot_nt_f32(wf32_s[:, :half], c_rep[:, :half])
    lag_b = _dot_nt_f32(wf32_s[:, half:], c_rep[:, half:])
    row_t = lax.broadcasted_iota(jnp.int32, (PW, LANES), 0) // PG
    lane_q = lax.broadcasted_iota(jnp.int32, (PW, LANES), 1) // PG
    zeros = lambda n: jnp.zeros((n * PG, LANES), F32)
    for q in range(PW // LANES):
        tile = jnp.zeros((PW, LANES), F32)
        for u in range(LAG_REP):
            t = q * LAG_REP + u
            up = CHUNK - 1 - t
            sh_f = lag_f if up == 0 else jnp.concatenate([lag_f[up * PG:], zeros(up)], axis=0)
            sh_b = lag_b if t == 0 else jnp.concatenate([zeros(t), lag_b[:(CHUNK - t) * PG]], axis=0)
            col_t = jnp.where(row_t <= t, sh_f, 0.0) + jnp.where(row_t >= t, sh_b, 0.0)
            tile = jnp.where(lane_q == u, col_t, tile)
        toep_ref[0, :, q * LANES:(q + 1) * LANES] = tile.astype(BF16)


def _s5_prep(lam_re, lam_im, log_dt, b_re, b_im, c_re, c_im):
    lanes = lambda t: t.reshape(2, N_PAIRS, 1, LANES)
    dt_b = jnp.broadcast_to(log_dt[..., None], lam_re.shape)

    tiles = lambda t_ghp: t_ghp.reshape(2, N_PAIRS, PG, SSM_STATE)

    row = pl.BlockSpec((2, 1, 1, LANES), lambda i: (0, i, 0, 0))
    mat = pl.BlockSpec((2, 1, PG, SSM_STATE), lambda i: (0, i, 0, 0))
    out = lambda r: pl.BlockSpec((1, r, SW), lambda i: (i, 0, 0))
    return pl.pallas_call(
        _s5_prep_kernel,
        grid=(N_PAIRS,),
        in_specs=[row, row, row, mat, mat, mat, mat],
        out_specs=[out(PW), out(PW), out(PW), out(SUBLANES)],
        out_shape=[jax.ShapeDtypeStruct((N_PAIRS, PW, PW), BF16),
                   jax.ShapeDtypeStruct((N_PAIRS, PW, SW), BF16),
                   jax.ShapeDtypeStruct((N_PAIRS, PW, SW), BF16),
                   jax.ShapeDtypeStruct((N_PAIRS, SUBLANES, SW), F32)],
        scratch_shapes=[pltpu.VMEM((PW, SW), F32)],
        compiler_params=pltpu.CompilerParams(dimension_semantics=("arbitrary",)),
        name="s5_prep",
    )(lanes(lam_re), lanes(lam_im), lanes(dt_b), tiles(b_re.transpose(0, 1, 3, 2)),
      tiles(b_im.transpose(0, 1, 3, 2)), tiles(c_re), tiles(c_im))


SCAN_STEPS = 8
REGROUP_CHUNKS = 32


def _s5_kernel(u_ref, d_ref, toep_ref, win_ref, wout_ref, a_ref, y_ref, x_s, p_s, st_s):
    nc = u_ref.shape[1]
    bsz = u_ref.shape[2] // CHUNK
    piece = REGROUP_CHUNKS * bsz
    pp = pl.program_id(1)
    slot = lax.broadcasted_iota(jnp.int32, (piece, LANES), 1) // PG
    time_rows = lambda t: slice(t * bsz, (t + 1) * bsz)

    def gather(cb, carry):
        chunks = pl.ds(pl.multiple_of(cb * REGROUP_CHUNKS, REGROUP_CHUNKS), REGROUP_CHUNKS)
        rows = pl.ds(pl.multiple_of(cb * piece, piece), piece)
        for q in range(CHUNK // GPS):
            acc = None
            for k in range(GPS):
                v = u_ref[0, chunks, time_rows(q * GPS + k), :].reshape(piece, LANES)
                acc = v if acc is None else jnp.where(slot == (pp + k) % GPS, v, acc)
            x_s[rows, q * LANES:(q + 1) * LANES] = pltpu.roll(
                acc, ((GPS - pp) % GPS) * PG, 1).astype(BF16)
        return carry

    lax.fori_loop(0, nc // REGROUP_CHUNKS, gather, 0)

    x = x_s[...]
    p_s[...] = _dot(x, win_ref[0])
    a_fr, a_fi, a_br, a_bi = (a_ref[0, :, q * LANES:(q + 1) * LANES] for q in range(4))
    tile = lambda q: slice(q * LANES, (q + 1) * LANES)

    def scan(i, carry):
        base_f = pl.multiple_of(i * (SCAN_STEPS * bsz), SCAN_STEPS * bsz)
        base_b = pl.multiple_of((nc - (i + 1) * SCAN_STEPS) * bsz, SCAN_STEPS * bsz)
        s_fr, s_fi, s_br, s_bi = carry
        for k in range(SCAN_STEPS):
            rf = pl.ds(base_f + k * bsz, bsz)
            rb = pl.ds(base_b + (SCAN_STEPS - 1 - k) * bsz, bsz)
            st_s[rf, tile(0)] = s_fr
            st_s[rf, tile(1)] = s_fi
            st_s[rb, tile(2)] = s_br
            st_s[rb, tile(3)] = s_bi
            s_fr, s_fi, s_br, s_bi = (a_fr * s_fr - a_fi * s_fi + p_s[rf, tile(0)],
                                      a_fr * s_fi + a_fi * s_fr + p_s[rf, tile(1)],
                                      a_br * s_br - a_bi * s_bi + p_s[rb, tile(2)],
                                      a_br * s_bi + a_bi * s_br + p_s[rb, tile(3)])
        return s_fr, s_fi, s_br, s_bi

    zero = jnp.zeros((bsz, LANES), F32)
    lax.fori_loop(0, nc // SCAN_STEPS, scan, (zero,) * 4)
    p_s[...] = _dot(x, toep_ref[0]) + _dot_nt(st_s[...].astype(BF16), wout_ref[0])

    d_tile = jnp.broadcast_to(d_ref[0], (bsz, LANES))
    d_skip = [d_tile if k == 0 else pltpu.roll(d_tile, k * PG, 1) for k in range(GPS)]

    def scatter(cb, carry):
        chunks = pl.ds(pl.multiple_of(cb * REGROUP_CHUNKS, REGROUP_CHUNKS), REGROUP_CHUNKS)
        rows = pl.ds(pl.multiple_of(cb * piece, piece), piece)
        for q in range(CHUNK // GPS):
            w = pltpu.roll(p_s[rows, q * LANES:(q + 1) * LANES], pp * PG, 1)
            w = w.reshape(1, REGROUP_CHUNKS, bsz, LANES)
            for k in range(GPS):
                where = (pl.ds(0, 1), chunks, time_rows(q * GPS + k), slice(None))
                mine = (slot == (pp + k) % GPS).reshape(1, REGROUP_CHUNKS, bsz, LANES)
                pltpu.store(y_ref.at[where], w + d_skip[k] * u_ref[where], mask=mine)
        return carry

    lax.fori_loop(0, nc // REGROUP_CHUNKS, scatter, 0)


def _s5(u_tb, d_tiles, toep, w_in, w_out, a_pow, bsz):
    n_tiles, nc, cb, _ = u_tb.shape
    rows = nc * bsz
    tile = pl.BlockSpec((1, nc, cb, LANES), lambda j, p: (j, 0, 0, 0))
    pair = lambda r, c: pl.BlockSpec((1, r, c), lambda j, p: (j * GPS + p, 0, 0))
    return pl.pallas_call(
        _s5_kernel,
        grid=(n_tiles, GPS),
        in_specs=[tile, pl.BlockSpec((1, 1, LANES), lambda j, p: (j, 0, 0)),
                  pair(PW, PW), pair(PW, SW), pair(PW, SW), pair(bsz, SW)],
        out_specs=tile,
        out_shape=jax.ShapeDtypeStruct(u_tb.shape, F32),
        scratch_shapes=[pltpu.VMEM((rows, PW), BF16),
                        pltpu.VMEM((rows, SW), F32),
                        pltpu.VMEM((rows, SW), F32)],
        compiler_params=pltpu.CompilerParams(
            dimension_semantics=("arbitrary", "arbitrary"), vmem_limit_bytes=VMEM_LIMIT),
        name="s5_chunked",
    )(u_tb, d_tiles, toep, w_in, w_out, a_pow)


TAIL_TT = 512


def _tail_kernel(x_ref, ytb_ref, za_ref, at_ref, zb_ref, g_ref, wg_ref, bg_ref,
                 wps_ref, wpa_ref, wo_ref, o_ref):
    bsz = ytb_ref.shape[1] // TAIL_TT
    batch = pl.program_id(0) % bsz
    y_s5 = jnp.concatenate(
        [_rotate_rows(ytb_ref[lt, pl.ds(batch, TAIL_TT, stride=bsz), :], forward=False)
         for lt in range(SSM_WIDTH // LANES)], axis=1)
    ys = jax.nn.gelu(y_s5).astype(BF16)
    glu = _dot(ys, wg_ref[...]) + bg_ref[...]
    z_b = zb_ref[...].astype(F32)
    y_b = _dot((at_ref[...].astype(F32) * (z_b * _sigmoid(z_b))).astype(BF16), wpa_ref[...])
    z_a = za_ref[...].astype(F32)
    a_in = glu[:, :SSM_WIDTH] * _sigmoid(glu[:, SSM_WIDTH:]) * (z_a * _sigmoid(z_a))
    y_a = _dot(a_in.astype(BF16), wps_ref[...])
    g = g_ref[...].astype(F32)
    mix = _sigmoid(g[:, :D_MODEL]) * y_a + _sigmoid(g[:, D_MODEL:]) * y_b
    o_ref[...] = x_ref[...] + _dot(mix.astype(BF16), wo_ref[...])


def _tail(x2d, y_tb, attn2d, pz2d, w_glu, b_glu, w_ps, w_pa, w_out, bsz, seq):
    t = x2d.shape[0]
    tile = _token_tile(bsz, seq // TAIL_TT)
    row = lambda w, c: pl.BlockSpec((TAIL_TT, w), lambda i, c=c: (tile(i), c))
    const = lambda shape: pl.BlockSpec(shape, lambda i: (0, 0))
    return pl.pallas_call(
        _tail_kernel,
        grid=(t // TAIL_TT,),
        in_specs=[
            row(D_MODEL, 0),
            pl.BlockSpec((SSM_WIDTH // LANES, TAIL_TT * bsz, LANES), lambda i: (0, i // bsz, 0)),
            row(SSM_WIDTH, PZ_ZA // SSM_WIDTH),
            row(ATTN_WIDTH, 0),
            row(ATTN_WIDTH, PZ_ZB // ATTN_WIDTH),
            row(2 * D_MODEL, PZ_G // (2 * D_MODEL)),
            const((SSM_WIDTH, 2 * SSM_WIDTH)), const((1, 2 * SSM_WIDTH)),
            const((SSM_WIDTH, D_MODEL)), const((ATTN_WIDTH, D_MODEL)), const((D_MODEL, D_MODEL)),
        ],
        out_specs=pl.BlockSpec((TAIL_TT, D_MODEL), lambda i: (tile(i), 0)),
        out_shape=jax.ShapeDtypeStruct((t, D_MODEL), F32),
        compiler_params=pltpu.CompilerParams(
            dimension_semantics=("arbitrary",), vmem_limit_bytes=VMEM_LIMIT),
        name="tail",
    )(x2d, y_tb, pz2d, attn2d, pz2d, pz2d, w_glu, b_glu, w_ps, w_pa, w_out)


def _rope_tables(seq):
    half = ROPE_DIM // 2
    inv = ROPE_THETA ** (-np.arange(0, ROPE_DIM, 2, dtype=np.float64) / ROPE_DIM)
    ang = np.arange(seq, dtype=np.float64)[:, None] * inv[None, :]
    cos, sin = np.cos(ang).astype(np.float32), np.sin(ang).astype(np.float32)
    zeros = np.zeros((seq, HEAD_DIM - ROPE_DIM), np.float32)
    z8 = np.zeros((seq, half), np.float32)
    cos_h = np.concatenate([cos, cos, np.ones_like(zeros)], axis=1)
    sa_h = np.concatenate([-sin, z8, zeros], axis=1)
    sb_h = np.concatenate([z8, sin, zeros], axis=1)
    two = lambda t: np.concatenate([t, t], axis=1).astype(np.float32)
    return two(cos_h), two(sa_h), two(sb_h)


def kernel(x, norm_w, w_in, b_gate, q_norm_w, k_norm_w, ssm_lam_re, ssm_lam_im, ssm_log_dt,
           ssm_b_re, ssm_b_im, ssm_c_re, ssm_c_im, ssm_d, w_glu, b_glu,
           w_proj_ssm, w_proj_attn, w_out):
    bsz, seq, d_model = x.shape
    depth = norm_w.shape[0]
    assert d_model == D_MODEL and w_in.shape[-1] == IN_WIDTH
    assert bsz == SUBLANES, "S5 rows (chunk, batch) must fill whole sublane tiles"
    assert seq % IN_TT == 0 and seq % TAIL_TT == 0 and seq % CHUNK == 0
    assert IN_TT % GPS == 0 and TAIL_TT % GPS == 0 and CHUNK % GPS == 0
    assert all(seq % (d * QBLK) == 0 for d in DILATIONS) and (seq // QBLK) % 4 == 0
    cosf, sa, sb = _rope_tables(seq)
    ones_blk = jnp.asarray(np.kron(np.eye(2 * LANES // HEAD_DIM, dtype=np.float32),
                                   np.full((HEAD_DIM, HEAD_DIM), 1.0 / HEAD_DIM, np.float32)), BF16)
    for layer in range(depth):
        x2d = x.reshape(bsz * seq, D_MODEL)
        q_gain = q_norm_w[layer].astype(F32) * (LOG2E * HEAD_DIM ** -0.5)
        qk_w_row = jnp.concatenate([jnp.tile(q_gain, len(DILATIONS) * ATTN_SLOTS),
                                    jnp.tile(k_norm_w[layer].astype(F32), ATTN_SLOTS)])[None, :]
        pf2d, pz2d, u_tb = _in_proj(
            x2d, norm_w[layer][None, :].astype(F32), w_in, layer,
            b_gate[layer][None, :].astype(F32), cosf, sa, sb, qk_w_row, ones_blk, bsz, seq)
        attn = _attention(pf2d.reshape(bsz, seq, PF_WIDTH))

        toep, s5_in, s5_out, a_pow = _s5_prep(
            ssm_lam_re[layer].astype(F32), ssm_lam_im[layer].astype(F32),
            ssm_log_dt[layer].astype(F32), ssm_b_re[layer].astype(F32),
            ssm_b_im[layer].astype(F32), ssm_c_re[layer].astype(F32),
            ssm_c_im[layer].astype(F32))
        n_tiles = SSM_WIDTH // LANES
        y_tb = _s5(u_tb.reshape(n_tiles, seq // CHUNK, CHUNK * bsz, LANES),
                   ssm_d[layer].astype(F32).reshape(n_tiles, 1, LANES), toep, s5_in, s5_out,
                   a_pow, bsz)

        out2d = _tail(x2d, y_tb.reshape(n_tiles, seq * bsz, LANES),
                      attn.reshape(bsz * seq, ATTN_WIDTH), pz2d,
                      w_glu[layer].astype(BF16), b_glu[layer][None, :].astype(F32),
                      w_proj_ssm[layer].astype(BF16), w_proj_attn[layer].astype(BF16),
                      w_out[layer].astype(BF16), bsz, seq)
        x = out2d.reshape(bsz, seq, D_MODEL)
    return x
```

```python
import functools
import math

import jax
import jax.numpy as jnp
import numpy as np
from jax import lax
from jax.experimental import pallas as pl
from jax.experimental.pallas import tpu as pltpu

F32 = jnp.float32
BF16 = jnp.bfloat16

D_MODEL = 1024
SSM_WIDTH = 512
SSM_GROUP = 16
SSM_GROUPS = 32
SSM_STATE = 64
HEAD_DIM = 64
ATTN_SLOTS = 8
ATTN_WIDTH = 512
DILATIONS = (1, 4, 16)
BAND_HALF = 64
ROPE_THETA = 500000.0
ROPE_DIM = 16
EPS = 1e-6
NEG_INF = -1e30
IN_WIDTH = 6144
COL_U, COL_ZA, COL_Q, COL_K, COL_V, COL_ZB, COL_G = 0, 512, 1024, 2560, 3072, 3584, 4096

LANES = 128
SUBLANES = 8
VMEM_LIMIT = 56 * 1024 * 1024

CHUNK = 16
GP = 2
N_PAIRS = SSM_GROUPS // GP
PG = GP * SSM_GROUP
PW = CHUNK * PG
SW = 4 * LANES
GPS = LANES // PG


def _dot(a, b):
    return jnp.dot(a, b, preferred_element_type=F32)


def _dot_nt(a, b):
    return lax.dot_general(a, b, (((1,), (1,)), ((), ())), preferred_element_type=F32)


def _dot_nt_f32(a, b):
    a_hi, b_hi = a.astype(BF16), b.astype(BF16)
    a_lo = (a - a_hi.astype(F32)).astype(BF16)
    b_lo = (b - b_hi.astype(F32)).astype(BF16)
    return _dot_nt(a_hi, b_hi) + (_dot_nt(a_hi, b_lo) + _dot_nt(a_lo, b_hi))


def _rotate_rows(tile, forward):
    residue = lax.broadcasted_iota(jnp.int32, tile.shape, 0) % GPS
    out = tile
    for k in range(1, GPS):
        out = jnp.where(residue == k, pltpu.roll(tile, (k if forward else GPS - k) * PG, 1), out)
    return out


def _sigmoid(v):
    return 0.5 * jnp.tanh(0.5 * v) + 0.5


IN_TT = 512
IN_TN = 512
QK_ROWS = 256
LOG2E = math.log2(math.e)
PF_Q, PF_K, PF_V, PF_WIDTH = 0, 1536, 2048, 2560
PZ_G, PZ_ZA, PZ_ZB, PZ_WIDTH = 0, 2048, 2560, 3072
_Q_COLS = tuple((COL_Q + o, PF_Q + o) for o in range(0, COL_K - COL_Q, IN_TN))
_G_COLS = tuple((COL_G + o, PZ_G + o) for o in range(0, IN_WIDTH - COL_G, IN_TN))
F32_DEST = dict(((COL_K, PF_K), (COL_V, PF_V)) + _Q_COLS)
BF16_DEST = dict(((COL_ZA, PZ_ZA), (COL_ZB, PZ_ZB)) + _G_COLS)


W_STAGE_COLS = 256
W_STAGE_SLOTS = 3


def _in_proj_kernel(layer, x_ref, nw_ref, w_hbm, b_ref, cos_ref, sa_ref, sb_ref, qkw_ref,
                    ones_ref, pf_ref, pz_ref, utb_ref, w_ref, stage_ref, sem_ref):
    bsz = utb_ref.shape[1] // IN_TT
    batch = pl.program_id(0) % bsz

    def w_copy(j):
        return pltpu.make_async_copy(
            w_hbm.at[layer, :, pl.ds(j * W_STAGE_COLS, W_STAGE_COLS)],
            stage_ref.at[j % W_STAGE_SLOTS], sem_ref.at[j % W_STAGE_SLOTS])

    n_copies = IN_WIDTH // W_STAGE_COLS

    def w_stage_in(j):
        w_copy(j).wait()
        w_ref[:, j * W_STAGE_COLS:(j + 1) * W_STAGE_COLS] = (
            stage_ref[j % W_STAGE_SLOTS].astype(BF16))
        if j + W_STAGE_SLOTS < n_copies:
            w_copy(j + W_STAGE_SLOTS).start()

    x = x_ref[...]
    var = jnp.mean(x * x, axis=-1, keepdims=True)
    h = (x * lax.rsqrt(var + EPS) * nw_ref[...]).astype(BF16)
    ones_blk = ones_ref[...]

    def project(streaming):
        for c0 in range(0, IN_WIDTH, IN_TN):
            if streaming:
                for j in range(c0 // W_STAGE_COLS, (c0 + IN_TN) // W_STAGE_COLS):
                    w_stage_in(j)
            acc = _dot(h, w_ref[:, c0:c0 + IN_TN])
            if c0 in BF16_DEST:
                if c0 >= COL_G:
                    acc = acc + b_ref[:, c0 - COL_G:c0 - COL_G + IN_TN]
                pz_ref[:, BF16_DEST[c0]:BF16_DEST[c0] + IN_TN] = acc.astype(BF16)
                continue
            if c0 == COL_U:
                for lt in range(SSM_WIDTH // LANES):
                    utb_ref[lt, pl.ds(batch, IN_TT, stride=bsz), :] = _rotate_rows(
                        acc[:, lt * LANES:(lt + 1) * LANES], forward=True)
                continue
            d0 = F32_DEST[c0]
            if not COL_Q <= c0 < COL_V:
                pf_ref[:, d0:d0 + IN_TN] = acc
                continue
            for r0 in range(0, IN_TT, QK_ROWS):
                rows = slice(r0, r0 + QK_ROWS)
                for t2 in range(0, IN_TN, 2 * LANES):
                    xq2 = acc[rows, t2:t2 + 2 * LANES]
                    ms2 = _dot((xq2 * xq2).astype(BF16), ones_blk)
                    xn2 = xq2 * lax.rsqrt(ms2 + EPS) * qkw_ref[:, c0 - COL_Q + t2:
                                                                c0 - COL_Q + t2 + 2 * LANES]
                    for t in (0, LANES):
                        xn = xn2[:, t:t + LANES]
                        pf_ref[rows, d0 + t2 + t:d0 + t2 + t + LANES] = (
                            xn * cos_ref[rows, :]
                            + pltpu.roll(xn, LANES - ROPE_DIM // 2, 1) * sa_ref[rows, :]
                            + pltpu.roll(xn, ROPE_DIM // 2, 1) * sb_ref[rows, :])

    first = pl.program_id(0) == 0

    @pl.when(first)
    def _():
        for j in range(W_STAGE_SLOTS):
            w_copy(j).start()
        project(True)

    @pl.when(jnp.logical_not(first))
    def _():
        project(False)


def _token_tile(bsz, tiles_per_seq):
    return lambda i: (i % bsz) * tiles_per_seq + i // bsz


def _in_proj(x2d, norm_w, w_in, layer, b_gate_row, cosf, sa, sb, qk_w_row, ones_blk, bsz, seq):
    t = x2d.shape[0]
    tile = _token_tile(bsz, seq // IN_TT)
    const = lambda shape: pl.BlockSpec(shape, lambda i: (0, 0))
    rope = lambda: pl.BlockSpec((IN_TT, LANES), lambda i: (i // bsz, 0))
    return pl.pallas_call(
        functools.partial(_in_proj_kernel, layer),
        grid=(t // IN_TT,),
        in_specs=[
            pl.BlockSpec((IN_TT, D_MODEL), lambda i: (tile(i), 0)),
            const((1, D_MODEL)),
            pl.BlockSpec(memory_space=pl.ANY),
            const((1, IN_WIDTH - COL_G)),
            rope(), rope(), rope(),
            const((1, COL_V - COL_Q)),
            const((2 * LANES, 2 * LANES)),
        ],
        out_specs=[
            pl.BlockSpec((IN_TT, PF_WIDTH), lambda i: (tile(i), 0)),
            pl.BlockSpec((IN_TT, PZ_WIDTH), lambda i: (tile(i), 0)),
            pl.BlockSpec((SSM_WIDTH // LANES, IN_TT * bsz, LANES), lambda i: (0, i // bsz, 0)),
        ],
        out_shape=[
            jax.ShapeDtypeStruct((t, PF_WIDTH), F32),
            jax.ShapeDtypeStruct((t, PZ_WIDTH), BF16),
            jax.ShapeDtypeStruct((SSM_WIDTH // LANES, t, LANES), F32),
        ],
        scratch_shapes=[
            pltpu.VMEM((D_MODEL, IN_WIDTH), BF16),
            pltpu.VMEM((W_STAGE_SLOTS, D_MODEL, W_STAGE_COLS), w_in.dtype),
            pltpu.SemaphoreType.DMA((W_STAGE_SLOTS,)),
        ],
        compiler_params=pltpu.CompilerParams(
            dimension_semantics=("arbitrary",), vmem_limit_bytes=VMEM_LIMIT),
        name="in_proj",
    )(x2d, norm_w, w_in, b_gate_row, cosf, sa, sb, qk_w_row, ones_blk)


QBLK = 128
KWIN = QBLK + 2 * BAND_HALF
NORM_ROWS = 256
MERGE_ROWS = 64
LOOKAHEAD = 3
SCORE_SLOTS = LOOKAHEAD + 2


def _attn_kernel(q0_ref, q1_ref, q2_ref, k_ref, v_ref, bias_ref, o_ref,
                 va_s, vb_s, oacc_s, den_s, max_s, s_s):
    seq = k_ref.shape[1]
    q_refs = (q0_ref, q1_ref, q2_ref)
    head0 = lax.broadcasted_iota(jnp.int32, (QBLK, LANES), 1) < HEAD_DIM
    head0_rows = lax.broadcasted_iota(jnp.int32, (NORM_ROWS, LANES), 1) < HEAD_DIM

    def prep(i, carry):
        rows = pl.ds(pl.multiple_of(i * NORM_ROWS, NORM_ROWS), NORM_ROWS)
        v = v_ref[0, rows, :]
        va_s[rows, :] = jnp.where(head0_rows, v, 1.0)
        vb_s[rows, :] = jnp.where(head0_rows, 1.0, v)
        return carry

    lax.fori_loop(0, seq // NORM_ROWS, prep, 0, unroll=True)

    blocks = []
    for p, d in enumerate(DILATIONS):
        n = seq // d
        nblk = n // QBLK
        kw = min(n, KWIN)

        def rows_of(idx, d=d, n=n, nblk=nblk, kw=kw):
            r, q0 = idx // nblk, (idx % nblk) * QBLK
            ks = min(max(q0 - BAND_HALF, 0), n - kw)
            return (pl.ds(r + d * q0, QBLK, stride=d), pl.ds(r + d * ks, kw, stride=d),
                    (q0 - ks) // BAND_HALF)

        def scores(idx, slot, p=p, kw=kw, rows_of=rows_of):
            qrows, krows, case = rows_of(idx)
            qb = q_refs[p][0, qrows, :]
            kb = k_ref[0, krows, :].astype(BF16)
            bias = bias_ref[case, :, :kw]
            s_s[slot, :QBLK, :kw] = _dot_nt(jnp.where(head0, qb, 0.0).astype(BF16), kb) + bias
            s_s[slot, QBLK:, :kw] = _dot_nt(jnp.where(head0, 0.0, qb).astype(BF16), kb) + bias

        def weigh(idx, slot, p=p, kw=kw, rows_of=rows_of):
            qrows, krows, _ = rows_of(idx)
            s = s_s[slot, :, :kw]
            m = jnp.max(s, axis=-1, keepdims=True)
            e = jnp.exp2(s - m).astype(BF16)
            o_a = _dot(e[:QBLK], va_s[krows, :].astype(BF16))
            o_b = _dot(e[QBLK:], vb_s[krows, :].astype(BF16))
            oacc_s[p, qrows, :] = jnp.where(head0, o_a, o_b)
            den_s[p, qrows, :] = jnp.where(head0, o_b, o_a)
            max_s[p, qrows, :] = jnp.where(head0, jnp.broadcast_to(m[:QBLK], (QBLK, LANES)),
                                           jnp.broadcast_to(m[QBLK:], (QBLK, LANES)))

        blocks += [(scores, weigh, i) for i in range(d * nblk)]

    for g in range(LOOKAHEAD):
        blocks[g][0](blocks[g][2], g % SCORE_SLOTS)
    for g, (_, weigh, i) in enumerate(blocks):
        if g + LOOKAHEAD < len(blocks):
            ahead_scores, _, ahead_i = blocks[g + LOOKAHEAD]
            ahead_scores(ahead_i, (g + LOOKAHEAD) % SCORE_SLOTS)
        weigh(i, g % SCORE_SLOTS)

    def combine(i, carry):
        rows = pl.ds(pl.multiple_of(i * MERGE_ROWS, MERGE_ROWS), MERGE_ROWS)
        m0, m1, m2 = max_s[0, rows, :], max_s[1, rows, :], max_s[2, rows, :]
        m = jnp.maximum(jnp.maximum(m0, m1), m2)
        w0, w1, w2 = jnp.exp2(m0 - m), jnp.exp2(m1 - m), jnp.exp2(m2 - m)
        num = w0 * oacc_s[0, rows, :] + w1 * oacc_s[1, rows, :] + w2 * oacc_s[2, rows, :]
        d0, d1, d2 = (pltpu.roll(den_s[p, rows, :], HEAD_DIM, 1) for p in range(3))
        o_ref[0, rows, :] = (num / (w0 * d0 + w1 * d1 + w2 * d2)).astype(o_ref.dtype)
        return carry

    lax.fori_loop(0, seq // MERGE_ROWS, combine, 0, unroll=True)


def _band_bias():
    i = np.arange(QBLK)[:, None]
    j = np.arange(KWIN)[None, :]
    off = np.arange(3)[:, None, None] * BAND_HALF
    return np.where(np.abs(j - i - off) <= BAND_HALF, 0.0, NEG_INF).astype(np.float32)


def _attention(proj3d):
    bsz, seq, _ = proj3d.shape
    n_pairs = ATTN_WIDTH // LANES

    def qspec(p):
        return pl.BlockSpec((1, seq, LANES),
                            lambda b, hp, p=p: (b, 0, PF_Q // LANES + p * n_pairs + hp))

    seq_tile = lambda: pltpu.VMEM((seq, LANES), F32)
    pat_tile = lambda: pltpu.VMEM((len(DILATIONS), seq, LANES), F32)
    return pl.pallas_call(
        _attn_kernel,
        grid=(bsz, n_pairs),
        in_specs=[
            qspec(0), qspec(1), qspec(2),
            pl.BlockSpec((1, seq, LANES), lambda b, hp: (b, 0, PF_K // LANES + hp)),
            pl.BlockSpec((1, seq, LANES), lambda b, hp: (b, 0, PF_V // LANES + hp)),
            pl.BlockSpec((3, QBLK, KWIN), lambda b, hp: (0, 0, 0)),
        ],
        out_specs=pl.BlockSpec((1, seq, LANES), lambda b, hp: (b, 0, hp)),
        out_shape=jax.ShapeDtypeStruct((bsz, seq, ATTN_WIDTH), BF16),
        scratch_shapes=[
            seq_tile(), seq_tile(),
            pat_tile(), pat_tile(), pat_tile(),
            pltpu.VMEM((SCORE_SLOTS, 2 * QBLK, KWIN), F32),
        ],
        compiler_params=pltpu.CompilerParams(
            dimension_semantics=("arbitrary", "arbitrary"), vmem_limit_bytes=VMEM_LIMIT),
        name="dilated_attention",
    )(proj3d, proj3d, proj3d, proj3d, proj3d, _band_bias())


LAG_REP = LANES // PG


def _s5_prep_kernel(lre_ref, lim_ref, ldt_ref, bre_ref, bim_ref, cre_ref, cim_ref,
                    toep_ref, win_ref, wout_ref, a_ref, wf32_s):
    four = lambda ref: jnp.concatenate([ref[0, 0], ref[0, 0], ref[1, 0], ref[1, 0]], axis=1)
    lr = jnp.minimum(four(lre_ref), -1e-4)
    li = four(lim_ref)
    dt = jnp.exp(four(ldt_ref))
    col = lax.broadcasted_iota(jnp.int32, (1, SW), 1)
    is_re = (col // LANES) % 2 == 0
    is_fwd = col < 2 * LANES

    mag = jnp.exp(lr * dt)
    pw_re = [jnp.ones_like(mag), mag * jnp.cos(li * dt)]
    pw_im = [jnp.zeros_like(mag), mag * jnp.sin(li * dt)]
    for _ in range(CHUNK - 1):
        pw_re.append(pw_re[-1] * pw_re[1] - pw_im[-1] * pw_im[1])
        pw_im.append(pw_re[-2] * pw_im[1] + pw_im[-1] * pw_re[1])

    a_re, a_im = pw_re[1], pw_im[1]
    nr, ni, mag2 = a_re - 1.0, a_im, lr * lr + li * li
    f_re, f_im = (nr * lr + ni * li) / mag2, (ni * lr - nr * li) / mag2

    row_g = lax.broadcasted_iota(jnp.int32, (PG, SW), 0) // SSM_GROUP
    col_g = (lax.broadcasted_iota(jnp.int32, (PG, SW), 1) % LANES) // SSM_STATE
    diag = row_g == col_g
    wide = lambda ref: jnp.concatenate([ref[0, 0]] * (2 * GP) + [ref[1, 0]] * (2 * GP), axis=1)
    b_re = jnp.where(diag, wide(bre_ref), 0.0)
    b_im = jnp.where(diag, wide(bim_ref), 0.0)
    c_re = jnp.where(diag, wide(cre_ref), 0.0)
    c_im = jnp.where(diag, wide(cim_ref), 0.0)
    bb_re = f_re * b_re - f_im * b_im
    bb_im = f_re * b_im + f_im * b_re

    for s in range(CHUNK):
        rows = slice(s * PG, (s + 1) * PG)
        in_re = jnp.where(is_fwd, pw_re[CHUNK - 1 - s], pw_re[s])
        in_im = jnp.where(is_fwd, pw_im[CHUNK - 1 - s], pw_im[s])
        w = (bb_re * jnp.where(is_re, in_re, in_im)
             + bb_im * jnp.where(is_re, -in_im, in_re))
        wf32_s[rows, :] = w
        win_ref[0, rows, :] = w.astype(BF16)
        out_re = jnp.where(is_fwd, pw_re[s + 1], pw_re[CHUNK - s])
        out_im = jnp.where(is_fwd, pw_im[s + 1], pw_im[CHUNK - s])
        wout_ref[0, rows, :] = (c_re * jnp.where(is_re, out_re, -out_im)
                                + c_im * jnp.where(is_re, -out_im, -out_re)).astype(BF16)

    a_ref[0] = jnp.broadcast_to(jnp.where(is_re, pw_re[CHUNK], pw_im[CHUNK]), a_ref.shape[1:])

    c_cat = jnp.where(is_re, c_re, -c_im)
    c_rep = jnp.concatenate([c_cat] * LAG_REP, axis=0)
    half = 2 * LANES
    lag_f = _dot_nt_f32(wf32_s[:, :half], c_rep[:, :half])
    lag_b = _dot_nt_f32(wf32_s[:, half:], c_rep[:, half:])
    row_t = lax.broadcasted_iota(jnp.int32, (PW, LANES), 0) // PG
    lane_q = lax.broadcasted_iota(jnp.int32, (PW, LANES), 1) // PG
    zeros = lambda n: jnp.zeros((n * PG, LANES), F32)
    for q in range(PW // LANES):
        tile = jnp.zeros((PW, LANES), F32)
        for u in range(LAG_REP):
            t = q * LAG_REP + u
            up = CHUNK - 1 - t
            sh_f = lag_f if up == 0 else jnp.concatenate([lag_f[up * PG:], zeros(up)], axis=0)
            sh_b = lag_b if t == 0 else jnp.concatenate([zeros(t), lag_b[:(CHUNK - t) * PG]], axis=0)
            col_t = jnp.where(row_t <= t, sh_f, 0.0) + jnp.where(row_t >= t, sh_b, 0.0)
            tile = jnp.where(lane_q == u, col_t, tile)
        toep_ref[0, :, q * LANES:(q + 1) * LANES] = tile.astype(BF16)


def _s5_prep(lam_re, lam_im, log_dt, b_re, b_im, c_re, c_im):
    lanes = lambda t: t.reshape(2, N_PAIRS, 1, LANES)
    dt_b = jnp.broadcast_to(log_dt[..., None], lam_re.shape)

    tiles = lambda t_ghp: t_ghp.reshape(2, N_PAIRS, PG, SSM_STATE)

    row = pl.BlockSpec((2, 1, 1, LANES), lambda i: (0, i, 0, 0))
    mat = pl.BlockSpec((2, 1, PG, SSM_STATE), lambda i: (0, i, 0, 0))
    out = lambda r: pl.BlockSpec((1, r, SW), lambda i: (i, 0, 0))
    return pl.pallas_call(
        _s5_prep_kernel,
        grid=(N_PAIRS,),
        in_specs=[row, row, row, mat, mat, mat, mat],
        out_specs=[out(PW), out(PW), out(PW), out(SUBLANES)],
        out_shape=[jax.ShapeDtypeStruct((N_PAIRS, PW, PW), BF16),
                   jax.ShapeDtypeStruct((N_PAIRS, PW, SW), BF16),
                   jax.ShapeDtypeStruct((N_PAIRS, PW, SW), BF16),
                   jax.ShapeDtypeStruct((N_PAIRS, SUBLANES, SW), F32)],
        scratch_shapes=[pltpu.VMEM((PW, SW), F32)],
        compiler_params=pltpu.CompilerParams(dimension_semantics=("arbitrary",)),
        name="s5_prep",
    )(lanes(lam_re), lanes(lam_im), lanes(dt_b), tiles(b_re.transpose(0, 1, 3, 2)),
      tiles(b_im.transpose(0, 1, 3, 2)), tiles(c_re), tiles(c_im))


SCAN_STEPS = 8
REGROUP_CHUNKS = 32


def _s5_kernel(u_ref, d_ref, toep_ref, win_ref, wout_ref, a_ref, y_ref, x_s, p_s, st_s):
    nc = u_ref.shape[1]
    bsz = u_ref.shape[2] // CHUNK
    piece = REGROUP_CHUNKS * bsz
    pp = pl.program_id(1)
    slot = lax.broadcasted_iota(jnp.int32, (piece, LANES), 1) // PG
    time_rows = lambda t: slice(t * bsz, (t + 1) * bsz)

    def gather(cb, carry):
        chunks = pl.ds(pl.multiple_of(cb * REGROUP_CHUNKS, REGROUP_CHUNKS), REGROUP_CHUNKS)
        rows = pl.ds(pl.multiple_of(cb * piece, piece), piece)
        for q in range(CHUNK // GPS):
            acc = None
            for k in range(GPS):
                v = u_ref[0, chunks, time_rows(q * GPS + k), :].reshape(piece, LANES)
                acc = v if acc is None else jnp.where(slot == (pp + k) % GPS, v, acc)
            x_s[rows, q * LANES:(q + 1) * LANES] = pltpu.roll(
                acc, ((GPS - pp) % GPS) * PG, 1).astype(BF16)
        return carry

    lax.fori_loop(0, nc // REGROUP_CHUNKS, gather, 0)

    x = x_s[...]
    p_s[...] = _dot(x, win_ref[0])
    a_fr, a_fi, a_br, a_bi = (a_ref[0, :, q * LANES:(q + 1) * LANES] for q in range(4))
    tile = lambda q: slice(q * LANES, (q + 1) * LANES)

    def scan(i, carry):
        base_f = pl.multiple_of(i * (SCAN_STEPS * bsz), SCAN_STEPS * bsz)
        base_b = pl.multiple_of((nc - (i + 1) * SCAN_STEPS) * bsz, SCAN_STEPS * bsz)
        s_fr, s_fi, s_br, s_bi = carry
        for k in range(SCAN_STEPS):
            rf = pl.ds(base_f + k * bsz, bsz)
            rb = pl.ds(base_b + (SCAN_STEPS - 1 - k) * bsz, bsz)
            st_s[rf, tile(0)] = s_fr
            st_s[rf, tile(1)] = s_fi
            st_s[rb, tile(2)] = s_br
            st_s[rb, tile(3)] = s_bi
            s_fr, s_fi, s_br, s_bi = (a_fr * s_fr - a_fi * s_fi + p_s[rf, tile(0)],
                                      a_fr * s_fi + a_fi * s_fr + p_s[rf, tile(1)],
                                      a_br * s_br - a_bi * s_bi + p_s[rb, tile(2)],
                                      a_br * s_bi + a_bi * s_br + p_s[rb, tile(3)])
        return s_fr, s_fi, s_br, s_bi

    zero = jnp.zeros((bsz, LANES), F32)
    lax.fori_loop(0, nc // SCAN_STEPS, scan, (zero,) * 4)
    p_s[...] = _dot(x, toep_ref[0]) + _dot_nt(st_s[...].astype(BF16), wout_ref[0])

    d_tile = jnp.broadcast_to(d_ref[0], (bsz, LANES))
    d_skip = [d_tile if k == 0 else pltpu.roll(d_tile, k * PG, 1) for k in range(GPS)]

    def scatter(cb, carry):
        chunks = pl.ds(pl.multiple_of(cb * REGROUP_CHUNKS, REGROUP_CHUNKS), REGROUP_CHUNKS)
        rows = pl.ds(pl.multiple_of(cb * piece, piece), piece)
        for q in range(CHUNK // GPS):
            w = pltpu.roll(p_s[rows, q * LANES:(q + 1) * LANES], pp * PG, 1)
            w = w.reshape(1, REGROUP_CHUNKS, bsz, LANES)
            for k in range(GPS):
                where = (pl.ds(0, 1), chunks, time_rows(q * GPS + k), slice(None))
                mine = (slot == (pp + k) % GPS).reshape(1, REGROUP_CHUNKS, bsz, LANES)
                pltpu.store(y_ref.at[where], w + d_skip[k] * u_ref[where], mask=mine)
        return carry

    lax.fori_loop(0, nc // REGROUP_CHUNKS, scatter, 0)


def _s5(u_tb, d_tiles, toep, w_in, w_out, a_pow, bsz):
    n_tiles, nc, cb, _ = u_tb.shape
    rows = nc * bsz
    tile = pl.BlockSpec((1, nc, cb, LANES), lambda j, p: (j, 0, 0, 0))
    pair = lambda r, c: pl.BlockSpec((1, r, c), lambda j, p: (j * GPS + p, 0, 0))
    return pl.pallas_call(
        _s5_kernel,
        grid=(n_tiles, GPS),
        in_specs=[tile, pl.BlockSpec((1, 1, LANES), lambda j, p: (j, 0, 0)),
                  pair(PW, PW), pair(PW, SW), pair(PW, SW), pair(bsz, SW)],
        out_specs=tile,
        out_shape=jax.ShapeDtypeStruct(u_tb.shape, F32),
        scratch_shapes=[pltpu.VMEM((rows, PW), BF16),
                        pltpu.VMEM((rows, SW), F32),
                        pltpu.VMEM((rows, SW), F32)],
        compiler_params=pltpu.CompilerParams(
            dimension_semantics=("arbitrary", "arbitrary"), vmem_limit_bytes=VMEM_LIMIT),
        name="s5_chunked",
    )(u_tb, d_tiles, toep, w_in, w_out, a_pow)


TAIL_TT = 512


def _tail_kernel(x_ref, ytb_ref, za_ref, at_ref, zb_ref, g_ref, wg_ref, bg_ref,
                 wps_ref, wpa_ref, wo_ref, o_ref):
    bsz = ytb_ref.shape[1] // TAIL_TT
    batch = pl.program_id(0) % bsz
    y_s5 = jnp.concatenate(
        [_rotate_rows(ytb_ref[lt, pl.ds(batch, TAIL_TT, stride=bsz), :], forward=False)
         for lt in range(SSM_WIDTH // LANES)], axis=1)
    ys = jax.nn.gelu(y_s5).astype(BF16)
    glu = _dot(ys, wg_ref[...]) + bg_ref[...]
    z_b = zb_ref[...].astype(F32)
    y_b = _dot((at_ref[...].astype(F32) * (z_b * _sigmoid(z_b))).astype(BF16), wpa_ref[...])
    z_a = za_ref[...].astype(F32)
    a_in = glu[:, :SSM_WIDTH] * _sigmoid(glu[:, SSM_WIDTH:]) * (z_a * _sigmoid(z_a))
    y_a = _dot(a_in.astype(BF16), wps_ref[...])
    g = g_ref[...].astype(F32)
    mix = _sigmoid(g[:, :D_MODEL]) * y_a + _sigmoid(g[:, D_MODEL:]) * y_b
    o_ref[...] = x_ref[...] + _dot(mix.astype(BF16), wo_ref[...])


def _tail(x2d, y_tb, attn2d, pz2d, w_glu, b_glu, w_ps, w_pa, w_out, bsz, seq):
    t = x2d.shape[0]
    tile = _token_tile(bsz, seq // TAIL_TT)
    row = lambda w, c: pl.BlockSpec((TAIL_TT, w), lambda i, c=c: (tile(i), c))
    const = lambda shape: pl.BlockSpec(shape, lambda i: (0, 0))
    return pl.pallas_call(
        _tail_kernel,
        grid=(t // TAIL_TT,),
        in_specs=[
            row(D_MODEL, 0),
            pl.BlockSpec((SSM_WIDTH // LANES, TAIL_TT * bsz, LANES), lambda i: (0, i // bsz, 0)),
            row(SSM_WIDTH, PZ_ZA // SSM_WIDTH),
            row(ATTN_WIDTH, 0),
            row(ATTN_WIDTH, PZ_ZB // ATTN_WIDTH),
            row(2 * D_MODEL, PZ_G // (2 * D_MODEL)),
            const((SSM_WIDTH, 2 * SSM_WIDTH)), const((1, 2 * SSM_WIDTH)),
            const((SSM_WIDTH, D_MODEL)), const((ATTN_WIDTH, D_MODEL)), const((D_MODEL, D_MODEL)),
        ],
        out_specs=pl.BlockSpec((TAIL_TT, D_MODEL), lambda i: (tile(i), 0)),
        out_shape=jax.ShapeDtypeStruct((t, D_MODEL), F32),
        compiler_params=pltpu.CompilerParams(
            dimension_semantics=("arbitrary",), vmem_limit_bytes=VMEM_LIMIT),
        name="tail",
    )(x2d, y_tb, pz2d, attn2d, pz2d, pz2d, w_glu, b_glu, w_ps, w_pa, w_out)


def _rope_tables(seq):
    half = ROPE_DIM // 2
    inv = ROPE_THETA ** (-np.arange(0, ROPE_DIM, 2, dtype=np.float64) / ROPE_DIM)
    ang = np.arange(seq, dtype=np.float64)[:, None] * inv[None, :]
    cos, sin = np.cos(ang).astype(np.float32), np.sin(ang).astype(np.float32)
    zeros = np.zeros((seq, HEAD_DIM - ROPE_DIM), np.float32)
    z8 = np.zeros((seq, half), np.float32)
    cos_h = np.concatenate([cos, cos, np.ones_like(zeros)], axis=1)
    sa_h = np.concatenate([-sin, z8, zeros], axis=1)
    sb_h = np.concatenate([z8, sin, zeros], axis=1)
    two = lambda t: np.concatenate([t, t], axis=1).astype(np.float32)
    return two(cos_h), two(sa_h), two(sb_h)


def kernel(x, norm_w, w_in, b_gate, q_norm_w, k_norm_w, ssm_lam_re, ssm_lam_im, ssm_log_dt,
           ssm_b_re, ssm_b_im, ssm_c_re, ssm_c_im, ssm_d, w_glu, b_glu,
           w_proj_ssm, w_proj_attn, w_out):
    bsz, seq, d_model = x.shape
    depth = norm_w.shape[0]
    assert d_model == D_MODEL and w_in.shape[-1] == IN_WIDTH
    assert bsz == SUBLANES, "S5 rows (chunk, batch) must fill whole sublane tiles"
    assert seq % IN_TT == 0 and seq % TAIL_TT == 0 and seq % CHUNK == 0
    assert IN_TT % GPS == 0 and TAIL_TT % GPS == 0 and CHUNK % GPS == 0
    assert all(seq % (d * QBLK) == 0 for d in DILATIONS) and (seq // QBLK) % 4 == 0
    cosf, sa, sb = _rope_tables(seq)
    ones_blk = jnp.asarray(np.kron(np.eye(2 * LANES // HEAD_DIM, dtype=np.float32),
                                   np.full((HEAD_DIM, HEAD_DIM), 1.0 / HEAD_DIM, np.float32)), BF16)
    for layer in range(depth):
        x2d = x.reshape(bsz * seq, D_MODEL)
        q_gain = q_norm_w[layer].astype(F32) * (LOG2E * HEAD_DIM ** -0.5)
        qk_w_row = jnp.concatenate([jnp.tile(q_gain, len(DILATIONS) * ATTN_SLOTS),
                                    jnp.tile(k_norm_w[layer].astype(F32), ATTN_SLOTS)])[None, :]
        pf2d, pz2d, u_tb = _in_proj(
            x2d, norm_w[layer][None, :].astype(F32), w_in, layer,
            b_gate[layer][None, :].astype(F32), cosf, sa, sb, qk_w_row, ones_blk, bsz, seq)
        attn = _attention(pf2d.reshape(bsz, seq, PF_WIDTH))

        toep, s5_in, s5_out, a_pow = _s5_prep(
            ssm_lam_re[layer].astype(F32), ssm_lam_im[layer].astype(F32),
            ssm_log_dt[layer].astype(F32), ssm_b_re[layer].astype(F32),
            ssm_b_im[layer].astype(F32), ssm_c_re[layer].astype(F32),
            ssm_c_im[layer].astype(F32))
        n_tiles = SSM_WIDTH // LANES
        y_tb = _s5(u_tb.reshape(n_tiles, seq // CHUNK, CHUNK * bsz, LANES),
                   ssm_d[layer].astype(F32).reshape(n_tiles, 1, LANES), toep, s5_in, s5_out,
                   a_pow, bsz)

        out2d = _tail(x2d, y_tb.reshape(n_tiles, seq * bsz, LANES),
                      attn.reshape(bsz * seq, ATTN_WIDTH), pz2d,
                      w_glu[layer].astype(BF16), b_glu[layer][None, :].astype(F32),
                      w_proj_ssm[layer].astype(BF16), w_proj_attn[layer].astype(BF16),
                      w_out[layer].astype(BF16), bsz, seq)
        x = out2d.reshape(bsz, seq, D_MODEL)
    return x
```

```python
import functools
import math

import jax
import jax.numpy as jnp
import numpy as np
from jax import lax
from jax.experimental import pallas as pl
from jax.experimental.pallas import tpu as pltpu

F32 = jnp.float32
BF16 = jnp.bfloat16

D_MODEL = 1024
SSM_WIDTH = 512
SSM_GROUP = 16
SSM_GROUPS = 32
SSM_STATE = 64
HEAD_DIM = 64
ATTN_SLOTS = 8
ATTN_WIDTH = 512
DILATIONS = (1, 4, 16)
BAND_HALF = 64
ROPE_THETA = 500000.0
ROPE_DIM = 16
EPS = 1e-6
NEG_INF = -1e30
IN_WIDTH = 6144
COL_U, COL_ZA, COL_Q, COL_K, COL_V, COL_ZB, COL_G = 0, 512, 1024, 2560, 3072, 3584, 4096

LANES = 128
SUBLANES = 8
VMEM_LIMIT = 56 * 1024 * 1024

CHUNK = 16
GP = 2
N_PAIRS = SSM_GROUPS // GP
PG = GP * SSM_GROUP
PW = CHUNK * PG
SW = 4 * LANES
GPS = LANES // PG


def _dot(a, b):
    return jnp.dot(a, b, preferred_element_type=F32)


def _dot_nt(a, b):
    return lax.dot_general(a, b, (((1,), (1,)), ((), ())), preferred_element_type=F32)


def _dot_nt_f32(a, b):
    a_hi, b_hi = a.astype(BF16), b.astype(BF16)
    a_lo = (a - a_hi.astype(F32)).astype(BF16)
    b_lo = (b - b_hi.astype(F32)).astype(BF16)
    return _dot_nt(a_hi, b_hi) + (_dot_nt(a_hi, b_lo) + _dot_nt(a_lo, b_hi))


def _rotate_rows(tile, forward):
    residue = lax.broadcasted_iota(jnp.int32, tile.shape, 0) % GPS
    out = tile
    for k in range(1, GPS):
        out = jnp.where(residue == k, pltpu.roll(tile, (k if forward else GPS - k) * PG, 1), out)
    return out


def _sigmoid(v):
    return 0.5 * jnp.tanh(0.5 * v) + 0.5


IN_TT = 512
IN_TN = 512
QK_ROWS = 256
LOG2E = math.log2(math.e)
PF_Q, PF_K, PF_V, PF_WIDTH = 0, 1536, 2048, 2560
PZ_G, PZ_ZA, PZ_ZB, PZ_WIDTH = 0, 2048, 2560, 3072
_Q_COLS = tuple((COL_Q + o, PF_Q + o) for o in range(0, COL_K - COL_Q, IN_TN))
_G_COLS = tuple((COL_G + o, PZ_G + o) for o in range(0, IN_WIDTH - COL_G, IN_TN))
F32_DEST = dict(((COL_K, PF_K), (COL_V, PF_V)) + _Q_COLS)
BF16_DEST = dict(((COL_ZA, PZ_ZA), (COL_ZB, PZ_ZB)) + _G_COLS)


W_STAGE_COLS = 256
W_STAGE_SLOTS = 3


def _in_proj_kernel(layer, x_ref, nw_ref, w_hbm, b_ref, cos_ref, sa_ref, sb_ref, qkw_ref,
                    ones_ref, pf_ref, pz_ref, utb_ref, w_ref, stage_ref, sem_ref):
    bsz = utb_ref.shape[1] // IN_TT
    batch = pl.program_id(0) % bsz

    def w_copy(j):
        return pltpu.make_async_copy(
            w_hbm.at[layer, :, pl.ds(j * W_STAGE_COLS, W_STAGE_COLS)],
            stage_ref.at[j % W_STAGE_SLOTS], sem_ref.at[j % W_STAGE_SLOTS])

    @pl.when(pl.program_id(0) == 0)
    def _():
        n_copies = IN_WIDTH // W_STAGE_COLS
        for j in range(W_STAGE_SLOTS):
            w_copy(j).start()
        for j in range(n_copies):
            w_copy(j).wait()
            w_ref[:, j * W_STAGE_COLS:(j + 1) * W_STAGE_COLS] = (
                stage_ref[j % W_STAGE_SLOTS].astype(BF16))
            if j + W_STAGE_SLOTS < n_copies:
                w_copy(j + W_STAGE_SLOTS).start()

    x = x_ref[...]
    var = jnp.mean(x * x, axis=-1, keepdims=True)
    h = (x * lax.rsqrt(var + EPS) * nw_ref[...]).astype(BF16)
    ones_blk = ones_ref[...]
    for c0 in range(0, IN_WIDTH, IN_TN):
        acc = _dot(h, w_ref[:, c0:c0 + IN_TN])
        if c0 in BF16_DEST:
            if c0 >= COL_G:
                acc = acc + b_ref[:, c0 - COL_G:c0 - COL_G + IN_TN]
            pz_ref[:, BF16_DEST[c0]:BF16_DEST[c0] + IN_TN] = acc.astype(BF16)
            continue
        if c0 == COL_U:
            for lt in range(SSM_WIDTH // LANES):
                utb_ref[lt, pl.ds(batch, IN_TT, stride=bsz), :] = _rotate_rows(
                    acc[:, lt * LANES:(lt + 1) * LANES], forward=True)
            continue
        d0 = F32_DEST[c0]
        if not COL_Q <= c0 < COL_V:
            pf_ref[:, d0:d0 + IN_TN] = acc
            continue
        for r0 in range(0, IN_TT, QK_ROWS):
            rows = slice(r0, r0 + QK_ROWS)
            for t2 in range(0, IN_TN, 2 * LANES):
                xq2 = acc[rows, t2:t2 + 2 * LANES]
                ms2 = _dot((xq2 * xq2).astype(BF16), ones_blk)
                xn2 = xq2 * lax.rsqrt(ms2 + EPS) * qkw_ref[:, c0 - COL_Q + t2:
                                                            c0 - COL_Q + t2 + 2 * LANES]
                for t in (0, LANES):
                    xn = xn2[:, t:t + LANES]
                    pf_ref[rows, d0 + t2 + t:d0 + t2 + t + LANES] = (
                        xn * cos_ref[rows, :]
                        + pltpu.roll(xn, LANES - ROPE_DIM // 2, 1) * sa_ref[rows, :]
                        + pltpu.roll(xn, ROPE_DIM // 2, 1) * sb_ref[rows, :])


def _token_tile(bsz, tiles_per_seq):
    return lambda i: (i % bsz) * tiles_per_seq + i // bsz


def _in_proj(x2d, norm_w, w_in, layer, b_gate_row, cosf, sa, sb, qk_w_row, ones_blk, bsz, seq):
    t = x2d.shape[0]
    tile = _token_tile(bsz, seq // IN_TT)
    const = lambda shape: pl.BlockSpec(shape, lambda i: (0, 0))
    rope = lambda: pl.BlockSpec((IN_TT, LANES), lambda i: (i // bsz, 0))
    return pl.pallas_call(
        functools.partial(_in_proj_kernel, layer),
        grid=(t // IN_TT,),
        in_specs=[
            pl.BlockSpec((IN_TT, D_MODEL), lambda i: (tile(i), 0)),
            const((1, D_MODEL)),
            pl.BlockSpec(memory_space=pl.ANY),
            const((1, IN_WIDTH - COL_G)),
            rope(), rope(), rope(),
            const((1, COL_V - COL_Q)),
            const((2 * LANES, 2 * LANES)),
        ],
        out_specs=[
            pl.BlockSpec((IN_TT, PF_WIDTH), lambda i: (tile(i), 0)),
            pl.BlockSpec((IN_TT, PZ_WIDTH), lambda i: (tile(i), 0)),
            pl.BlockSpec((SSM_WIDTH // LANES, IN_TT * bsz, LANES), lambda i: (0, i // bsz, 0)),
        ],
        out_shape=[
            jax.ShapeDtypeStruct((t, PF_WIDTH), F32),
            jax.ShapeDtypeStruct((t, PZ_WIDTH), BF16),
            jax.ShapeDtypeStruct((SSM_WIDTH // LANES, t, LANES), F32),
        ],
        scratch_shapes=[
            pltpu.VMEM((D_MODEL, IN_WIDTH), BF16),
            pltpu.VMEM((W_STAGE_SLOTS, D_MODEL, W_STAGE_COLS), w_in.dtype),
            pltpu.SemaphoreType.DMA((W_STAGE_SLOTS,)),
        ],
        compiler_params=pltpu.CompilerParams(
            dimension_semantics=("arbitrary",), vmem_limit_bytes=VMEM_LIMIT),
        name="in_proj",
    )(x2d, norm_w, w_in, b_gate_row, cosf, sa, sb, qk_w_row, ones_blk)


QBLK = 128
KWIN = QBLK + 2 * BAND_HALF
NORM_ROWS = 256
MERGE_ROWS = 64
LOOKAHEAD = 3
SCORE_SLOTS = LOOKAHEAD + 2


def _attn_kernel(q0_ref, q1_ref, q2_ref, k_ref, v_ref, bias_ref, o_ref,
                 va_s, vb_s, oacc_s, den_s, max_s, s_s):
    seq = k_ref.shape[1]
    q_refs = (q0_ref, q1_ref, q2_ref)
    head0 = lax.broadcasted_iota(jnp.int32, (QBLK, LANES), 1) < HEAD_DIM
    head0_rows = lax.broadcasted_iota(jnp.int32, (NORM_ROWS, LANES), 1) < HEAD_DIM

    def prep(i, carry):
        rows = pl.ds(pl.multiple_of(i * NORM_ROWS, NORM_ROWS), NORM_ROWS)
        v = v_ref[0, rows, :]
        va_s[rows, :] = jnp.where(head0_rows, v, 1.0)
        vb_s[rows, :] = jnp.where(head0_rows, 1.0, v)
        return carry

    lax.fori_loop(0, seq // NORM_ROWS, prep, 0, unroll=True)

    blocks = []
    for p, d in enumerate(DILATIONS):
        n = seq // d
        nblk = n // QBLK
        kw = min(n, KWIN)

        def rows_of(idx, d=d, n=n, nblk=nblk, kw=kw):
            r, q0 = idx // nblk, (idx % nblk) * QBLK
            ks = min(max(q0 - BAND_HALF, 0), n - kw)
            return (pl.ds(r + d * q0, QBLK, stride=d), pl.ds(r + d * ks, kw, stride=d),
                    (q0 - ks) // BAND_HALF)

        def scores(idx, slot, p=p, kw=kw, rows_of=rows_of):
            qrows, krows, case = rows_of(idx)
            qb = q_refs[p][0, qrows, :]
            kb = k_ref[0, krows, :].astype(BF16)
            bias = bias_ref[case, :, :kw]
            s_s[slot, :QBLK, :kw] = _dot_nt(jnp.where(head0, qb, 0.0).astype(BF16), kb) + bias
            s_s[slot, QBLK:, :kw] = _dot_nt(jnp.where(head0, 0.0, qb).astype(BF16), kb) + bias

        def weigh(idx, slot, p=p, kw=kw, rows_of=rows_of):
            qrows, krows, _ = rows_of(idx)
            s = s_s[slot, :, :kw]
            m = jnp.max(s, axis=-1, keepdims=True)
            e = jnp.exp2(s - m).astype(BF16)
            o_a = _dot(e[:QBLK], va_s[krows, :].astype(BF16))
            o_b = _dot(e[QBLK:], vb_s[krows, :].astype(BF16))
            oacc_s[p, qrows, :] = jnp.where(head0, o_a, o_b)
            den_s[p, qrows, :] = jnp.where(head0, o_b, o_a)
            max_s[p, qrows, :] = jnp.where(head0, jnp.broadcast_to(m[:QBLK], (QBLK, LANES)),
                                           jnp.broadcast_to(m[QBLK:], (QBLK, LANES)))

        blocks += [(scores, weigh, i) for i in range(d * nblk)]

    for g in range(LOOKAHEAD):
        blocks[g][0](blocks[g][2], g % SCORE_SLOTS)
    for g, (_, weigh, i) in enumerate(blocks):
        if g + LOOKAHEAD < len(blocks):
            ahead_scores, _, ahead_i = blocks[g + LOOKAHEAD]
            ahead_scores(ahead_i, (g + LOOKAHEAD) % SCORE_SLOTS)
        weigh(i, g % SCORE_SLOTS)

    def combine(i, carry):
        rows = pl.ds(pl.multiple_of(i * MERGE_ROWS, MERGE_ROWS), MERGE_ROWS)
        m0, m1, m2 = max_s[0, rows, :], max_s[1, rows, :], max_s[2, rows, :]
        m = jnp.maximum(jnp.maximum(m0, m1), m2)
        w0, w1, w2 = jnp.exp2(m0 - m), jnp.exp2(m1 - m), jnp.exp2(m2 - m)
        num = w0 * oacc_s[0, rows, :] + w1 * oacc_s[1, rows, :] + w2 * oacc_s[2, rows, :]
        d0, d1, d2 = (pltpu.roll(den_s[p, rows, :], HEAD_DIM, 1) for p in range(3))
        o_ref[0, rows, :] = (num / (w0 * d0 + w1 * d1 + w2 * d2)).astype(o_ref.dtype)
        return carry

    lax.fori_loop(0, seq // MERGE_ROWS, combine, 0, unroll=True)


def _band_bias():
    i = np.arange(QBLK)[:, None]
    j = np.arange(KWIN)[None, :]
    off = np.arange(3)[:, None, None] * BAND_HALF
    return np.where(np.abs(j - i - off) <= BAND_HALF, 0.0, NEG_INF).astype(np.float32)


def _attention(proj3d):
    bsz, seq, _ = proj3d.shape
    n_pairs = ATTN_WIDTH // LANES

    def qspec(p):
        return pl.BlockSpec((1, seq, LANES),
                            lambda b, hp, p=p: (b, 0, PF_Q // LANES + p * n_pairs + hp))

    seq_tile = lambda: pltpu.VMEM((seq, LANES), F32)
    pat_tile = lambda: pltpu.VMEM((len(DILATIONS), seq, LANES), F32)
    return pl.pallas_call(
        _attn_kernel,
        grid=(bsz, n_pairs),
        in_specs=[
            qspec(0), qspec(1), qspec(2),
            pl.BlockSpec((1, seq, LANES), lambda b, hp: (b, 0, PF_K // LANES + hp)),
            pl.BlockSpec((1, seq, LANES), lambda b, hp: (b, 0, PF_V // LANES + hp)),
            pl.BlockSpec((3, QBLK, KWIN), lambda b, hp: (0, 0, 0)),
        ],
        out_specs=pl.BlockSpec((1, seq, LANES), lambda b, hp: (b, 0, hp)),
        out_shape=jax.ShapeDtypeStruct((bsz, seq, ATTN_WIDTH), BF16),
        scratch_shapes=[
            seq_tile(), seq_tile(),
            pat_tile(), pat_tile(), pat_tile(),
            pltpu.VMEM((SCORE_SLOTS, 2 * QBLK, KWIN), F32),
        ],
        compiler_params=pltpu.CompilerParams(
            dimension_semantics=("arbitrary", "arbitrary"), vmem_limit_bytes=VMEM_LIMIT),
        name="dilated_attention",
    )(proj3d, proj3d, proj3d, proj3d, proj3d, _band_bias())


LAG_REP = LANES // PG


def _s5_prep_kernel(lre_ref, lim_ref, ldt_ref, bre_ref, bim_ref, cre_ref, cim_ref,
                    toep_ref, win_ref, wout_ref, a_ref, wf32_s):
    four = lambda ref: jnp.concatenate([ref[0, 0], ref[0, 0], ref[1, 0], ref[1, 0]], axis=1)
    lr = jnp.minimum(four(lre_ref), -1e-4)
    li = four(lim_ref)
    dt = jnp.exp(four(ldt_ref))
    col = lax.broadcasted_iota(jnp.int32, (1, SW), 1)
    is_re = (col // LANES) % 2 == 0
    is_fwd = col < 2 * LANES

    mag = jnp.exp(lr * dt)
    pw_re = [jnp.ones_like(mag), mag * jnp.cos(li * dt)]
    pw_im = [jnp.zeros_like(mag), mag * jnp.sin(li * dt)]
    for _ in range(CHUNK - 1):
        pw_re.append(pw_re[-1] * pw_re[1] - pw_im[-1] * pw_im[1])
        pw_im.append(pw_re[-2] * pw_im[1] + pw_im[-1] * pw_re[1])

    a_re, a_im = pw_re[1], pw_im[1]
    nr, ni, mag2 = a_re - 1.0, a_im, lr * lr + li * li
    f_re, f_im = (nr * lr + ni * li) / mag2, (ni * lr - nr * li) / mag2

    row_g = lax.broadcasted_iota(jnp.int32, (PG, SW), 0) // SSM_GROUP
    col_g = (lax.broadcasted_iota(jnp.int32, (PG, SW), 1) % LANES) // SSM_STATE
    diag = row_g == col_g
    wide = lambda ref: jnp.concatenate([ref[0, 0]] * (2 * GP) + [ref[1, 0]] * (2 * GP), axis=1)
    b_re = jnp.where(diag, wide(bre_ref), 0.0)
    b_im = jnp.where(diag, wide(bim_ref), 0.0)
    c_re = jnp.where(diag, wide(cre_ref), 0.0)
    c_im = jnp.where(diag, wide(cim_ref), 0.0)
    bb_re = f_re * b_re - f_im * b_im
    bb_im = f_re * b_im + f_im * b_re

    for s in range(CHUNK):
        rows = slice(s * PG, (s + 1) * PG)
        in_re = jnp.where(is_fwd, pw_re[CHUNK - 1 - s], pw_re[s])
        in_im = jnp.where(is_fwd, pw_im[CHUNK - 1 - s], pw_im[s])
        w = (bb_re * jnp.where(is_re, in_re, in_im)
             + bb_im * jnp.where(is_re, -in_im, in_re))
        wf32_s[rows, :] = w
        win_ref[0, rows, :] = w.astype(BF16)
        out_re = jnp.where(is_fwd, pw_re[s + 1], pw_re[CHUNK - s])
        out_im = jnp.where(is_fwd, pw_im[s + 1], pw_im[CHUNK - s])
        wout_ref[0, rows, :] = (c_re * jnp.where(is_re, out_re, -out_im)
                                + c_im * jnp.where(is_re, -out_im, -out_re)).astype(BF16)

    a_ref[0] = jnp.broadcast_to(jnp.where(is_re, pw_re[CHUNK], pw_im[CHUNK]), a_ref.shape[1:])

    c_cat = jnp.where(is_re, c_re, -c_im)
    c_rep = jnp.concatenate([c_cat] * LAG_REP, axis=0)
    half = 2 * LANES
    lag_f = _dot_nt_f32(wf32_s[:, :half], c_rep[:, :half])
    lag_b = _dot_nt_f32(wf32_s[:, half:], c_rep[:, half:])
    row_t = lax.broadcasted_iota(jnp.int32, (PW, LANES), 0) // PG
    lane_q = lax.broadcasted_iota(jnp.int32, (PW, LANES), 1) // PG
    zeros = lambda n: jnp.zeros((n * PG, LANES), F32)
    for q in range(PW // LANES):
        tile = jnp.zeros((PW, LANES), F32)
        for u in range(LAG_REP):
            t = q * LAG_REP + u
            up = CHUNK - 1 - t
            sh_f = lag_f if up == 0 else jnp.concatenate([lag_f[up * PG:], zeros(up)], axis=0)
            sh_b = lag_b if t == 0 else jnp.concatenate([zeros(t), lag_b[:(CHUNK - t) * PG]], axis=0)
            col_t = jnp.where(row_t <= t, sh_f, 0.0) + jnp.where(row_t >= t, sh_b, 0.0)
            tile = jnp.where(lane_q == u, col_t, tile)
        toep_ref[0, :, q * LANES:(q + 1) * LANES] = tile.astype(BF16)


def _s5_prep(lam_re, lam_im, log_dt, b_re, b_im, c_re, c_im):
    lanes = lambda t: t.reshape(2, N_PAIRS, 1, LANES)
    dt_b = jnp.broadcast_to(log_dt[..., None], lam_re.shape)

    tiles = lambda t_ghp: t_ghp.reshape(2, N_PAIRS, PG, SSM_STATE)

    row = pl.BlockSpec((2, 1, 1, LANES), lambda i: (0, i, 0, 0))
    mat = pl.BlockSpec((2, 1, PG, SSM_STATE), lambda i: (0, i, 0, 0))
    out = lambda r: pl.BlockSpec((1, r, SW), lambda i: (i, 0, 0))
    return pl.pallas_call(
        _s5_prep_kernel,
        grid=(N_PAIRS,),
        in_specs=[row, row, row, mat, mat, mat, mat],
        out_specs=[out(PW), out(PW), out(PW), out(SUBLANES)],
        out_shape=[jax.ShapeDtypeStruct((N_PAIRS, PW, PW), BF16),
                   jax.ShapeDtypeStruct((N_PAIRS, PW, SW), BF16),
                   jax.ShapeDtypeStruct((N_PAIRS, PW, SW), BF16),
                   jax.ShapeDtypeStruct((N_PAIRS, SUBLANES, SW), F32)],
        scratch_shapes=[pltpu.VMEM((PW, SW), F32)],
        compiler_params=pltpu.CompilerParams(dimension_semantics=("arbitrary",)),
        name="s5_prep",
    )(lanes(lam_re), lanes(lam_im), lanes(dt_b), tiles(b_re.transpose(0, 1, 3, 2)),
      tiles(b_im.transpose(0, 1, 3, 2)), tiles(c_re), tiles(c_im))


SCAN_STEPS = 8
REGROUP_CHUNKS = 32


def _s5_kernel(u_ref, d_ref, toep_ref, win_ref, wout_ref, a_ref, y_ref, x_s, p_s, st_s):
    nc = u_ref.shape[1]
    bsz = u_ref.shape[2] // CHUNK
    piece = REGROUP_CHUNKS * bsz
    pp = pl.program_id(1)
    slot = lax.broadcasted_iota(jnp.int32, (piece, LANES), 1) // PG
    time_rows = lambda t: slice(t * bsz, (t + 1) * bsz)

    def gather(cb, carry):
        chunks = pl.ds(pl.multiple_of(cb * REGROUP_CHUNKS, REGROUP_CHUNKS), REGROUP_CHUNKS)
        rows = pl.ds(pl.multiple_of(cb * piece, piece), piece)
        for q in range(CHUNK // GPS):
            acc = None
            for k in range(GPS):
                v = u_ref[0, chunks, time_rows(q * GPS + k), :].reshape(piece, LANES)
                acc = v if acc is None else jnp.where(slot == (pp + k) % GPS, v, acc)
            x_s[rows, q * LANES:(q + 1) * LANES] = pltpu.roll(
                acc, ((GPS - pp) % GPS) * PG, 1).astype(BF16)
        return carry

    lax.fori_loop(0, nc // REGROUP_CHUNKS, gather, 0)

    x = x_s[...]
    p_s[...] = _dot(x, win_ref[0])
    a_fr, a_fi, a_br, a_bi = (a_ref[0, :, q * LANES:(q + 1) * LANES] for q in range(4))
    tile = lambda q: slice(q * LANES, (q + 1) * LANES)

    def scan(i, carry):
        base_f = pl.multiple_of(i * (SCAN_STEPS * bsz), SCAN_STEPS * bsz)
        base_b = pl.multiple_of((nc - (i + 1) * SCAN_STEPS) * bsz, SCAN_STEPS * bsz)
        s_fr, s_fi, s_br, s_bi = carry
        for k in range(SCAN_STEPS):
            rf = pl.ds(base_f + k * bsz, bsz)
            rb = pl.ds(base_b + (SCAN_STEPS - 1 - k) * bsz, bsz)
            st_s[rf, tile(0)] = s_fr
            st_s[rf, tile(1)] = s_fi
            st_s[rb, tile(2)] = s_br
            st_s[rb, tile(3)] = s_bi
            s_fr, s_fi, s_br, s_bi = (a_fr * s_fr - a_fi * s_fi + p_s[rf, tile(0)],
                                      a_fr * s_fi + a_fi * s_fr + p_s[rf, tile(1)],
                                      a_br * s_br - a_bi * s_bi + p_s[rb, tile(2)],
                                      a_br * s_bi + a_bi * s_br + p_s[rb, tile(3)])
        return s_fr, s_fi, s_br, s_bi

    zero = jnp.zeros((bsz, LANES), F32)
    lax.fori_loop(0, nc // SCAN_STEPS, scan, (zero,) * 4)
    p_s[...] = _dot(x, toep_ref[0]) + _dot_nt(st_s[...].astype(BF16), wout_ref[0])

    d_tile = jnp.broadcast_to(d_ref[0], (bsz, LANES))
    d_skip = [d_tile if k == 0 else pltpu.roll(d_tile, k * PG, 1) for k in range(GPS)]

    def scatter(cb, carry):
        chunks = pl.ds(pl.multiple_of(cb * REGROUP_CHUNKS, REGROUP_CHUNKS), REGROUP_CHUNKS)
        rows = pl.ds(pl.multiple_of(cb * piece, piece), piece)
        for q in range(CHUNK // GPS):
            w = pltpu.roll(p_s[rows, q * LANES:(q + 1) * LANES], pp * PG, 1)
            w = w.reshape(1, REGROUP_CHUNKS, bsz, LANES)
            for k in range(GPS):
                where = (pl.ds(0, 1), chunks, time_rows(q * GPS + k), slice(None))
                mine = (slot == (pp + k) % GPS).reshape(1, REGROUP_CHUNKS, bsz, LANES)
                pltpu.store(y_ref.at[where], w + d_skip[k] * u_ref[where], mask=mine)
        return carry

    lax.fori_loop(0, nc // REGROUP_CHUNKS, scatter, 0)


def _s5(u_tb, d_tiles, toep, w_in, w_out, a_pow, bsz):
    n_tiles, nc, cb, _ = u_tb.shape
    rows = nc * bsz
    tile = pl.BlockSpec((1, nc, cb, LANES), lambda j, p: (j, 0, 0, 0))
    pair = lambda r, c: pl.BlockSpec((1, r, c), lambda j, p: (j * GPS + p, 0, 0))
    return pl.pallas_call(
        _s5_kernel,
        grid=(n_tiles, GPS),
        in_specs=[tile, pl.BlockSpec((1, 1, LANES), lambda j, p: (j, 0, 0)),
                  pair(PW, PW), pair(PW, SW), pair(PW, SW), pair(bsz, SW)],
        out_specs=tile,
        out_shape=jax.ShapeDtypeStruct(u_tb.shape, F32),
        scratch_shapes=[pltpu.VMEM((rows, PW), BF16),
                        pltpu.VMEM((rows, SW), F32),
                        pltpu.VMEM((rows, SW), F32)],
        compiler_params=pltpu.CompilerParams(
            dimension_semantics=("arbitrary", "arbitrary"), vmem_limit_bytes=VMEM_LIMIT),
        name="s5_chunked",
    )(u_tb, d_tiles, toep, w_in, w_out, a_pow)


TAIL_TT = 512


TAIL_W_ROWS = 256


def _tail_kernel(layer, x_ref, ytb_ref, za_ref, at_ref, zb_ref, g_ref, wg_hbm, bg_ref,
                 wps_hbm, wpa_hbm, wo_hbm, o_ref, wg_ref, wps_ref, wpa_ref, wo_ref,
                 stage_ref, sem_ref):
    bsz = ytb_ref.shape[1] // TAIL_TT
    batch = pl.program_id(0) % bsz
    pieces = [(src, dst, r0)
              for src, dst in ((wg_hbm, wg_ref), (wpa_hbm, wpa_ref), (wps_hbm, wps_ref),
                               (wo_hbm, wo_ref))
              for r0 in range(0, dst.shape[0], TAIL_W_ROWS)]

    def w_copy(j):
        src, _, r0 = pieces[j]
        return pltpu.make_async_copy(src.at[layer, pl.ds(r0, TAIL_W_ROWS), :],
                                     stage_ref.at[j % W_STAGE_SLOTS],
                                     sem_ref.at[j % W_STAGE_SLOTS])

    @pl.when(pl.program_id(0) == 0)
    def _():
        for j in range(W_STAGE_SLOTS):
            w_copy(j).start()
        for j, (_, dst, r0) in enumerate(pieces):
            w_copy(j).wait()
            dst[r0:r0 + TAIL_W_ROWS, :] = stage_ref[j % W_STAGE_SLOTS].astype(BF16)
            if j + W_STAGE_SLOTS < len(pieces):
                w_copy(j + W_STAGE_SLOTS).start()

    y_s5 = jnp.concatenate(
        [_rotate_rows(ytb_ref[lt, pl.ds(batch, TAIL_TT, stride=bsz), :], forward=False)
         for lt in range(SSM_WIDTH // LANES)], axis=1)
    ys = jax.nn.gelu(y_s5).astype(BF16)
    glu = _dot(ys, wg_ref[...]) + bg_ref[...]
    z_b = zb_ref[...].astype(F32)
    y_b = _dot((at_ref[...].astype(F32) * (z_b * _sigmoid(z_b))).astype(BF16), wpa_ref[...])
    z_a = za_ref[...].astype(F32)
    a_in = glu[:, :SSM_WIDTH] * _sigmoid(glu[:, SSM_WIDTH:]) * (z_a * _sigmoid(z_a))
    y_a = _dot(a_in.astype(BF16), wps_ref[...])
    g = g_ref[...].astype(F32)
    mix = _sigmoid(g[:, :D_MODEL]) * y_a + _sigmoid(g[:, D_MODEL:]) * y_b
    o_ref[...] = x_ref[...] + _dot(mix.astype(BF16), wo_ref[...])


def _tail(x2d, y_tb, attn2d, pz2d, layer, w_glu, b_glu, w_ps, w_pa, w_out, bsz, seq):
    t = x2d.shape[0]
    tile = _token_tile(bsz, seq // TAIL_TT)
    row = lambda w, c: pl.BlockSpec((TAIL_TT, w), lambda i, c=c: (tile(i), c))
    const = lambda shape: pl.BlockSpec(shape, lambda i: (0, 0))
    hbm = pl.BlockSpec(memory_space=pl.ANY)
    weights = (w_glu, w_ps, w_pa, w_out)
    assert all(w.dtype == w_glu.dtype and w.shape[-1] == D_MODEL for w in weights)
    assert 2 * SSM_WIDTH == D_MODEL and SSM_WIDTH % TAIL_W_ROWS == 0
    return pl.pallas_call(
        functools.partial(_tail_kernel, layer),
        grid=(t // TAIL_TT,),
        in_specs=[
            row(D_MODEL, 0),
            pl.BlockSpec((SSM_WIDTH // LANES, TAIL_TT * bsz, LANES), lambda i: (0, i // bsz, 0)),
            row(SSM_WIDTH, PZ_ZA // SSM_WIDTH),
            row(ATTN_WIDTH, 0),
            row(ATTN_WIDTH, PZ_ZB // ATTN_WIDTH),
            row(2 * D_MODEL, PZ_G // (2 * D_MODEL)),
            hbm, const((1, 2 * SSM_WIDTH)), hbm, hbm, hbm,
        ],
        out_specs=pl.BlockSpec((TAIL_TT, D_MODEL), lambda i: (tile(i), 0)),
        out_shape=jax.ShapeDtypeStruct((t, D_MODEL), F32),
        scratch_shapes=[
            pltpu.VMEM((SSM_WIDTH, 2 * SSM_WIDTH), BF16), pltpu.VMEM((SSM_WIDTH, D_MODEL), BF16),
            pltpu.VMEM((ATTN_WIDTH, D_MODEL), BF16), pltpu.VMEM((D_MODEL, D_MODEL), BF16),
            pltpu.VMEM((W_STAGE_SLOTS, TAIL_W_ROWS, D_MODEL), w_glu.dtype),
            pltpu.SemaphoreType.DMA((W_STAGE_SLOTS,)),
        ],
        compiler_params=pltpu.CompilerParams(
            dimension_semantics=("arbitrary",), vmem_limit_bytes=VMEM_LIMIT),
        name="tail",
    )(x2d, y_tb, pz2d, attn2d, pz2d, pz2d, w_glu, b_glu, w_ps, w_pa, w_out)


def _rope_tables(seq):
    half = ROPE_DIM // 2
    inv = ROPE_THETA ** (-np.arange(0, ROPE_DIM, 2, dtype=np.float64) / ROPE_DIM)
    ang = np.arange(seq, dtype=np.float64)[:, None] * inv[None, :]
    cos, sin = np.cos(ang).astype(np.float32), np.sin(ang).astype(np.float32)
    zeros = np.zeros((seq, HEAD_DIM - ROPE_DIM), np.float32)
    z8 = np.zeros((seq, half), np.float32)
    cos_h = np.concatenate([cos, cos, np.ones_like(zeros)], axis=1)
    sa_h = np.concatenate([-sin, z8, zeros], axis=1)
    sb_h = np.concatenate([z8, sin, zeros], axis=1)
    two = lambda t: np.concatenate([t, t], axis=1).astype(np.float32)
    return two(cos_h), two(sa_h), two(sb_h)


def kernel(x, norm_w, w_in, b_gate, q_norm_w, k_norm_w, ssm_lam_re, ssm_lam_im, ssm_log_dt,
           ssm_b_re, ssm_b_im, ssm_c_re, ssm_c_im, ssm_d, w_glu, b_glu,
           w_proj_ssm, w_proj_attn, w_out):
    bsz, seq, d_model = x.shape
    depth = norm_w.shape[0]
    assert d_model == D_MODEL and w_in.shape[-1] == IN_WIDTH
    assert bsz == SUBLANES, "S5 rows (chunk, batch) must fill whole sublane tiles"
    assert seq % IN_TT == 0 and seq % TAIL_TT == 0 and seq % CHUNK == 0
    assert IN_TT % GPS == 0 and TAIL_TT % GPS == 0 and CHUNK % GPS == 0
    assert all(seq % (d * QBLK) == 0 for d in DILATIONS) and (seq // QBLK) % 4 == 0
    cosf, sa, sb = _rope_tables(seq)
    ones_blk = jnp.asarray(np.kron(np.eye(2 * LANES // HEAD_DIM, dtype=np.float32),
                                   np.full((HEAD_DIM, HEAD_DIM), 1.0 / HEAD_DIM, np.float32)), BF16)
    for layer in range(depth):
        x2d = x.reshape(bsz * seq, D_MODEL)
        q_gain = q_norm_w[layer].astype(F32) * (LOG2E * HEAD_DIM ** -0.5)
        qk_w_row = jnp.concatenate([jnp.tile(q_gain, len(DILATIONS) * ATTN_SLOTS),
                                    jnp.tile(k_norm_w[layer].astype(F32), ATTN_SLOTS)])[None, :]
        pf2d, pz2d, u_tb = _in_proj(
            x2d, norm_w[layer][None, :].astype(F32), w_in, layer,
            b_gate[layer][None, :].astype(F32), cosf, sa, sb, qk_w_row, ones_blk, bsz, seq)
        attn = _attention(pf2d.reshape(bsz, seq, PF_WIDTH))

        toep, s5_in, s5_out, a_pow = _s5_prep(
            ssm_lam_re[layer].astype(F32), ssm_lam_im[layer].astype(F32),
            ssm_log_dt[layer].astype(F32), ssm_b_re[layer].astype(F32),
            ssm_b_im[layer].astype(F32), ssm_c_re[layer].astype(F32),
            ssm_c_im[layer].astype(F32))
        n_tiles = SSM_WIDTH // LANES
        y_tb = _s5(u_tb.reshape(n_tiles, seq // CHUNK, CHUNK * bsz, LANES),
                   ssm_d[layer].astype(F32).reshape(n_tiles, 1, LANES), toep, s5_in, s5_out,
                   a_pow, bsz)

        out2d = _tail(x2d, y_tb.reshape(n_tiles, seq * bsz, LANES),
                      attn.reshape(bsz * seq, ATTN_WIDTH), pz2d, layer,
                      w_glu, b_glu[layer][None, :].astype(F32),
                      w_proj_ssm, w_proj_attn, w_out, bsz, seq)
        x = out2d.reshape(bsz, seq, D_MODEL)
    return x
```

```python
import functools
import math

import jax
import jax.numpy as jnp
import numpy as np
from jax import lax
from jax.experimental import pallas as pl
from jax.experimental.pallas import tpu as pltpu

F32 = jnp.float32
BF16 = jnp.bfloat16

D_MODEL = 1024
SSM_WIDTH = 512
SSM_GROUP = 16
SSM_GROUPS = 32
SSM_STATE = 64
HEAD_DIM = 64
ATTN_SLOTS = 8
ATTN_WIDTH = 512
DILATIONS = (1, 4, 16)
BAND_HALF = 64
ROPE_THETA = 500000.0
ROPE_DIM = 16
EPS = 1e-6
NEG_INF = -1e30
IN_WIDTH = 6144
COL_U, COL_ZA, COL_Q, COL_K, COL_V, COL_ZB, COL_G = 0, 512, 1024, 2560, 3072, 3584, 4096

LANES = 128
SUBLANES = 8
VMEM_LIMIT = 56 * 1024 * 1024

CHUNK = 16
GP = 2
N_PAIRS = SSM_GROUPS // GP
PG = GP * SSM_GROUP
PW = CHUNK * PG
SW = 4 * LANES
GPS = LANES // PG


def _dot(a, b):
    return jnp.dot(a, b, preferred_element_type=F32)


def _dot_nt(a, b):
    return lax.dot_general(a, b, (((1,), (1,)), ((), ())), preferred_element_type=F32)


def _dot_nt_f32(a, b):
    a_hi, b_hi = a.astype(BF16), b.astype(BF16)
    a_lo = (a - a_hi.astype(F32)).astype(BF16)
    b_lo = (b - b_hi.astype(F32)).astype(BF16)
    return _dot_nt(a_hi, b_hi) + (_dot_nt(a_hi, b_lo) + _dot_nt(a_lo, b_hi))


def _rotate_rows(tile, forward):
    residue = lax.broadcasted_iota(jnp.int32, tile.shape, 0) % GPS
    out = tile
    for k in range(1, GPS):
        out = jnp.where(residue == k, pltpu.roll(tile, (k if forward else GPS - k) * PG, 1), out)
    return out


def _sigmoid(v):
    return 0.5 * jnp.tanh(0.5 * v) + 0.5


IN_TT = 512
IN_TN = 512
QK_ROWS = 256
LOG2E = math.log2(math.e)
PF_Q, PF_K, PF_V, PF_WIDTH = 0, 1536, 2048, 2560
PZ_G, PZ_ZA, PZ_ZB, PZ_WIDTH = 0, 2048, 2560, 3072
_Q_COLS = tuple((COL_Q + o, PF_Q + o) for o in range(0, COL_K - COL_Q, IN_TN))
_G_COLS = tuple((COL_G + o, PZ_G + o) for o in range(0, IN_WIDTH - COL_G, IN_TN))
F32_DEST = dict(((COL_K, PF_K), (COL_V, PF_V)) + _Q_COLS)
BF16_DEST = dict(((COL_ZA, PZ_ZA), (COL_ZB, PZ_ZB)) + _G_COLS)


W_STAGE_ROWS = 32
W_STAGE_SLOTS = 3


def _in_proj_kernel(layer, x_ref, nw_ref, w_hbm, b_ref, cos_ref, sa_ref, sb_ref, qkw_ref,
                    ones_ref, pf_ref, pz_ref, utb_ref, w_ref, stage_ref, sem_ref):
    bsz = utb_ref.shape[1] // IN_TT
    batch = pl.program_id(0) % bsz

    def w_copy(j):
        return pltpu.make_async_copy(
            w_hbm.at[layer, pl.ds(j * W_STAGE_ROWS, W_STAGE_ROWS), :],
            stage_ref.at[j % W_STAGE_SLOTS], sem_ref.at[j % W_STAGE_SLOTS])

    @pl.when(pl.program_id(0) == 0)
    def _():
        n_copies = D_MODEL // W_STAGE_ROWS
        for j in range(W_STAGE_SLOTS):
            w_copy(j).start()
        for j in range(n_copies):
            w_copy(j).wait()
            w_ref[j * W_STAGE_ROWS:(j + 1) * W_STAGE_ROWS, :] = (
                stage_ref[j % W_STAGE_SLOTS].astype(BF16))
            if j + W_STAGE_SLOTS < n_copies:
                w_copy(j + W_STAGE_SLOTS).start()

    x = x_ref[...]
    var = jnp.mean(x * x, axis=-1, keepdims=True)
    h = (x * lax.rsqrt(var + EPS) * nw_ref[...]).astype(BF16)
    ones_blk = ones_ref[...]
    for c0 in range(0, IN_WIDTH, IN_TN):
        acc = _dot(h, w_ref[:, c0:c0 + IN_TN])
        if c0 in BF16_DEST:
            if c0 >= COL_G:
                acc = acc + b_ref[:, c0 - COL_G:c0 - COL_G + IN_TN]
            pz_ref[:, BF16_DEST[c0]:BF16_DEST[c0] + IN_TN] = acc.astype(BF16)
            continue
        if c0 == COL_U:
            for lt in range(SSM_WIDTH // LANES):
                utb_ref[lt, pl.ds(batch, IN_TT, stride=bsz), :] = _rotate_rows(
                    acc[:, lt * LANES:(lt + 1) * LANES], forward=True)
            continue
        d0 = F32_DEST[c0]
        if not COL_Q <= c0 < COL_V:
            pf_ref[:, d0:d0 + IN_TN] = acc
            continue
        for r0 in range(0, IN_TT, QK_ROWS):
            rows = slice(r0, r0 + QK_ROWS)
            for t2 in range(0, IN_TN, 2 * LANES):
                xq2 = acc[rows, t2:t2 + 2 * LANES]
                ms2 = _dot((xq2 * xq2).astype(BF16), ones_blk)
                xn2 = xq2 * lax.rsqrt(ms2 + EPS) * qkw_ref[:, c0 - COL_Q + t2:
                                                            c0 - COL_Q + t2 + 2 * LANES]
                for t in (0, LANES):
                    xn = xn2[:, t:t + LANES]
                    pf_ref[rows, d0 + t2 + t:d0 + t2 + t + LANES] = (
                        xn * cos_ref[rows, :]
                        + pltpu.roll(xn, LANES - ROPE_DIM // 2, 1) * sa_ref[rows, :]
                        + pltpu.roll(xn, ROPE_DIM // 2, 1) * sb_ref[rows, :])


def _token_tile(bsz, tiles_per_seq):
    return lambda i: (i % bsz) * tiles_per_seq + i // bsz


def _in_proj(x2d, norm_w, w_in, layer, b_gate_row, cosf, sa, sb, qk_w_row, ones_blk, bsz, seq):
    t = x2d.shape[0]
    tile = _token_tile(bsz, seq // IN_TT)
    const = lambda shape: pl.BlockSpec(shape, lambda i: (0, 0))
    rope = lambda: pl.BlockSpec((IN_TT, LANES), lambda i: (i // bsz, 0))
    return pl.pallas_call(
        functools.partial(_in_proj_kernel, layer),
        grid=(t // IN_TT,),
        in_specs=[
            pl.BlockSpec((IN_TT, D_MODEL), lambda i: (tile(i), 0)),
            const((1, D_MODEL)),
            pl.BlockSpec(memory_space=pl.ANY),
            const((1, IN_WIDTH - COL_G)),
            rope(), rope(), rope(),
            const((1, COL_V - COL_Q)),
            const((2 * LANES, 2 * LANES)),
        ],
        out_specs=[
            pl.BlockSpec((IN_TT, PF_WIDTH), lambda i: (tile(i), 0)),
            pl.BlockSpec((IN_TT, PZ_WIDTH), lambda i: (tile(i), 0)),
            pl.BlockSpec((SSM_WIDTH // LANES, IN_TT * bsz, LANES), lambda i: (0, i // bsz, 0)),
        ],
        out_shape=[
            jax.ShapeDtypeStruct((t, PF_WIDTH), F32),
            jax.ShapeDtypeStruct((t, PZ_WIDTH), BF16),
            jax.ShapeDtypeStruct((SSM_WIDTH // LANES, t, LANES), F32),
        ],
        scratch_shapes=[
            pltpu.VMEM((D_MODEL, IN_WIDTH), BF16),
            pltpu.VMEM((W_STAGE_SLOTS, W_STAGE_ROWS, IN_WIDTH), w_in.dtype),
            pltpu.SemaphoreType.DMA((W_STAGE_SLOTS,)),
        ],
        compiler_params=pltpu.CompilerParams(
            dimension_semantics=("arbitrary",), vmem_limit_bytes=VMEM_LIMIT),
        name="in_proj",
    )(x2d, norm_w, w_in, b_gate_row, cosf, sa, sb, qk_w_row, ones_blk)


QBLK = 128
KWIN = QBLK + 2 * BAND_HALF
NORM_ROWS = 256
MERGE_ROWS = 64
LOOKAHEAD = 3
SCORE_SLOTS = LOOKAHEAD + 2


def _attn_kernel(q0_ref, q1_ref, q2_ref, k_ref, v_ref, bias_ref, o_ref,
                 va_s, vb_s, oacc_s, den_s, max_s, s_s):
    seq = k_ref.shape[1]
    q_refs = (q0_ref, q1_ref, q2_ref)
    head0 = lax.broadcasted_iota(jnp.int32, (QBLK, LANES), 1) < HEAD_DIM
    head0_rows = lax.broadcasted_iota(jnp.int32, (NORM_ROWS, LANES), 1) < HEAD_DIM

    def prep(i, carry):
        rows = pl.ds(pl.multiple_of(i * NORM_ROWS, NORM_ROWS), NORM_ROWS)
        v = v_ref[0, rows, :]
        va_s[rows, :] = jnp.where(head0_rows, v, 1.0)
        vb_s[rows, :] = jnp.where(head0_rows, 1.0, v)
        return carry

    lax.fori_loop(0, seq // NORM_ROWS, prep, 0, unroll=True)

    blocks = []
    for p, d in enumerate(DILATIONS):
        n = seq // d
        nblk = n // QBLK
        kw = min(n, KWIN)

        def rows_of(idx, d=d, n=n, nblk=nblk, kw=kw):
            r, q0 = idx // nblk, (idx % nblk) * QBLK
            ks = min(max(q0 - BAND_HALF, 0), n - kw)
            return (pl.ds(r + d * q0, QBLK, stride=d), pl.ds(r + d * ks, kw, stride=d),
                    (q0 - ks) // BAND_HALF)

        def scores(idx, slot, p=p, kw=kw, rows_of=rows_of):
            qrows, krows, case = rows_of(idx)
            qb = q_refs[p][0, qrows, :]
            kb = k_ref[0, krows, :].astype(BF16)
            bias = bias_ref[case, :, :kw]
            s_s[slot, :QBLK, :kw] = _dot_nt(jnp.where(head0, qb, 0.0).astype(BF16), kb) + bias
            s_s[slot, QBLK:, :kw] = _dot_nt(jnp.where(head0, 0.0, qb).astype(BF16), kb) + bias

        def weigh(idx, slot, p=p, kw=kw, rows_of=rows_of):
            qrows, krows, _ = rows_of(idx)
            s = s_s[slot, :, :kw]
            m = jnp.max(s, axis=-1, keepdims=True)
            e = jnp.exp2(s - m).astype(BF16)
            o_a = _dot(e[:QBLK], va_s[krows, :].astype(BF16))
            o_b = _dot(e[QBLK:], vb_s[krows, :].astype(BF16))
            oacc_s[p, qrows, :] = jnp.where(head0, o_a, o_b)
            den_s[p, qrows, :] = jnp.where(head0, o_b, o_a)
            max_s[p, qrows, :] = jnp.where(head0, jnp.broadcast_to(m[:QBLK], (QBLK, LANES)),
                                           jnp.broadcast_to(m[QBLK:], (QBLK, LANES)))

        blocks += [(scores, weigh, i) for i in range(d * nblk)]

    for g in range(LOOKAHEAD):
        blocks[g][0](blocks[g][2], g % SCORE_SLOTS)
    for g, (_, weigh, i) in enumerate(blocks):
        if g + LOOKAHEAD < len(blocks):
            ahead_scores, _, ahead_i = blocks[g + LOOKAHEAD]
            ahead_scores(ahead_i, (g + LOOKAHEAD) % SCORE_SLOTS)
        weigh(i, g % SCORE_SLOTS)

    def combine(i, carry):
        rows = pl.ds(pl.multiple_of(i * MERGE_ROWS, MERGE_ROWS), MERGE_ROWS)
        m0, m1, m2 = max_s[0, rows, :], max_s[1, rows, :], max_s[2, rows, :]
        m = jnp.maximum(jnp.maximum(m0, m1), m2)
        w0, w1, w2 = jnp.exp2(m0 - m), jnp.exp2(m1 - m), jnp.exp2(m2 - m)
        num = w0 * oacc_s[0, rows, :] + w1 * oacc_s[1, rows, :] + w2 * oacc_s[2, rows, :]
        d0, d1, d2 = (pltpu.roll(den_s[p, rows, :], HEAD_DIM, 1) for p in range(3))
        o_ref[0, rows, :] = (num / (w0 * d0 + w1 * d1 + w2 * d2)).astype(o_ref.dtype)
        return carry

    lax.fori_loop(0, seq // MERGE_ROWS, combine, 0, unroll=True)


def _band_bias():
    i = np.arange(QBLK)[:, None]
    j = np.arange(KWIN)[None, :]
    off = np.arange(3)[:, None, None] * BAND_HALF
    return np.where(np.abs(j - i - off) <= BAND_HALF, 0.0, NEG_INF).astype(np.float32)


def _attention(proj3d):
    bsz, seq, _ = proj3d.shape
    n_pairs = ATTN_WIDTH // LANES

    def qspec(p):
        return pl.BlockSpec((1, seq, LANES),
                            lambda b, hp, p=p: (b, 0, PF_Q // LANES + p * n_pairs + hp))

    seq_tile = lambda: pltpu.VMEM((seq, LANES), F32)
    pat_tile = lambda: pltpu.VMEM((len(DILATIONS), seq, LANES), F32)
    return pl.pallas_call(
        _attn_kernel,
        grid=(bsz, n_pairs),
        in_specs=[
            qspec(0), qspec(1), qspec(2),
            pl.BlockSpec((1, seq, LANES), lambda b, hp: (b, 0, PF_K // LANES + hp)),
            pl.BlockSpec((1, seq, LANES), lambda b, hp: (b, 0, PF_V // LANES + hp)),
            pl.BlockSpec((3, QBLK, KWIN), lambda b, hp: (0, 0, 0)),
        ],
        out_specs=pl.BlockSpec((1, seq, LANES), lambda b, hp: (b, 0, hp)),
        out_shape=jax.ShapeDtypeStruct((bsz, seq, ATTN_WIDTH), BF16),
        scratch_shapes=[
            seq_tile(), seq_tile(),
            pat_tile(), pat_tile(), pat_tile(),
            pltpu.VMEM((SCORE_SLOTS, 2 * QBLK, KWIN), F32),
        ],
        compiler_params=pltpu.CompilerParams(
            dimension_semantics=("arbitrary", "arbitrary"), vmem_limit_bytes=VMEM_LIMIT),
        name="dilated_attention",
    )(proj3d, proj3d, proj3d, proj3d, proj3d, _band_bias())


LAG_REP = LANES // PG


def _s5_prep_kernel(lre_ref, lim_ref, ldt_ref, bre_ref, bim_ref, cre_ref, cim_ref,
                    toep_ref, win_ref, wout_ref, a_ref, wf32_s):
    four = lambda ref: jnp.concatenate([ref[0, 0], ref[0, 0], ref[1, 0], ref[1, 0]], axis=1)
    lr = jnp.minimum(four(lre_ref), -1e-4)
    li = four(lim_ref)
    dt = jnp.exp(four(ldt_ref))
    col = lax.broadcasted_iota(jnp.int32, (1, SW), 1)
    is_re = (col // LANES) % 2 == 0
    is_fwd = col < 2 * LANES

    mag = jnp.exp(lr * dt)
    pw_re = [jnp.ones_like(mag), mag * jnp.cos(li * dt)]
    pw_im = [jnp.zeros_like(mag), mag * jnp.sin(li * dt)]
    for _ in range(CHUNK - 1):
        pw_re.append(pw_re[-1] * pw_re[1] - pw_im[-1] * pw_im[1])
        pw_im.append(pw_re[-2] * pw_im[1] + pw_im[-1] * pw_re[1])

    a_re, a_im = pw_re[1], pw_im[1]
    nr, ni, mag2 = a_re - 1.0, a_im, lr * lr + li * li
    f_re, f_im = (nr * lr + ni * li) / mag2, (ni * lr - nr * li) / mag2

    row_g = lax.broadcasted_iota(jnp.int32, (PG, SW), 0) // SSM_GROUP
    col_g = (lax.broadcasted_iota(jnp.int32, (PG, SW), 1) % LANES) // SSM_STATE
    diag = row_g == col_g
    wide = lambda ref: jnp.concatenate([ref[0, 0]] * (2 * GP) + [ref[1, 0]] * (2 * GP), axis=1)
    b_re = jnp.where(diag, wide(bre_ref), 0.0)
    b_im = jnp.where(diag, wide(bim_ref), 0.0)
    c_re = jnp.where(diag, wide(cre_ref), 0.0)
    c_im = jnp.where(diag, wide(cim_ref), 0.0)
    bb_re = f_re * b_re - f_im * b_im
    bb_im = f_re * b_im + f_im * b_re

    for s in range(CHUNK):
        rows = slice(s * PG, (s + 1) * PG)
        in_re = jnp.where(is_fwd, pw_re[CHUNK - 1 - s], pw_re[s])
        in_im = jnp.where(is_fwd, pw_im[CHUNK - 1 - s], pw_im[s])
        w = (bb_re * jnp.where(is_re, in_re, in_im)
             + bb_im * jnp.where(is_re, -in_im, in_re))
        wf32_s[rows, :] = w
        win_ref[0, rows, :] = w.astype(BF16)
        out_re = jnp.where(is_fwd, pw_re[s + 1], pw_re[CHUNK - s])
        out_im = jnp.where(is_fwd, pw_im[s + 1], pw_im[CHUNK - s])
        wout_ref[0, rows, :] = (c_re * jnp.where(is_re, out_re, -out_im)
                                + c_im * jnp.where(is_re, -out_im, -out_re)).astype(BF16)

    a_ref[0] = jnp.broadcast_to(jnp.where(is_re, pw_re[CHUNK], pw_im[CHUNK]), a_ref.shape[1:])

    c_cat = jnp.where(is_re, c_re, -c_im)
    c_rep = jnp.concatenate([c_cat] * LAG_REP, axis=0)
    half = 2 * LANES
    lag_f = _dot_nt_f32(wf32_s[:, :half], c_rep[:, :half])
    lag_b = _dot_nt_f32(wf32_s[:, half:], c_rep[:, half:])
    row_t = lax.broadcasted_iota(jnp.int32, (PW, LANES), 0) // PG
    lane_q = lax.broadcasted_iota(jnp.int32, (PW, LANES), 1) // PG
    zeros = lambda n: jnp.zeros((n * PG, LANES), F32)
    for q in range(PW // LANES):
        tile = jnp.zeros((PW, LANES), F32)
        for u in range(LAG_REP):
            t = q * LAG_REP + u
            up = CHUNK - 1 - t
            sh_f = lag_f if up == 0 else jnp.concatenate([lag_f[up * PG:], zeros(up)], axis=0)
            sh_b = lag_b if t == 0 else jnp.concatenate([zeros(t), lag_b[:(CHUNK - t) * PG]], axis=0)
            col_t = jnp.where(row_t <= t, sh_f, 0.0) + jnp.where(row_t >= t, sh_b, 0.0)
            tile = jnp.where(lane_q == u, col_t, tile)
        toep_ref[0, :, q * LANES:(q + 1) * LANES] = tile.astype(BF16)


def _s5_prep(lam_re, lam_im, log_dt, b_re, b_im, c_re, c_im):
    lanes = lambda t: t.reshape(2, N_PAIRS, 1, LANES)
    dt_b = jnp.broadcast_to(log_dt[..., None], lam_re.shape)

    tiles = lambda t_ghp: t_ghp.reshape(2, N_PAIRS, PG, SSM_STATE)

    row = pl.BlockSpec((2, 1, 1, LANES), lambda i: (0, i, 0, 0))
    mat = pl.BlockSpec((2, 1, PG, SSM_STATE), lambda i: (0, i, 0, 0))
    out = lambda r: pl.BlockSpec((1, r, SW), lambda i: (i, 0, 0))
    return pl.pallas_call(
        _s5_prep_kernel,
        grid=(N_PAIRS,),
        in_specs=[row, row, row, mat, mat, mat, mat],
        out_specs=[out(PW), out(PW), out(PW), out(SUBLANES)],
        out_shape=[jax.ShapeDtypeStruct((N_PAIRS, PW, PW), BF16),
                   jax.ShapeDtypeStruct((N_PAIRS, PW, SW), BF16),
                   jax.ShapeDtypeStruct((N_PAIRS, PW, SW), BF16),
                   jax.ShapeDtypeStruct((N_PAIRS, SUBLANES, SW), F32)],
        scratch_shapes=[pltpu.VMEM((PW, SW), F32)],
        compiler_params=pltpu.CompilerParams(dimension_semantics=("arbitrary",)),
        name="s5_prep",
    )(lanes(lam_re), lanes(lam_im), lanes(dt_b), tiles(b_re.transpose(0, 1, 3, 2)),
      tiles(b_im.transpose(0, 1, 3, 2)), tiles(c_re), tiles(c_im))


SCAN_STEPS = 8
REGROUP_CHUNKS = 32


def _s5_kernel(u_ref, d_ref, toep_ref, win_ref, wout_ref, a_ref, y_ref, x_s, p_s, st_s):
    nc = u_ref.shape[1]
    bsz = u_ref.shape[2] // CHUNK
    piece = REGROUP_CHUNKS * bsz
    pp = pl.program_id(1)
    slot = lax.broadcasted_iota(jnp.int32, (piece, LANES), 1) // PG
    time_rows = lambda t: slice(t * bsz, (t + 1) * bsz)

    def gather(cb, carry):
        chunks = pl.ds(pl.multiple_of(cb * REGROUP_CHUNKS, REGROUP_CHUNKS), REGROUP_CHUNKS)
        rows = pl.ds(pl.multiple_of(cb * piece, piece), piece)
        for q in range(CHUNK // GPS):
            acc = None
            for k in range(GPS):
                v = u_ref[0, chunks, time_rows(q * GPS + k), :].reshape(piece, LANES)
                acc = v if acc is None else jnp.where(slot == (pp + k) % GPS, v, acc)
            x_s[rows, q * LANES:(q + 1) * LANES] = pltpu.roll(
                acc, ((GPS - pp) % GPS) * PG, 1).astype(BF16)
        return carry

    lax.fori_loop(0, nc // REGROUP_CHUNKS, gather, 0)

    x = x_s[...]
    p_s[...] = _dot(x, win_ref[0])
    a_fr, a_fi, a_br, a_bi = (a_ref[0, :, q * LANES:(q + 1) * LANES] for q in range(4))
    tile = lambda q: slice(q * LANES, (q + 1) * LANES)

    def scan(i, carry):
        base_f = pl.multiple_of(i * (SCAN_STEPS * bsz), SCAN_STEPS * bsz)
        base_b = pl.multiple_of((nc - (i + 1) * SCAN_STEPS) * bsz, SCAN_STEPS * bsz)
        s_fr, s_fi, s_br, s_bi = carry
        for k in range(SCAN_STEPS):
            rf = pl.ds(base_f + k * bsz, bsz)
            rb = pl.ds(base_b + (SCAN_STEPS - 1 - k) * bsz, bsz)
            st_s[rf, tile(0)] = s_fr
            st_s[rf, tile(1)] = s_fi
            st_s[rb, tile(2)] = s_br
            st_s[rb, tile(3)] = s_bi
            s_fr, s_fi, s_br, s_bi = (a_fr * s_fr - a_fi * s_fi + p_s[rf, tile(0)],
                                      a_fr * s_fi + a_fi * s_fr + p_s[rf, tile(1)],
                                      a_br * s_br - a_bi * s_bi + p_s[rb, tile(2)],
                                      a_br * s_bi + a_bi * s_br + p_s[rb, tile(3)])
        return s_fr, s_fi, s_br, s_bi

    zero = jnp.zeros((bsz, LANES), F32)
    lax.fori_loop(0, nc // SCAN_STEPS, scan, (zero,) * 4)
    p_s[...] = _dot(x, toep_ref[0]) + _dot_nt(st_s[...].astype(BF16), wout_ref[0])

    d_tile = jnp.broadcast_to(d_ref[0], (bsz, LANES))
    d_skip = [d_tile if k == 0 else pltpu.roll(d_tile, k * PG, 1) for k in range(GPS)]

    def scatter(cb, carry):
        chunks = pl.ds(pl.multiple_of(cb * REGROUP_CHUNKS, REGROUP_CHUNKS), REGROUP_CHUNKS)
        rows = pl.ds(pl.multiple_of(cb * piece, piece), piece)
        for q in range(CHUNK // GPS):
            w = pltpu.roll(p_s[rows, q * LANES:(q + 1) * LANES], pp * PG, 1)
            w = w.reshape(1, REGROUP_CHUNKS, bsz, LANES)
            for k in range(GPS):
                where = (pl.ds(0, 1), chunks, time_rows(q * GPS + k), slice(None))
                mine = (slot == (pp + k) % GPS).reshape(1, REGROUP_CHUNKS, bsz, LANES)
                pltpu.store(y_ref.at[where], w + d_skip[k] * u_ref[where], mask=mine)
        return carry

    lax.fori_loop(0, nc // REGROUP_CHUNKS, scatter, 0)


def _s5(u_tb, d_tiles, toep, w_in, w_out, a_pow, bsz):
    n_tiles, nc, cb, _ = u_tb.shape
    rows = nc * bsz
    tile = pl.BlockSpec((1, nc, cb, LANES), lambda j, p: (j, 0, 0, 0))
    pair = lambda r, c: pl.BlockSpec((1, r, c), lambda j, p: (j * GPS + p, 0, 0))
    return pl.pallas_call(
        _s5_kernel,
        grid=(n_tiles, GPS),
        in_specs=[tile, pl.BlockSpec((1, 1, LANES), lambda j, p: (j, 0, 0)),
                  pair(PW, PW), pair(PW, SW), pair(PW, SW), pair(bsz, SW)],
        out_specs=tile,
        out_shape=jax.ShapeDtypeStruct(u_tb.shape, F32),
        scratch_shapes=[pltpu.VMEM((rows, PW), BF16),
                        pltpu.VMEM((rows, SW), F32),
                        pltpu.VMEM((rows, SW), F32)],
        compiler_params=pltpu.CompilerParams(
            dimension_semantics=("arbitrary", "arbitrary"), vmem_limit_bytes=VMEM_LIMIT),
        name="s5_chunked",
    )(u_tb, d_tiles, toep, w_in, w_out, a_pow)


TAIL_TT = 512


def _tail_kernel(x_ref, ytb_ref, za_ref, at_ref, zb_ref, g_ref, wg_ref, bg_ref,
                 wps_ref, wpa_ref, wo_ref, o_ref):
    bsz = ytb_ref.shape[1] // TAIL_TT
    batch = pl.program_id(0) % bsz
    y_s5 = jnp.concatenate(
        [_rotate_rows(ytb_ref[lt, pl.ds(batch, TAIL_TT, stride=bsz), :], forward=False)
         for lt in range(SSM_WIDTH // LANES)], axis=1)
    ys = jax.nn.gelu(y_s5).astype(BF16)
    glu = _dot(ys, wg_ref[...]) + bg_ref[...]
    z_b = zb_ref[...].astype(F32)
    y_b = _dot((at_ref[...].astype(F32) * (z_b * _sigmoid(z_b))).astype(BF16), wpa_ref[...])
    z_a = za_ref[...].astype(F32)
    a_in = glu[:, :SSM_WIDTH] * _sigmoid(glu[:, SSM_WIDTH:]) * (z_a * _sigmoid(z_a))
    y_a = _dot(a_in.astype(BF16), wps_ref[...])
    g = g_ref[...].astype(F32)
    mix = _sigmoid(g[:, :D_MODEL]) * y_a + _sigmoid(g[:, D_MODEL:]) * y_b
    o_ref[...] = x_ref[...] + _dot(mix.astype(BF16), wo_ref[...])


def _tail(x2d, y_tb, attn2d, pz2d, w_glu, b_glu, w_ps, w_pa, w_out, bsz, seq):
    t = x2d.shape[0]
    tile = _token_tile(bsz, seq // TAIL_TT)
    row = lambda w, c: pl.BlockSpec((TAIL_TT, w), lambda i, c=c: (tile(i), c))
    const = lambda shape: pl.BlockSpec(shape, lambda i: (0, 0))
    return pl.pallas_call(
        _tail_kernel,
        grid=(t // TAIL_TT,),
        in_specs=[
            row(D_MODEL, 0),
            pl.BlockSpec((SSM_WIDTH // LANES, TAIL_TT * bsz, LANES), lambda i: (0, i // bsz, 0)),
            row(SSM_WIDTH, PZ_ZA // SSM_WIDTH),
            row(ATTN_WIDTH, 0),
            row(ATTN_WIDTH, PZ_ZB // ATTN_WIDTH),
            row(2 * D_MODEL, PZ_G // (2 * D_MODEL)),
            const((SSM_WIDTH, 2 * SSM_WIDTH)), const((1, 2 * SSM_WIDTH)),
            const((SSM_WIDTH, D_MODEL)), const((ATTN_WIDTH, D_MODEL)), const((D_MODEL, D_MODEL)),
        ],
        out_specs=pl.BlockSpec((TAIL_TT, D_MODEL), lambda i: (tile(i), 0)),
        out_shape=jax.ShapeDtypeStruct((t, D_MODEL), F32),
        compiler_params=pltpu.CompilerParams(
            dimension_semantics=("arbitrary",), vmem_limit_bytes=VMEM_LIMIT),
        name="tail",
    )(x2d, y_tb, pz2d, attn2d, pz2d, pz2d, w_glu, b_glu, w_ps, w_pa, w_out)


def _rope_tables(seq):
    half = ROPE_DIM // 2
    inv = ROPE_THETA ** (-np.arange(0, ROPE_DIM, 2, dtype=np.float64) / ROPE_DIM)
    ang = np.arange(seq, dtype=np.float64)[:, None] * inv[None, :]
    cos, sin = np.cos(ang).astype(np.float32), np.sin(ang).astype(np.float32)
    zeros = np.zeros((seq, HEAD_DIM - ROPE_DIM), np.float32)
    z8 = np.zeros((seq, half), np.float32)
    cos_h = np.concatenate([cos, cos, np.ones_like(zeros)], axis=1)
    sa_h = np.concatenate([-sin, z8, zeros], axis=1)
    sb_h = np.concatenate([z8, sin, zeros], axis=1)
    two = lambda t: np.concatenate([t, t], axis=1).astype(np.float32)
    return two(cos_h), two(sa_h), two(sb_h)


def kernel(x, norm_w, w_in, b_gate, q_norm_w, k_norm_w, ssm_lam_re, ssm_lam_im, ssm_log_dt,
           ssm_b_re, ssm_b_im, ssm_c_re, ssm_c_im, ssm_d, w_glu, b_glu,
           w_proj_ssm, w_proj_attn, w_out):
    bsz, seq, d_model = x.shape
    depth = norm_w.shape[0]
    assert d_model == D_MODEL and w_in.shape[-1] == IN_WIDTH
    assert bsz == SUBLANES, "S5 rows (chunk, batch) must fill whole sublane tiles"
    assert seq % IN_TT == 0 and seq % TAIL_TT == 0 and seq % CHUNK == 0
    assert IN_TT % GPS == 0 and TAIL_TT % GPS == 0 and CHUNK % GPS == 0
    assert all(seq % (d * QBLK) == 0 for d in DILATIONS) and (seq // QBLK) % 4 == 0
    cosf, sa, sb = _rope_tables(seq)
    ones_blk = jnp.asarray(np.kron(np.eye(2 * LANES // HEAD_DIM, dtype=np.float32),
                                   np.full((HEAD_DIM, HEAD_DIM), 1.0 / HEAD_DIM, np.float32)), BF16)
    for layer in range(depth):
        x2d = x.reshape(bsz * seq, D_MODEL)
        q_gain = q_norm_w[layer].astype(F32) * (LOG2E * HEAD_DIM ** -0.5)
        qk_w_row = jnp.concatenate([jnp.tile(q_gain, len(DILATIONS) * ATTN_SLOTS),
                                    jnp.tile(k_norm_w[layer].astype(F32), ATTN_SLOTS)])[None, :]
        pf2d, pz2d, u_tb = _in_proj(
            x2d, norm_w[layer][None, :].astype(F32), w_in, layer,
            b_gate[layer][None, :].astype(F32), cosf, sa, sb, qk_w_row, ones_blk, bsz, seq)
        attn = _attention(pf2d.reshape(bsz, seq, PF_WIDTH))

        toep, s5_in, s5_out, a_pow = _s5_prep(
            ssm_lam_re[layer].astype(F32), ssm_lam_im[layer].astype(F32),
            ssm_log_dt[layer].astype(F32), ssm_b_re[layer].astype(F32),
            ssm_b_im[layer].astype(F32), ssm_c_re[layer].astype(F32),
            ssm_c_im[layer].astype(F32))
        n_tiles = SSM_WIDTH // LANES
        y_tb = _s5(u_tb.reshape(n_tiles, seq // CHUNK, CHUNK * bsz, LANES),
                   ssm_d[layer].astype(F32).reshape(n_tiles, 1, LANES), toep, s5_in, s5_out,
                   a_pow, bsz)

        out2d = _tail(x2d, y_tb.reshape(n_tiles, seq * bsz, LANES),
                      attn.reshape(bsz * seq, ATTN_WIDTH), pz2d,
                      w_glu[layer].astype(BF16), b_glu[layer][None, :].astype(F32),
                      w_proj_ssm[layer].astype(BF16), w_proj_attn[layer].astype(BF16),
                      w_out[layer].astype(BF16), bsz, seq)
        x = out2d.reshape(bsz, seq, D_MODEL)
    return x
```
